```python
import math
import jax, jax.numpy as jnp
from jax import lax
import numpy as np


D_MODEL = 1024
BATCH = 8
SEQ = 2048
DEPTH = 2
DEC_BATCH = 8
DEC_SEQ = 32
PAST_LEN = 2048

CHUNK = 64
Q_BLOCK = 128
POOL_WIDTH = D_MODEL // 4
POOL_WINDOWS = (2, 4, 8, 16)
POOL_GROUPS = len(POOL_WINDOWS)
POOL_GROUP_DIM = POOL_WIDTH // POOL_GROUPS
POOL_HIST = max(POOL_WINDOWS) - 1
ATTN_WIDTH = D_MODEL // 2
N_HEADS = 4
HEAD_DIM = ATTN_WIDTH // (2 * N_HEADS)
V_DIM = 2 * HEAD_DIM
ROPE_THETA = 10000.0
CONV_WIDTH = D_MODEL // 4
CONV_K = 31
CONV_HIST = CONV_K - 1
MIX_WIDTH = POOL_WIDTH + ATTN_WIDTH + CONV_WIDTH
IN_WIDTH = POOL_WIDTH + 3 * ATTN_WIDTH + 2 * CONV_WIDTH
D_FF = ((-(-8 * D_MODEL // 3) + 255) // 256) * 256
EPS = 1e-6

kernel_name = "hybrid_pool_diffattn_conformer_stream_step"


def rms_norm(x, g):
    xf = x.astype(jnp.float32)
    y = xf * lax.rsqrt(jnp.mean(xf * xf, axis=-1, keepdims=True) + EPS)
    return (y * g.astype(jnp.float32)).astype(x.dtype)


def layer_norm(x, g, b):
    xf = x.astype(jnp.float32)
    mu = jnp.mean(xf, axis=-1, keepdims=True)
    xc = xf - mu
    y = xc * lax.rsqrt(jnp.mean(xc * xc, axis=-1, keepdims=True) + EPS)
    return (y * g.astype(jnp.float32) + b.astype(jnp.float32)).astype(x.dtype)


def rope(x, pos):
    half = HEAD_DIM // 2
    inv = ROPE_THETA ** (-jnp.arange(half, dtype=jnp.float32) / half)
    ang = pos.astype(jnp.float32)[:, None] * inv[None, :]
    cos = jnp.cos(ang)[:, None, None, :]
    sin = jnp.sin(ang)[:, None, None, :]
    xf = x.astype(jnp.float32)
    x1, x2 = xf[..., :half], xf[..., half:]
    return jnp.concatenate([x1 * cos - x2 * sin, x2 * cos + x1 * sin], axis=-1).astype(x.dtype)


def pool_mixer(u, hist, pos, pool_w, pool_scale):
    L = u.shape[1]
    ext = jnp.concatenate([hist, u], axis=1)
    c = jnp.cumsum(ext.astype(jnp.float32), axis=1)
    cz = jnp.concatenate([jnp.zeros_like(c[:, :1]), c], axis=1)
    uf = u.astype(jnp.float32)
    outs = []
    for g, w in enumerate(POOL_WINDOWS):
        sl = slice(g * POOL_GROUP_DIM, (g + 1) * POOL_GROUP_DIM)
        win_sum = cz[:, POOL_HIST + 1:POOL_HIST + 1 + L, sl] - cz[:, POOL_HIST + 1 - w:POOL_HIST + 1 - w + L, sl]
        count = jnp.minimum(pos + 1, w).astype(jnp.float32)[None, :, None]
        d = win_sum / count - uf[..., sl]
        outs.append(jnp.einsum('blc,cd->bld', d.astype(u.dtype), pool_w[g]))
    y = jnp.concatenate(outs, axis=-1) * pool_scale
    return y, ext[:, -POOL_HIST:]


def diff_attention(q, k, v, lam, mask):
    s = jnp.einsum('bqhcd,bkhcd->bhcqk', q, k, preferred_element_type=jnp.float32) * (HEAD_DIM ** -0.5)
    if mask is not None:
        s = jnp.where(mask, s, -1e30)
    p = jax.nn.softmax(s, axis=-1)
    pd = p[:, :, 0] - lam * p[:, :, 1]
    return jnp.einsum('bhqk,bkhe->bqhe', pd.astype(v.dtype), v)


def prompt_attention(q, k, v, lam):
    B, S = q.shape[0], q.shape[1]
    nb = S // Q_BLOCK
    qb = q.reshape(B, nb, Q_BLOCK, N_HEADS, 2, HEAD_DIM).transpose(1, 0, 2, 3, 4, 5)
    key_chunk = jnp.arange(S) // CHUNK

    def block(args):
        qi, i = args
        q_chunk = (i * Q_BLOCK + jnp.arange(Q_BLOCK)) // CHUNK
        mask = key_chunk[None, :] <= q_chunk[:, None]
        return diff_attention(qi, k, v, lam, mask)

    o = lax.map(block, (qb, jnp.arange(nb)))
    return o.transpose(1, 0, 2, 3, 4).reshape(B, S, N_HEADS, V_DIM)


def conv_mixer(a, b, hist, conv_dw, conv_dw_b, conv_ln_g, conv_ln_b, conv_pw):
    u = a * jax.nn.sigmoid(b)
    ext = jnp.concatenate([hist, u], axis=1)
    y = lax.conv_general_dilated(ext, conv_dw[:, None, :], window_strides=(1,), padding='VALID',
                                 dimension_numbers=('NWC', 'WIO', 'NWC'), feature_group_count=CONV_WIDTH)
    y = layer_norm(y + conv_dw_b, conv_ln_g, conv_ln_b)
    y = jax.nn.silu(y) @ conv_pw
    return y, ext[:, -CONV_HIST:]


def hybrid_layer(x, pos, pool_hist, conv_hist, past_k, past_v, layer,
                 norm_mix_g, w_in, pool_w, pool_scale, lambda_qk, diff_norm_g,
                 conv_dw, conv_dw_b, conv_ln_g, conv_ln_b, conv_pw, w_out,
                 norm_ffn_g, w_gate_up, w_down):
    B, L, _ = x.shape
    h = rms_norm(x, norm_mix_g)
    z = h @ w_in
    o1 = POOL_WIDTH
    o2 = o1 + ATTN_WIDTH
    o3 = o2 + ATTN_WIDTH
    o4 = o3 + ATTN_WIDTH
    o5 = o4 + CONV_WIDTH
    u_pool = z[..., :o1]
    q = rope(z[..., o1:o2].reshape(B, L, N_HEADS, 2, HEAD_DIM), pos)
    k = rope(z[..., o2:o3].reshape(B, L, N_HEADS, 2, HEAD_DIM), pos)
    v = z[..., o3:o4].reshape(B, L, N_HEADS, V_DIM)
    a_conv = z[..., o4:o5]
    b_conv = z[..., o5:]

    y_pool, new_pool = pool_mixer(u_pool, pool_hist, pos, pool_w, pool_scale)

    lam_init = 0.8 - 0.6 * math.exp(-0.3 * layer)
    lq = lambda_qk.astype(jnp.float32)
    lam = jnp.exp(jnp.sum(lq[0] * lq[1])) - jnp.exp(jnp.sum(lq[2] * lq[3])) + lam_init
    if past_k is None:
        o = prompt_attention(q, k, v, lam)
    else:
        o = diff_attention(q, jnp.concatenate([past_k, k], axis=1), jnp.concatenate([past_v, v], axis=1), lam, None)
    o = (rms_norm(o, diff_norm_g) * (1.0 - lam_init)).reshape(B, L, ATTN_WIDTH)

    y_conv, new_conv = conv_mixer(a_conv, b_conv, conv_hist, conv_dw, conv_dw_b, conv_ln_g, conv_ln_b, conv_pw)

    x = x + jnp.concatenate([y_pool, o, y_conv], axis=-1) @ w_out
    h = rms_norm(x, norm_ffn_g)
    g, u = jnp.split(h @ w_gate_up, 2, axis=-1)
    x = x + (jax.nn.silu(g) * u) @ w_down
    return x, k, v, new_pool, new_conv


def setup_inputs(seed: int = 0) -> dict:
    key = jax.random.key(seed)
    ks = jax.random.split(key, 24)
    f = jnp.float32
    nrm = lambda k, shape, s: jax.random.normal(k, shape, f) * s
    return {
        "x_prompt": nrm(ks[0], (BATCH, SEQ, D_MODEL), 1.0),
        "x_sample": nrm(ks[1], (DEC_BATCH, DEC_SEQ, D_MODEL), 1.0),
        "cache_k": nrm(ks[2], (DEPTH, DEC_BATCH, PAST_LEN, N_HEADS, 2, HEAD_DIM), 1.0),
        "cache_v": nrm(ks[3], (DEPTH, DEC_BATCH, PAST_LEN, N_HEADS, V_DIM), 1.0),
        "state_pool": nrm(ks[4], (DEPTH, DEC_BATCH, POOL_HIST, POOL_WIDTH), 1.0),
        "state_conv": nrm(ks[5], (DEPTH, DEC_BATCH, CONV_HIST, CONV_WIDTH), 0.5),
        "norm_mix_g": 1.0 + nrm(ks[6], (DEPTH, D_MODEL), 0.05),
        "w_in": nrm(ks[7], (DEPTH, D_MODEL, IN_WIDTH), D_MODEL ** -0.5),
        "pool_w": nrm(ks[8], (DEPTH, POOL_GROUPS, POOL_GROUP_DIM, POOL_GROUP_DIM), POOL_GROUP_DIM ** -0.5),
        "pool_scale": 1.0 + nrm(ks[9], (DEPTH, POOL_WIDTH), 0.1),
        "lambda_qk": nrm(ks[10], (DEPTH, 4, HEAD_DIM), 0.1),
        "diff_norm_g": 1.0 + nrm(ks[11], (DEPTH, V_DIM), 0.05),
        "conv_dw": nrm(ks[12], (DEPTH, CONV_K, CONV_WIDTH), CONV_K ** -0.5),
        "conv_dw_b": nrm(ks[13], (DEPTH, CONV_WIDTH), 0.01),
        "conv_ln_g": 1.0 + nrm(ks[14], (DEPTH, CONV_WIDTH), 0.05),
        "conv_ln_b": nrm(ks[15], (DEPTH, CONV_WIDTH), 0.01),
        "conv_pw": nrm(ks[16], (DEPTH, CONV_WIDTH, CONV_WIDTH), CONV_WIDTH ** -0.5),
        "w_out": nrm(ks[17], (DEPTH, MIX_WIDTH, D_MODEL), MIX_WIDTH ** -0.5),
        "norm_ffn_g": 1.0 + nrm(ks[18], (DEPTH, D_MODEL), 0.05),
        "w_gate_up": nrm(ks[19], (DEPTH, D_MODEL, 2 * D_FF), D_MODEL ** -0.5),
        "w_down": nrm(ks[20], (DEPTH, D_FF, D_MODEL), D_FF ** -0.5),
        "final_norm_g": 1.0 + nrm(ks[21], (D_MODEL,), 0.05),
    }


def reference(x_prompt, x_sample, cache_k, cache_v, state_pool, state_conv,
              norm_mix_g, w_in, pool_w, pool_scale, lambda_qk, diff_norm_g,
              conv_dw, conv_dw_b, conv_ln_g, conv_ln_b, conv_pw, w_out,
              norm_ffn_g, w_gate_up, w_down, final_norm_g):
    B, S, _ = x_prompt.shape
    Ls = x_sample.shape[1]
    past_len = cache_k.shape[2]
    pos_p = jnp.arange(S, dtype=jnp.int32)
    pos_s = past_len + jnp.arange(Ls, dtype=jnp.int32)
    zero_pool = jnp.zeros((B, POOL_HIST, POOL_WIDTH), x_prompt.dtype)
    zero_conv = jnp.zeros((B, CONV_HIST, CONV_WIDTH), x_prompt.dtype)

    xp, xs = x_prompt, x_sample
    kp, vp, pp, cp = [], [], [], []
    ks_, vs_, ps_, cs_ = [], [], [], []
    for l in range(DEPTH):
        w = (norm_mix_g[l], w_in[l], pool_w[l], pool_scale[l], lambda_qk[l], diff_norm_g[l],
             conv_dw[l], conv_dw_b[l], conv_ln_g[l], conv_ln_b[l], conv_pw[l], w_out[l],
             norm_ffn_g[l], w_gate_up[l], w_down[l])
        xp, k1, v1, p1, c1 = hybrid_layer(xp, pos_p, zero_pool, zero_conv, None, None, l, *w)
        xs, k2, v2, p2, c2 = hybrid_layer(xs, pos_s, state_pool[l], state_conv[l], cache_k[l], cache_v[l], l, *w)
        kp.append(k1); vp.append(v1); pp.append(p1); cp.append(c1)
        ks_.append(k2); vs_.append(v2); ps_.append(p2); cs_.append(c2)

    y_prompt = rms_norm(xp, final_norm_g)
    y_sample = rms_norm(xs, final_norm_g)
    return (y_prompt, y_sample,
            jnp.stack(kp), jnp.stack(vp), jnp.stack(pp), jnp.stack(cp),
            jnp.stack(ks_), jnp.stack(vs_), jnp.stack(ps_), jnp.stack(cs_))
```

```python
import functools
import math

import jax
import jax.numpy as jnp
from jax import lax
from jax.experimental import pallas as pl
from jax.experimental.pallas import tpu as pltpu

D_MODEL = 1024
CHUNK = 64
POOL_WIDTH = D_MODEL // 4
POOL_WINDOWS = (2, 4, 8, 16)
POOL_GROUP_DIM = POOL_WIDTH // len(POOL_WINDOWS)
POOL_HIST = max(POOL_WINDOWS) - 1
ATTN_WIDTH = D_MODEL // 2
N_HEADS = 4
HEAD_DIM = ATTN_WIDTH // (2 * N_HEADS)
V_DIM = 2 * HEAD_DIM
ROPE_THETA = 10000.0
CONV_WIDTH = D_MODEL // 4
CONV_K = 31
CONV_HIST = CONV_K - 1
MIX_WIDTH = POOL_WIDTH + ATTN_WIDTH + CONV_WIDTH
IN_WIDTH = POOL_WIDTH + 3 * ATTN_WIDTH + 2 * CONV_WIDTH
D_FF = ((-(-8 * D_MODEL // 3) + 255) // 256) * 256
EPS = 1e-6

LANES = 128
HALO = 32
ROW_TILE = 256
ATTN_TILE = 256
VMEM_LIMIT = 56 * 1024 * 1024

F32 = jnp.float32
BF16 = jnp.bfloat16


def _rms(x, g):
    return x * lax.rsqrt(jnp.mean(x * x, axis=-1, keepdims=True) + EPS) * g


def _const_spec(shape):
    return pl.BlockSpec(shape, lambda *_: (0,) * len(shape), pipeline_mode=pl.Buffered(1))


def _in_proj_kernel(x_ref, g_ref, w_ref, cos_ref, sin_ref,
                    upool_ref, q_ref, k_ref, kb_ref, v_ref, vb_ref, uconv_ref):
    h = _rms(x_ref[...], g_ref[...]).astype(BF16)
    z = jnp.dot(h, w_ref[...], preferred_element_type=F32)
    o1 = POOL_WIDTH
    o2 = o1 + ATTN_WIDTH
    o3 = o2 + ATTN_WIDTH
    o4 = o3 + ATTN_WIDTH
    o5 = o4 + CONV_WIDTH
    upool_ref[...] = z[:, :o1]

    cos = cos_ref[...]
    sin = sin_ref[...]
    lane = lax.broadcasted_iota(jnp.int32, cos.shape, 1)
    first_half = (lane & (HEAD_DIM // 2)) == 0

    def rope(t):
        up = pltpu.roll(t, LANES - HEAD_DIM // 2, axis=1)
        down = pltpu.roll(t, HEAD_DIM // 2, axis=1)
        return t * cos + jnp.where(first_half, up, down) * sin

    scale = HEAD_DIM ** -0.5
    for j in range(ATTN_WIDTH // LANES):
        sl = slice(j * LANES, (j + 1) * LANES)
        qj = rope(z[:, o1 + j * LANES:o1 + (j + 1) * LANES])
        kj = rope(z[:, o2 + j * LANES:o2 + (j + 1) * LANES])
        vj = z[:, o3 + j * LANES:o3 + (j + 1) * LANES]
        q_ref[:, sl] = (qj * scale).astype(BF16)
        k_ref[:, sl] = kj
        kb_ref[:, sl] = kj.astype(BF16)
        v_ref[:, sl] = vj
        vb_ref[:, sl] = vj.astype(BF16)
    uconv_ref[...] = z[:, o4:o5] * jax.nn.sigmoid(z[:, o5:])


def _in_proj(x, g, w_bf, cos_tab, sin_tab, tm):
    n = x.shape[0]
    tab_tiles = cos_tab.shape[0] // tm
    row = lambda w: pl.BlockSpec((tm, w), lambda i: (i, 0))
    tab = pl.BlockSpec((tm, LANES), lambda i: (i % tab_tiles, 0))
    outs = [
        jax.ShapeDtypeStruct((n, POOL_WIDTH), F32),
        jax.ShapeDtypeStruct((n, ATTN_WIDTH), BF16),
        jax.ShapeDtypeStruct((n, ATTN_WIDTH), F32),
        jax.ShapeDtypeStruct((n, ATTN_WIDTH), BF16),
        jax.ShapeDtypeStruct((n, ATTN_WIDTH), F32),
        jax.ShapeDtypeStruct((n, ATTN_WIDTH), BF16),
        jax.ShapeDtypeStruct((n, CONV_WIDTH), F32),
    ]
    return pl.pallas_call(
        _in_proj_kernel,
        grid=(n // tm,),
        in_specs=[row(D_MODEL), _const_spec((1, D_MODEL)), _const_spec((D_MODEL, IN_WIDTH)), tab, tab],
        out_specs=[row(POOL_WIDTH), row(ATTN_WIDTH), row(ATTN_WIDTH), row(ATTN_WIDTH),
                   row(ATTN_WIDTH), row(ATTN_WIDTH), row(CONV_WIDTH)],
        out_shape=outs,
        compiler_params=pltpu.CompilerParams(dimension_semantics=("arbitrary",),
                                             vmem_limit_bytes=VMEM_LIMIT),
        name="in_proj",
    )(x, g, w_bf, cos_tab, sin_tab)


def _lambda_scalar(lq, lam_init):
    a = jnp.sum(lq[0:1, :] * lq[1:2, :], axis=-1, keepdims=True)
    b = jnp.sum(lq[2:3, :] * lq[3:4, :], axis=-1, keepdims=True)
    return jnp.exp(a) - jnp.exp(b) + lam_init


def _stack_maps(q):
    lane = lax.broadcasted_iota(jnp.int32, q.shape, 1)
    zero = jnp.zeros_like(q)
    return jnp.concatenate([jnp.where(lane < HEAD_DIM, q, zero), jnp.where(lane >= HEAD_DIM, q, zero)], axis=0)


def _flash_step(carry, qq, kt, vt, mask):
    m, l, acc = carry
    s = lax.dot_general(qq, kt, (((1,), (1,)), ((), ())), preferred_element_type=F32)
    if mask is not None:
        s = jnp.where(mask, s, -1e30)
    m_new = jnp.maximum(m, jnp.max(s, axis=-1, keepdims=True))
    alpha = jnp.exp(m - m_new)
    p = jnp.exp(s - m_new)
    l = alpha * l + jnp.sum(p, axis=-1, keepdims=True)
    acc = alpha * acc + jnp.dot(p.astype(BF16), vt, preferred_element_type=F32)
    return m_new, l, acc


def _flash_init(rows):
    return (jnp.full((rows, 1), -jnp.inf, F32), jnp.zeros((rows, 1), F32), jnp.zeros((rows, V_DIM), F32))


def _attn_finish(carry, lam, g, lam_init, tq):
    _, l, acc = carry
    o = acc[:tq] / l[:tq] - lam * (acc[tq:] / l[tq:])
    return _rms(o, g) * (1.0 - lam_init)


def _prompt_attn_kernel(lq_ref, g_ref, q_ref, k_ref, v_ref, o_ref, *, lam_init):
    t = ATTN_TILE
    qi = pl.program_id(2)
    qq = _stack_maps(q_ref[...])

    def body(j, carry):
        r = pl.ds(pl.multiple_of(j * t, t), t)
        return _flash_step(carry, qq, k_ref[r, :], v_ref[r, :], None)

    carry = lax.fori_loop(0, qi, body, _flash_init(2 * t))
    rows = lax.broadcasted_iota(jnp.int32, (2 * t, t), 0)
    cols = lax.broadcasted_iota(jnp.int32, (2 * t, t), 1)
    mask = (cols // CHUNK) <= ((rows % t) // CHUNK)
    r = pl.ds(pl.multiple_of(qi * t, t), t)
    carry = _flash_step(carry, qq, k_ref[r, :], v_ref[r, :], mask)
    lam = _lambda_scalar(lq_ref[...], lam_init)
    o_ref[...] = _attn_finish(carry, lam, g_ref[...], lam_init, t).astype(BF16)


def _prompt_attention(lq, g, q, kb, vb, lam_init):
    b, s, _ = q.shape
    t = ATTN_TILE
    return pl.pallas_call(
        functools.partial(_prompt_attn_kernel, lam_init=lam_init),
        grid=(b, N_HEADS, s // t),
        in_specs=[
            pl.BlockSpec((4, HEAD_DIM), lambda b_, h, i: (0, 0)),
            pl.BlockSpec((1, V_DIM), lambda b_, h, i: (0, 0)),
            pl.BlockSpec((None, t, LANES), lambda b_, h, i: (b_, i, h)),
            pl.BlockSpec((None, s, LANES), lambda b_, h, i: (b_, 0, h)),
            pl.BlockSpec((None, s, LANES), lambda b_, h, i: (b_, 0, h)),
        ],
        out_specs=pl.BlockSpec((None, t, LANES), lambda b_, h, i: (b_, i, h)),
        out_shape=jax.ShapeDtypeStruct((b, s, ATTN_WIDTH), BF16),
        compiler_params=pltpu.CompilerParams(dimension_semantics=("arbitrary",) * 3,
                                             vmem_limit_bytes=VMEM_LIMIT),
        name="prompt_attention",
    )(lq, g, q, kb, vb)


def _sample_attn_kernel(lq_ref, g_ref, q_ref, ck_ref, cv_ref, k_ref, v_ref, o_ref, *, lam_init, past_len):
    t = ATTN_TILE
    tq = q_ref.shape[0]
    qq = _stack_maps(q_ref[...])

    def body(j, carry):
        r = pl.ds(pl.multiple_of(j * t, t), t)
        return _flash_step(carry, qq, ck_ref[r, :].astype(BF16), cv_ref[r, :].astype(BF16), None)

    carry = lax.fori_loop(0, past_len // t, body, _flash_init(2 * tq))
    carry = _flash_step(carry, qq, k_ref[...], v_ref[...], None)
    lam = _lambda_scalar(lq_ref[...], lam_init)
    o_ref[...] = _attn_finish(carry, lam, g_ref[...], lam_init, tq).astype(BF16)


def _sample_attention(lq, g, q, cache_k, cache_v, kb, vb, lam_init):
    b, ls, _ = q.shape
    past_len = cache_k.shape[1]
    assert past_len % ATTN_TILE == 0
    return pl.pallas_call(
        functools.partial(_sample_attn_kernel, lam_init=lam_init, past_len=past_len),
        grid=(b, N_HEADS),
        in_specs=[
            pl.BlockSpec((4, HEAD_DIM), lambda b_, h: (0, 0)),
            pl.BlockSpec((1, V_DIM), lambda b_, h: (0, 0)),
            pl.BlockSpec((None, ls, LANES), lambda b_, h: (b_, 0, h)),
            pl.BlockSpec((None, past_len, LANES), lambda b_, h: (b_, 0, h)),
            pl.BlockSpec((None, past_len, LANES), lambda b_, h: (b_, 0, h)),
            pl.BlockSpec((None, ls, LANES), lambda b_, h: (b_, 0, h)),
            pl.BlockSpec((None, ls, LANES), lambda b_, h: (b_, 0, h)),
        ],
        out_specs=pl.BlockSpec((None, ls, LANES), lambda b_, h: (b_, 0, h)),
        out_shape=jax.ShapeDtypeStruct((b, ls, ATTN_WIDTH), BF16),
        compiler_params=pltpu.CompilerParams(dimension_semantics=("arbitrary",) * 2,
                                             vmem_limit_bytes=VMEM_LIMIT),
        name="sample_attention",
    )(lq, g, q, cache_k, cache_v, kb, vb)


def _mixers_kernel(up_ref, up_halo_ref, up_hist_ref, uc_ref, uc_halo_ref, uc_hist_ref,
                   pool_w_ref, pool_scale_ref, dw_ref, dw_b_ref, ln_g_ref, ln_b_ref, pw_ref,
                   ypool_ref, yconv_ref, ext_p, ext_c, *, tiles_per_seq, pos_base):
    tm = up_ref.shape[0]
    i = pl.program_id(0)
    tile_in_seq = i % tiles_per_seq
    first = tile_in_seq == 0

    ext_p[0:HALO, :] = jnp.where(first, up_hist_ref[...], up_halo_ref[...])
    ext_p[HALO:, :] = up_ref[...]
    ext_c[0:HALO, :] = jnp.where(first, uc_hist_ref[...], uc_halo_ref[...])
    ext_c[HALO:, :] = uc_ref[...]

    back = lambda ref, k: ref[pl.ds(HALO - k, tm), :]
    u = back(ext_p, 0)
    sums = {}
    acc = u
    for k in range(1, max(POOL_WINDOWS)):
        acc = acc + back(ext_p, k)
        if k + 1 in POOL_WINDOWS:
            sums[k + 1] = acc
    lane = lax.broadcasted_iota(jnp.int32, (tm, POOL_WIDTH), 1)
    group = lane // POOL_GROUP_DIM
    win = sums[POOL_WINDOWS[-1]]
    width = jnp.full((tm, POOL_WIDTH), POOL_WINDOWS[-1], jnp.int32)
    for gi in range(len(POOL_WINDOWS) - 2, -1, -1):
        win = jnp.where(group == gi, sums[POOL_WINDOWS[gi]], win)
        width = jnp.where(group == gi, POOL_WINDOWS[gi], width)
    pos = pos_base + tile_in_seq * tm + lax.broadcasted_iota(jnp.int32, (tm, POOL_WIDTH), 0)
    count = jnp.minimum(pos + 1, width).astype(F32)
    d = (win / count - u).astype(BF16)
    ypool = jnp.dot(d, pool_w_ref[...], preferred_element_type=F32) * pool_scale_ref[...]
    ypool_ref[...] = ypool.astype(BF16)

    y = jnp.zeros((tm, CONV_WIDTH), F32)
    for k in range(CONV_K):
        y = y + ext_c[pl.ds(HALO - CONV_HIST + k, tm), :] * dw_ref[k:k + 1, :]
    y = y + dw_b_ref[...]
    mu = jnp.mean(y, axis=-1, keepdims=True)
    yc = y - mu
    yn = yc * lax.rsqrt(jnp.mean(yc * yc, axis=-1, keepdims=True) + EPS) * ln_g_ref[...] + ln_b_ref[...]
    act = (yn * jax.nn.sigmoid(yn)).astype(BF16)
    yconv_ref[...] = jnp.dot(act, pw_ref[...], preferred_element_type=F32).astype(BF16)


def _mixers(upool, uconv, hist_pool, hist_conv, pool_w_bd, pool_scale, dw, dw_b, ln_g, ln_b, pw_bf,
            tm, seq_len, pos_base):
    n = upool.shape[0]
    tiles_per_seq = seq_len // tm
    ratio = tm // HALO
    cur = pl.BlockSpec((tm, POOL_WIDTH), lambda i: (i, 0))
    halo = pl.BlockSpec((HALO, POOL_WIDTH), lambda i: (jnp.maximum(i * ratio - 1, 0), 0))
    hist = pl.BlockSpec((None, HALO, POOL_WIDTH), lambda i: (i // tiles_per_seq, 0, 0))
    return pl.pallas_call(
        functools.partial(_mixers_kernel, tiles_per_seq=tiles_per_seq, pos_base=pos_base),
        grid=(n // tm,),
        in_specs=[cur, halo, hist, cur, halo, hist,
                  _const_spec((POOL_WIDTH, POOL_WIDTH)), _const_spec((1, POOL_WIDTH)),
                  _const_spec((CONV_K, CONV_WIDTH)), _const_spec((1, CONV_WIDTH)),
                  _const_spec((1, CONV_WIDTH)), _const_spec((1, CONV_WIDTH)),
                  _const_spec((CONV_WIDTH, CONV_WIDTH))],
        out_specs=[cur, cur],
        out_shape=[jax.ShapeDtypeStruct((n, POOL_WIDTH), BF16), jax.ShapeDtypeStruct((n, CONV_WIDTH), BF16)],
        scratch_shapes=[pltpu.VMEM((HALO + tm, POOL_WIDTH), F32), pltpu.VMEM((HALO + tm, CONV_WIDTH), F32)],
        compiler_params=pltpu.CompilerParams(dimension_semantics=("arbitrary",),
                                             vmem_limit_bytes=VMEM_LIMIT),
        name="mixers",
    )(upool, upool, hist_pool, uconv, uconv, hist_conv, pool_w_bd, pool_scale, dw, dw_b, ln_g, ln_b, pw_bf)


def _dense_kernel(x_ref, yp_ref, o_ref, yc_ref, wout_ref, g_ref, wgu_ref, wd_ref, fg_ref, out_ref, *, final):
    o1 = POOL_WIDTH
    o2 = o1 + ATTN_WIDTH
    mix = (jnp.dot(yp_ref[...], wout_ref[0:o1, :], preferred_element_type=F32)
           + jnp.dot(o_ref[...], wout_ref[o1:o2, :], preferred_element_type=F32)
           + jnp.dot(yc_ref[...], wout_ref[o2:, :], preferred_element_type=F32))
    x1 = x_ref[...] + mix
    h = _rms(x1, g_ref[...]).astype(BF16)
    gu = jnp.dot(h, wgu_ref[...], preferred_element_type=F32)
    gate = gu[:, :D_FF]
    a = (gate * jax.nn.sigmoid(gate) * gu[:, D_FF:]).astype(BF16)
    x2 = x1 + jnp.dot(a, wd_ref[...], preferred_element_type=F32)
    out_ref[...] = _rms(x2, fg_ref[...]) if final else x2


def _dense(x, ypool, o, yconv, wout_bf, g, wgu_bf, wd_bf, fg, tm, final):
    n = x.shape[0]
    row = lambda w: pl.BlockSpec((tm, w), lambda i: (i, 0))
    return pl.pallas_call(
        functools.partial(_dense_kernel, final=final),
        grid=(n // tm,),
        in_specs=[row(D_MODEL), row(POOL_WIDTH), row(ATTN_WIDTH), row(CONV_WIDTH),
                  _const_spec((MIX_WIDTH, D_MODEL)), _const_spec((1, D_MODEL)),
                  _const_spec((D_MODEL, 2 * D_FF)), _const_spec((D_FF, D_MODEL)), _const_spec((1, D_MODEL))],
        out_specs=row(D_MODEL),
        out_shape=jax.ShapeDtypeStruct((n, D_MODEL), F32),
        compiler_params=pltpu.CompilerParams(dimension_semantics=("arbitrary",),
                                             vmem_limit_bytes=VMEM_LIMIT),
        name="dense",
    )(x, ypool, o, yconv, wout_bf, g, wgu_bf, wd_bf, fg)


def _rope_tables(pos):
    half = HEAD_DIM // 2
    inv = ROPE_THETA ** (-jnp.arange(half, dtype=F32) / half)
    ang = pos.astype(F32)[:, None] * inv[None, :]
    reps = LANES // half
    cos = jnp.tile(jnp.cos(ang), (1, reps))
    sign = jnp.tile(jnp.concatenate([-jnp.ones((half,), F32), jnp.ones((half,), F32)]), LANES // HEAD_DIM)
    sin = jnp.tile(jnp.sin(ang), (1, reps)) * sign[None, :]
    return cos, sin


def _block_diag(pool_w):
    g, d, _ = pool_w.shape
    out = jnp.zeros((g * d, g * d), pool_w.dtype)
    for i in range(g):
        out = out.at[i * d:(i + 1) * d, i * d:(i + 1) * d].set(pool_w[i])
    return out


def _pad_hist(h):
    return jnp.pad(h, ((0, 0), (HALO - h.shape[1], 0), (0, 0)))


def kernel(x_prompt, x_sample, cache_k, cache_v, state_pool, state_conv, norm_mix_g, w_in, pool_w, pool_scale,
           lambda_qk, diff_norm_g, conv_dw, conv_dw_b, conv_ln_g, conv_ln_b, conv_pw, w_out, norm_ffn_g,
           w_gate_up, w_down, final_norm_g):
    B, S, _ = x_prompt.shape
    Bs, Ls, _ = x_sample.shape
    depth = w_in.shape[0]
    past_len = cache_k.shape[2]
    assert S % ATTN_TILE == 0 and S % ROW_TILE == 0 and (Bs * Ls) % 16 == 0 and Ls % 16 == 0 and Ls >= CONV_HIST

    cos_p, sin_p = _rope_tables(jnp.arange(S, dtype=jnp.int32))
    cos_s, sin_s = _rope_tables(past_len + jnp.arange(Ls, dtype=jnp.int32))
    cos_s, sin_s = jnp.tile(cos_s, (Bs, 1)), jnp.tile(sin_s, (Bs, 1))
    zero_pool = jnp.zeros((B, HALO, POOL_WIDTH), F32)
    zero_conv = jnp.zeros((B, HALO, CONV_WIDTH), F32)

    xp = x_prompt.reshape(B * S, D_MODEL)
    xs = x_sample.reshape(Bs * Ls, D_MODEL)
    row2 = lambda a: a.reshape(1, -1)
    fg = row2(final_norm_g)
    kp, vp, pp, cp = [], [], [], []
    ks_, vs_, ps_, cs_ = [], [], [], []
    for l in range(depth):
        lam_init = 0.8 - 0.6 * math.exp(-0.3 * l)
        w_in_bf = w_in[l].astype(BF16)
        w_out_bf = w_out[l].astype(BF16)
        w_gu_bf = w_gate_up[l].astype(BF16)
        w_d_bf = w_down[l].astype(BF16)
        pw_bf = conv_pw[l].astype(BF16)
        pool_bd = _block_diag(pool_w[l]).astype(BF16)
        mixer_w = (pool_bd, row2(pool_scale[l]), conv_dw[l], row2(conv_dw_b[l]), row2(conv_ln_g[l]),
                   row2(conv_ln_b[l]), pw_bf)
        final = l == depth - 1

        upool, q, k, kb, v, vb, uconv = _in_proj(xp, row2(norm_mix_g[l]), w_in_bf, cos_p, sin_p, ROW_TILE)
        o = _prompt_attention(lambda_qk[l], row2(diff_norm_g[l]), q.reshape(B, S, -1), kb.reshape(B, S, -1),
                              vb.reshape(B, S, -1), lam_init)
        ypool, yconv = _mixers(upool, uconv, zero_pool, zero_conv, *mixer_w, ROW_TILE, S, 0)
        xp = _dense(xp, ypool, o.reshape(B * S, -1), yconv, w_out_bf, row2(norm_ffn_g[l]), w_gu_bf, w_d_bf, fg,
                    ROW_TILE, final)
        kp.append(k.reshape(B, S, N_HEADS, 2, HEAD_DIM))
        vp.append(v.reshape(B, S, N_HEADS, V_DIM))
        pp.append(upool.reshape(B, S, -1)[:, S - POOL_HIST:])
        cp.append(uconv.reshape(B, S, -1)[:, S - CONV_HIST:])

        upool, q, k, kb, v, vb, uconv = _in_proj(xs, row2(norm_mix_g[l]), w_in_bf, cos_s, sin_s, Bs * Ls)
        o = _sample_attention(lambda_qk[l], row2(diff_norm_g[l]), q.reshape(Bs, Ls, -1),
                              cache_k[l].reshape(Bs, past_len, -1), cache_v[l].reshape(Bs, past_len, -1),
                              kb.reshape(Bs, Ls, -1), vb.reshape(Bs, Ls, -1), lam_init)
        ypool, yconv = _mixers(upool, uconv, _pad_hist(state_pool[l]), _pad_hist(state_conv[l]), *mixer_w,
                               Ls, Ls, past_len)
        xs = _dense(xs, ypool, o.reshape(Bs * Ls, -1), yconv, w_out_bf, row2(norm_ffn_g[l]), w_gu_bf, w_d_bf, fg,
                    Bs * Ls, final)
        ks_.append(k.reshape(Bs, Ls, N_HEADS, 2, HEAD_DIM))
        vs_.append(v.reshape(Bs, Ls, N_HEADS, V_DIM))
        ps_.append(upool.reshape(Bs, Ls, -1)[:, Ls - POOL_HIST:])
        cs_.append(uconv.reshape(Bs, Ls, -1)[:, Ls - CONV_HIST:])

    return (xp.reshape(B, S, D_MODEL), xs.reshape(Bs, Ls, D_MODEL),
            jnp.stack(kp), jnp.stack(vp), jnp.stack(pp), jnp.stack(cp),
            jnp.stack(ks_), jnp.stack(vs_), jnp.stack(ps_), jnp.stack(cs_))
```

```python
import functools
import math

import jax
import jax.numpy as jnp
from jax import lax
from jax.experimental import pallas as pl
from jax.experimental.pallas import tpu as pltpu

D_MODEL = 1024
CHUNK = 64
POOL_WIDTH = D_MODEL // 4
POOL_WINDOWS = (2, 4, 8, 16)
POOL_GROUP_DIM = POOL_WIDTH // len(POOL_WINDOWS)
POOL_HIST = max(POOL_WINDOWS) - 1
ATTN_WIDTH = D_MODEL // 2
N_HEADS = 4
HEAD_DIM = ATTN_WIDTH // (2 * N_HEADS)
V_DIM = 2 * HEAD_DIM
ROPE_THETA = 10000.0
CONV_WIDTH = D_MODEL // 4
CONV_K = 31
CONV_HIST = CONV_K - 1
MIX_WIDTH = POOL_WIDTH + ATTN_WIDTH + CONV_WIDTH
IN_WIDTH = POOL_WIDTH + 3 * ATTN_WIDTH + 2 * CONV_WIDTH
D_FF = ((-(-8 * D_MODEL // 3) + 255) // 256) * 256
EPS = 1e-6

LANES = 128
HALO = 32
ROW_TILE = 256
ATTN_TILE = 256
HEADS_PER_STEP = 4
VMEM_LIMIT = 56 * 1024 * 1024

F32 = jnp.float32
BF16 = jnp.bfloat16


def _rms(x, g):
    return x * lax.rsqrt(jnp.mean(x * x, axis=-1, keepdims=True) + EPS) * g


def _const_spec(shape):
    return pl.BlockSpec(shape, lambda *_: (0,) * len(shape), pipeline_mode=pl.Buffered(1))


def _in_proj_kernel(x_ref, g_ref, w_ref, cos_ref, sin_ref,
                    upool_ref, q_ref, k_ref, kb_ref, v_ref, vb_ref, uconv_ref):
    h = _rms(x_ref[...], g_ref[...]).astype(BF16)
    z = jnp.dot(h, w_ref[...], preferred_element_type=F32)
    o1 = POOL_WIDTH
    o2 = o1 + ATTN_WIDTH
    o3 = o2 + ATTN_WIDTH
    o4 = o3 + ATTN_WIDTH
    o5 = o4 + CONV_WIDTH
    upool_ref[...] = z[:, :o1]

    cos = cos_ref[...]
    sin = sin_ref[...]
    lane = lax.broadcasted_iota(jnp.int32, cos.shape, 1)
    first_half = (lane & (HEAD_DIM // 2)) == 0

    def rope(t):
        up = pltpu.roll(t, LANES - HEAD_DIM // 2, axis=1)
        down = pltpu.roll(t, HEAD_DIM // 2, axis=1)
        return t * cos + jnp.where(first_half, up, down) * sin

    scale = HEAD_DIM ** -0.5
    for j in range(ATTN_WIDTH // LANES):
        sl = slice(j * LANES, (j + 1) * LANES)
        qj = rope(z[:, o1 + j * LANES:o1 + (j + 1) * LANES])
        kj = rope(z[:, o2 + j * LANES:o2 + (j + 1) * LANES])
        vj = z[:, o3 + j * LANES:o3 + (j + 1) * LANES]
        q_ref[:, sl] = (qj * scale).astype(BF16)
        k_ref[:, sl] = kj
        kb_ref[:, sl] = kj.astype(BF16)
        v_ref[:, sl] = vj
        vb_ref[:, sl] = vj.astype(BF16)
    uconv_ref[...] = z[:, o4:o5] * jax.nn.sigmoid(z[:, o5:])


def _in_proj(x, g, w_bf, cos_tab, sin_tab, tm):
    n = x.shape[0]
    tab_tiles = cos_tab.shape[0] // tm
    row = lambda w: pl.BlockSpec((tm, w), lambda i: (i, 0))
    tab = pl.BlockSpec((tm, LANES), lambda i: (i % tab_tiles, 0))
    outs = [
        jax.ShapeDtypeStruct((n, POOL_WIDTH), F32),
        jax.ShapeDtypeStruct((n, ATTN_WIDTH), BF16),
        jax.ShapeDtypeStruct((n, ATTN_WIDTH), F32),
        jax.ShapeDtypeStruct((n, ATTN_WIDTH), BF16),
        jax.ShapeDtypeStruct((n, ATTN_WIDTH), F32),
        jax.ShapeDtypeStruct((n, ATTN_WIDTH), BF16),
        jax.ShapeDtypeStruct((n, CONV_WIDTH), F32),
    ]
    return pl.pallas_call(
        _in_proj_kernel,
        grid=(n // tm,),
        in_specs=[row(D_MODEL), _const_spec((1, D_MODEL)), _const_spec((D_MODEL, IN_WIDTH)), tab, tab],
        out_specs=[row(POOL_WIDTH), row(ATTN_WIDTH), row(ATTN_WIDTH), row(ATTN_WIDTH),
                   row(ATTN_WIDTH), row(ATTN_WIDTH), row(CONV_WIDTH)],
        out_shape=outs,
        compiler_params=pltpu.CompilerParams(dimension_semantics=("arbitrary",),
                                             vmem_limit_bytes=VMEM_LIMIT),
        name="in_proj",
    )(x, g, w_bf, cos_tab, sin_tab)


def _lambda_scalar(lq, lam_init):
    a = jnp.sum(lq[0:1, :] * lq[1:2, :], axis=-1, keepdims=True)
    b = jnp.sum(lq[2:3, :] * lq[3:4, :], axis=-1, keepdims=True)
    return jnp.exp(a) - jnp.exp(b) + lam_init


def _stack_maps(q):
    lane = lax.broadcasted_iota(jnp.int32, q.shape, 1)
    zero = jnp.zeros_like(q)
    return jnp.concatenate([jnp.where(lane < HEAD_DIM, q, zero), jnp.where(lane >= HEAD_DIM, q, zero)], axis=0)


def _flash_step(carry, qq, kt, vt, mask):
    return _flash_update(carry, _flash_scores(qq, kt, mask), vt)


def _flash_scores(qq, kt, mask):
    s = lax.dot_general(kt, qq, (((1,), (1,)), ((), ())), preferred_element_type=F32)
    if mask is not None:
        s = jnp.where(mask, s, -1e30)
    return s


def _flash_update(carry, s, vt):
    m, l, acc = carry
    m_new = jnp.maximum(m, jnp.max(s, axis=0, keepdims=True))
    alpha = jnp.exp(m - m_new)
    p = jnp.exp(s - m_new)
    l = alpha * l + jnp.sum(p, axis=0, keepdims=True)
    pv = lax.dot_general(vt, p.astype(BF16), (((0,), (0,)), ((), ())), preferred_element_type=F32)
    return m_new, l, alpha * acc + pv


def _flash_init(cols):
    return (jnp.full((1, cols), -jnp.inf, F32), jnp.zeros((1, cols), F32), jnp.zeros((V_DIM, cols), F32))


def _attn_finish(carry, lam, g_col, lam_init, tq):
    _, l, acc = carry
    o = acc[:, :tq] / l[:, :tq] - lam * (acc[:, tq:] / l[:, tq:])
    o = o * lax.rsqrt(jnp.mean(o * o, axis=0, keepdims=True) + EPS) * g_col
    return (o * (1.0 - lam_init)).T


def _prompt_attn_kernel(lq_ref, g_ref, q_ref, k_ref, v_ref, o_ref, *, lam_init):
    t = ATTN_TILE
    qi = pl.program_id(2)
    heads = [slice(h * LANES, (h + 1) * LANES) for h in range(HEADS_PER_STEP)]
    qq = [_stack_maps(q_ref[:, c]) for c in heads]

    def step(j, carry, mask):
        r = pl.ds(pl.multiple_of(j * t, t), t)
        s = [_flash_scores(qq[h], k_ref[r, c], mask) for h, c in enumerate(heads)]
        return tuple(_flash_update(carry[h], s[h], v_ref[r, c]) for h, c in enumerate(heads))

    carry = lax.fori_loop(0, qi, lambda j, c: step(j, c, None), (_flash_init(2 * t),) * HEADS_PER_STEP)
    keys = lax.broadcasted_iota(jnp.int32, (t, 2 * t), 0)
    queries = lax.broadcasted_iota(jnp.int32, (t, 2 * t), 1)
    mask = (keys // CHUNK) <= ((queries % t) // CHUNK)
    carry = step(qi, carry, mask)
    lam = _lambda_scalar(lq_ref[...], lam_init)
    for h, c in enumerate(heads):
        o_ref[:, c] = _attn_finish(carry[h], lam, g_ref[...], lam_init, t).astype(BF16)


def _prompt_attention(lq, g, q, kb, vb, lam_init):
    b, s, _ = q.shape
    t = ATTN_TILE
    w = HEADS_PER_STEP * LANES
    return pl.pallas_call(
        functools.partial(_prompt_attn_kernel, lam_init=lam_init),
        grid=(b, N_HEADS // HEADS_PER_STEP, s // t),
        in_specs=[
            pl.BlockSpec((4, HEAD_DIM), lambda b_, h, i: (0, 0)),
            pl.BlockSpec((V_DIM, 1), lambda b_, h, i: (0, 0)),
            pl.BlockSpec((None, t, w), lambda b_, h, i: (b_, i, h)),
            pl.BlockSpec((None, s, w), lambda b_, h, i: (b_, 0, h)),
            pl.BlockSpec((None, s, w), lambda b_, h, i: (b_, 0, h)),
        ],
        out_specs=pl.BlockSpec((None, t, w), lambda b_, h, i: (b_, i, h)),
        out_shape=jax.ShapeDtypeStruct((b, s, ATTN_WIDTH), BF16),
        compiler_params=pltpu.CompilerParams(dimension_semantics=("arbitrary",) * 3,
                                             vmem_limit_bytes=VMEM_LIMIT),
        name="prompt_attention",
    )(lq, g, q, kb, vb)


def _sample_attn_kernel(lq_ref, g_ref, q_ref, ck_ref, cv_ref, k_ref, v_ref, o_ref, *, lam_init, past_len):
    t = ATTN_TILE
    tq = q_ref.shape[0]
    qq = _stack_maps(q_ref[...])

    def body(j, carry):
        r = pl.ds(pl.multiple_of(j * t, t), t)
        return _flash_step(carry, qq, ck_ref[r, :].astype(BF16), cv_ref[r, :].astype(BF16), None)

    carry = lax.fori_loop(0, past_len // t, body, _flash_init(2 * tq))
    carry = _flash_step(carry, qq, k_ref[...], v_ref[...], None)
    lam = _lambda_scalar(lq_ref[...], lam_init)
    o_ref[...] = _attn_finish(carry, lam, g_ref[...], lam_init, tq).astype(BF16)


def _sample_attention(lq, g, q, cache_k, cache_v, kb, vb, lam_init):
    b, ls, _ = q.shape
    past_len = cache_k.shape[1]
    assert past_len % ATTN_TILE == 0
    return pl.pallas_call(
        functools.partial(_sample_attn_kernel, lam_init=lam_init, past_len=past_len),
        grid=(b, N_HEADS),
        in_specs=[
            pl.BlockSpec((4, HEAD_DIM), lambda b_, h: (0, 0)),
            pl.BlockSpec((V_DIM, 1), lambda b_, h: (0, 0)),
            pl.BlockSpec((None, ls, LANES), lambda b_, h: (b_, 0, h)),
            pl.BlockSpec((None, past_len, LANES), lambda b_, h: (b_, 0, h)),
            pl.BlockSpec((None, past_len, LANES), lambda b_, h: (b_, 0, h)),
            pl.BlockSpec((None, ls, LANES), lambda b_, h: (b_, 0, h)),
            pl.BlockSpec((None, ls, LANES), lambda b_, h: (b_, 0, h)),
        ],
        out_specs=pl.BlockSpec((None, ls, LANES), lambda b_, h: (b_, 0, h)),
        out_shape=jax.ShapeDtypeStruct((b, ls, ATTN_WIDTH), BF16),
        compiler_params=pltpu.CompilerParams(dimension_semantics=("arbitrary",) * 2,
                                             vmem_limit_bytes=VMEM_LIMIT),
        name="sample_attention",
    )(lq, g, q, cache_k, cache_v, kb, vb)


def _mixers_kernel(up_ref, up_halo_ref, up_hist_ref, uc_ref, uc_halo_ref, uc_hist_ref,
                   pool_w_ref, pool_scale_ref, dw_ref, dw_b_ref, ln_g_ref, ln_b_ref, pw_ref,
                   ypool_ref, yconv_ref, ext_p, ext_c, *, tiles_per_seq, pos_base):
    tm = up_ref.shape[0]
    i = pl.program_id(0)
    tile_in_seq = i % tiles_per_seq
    first = tile_in_seq == 0

    ext_p[0:HALO, :] = jnp.where(first, up_hist_ref[...], up_halo_ref[...])
    ext_p[HALO:, :] = up_ref[...]
    ext_c[0:HALO, :] = jnp.where(first, uc_hist_ref[...], uc_halo_ref[...])
    ext_c[HALO:, :] = uc_ref[...]

    back = lambda ref, k: ref[pl.ds(HALO - k, tm), :]
    u = back(ext_p, 0)
    sums = {}
    acc = u
    for k in range(1, max(POOL_WINDOWS)):
        acc = acc + back(ext_p, k)
        if k + 1 in POOL_WINDOWS:
            sums[k + 1] = acc
    lane = lax.broadcasted_iota(jnp.int32, (tm, POOL_WIDTH), 1)
    group = lane // POOL_GROUP_DIM
    win = sums[POOL_WINDOWS[-1]]
    width = jnp.full((tm, POOL_WIDTH), POOL_WINDOWS[-1], jnp.int32)
    for gi in range(len(POOL_WINDOWS) - 2, -1, -1):
        win = jnp.where(group == gi, sums[POOL_WINDOWS[gi]], win)
        width = jnp.where(group == gi, POOL_WINDOWS[gi], width)
    pos = pos_base + tile_in_seq * tm + lax.broadcasted_iota(jnp.int32, (tm, POOL_WIDTH), 0)
    count = jnp.minimum(pos + 1, width).astype(F32)
    d = (win / count - u).astype(BF16)
    ypool = jnp.dot(d, pool_w_ref[...], preferred_element_type=F32) * pool_scale_ref[...]
    ypool_ref[...] = ypool.astype(BF16)

    y = jnp.zeros((tm, CONV_WIDTH), F32)
    for k in range(CONV_K):
        y = y + ext_c[pl.ds(HALO - CONV_HIST + k, tm), :] * dw_ref[k:k + 1, :]
    y = y + dw_b_ref[...]
    mu = jnp.mean(y, axis=-1, keepdims=True)
    yc = y - mu
    yn = yc * lax.rsqrt(jnp.mean(yc * yc, axis=-1, keepdims=True) + EPS) * ln_g_ref[...] + ln_b_ref[...]
    act = (yn * jax.nn.sigmoid(yn)).astype(BF16)
    yconv_ref[...] = jnp.dot(act, pw_ref[...], preferred_element_type=F32).astype(BF16)


def _mixers(upool, uconv, hist_pool, hist_conv, pool_w_bd, pool_scale, dw, dw_b, ln_g, ln_b, pw_bf,
            tm, seq_len, pos_base):
    n = upool.shape[0]
    tiles_per_seq = seq_len // tm
    ratio = tm // HALO
    cur = pl.BlockSpec((tm, POOL_WIDTH), lambda i: (i, 0))
    halo = pl.BlockSpec((HALO, POOL_WIDTH), lambda i: (jnp.maximum(i * ratio - 1, 0), 0))
    hist = pl.BlockSpec((None, HALO, POOL_WIDTH), lambda i: (i // tiles_per_seq, 0, 0))
    return pl.pallas_call(
        functools.partial(_mixers_kernel, tiles_per_seq=tiles_per_seq, pos_base=pos_base),
        grid=(n // tm,),
        in_specs=[cur, halo, hist, cur, halo, hist,
                  _const_spec((POOL_WIDTH, POOL_WIDTH)), _const_spec((1, POOL_WIDTH)),
                  _const_spec((CONV_K, CONV_WIDTH)), _const_spec((1, CONV_WIDTH)),
                  _const_spec((1, CONV_WIDTH)), _const_spec((1, CONV_WIDTH)),
                  _const_spec((CONV_WIDTH, CONV_WIDTH))],
        out_specs=[cur, cur],
        out_shape=[jax.ShapeDtypeStruct((n, POOL_WIDTH), BF16), jax.ShapeDtypeStruct((n, CONV_WIDTH), BF16)],
        scratch_shapes=[pltpu.VMEM((HALO + tm, POOL_WIDTH), F32), pltpu.VMEM((HALO + tm, CONV_WIDTH), F32)],
        compiler_params=pltpu.CompilerParams(dimension_semantics=("arbitrary",),
                                             vmem_limit_bytes=VMEM_LIMIT),
        name="mixers",
    )(upool, upool, hist_pool, uconv, uconv, hist_conv, pool_w_bd, pool_scale, dw, dw_b, ln_g, ln_b, pw_bf)


def _dense_kernel(x_ref, yp_ref, o_ref, yc_ref, wout_ref, g_ref, wgu_ref, wd_ref, fg_ref, out_ref, *, final):
    o1 = POOL_WIDTH
    o2 = o1 + ATTN_WIDTH
    mix = (jnp.dot(yp_ref[...], wout_ref[0:o1, :], preferred_element_type=F32)
           + jnp.dot(o_ref[...], wout_ref[o1:o2, :], preferred_element_type=F32)
           + jnp.dot(yc_ref[...], wout_ref[o2:, :], preferred_element_type=F32))
    x1 = x_ref[...] + mix
    h = _rms(x1, g_ref[...]).astype(BF16)
    gu = jnp.dot(h, wgu_ref[...], preferred_element_type=F32)
    gate = gu[:, :D_FF]
    a = (gate * jax.nn.sigmoid(gate) * gu[:, D_FF:]).astype(BF16)
    x2 = x1 + jnp.dot(a, wd_ref[...], preferred_element_type=F32)
    out_ref[...] = _rms(x2, fg_ref[...]) if final else x2


def _dense(x, ypool, o, yconv, wout_bf, g, wgu_bf, wd_bf, fg, tm, final):
    n = x.shape[0]
    row = lambda w: pl.BlockSpec((tm, w), lambda i: (i, 0))
    return pl.pallas_call(
        functools.partial(_dense_kernel, final=final),
        grid=(n // tm,),
        in_specs=[row(D_MODEL), row(POOL_WIDTH), row(ATTN_WIDTH), row(CONV_WIDTH),
                  _const_spec((MIX_WIDTH, D_MODEL)), _const_spec((1, D_MODEL)),
                  _const_spec((D_MODEL, 2 * D_FF)), _const_spec((D_FF, D_MODEL)), _const_spec((1, D_MODEL))],
        out_specs=row(D_MODEL),
        out_shape=jax.ShapeDtypeStruct((n, D_MODEL), F32),
        compiler_params=pltpu.CompilerParams(dimension_semantics=("arbitrary",),
                                             vmem_limit_bytes=VMEM_LIMIT),
        name="dense",
    )(x, ypool, o, yconv, wout_bf, g, wgu_bf, wd_bf, fg)


def _rope_tables(pos):
    half = HEAD_DIM // 2
    inv = ROPE_THETA ** (-jnp.arange(half, dtype=F32) / half)
    ang = pos.astype(F32)[:, None] * inv[None, :]
    reps = LANES // half
    cos = jnp.tile(jnp.cos(ang), (1, reps))
    sign = jnp.tile(jnp.concatenate([-jnp.ones((half,), F32), jnp.ones((half,), F32)]), LANES // HEAD_DIM)
    sin = jnp.tile(jnp.sin(ang), (1, reps)) * sign[None, :]
    return cos, sin


def _block_diag(pool_w):
    g, d, _ = pool_w.shape
    out = jnp.zeros((g * d, g * d), pool_w.dtype)
    for i in range(g):
        out = out.at[i * d:(i + 1) * d, i * d:(i + 1) * d].set(pool_w[i])
    return out


def _pad_hist(h):
    return jnp.pad(h, ((0, 0), (HALO - h.shape[1], 0), (0, 0)))


def kernel(x_prompt, x_sample, cache_k, cache_v, state_pool, state_conv, norm_mix_g, w_in, pool_w, pool_scale,
           lambda_qk, diff_norm_g, conv_dw, conv_dw_b, conv_ln_g, conv_ln_b, conv_pw, w_out, norm_ffn_g,
           w_gate_up, w_down, final_norm_g):
    B, S, _ = x_prompt.shape
    Bs, Ls, _ = x_sample.shape
    depth = w_in.shape[0]
    past_len = cache_k.shape[2]
    assert S % ATTN_TILE == 0 and S % ROW_TILE == 0 and (Bs * Ls) % 16 == 0 and Ls % 16 == 0 and Ls >= CONV_HIST

    cos_p, sin_p = _rope_tables(jnp.arange(S, dtype=jnp.int32))
    cos_s, sin_s = _rope_tables(past_len + jnp.arange(Ls, dtype=jnp.int32))
    cos_s, sin_s = jnp.tile(cos_s, (Bs, 1)), jnp.tile(sin_s, (Bs, 1))
    zero_pool = jnp.zeros((B, HALO, POOL_WIDTH), F32)
    zero_conv = jnp.zeros((B, HALO, CONV_WIDTH), F32)

    xp = x_prompt.reshape(B * S, D_MODEL)
    xs = x_sample.reshape(Bs * Ls, D_MODEL)
    row2 = lambda a: a.reshape(1, -1)
    fg = row2(final_norm_g)
    kp, vp, pp, cp = [], [], [], []
    ks_, vs_, ps_, cs_ = [], [], [], []
    for l in range(depth):
        lam_init = 0.8 - 0.6 * math.exp(-0.3 * l)
        w_in_bf = w_in[l].astype(BF16)
        w_out_bf = w_out[l].astype(BF16)
        w_gu_bf = w_gate_up[l].astype(BF16)
        w_d_bf = w_down[l].astype(BF16)
        pw_bf = conv_pw[l].astype(BF16)
        pool_bd = _block_diag(pool_w[l]).astype(BF16)
        mixer_w = (pool_bd, row2(pool_scale[l]), conv_dw[l], row2(conv_dw_b[l]), row2(conv_ln_g[l]),
                   row2(conv_ln_b[l]), pw_bf)
        final = l == depth - 1

        upool, q, k, kb, v, vb, uconv = _in_proj(xp, row2(norm_mix_g[l]), w_in_bf, cos_p, sin_p, ROW_TILE)
        o = _prompt_attention(lambda_qk[l], diff_norm_g[l].reshape(-1, 1), q.reshape(B, S, -1), kb.reshape(B, S, -1),
                              vb.reshape(B, S, -1), lam_init)
        ypool, yconv = _mixers(upool, uconv, zero_pool, zero_conv, *mixer_w, ROW_TILE, S, 0)
        xp = _dense(xp, ypool, o.reshape(B * S, -1), yconv, w_out_bf, row2(norm_ffn_g[l]), w_gu_bf, w_d_bf, fg,
                    ROW_TILE, final)
        kp.append(k.reshape(B, S, N_HEADS, 2, HEAD_DIM))
        vp.append(v.reshape(B, S, N_HEADS, V_DIM))
        pp.append(upool.reshape(B, S, -1)[:, S - POOL_HIST:])
        cp.append(uconv.reshape(B, S, -1)[:, S - CONV_HIST:])

        upool, q, k, kb, v, vb, uconv = _in_proj(xs, row2(norm_mix_g[l]), w_in_bf, cos_s, sin_s, Bs * Ls)
        o = _sample_attention(lambda_qk[l], diff_norm_g[l].reshape(-1, 1), q.reshape(Bs, Ls, -1),
                              cache_k[l].reshape(Bs, past_len, -1), cache_v[l].reshape(Bs, past_len, -1),
                              kb.reshape(Bs, Ls, -1), vb.reshape(Bs, Ls, -1), lam_init)
        ypool, yconv = _mixers(upool, uconv, _pad_hist(state_pool[l]), _pad_hist(state_conv[l]), *mixer_w,
                               Ls, Ls, past_len)
        xs = _dense(xs, ypool, o.reshape(Bs * Ls, -1), yconv, w_out_bf, row2(norm_ffn_g[l]), w_gu_bf, w_d_bf, fg,
                    Bs * Ls, final)
        ks_.append(k.reshape(Bs, Ls, N_HEADS, 2, HEAD_DIM))
        vs_.append(v.reshape(Bs, Ls, N_HEADS, V_DIM))
        ps_.append(upool.reshape(Bs, Ls, -1)[:, Ls - POOL_HIST:])
        cs_.append(uconv.reshape(Bs, Ls, -1)[:, Ls - CONV_HIST:])

    return (xp.reshape(B, S, D_MODEL), xs.reshape(Bs, Ls, D_MODEL),
            jnp.stack(kp), jnp.stack(vp), jnp.stack(pp), jnp.stack(cp),
            jnp.stack(ks_), jnp.stack(vs_), jnp.stack(ps_), jnp.stack(cs_))
```

```python
import functools
import math

import jax
import jax.numpy as jnp
from jax import lax
from jax.experimental import pallas as pl
from jax.experimental.pallas import tpu as pltpu

D_MODEL = 1024
CHUNK = 64
POOL_WIDTH = D_MODEL // 4
POOL_WINDOWS = (2, 4, 8, 16)
POOL_GROUP_DIM = POOL_WIDTH // len(POOL_WINDOWS)
POOL_HIST = max(POOL_WINDOWS) - 1
ATTN_WIDTH = D_MODEL // 2
N_HEADS = 4
HEAD_DIM = ATTN_WIDTH // (2 * N_HEADS)
V_DIM = 2 * HEAD_DIM
ROPE_THETA = 10000.0
CONV_WIDTH = D_MODEL // 4
CONV_K = 31
CONV_HIST = CONV_K - 1
MIX_WIDTH = POOL_WIDTH + ATTN_WIDTH + CONV_WIDTH
IN_WIDTH = POOL_WIDTH + 3 * ATTN_WIDTH + 2 * CONV_WIDTH
D_FF = ((-(-8 * D_MODEL // 3) + 255) // 256) * 256
EPS = 1e-6

LANES = 128
HALO = 32
ROW_TILE = 256
ATTN_TILE = 256
HEADS_PER_STEP = 4
VMEM_LIMIT = 56 * 1024 * 1024

F32 = jnp.float32
BF16 = jnp.bfloat16


def _rms(x, g):
    return x * lax.rsqrt(jnp.mean(x * x, axis=-1, keepdims=True) + EPS) * g


def _const_spec(shape):
    return pl.BlockSpec(shape, lambda *_: (0,) * len(shape), pipeline_mode=pl.Buffered(1))


def _in_proj_kernel(*refs, cache_layout, n_unused):
    x_ref, g_ref, w_ref, cos_ref, sin_ref = refs[:5]
    upool_ref, q_ref, k_ref, kb_ref, v_ref, vb_ref, uconv_ref = refs[5 + n_unused:]
    tm = x_ref.shape[0]
    h = _rms(x_ref[...], g_ref[...]).astype(BF16)
    z = jnp.dot(h, w_ref[...], preferred_element_type=F32)
    o1 = POOL_WIDTH
    o2 = o1 + ATTN_WIDTH
    o3 = o2 + ATTN_WIDTH
    o4 = o3 + ATTN_WIDTH
    o5 = o4 + CONV_WIDTH
    upool_ref[...] = z[:, :o1]

    cos = cos_ref[...]
    sin = sin_ref[...]
    lane = lax.broadcasted_iota(jnp.int32, cos.shape, 1)
    first_half = (lane & (HEAD_DIM // 2)) == 0

    def rope(t):
        up = pltpu.roll(t, LANES - HEAD_DIM // 2, axis=1)
        down = pltpu.roll(t, HEAD_DIM // 2, axis=1)
        return t * cos + jnp.where(first_half, up, down) * sin

    scale = HEAD_DIM ** -0.5
    for j in range(ATTN_WIDTH // LANES):
        sl = slice(j * LANES, (j + 1) * LANES)
        qj = rope(z[:, o1 + j * LANES:o1 + (j + 1) * LANES])
        kj = rope(z[:, o2 + j * LANES:o2 + (j + 1) * LANES])
        vj = z[:, o3 + j * LANES:o3 + (j + 1) * LANES]
        q_ref[:, sl] = (qj * scale).astype(BF16)
        kb_ref[:, sl] = kj.astype(BF16)
        vb_ref[:, sl] = vj.astype(BF16)
        if cache_layout:
            k_ref[sl, :] = kj.T
            v_ref[pl.ds(j, tm, stride=N_HEADS), :] = vj
        else:
            k_ref[:, sl] = kj
            v_ref[:, sl] = vj
    uconv_ref[...] = z[:, o4:o5] * jax.nn.sigmoid(z[:, o5:])


def _in_proj(x, g, w_bf, cos_tab, sin_tab, tm, kv_stack=None, layer=0, seq_len=None):
    n = x.shape[0]
    tab_tiles = cos_tab.shape[0] // tm
    row = lambda w: pl.BlockSpec((tm, w), lambda i: (i, 0))
    tab = pl.BlockSpec((tm, LANES), lambda i: (i % tab_tiles, 0))
    flat = lambda dt: jax.ShapeDtypeStruct((n, ATTN_WIDTH), dt)
    cache_layout = seq_len is not None
    extra_in, extra_specs, aliases = [], [], {}
    if cache_layout:
        tps = seq_len // tm
        batch = n // seq_len
        if isinstance(kv_stack, int):
            k_shape = jax.ShapeDtypeStruct((kv_stack, batch, ATTN_WIDTH, seq_len), F32)
            v_shape = jax.ShapeDtypeStruct((kv_stack, batch, N_HEADS * seq_len, V_DIM), F32)
        else:
            k_shape, v_shape = (jax.ShapeDtypeStruct(a.shape, a.dtype) for a in kv_stack)
            extra_in = list(kv_stack)
            extra_specs = [pl.BlockSpec(memory_space=pl.ANY)] * 2
            aliases = {5: 2, 6: 4}
        k_spec = pl.BlockSpec((None, None, ATTN_WIDTH, tm), lambda i: (layer, i // tps, 0, i % tps))
        v_spec = pl.BlockSpec((None, None, N_HEADS * tm, V_DIM), lambda i: (layer, i // tps, i % tps, 0))
    else:
        k_shape, v_shape, k_spec, v_spec = flat(F32), flat(F32), row(ATTN_WIDTH), row(ATTN_WIDTH)
    outs = [jax.ShapeDtypeStruct((n, POOL_WIDTH), F32), flat(BF16), k_shape, flat(BF16), v_shape, flat(BF16),
            jax.ShapeDtypeStruct((n, CONV_WIDTH), F32)]
    return pl.pallas_call(
        functools.partial(_in_proj_kernel, cache_layout=cache_layout, n_unused=len(extra_in)),
        grid=(n // tm,),
        in_specs=[row(D_MODEL), _const_spec((1, D_MODEL)), _const_spec((D_MODEL, IN_WIDTH)), tab, tab] + extra_specs,
        out_specs=[row(POOL_WIDTH), row(ATTN_WIDTH), k_spec, row(ATTN_WIDTH), v_spec, row(ATTN_WIDTH),
                   row(CONV_WIDTH)],
        out_shape=outs,
        input_output_aliases=aliases,
        compiler_params=pltpu.CompilerParams(dimension_semantics=("arbitrary",),
                                             vmem_limit_bytes=VMEM_LIMIT),
        name="in_proj",
    )(x, g, w_bf, cos_tab, sin_tab, *extra_in)


def _lambda_scalar(lq, lam_init):
    a = jnp.sum(lq[0:1, :] * lq[1:2, :], axis=-1, keepdims=True)
    b = jnp.sum(lq[2:3, :] * lq[3:4, :], axis=-1, keepdims=True)
    return jnp.exp(a) - jnp.exp(b) + lam_init


def _stack_maps(q):
    lane = lax.broadcasted_iota(jnp.int32, q.shape, 1)
    zero = jnp.zeros_like(q)
    return jnp.concatenate([jnp.where(lane < HEAD_DIM, q, zero), jnp.where(lane >= HEAD_DIM, q, zero)], axis=0)


def _flash_step(carry, qq, kt, vt, mask):
    return _flash_update(carry, _flash_scores(qq, kt, mask), vt)


def _flash_scores(qq, kt, mask):
    s = lax.dot_general(kt, qq, (((1,), (1,)), ((), ())), preferred_element_type=F32)
    if mask is not None:
        s = jnp.where(mask, s, -1e30)
    return s


def _flash_update(carry, s, vt):
    m, l, acc = carry
    m_new = jnp.maximum(m, jnp.max(s, axis=0, keepdims=True))
    alpha = jnp.exp(m - m_new)
    p = jnp.exp(s - m_new)
    l = alpha * l + jnp.sum(p, axis=0, keepdims=True)
    pv = lax.dot_general(vt, p.astype(BF16), (((0,), (0,)), ((), ())), preferred_element_type=F32)
    return m_new, l, alpha * acc + pv


def _flash_init(cols):
    return (jnp.full((1, cols), -jnp.inf, F32), jnp.zeros((1, cols), F32), jnp.zeros((V_DIM, cols), F32))


def _attn_finish(carry, lam, g_col, lam_init, tq):
    _, l, acc = carry
    o = acc[:, :tq] / l[:, :tq] - lam * (acc[:, tq:] / l[:, tq:])
    o = o * lax.rsqrt(jnp.mean(o * o, axis=0, keepdims=True) + EPS) * g_col
    return (o * (1.0 - lam_init)).T


def _prompt_attn_kernel(lq_ref, g_ref, q_ref, k_ref, v_ref, o_ref, *, lam_init):
    t = ATTN_TILE
    qi = pl.program_id(2)
    heads = [slice(h * LANES, (h + 1) * LANES) for h in range(HEADS_PER_STEP)]
    qq = [_stack_maps(q_ref[:, c]) for c in heads]

    def step(j, carry, mask):
        r = pl.ds(pl.multiple_of(j * t, t), t)
        s = [_flash_scores(qq[h], k_ref[r, c], mask) for h, c in enumerate(heads)]
        return tuple(_flash_update(carry[h], s[h], v_ref[r, c]) for h, c in enumerate(heads))

    carry = lax.fori_loop(0, qi, lambda j, c: step(j, c, None), (_flash_init(2 * t),) * HEADS_PER_STEP)
    keys = lax.broadcasted_iota(jnp.int32, (t, 2 * t), 0)
    queries = lax.broadcasted_iota(jnp.int32, (t, 2 * t), 1)
    mask = (keys // CHUNK) <= ((queries % t) // CHUNK)
    carry = step(qi, carry, mask)
    lam = _lambda_scalar(lq_ref[...], lam_init)
    for h, c in enumerate(heads):
        o_ref[:, c] = _attn_finish(carry[h], lam, g_ref[...], lam_init, t).astype(BF16)


def _prompt_attention(lq, g, q, kb, vb, lam_init):
    b, s, _ = q.shape
    t = ATTN_TILE
    w = HEADS_PER_STEP * LANES
    return pl.pallas_call(
        functools.partial(_prompt_attn_kernel, lam_init=lam_init),
        grid=(b, N_HEADS // HEADS_PER_STEP, s // t),
        in_specs=[
            pl.BlockSpec((4, HEAD_DIM), lambda b_, h, i: (0, 0)),
            pl.BlockSpec((V_DIM, 1), lambda b_, h, i: (0, 0)),
            pl.BlockSpec((None, t, w), lambda b_, h, i: (b_, i, h)),
            pl.BlockSpec((None, s, w), lambda b_, h, i: (b_, 0, h)),
            pl.BlockSpec((None, s, w), lambda b_, h, i: (b_, 0, h)),
        ],
        out_specs=pl.BlockSpec((None, t, w), lambda b_, h, i: (b_, i, h)),
        out_shape=jax.ShapeDtypeStruct((b, s, ATTN_WIDTH), BF16),
        compiler_params=pltpu.CompilerParams(dimension_semantics=("arbitrary",) * 3,
                                             vmem_limit_bytes=VMEM_LIMIT),
        name="prompt_attention",
    )(lq, g, q, kb, vb)


def _row_flash_step(carry, qq, kt, vt, *, keys_on_rows):
    m, l, acc = carry
    dims = (((1,), (1,)), ((), ())) if keys_on_rows else (((1,), (0,)), ((), ()))
    s = lax.dot_general(qq, kt, dims, preferred_element_type=F32)
    m_new = jnp.maximum(m, jnp.max(s, axis=-1, keepdims=True))
    alpha = jnp.exp(m - m_new)
    p = jnp.exp(s - m_new)
    l = alpha * l + jnp.sum(p, axis=-1, keepdims=True)
    acc = alpha * acc + jnp.dot(p.astype(BF16), vt, preferred_element_type=F32)
    return m_new, l, acc


def _sample_attn_kernel(lq_ref, g_ref, q_ref, ck_ref, cv_ref, k_ref, v_ref, o_ref, *, lam_init, past_len):
    t = ATTN_TILE
    tq = q_ref.shape[0]
    h = pl.program_id(1)
    qq = _stack_maps(q_ref[...])

    def body(j, carry):
        kt = ck_ref[:, pl.ds(pl.multiple_of(j * t, t), t)].astype(BF16)
        vt = cv_ref[pl.ds(j * (t * N_HEADS) + h, t, stride=N_HEADS), :].astype(BF16)
        return _row_flash_step(carry, qq, kt, vt, keys_on_rows=False)

    init = (jnp.full((2 * tq, 1), -jnp.inf, F32), jnp.zeros((2 * tq, 1), F32), jnp.zeros((2 * tq, V_DIM), F32))
    carry = lax.fori_loop(0, past_len // t, body, init)
    _, l, acc = _row_flash_step(carry, qq, k_ref[...], v_ref[...], keys_on_rows=True)
    lam = _lambda_scalar(lq_ref[...], lam_init)
    o = acc[:tq] / l[:tq] - lam * (acc[tq:] / l[tq:])
    o_ref[...] = (_rms(o, g_ref[...]) * (1.0 - lam_init)).astype(BF16)


def _sample_attention(lq, g, q, cache_kt, cache_v4, kb, vb, lam_init, layer):
    b, ls, _ = q.shape
    past_len = cache_kt.shape[-1]
    assert past_len % ATTN_TILE == 0
    return pl.pallas_call(
        functools.partial(_sample_attn_kernel, lam_init=lam_init, past_len=past_len),
        grid=(b, N_HEADS),
        in_specs=[
            pl.BlockSpec((4, HEAD_DIM), lambda b_, h: (0, 0)),
            pl.BlockSpec((1, V_DIM), lambda b_, h: (0, 0)),
            pl.BlockSpec((None, ls, LANES), lambda b_, h: (b_, 0, h)),
            pl.BlockSpec((None, None, LANES, past_len), lambda b_, h: (layer, b_, h, 0)),
            pl.BlockSpec((None, None, N_HEADS * past_len, V_DIM), lambda b_, h: (layer, b_, 0, 0)),
            pl.BlockSpec((None, ls, LANES), lambda b_, h: (b_, 0, h)),
            pl.BlockSpec((None, ls, LANES), lambda b_, h: (b_, 0, h)),
        ],
        out_specs=pl.BlockSpec((None, ls, LANES), lambda b_, h: (b_, 0, h)),
        out_shape=jax.ShapeDtypeStruct((b, ls, ATTN_WIDTH), BF16),
        compiler_params=pltpu.CompilerParams(dimension_semantics=("arbitrary",) * 2,
                                             vmem_limit_bytes=VMEM_LIMIT),
        name="sample_attention",
    )(lq, g, q, cache_kt, cache_v4, kb, vb)


def _mixers_kernel(up_ref, up_halo_ref, up_hist_ref, uc_ref, uc_halo_ref, uc_hist_ref,
                   pool_w_ref, pool_scale_ref, dw_ref, dw_b_ref, ln_g_ref, ln_b_ref, pw_ref,
                   ypool_ref, yconv_ref, ext_p, ext_c, *, tiles_per_seq, pos_base):
    tm = up_ref.shape[0]
    i = pl.program_id(0)
    tile_in_seq = i % tiles_per_seq
    first = tile_in_seq == 0

    ext_p[0:HALO, :] = jnp.where(first, up_hist_ref[...], up_halo_ref[...])
    ext_p[HALO:, :] = up_ref[...]
    ext_c[0:HALO, :] = jnp.where(first, uc_hist_ref[...], uc_halo_ref[...])
    ext_c[HALO:, :] = uc_ref[...]

    back = lambda ref, k: ref[pl.ds(HALO - k, tm), :]
    u = back(ext_p, 0)
    sums = {}
    acc = u
    for k in range(1, max(POOL_WINDOWS)):
        acc = acc + back(ext_p, k)
        if k + 1 in POOL_WINDOWS:
            sums[k + 1] = acc
    lane = lax.broadcasted_iota(jnp.int32, (tm, POOL_WIDTH), 1)
    group = lane // POOL_GROUP_DIM
    win = sums[POOL_WINDOWS[-1]]
    width = jnp.full((tm, POOL_WIDTH), POOL_WINDOWS[-1], jnp.int32)
    for gi in range(len(POOL_WINDOWS) - 2, -1, -1):
        win = jnp.where(group == gi, sums[POOL_WINDOWS[gi]], win)
        width = jnp.where(group == gi, POOL_WINDOWS[gi], width)
    pos = pos_base + tile_in_seq * tm + lax.broadcasted_iota(jnp.int32, (tm, POOL_WIDTH), 0)
    count = jnp.minimum(pos + 1, width).astype(F32)
    d = (win / count - u).astype(BF16)
    ypool = jnp.dot(d, pool_w_ref[...], preferred_element_type=F32) * pool_scale_ref[...]
    ypool_ref[...] = ypool.astype(BF16)

    y = jnp.zeros((tm, CONV_WIDTH), F32)
    for k in range(CONV_K):
        y = y + ext_c[pl.ds(HALO - CONV_HIST + k, tm), :] * dw_ref[k:k + 1, :]
    y = y + dw_b_ref[...]
    mu = jnp.mean(y, axis=-1, keepdims=True)
    yc = y - mu
    yn = yc * lax.rsqrt(jnp.mean(yc * yc, axis=-1, keepdims=True) + EPS) * ln_g_ref[...] + ln_b_ref[...]
    act = (yn * jax.nn.sigmoid(yn)).astype(BF16)
    yconv_ref[...] = jnp.dot(act, pw_ref[...], preferred_element_type=F32).astype(BF16)


def _mixers(upool, uconv, hist_pool, hist_conv, pool_w_bd, pool_scale, dw, dw_b, ln_g, ln_b, pw_bf,
            tm, seq_len, pos_base):
    n = upool.shape[0]
    tiles_per_seq = seq_len // tm
    ratio = tm // HALO
    cur = pl.BlockSpec((tm, POOL_WIDTH), lambda i: (i, 0))
    halo = pl.BlockSpec((HALO, POOL_WIDTH), lambda i: (jnp.maximum(i * ratio - 1, 0), 0))
    hist = pl.BlockSpec((None, HALO, POOL_WIDTH), lambda i: (i // tiles_per_seq, 0, 0))
    return pl.pallas_call(
        functools.partial(_mixers_kernel, tiles_per_seq=tiles_per_seq, pos_base=pos_base),
        grid=(n // tm,),
        in_specs=[cur, halo, hist, cur, halo, hist,
                  _const_spec((POOL_WIDTH, POOL_WIDTH)), _const_spec((1, POOL_WIDTH)),
                  _const_spec((CONV_K, CONV_WIDTH)), _const_spec((1, CONV_WIDTH)),
                  _const_spec((1, CONV_WIDTH)), _const_spec((1, CONV_WIDTH)),
                  _const_spec((CONV_WIDTH, CONV_WIDTH))],
        out_specs=[cur, cur],
        out_shape=[jax.ShapeDtypeStruct((n, POOL_WIDTH), BF16), jax.ShapeDtypeStruct((n, CONV_WIDTH), BF16)],
        scratch_shapes=[pltpu.VMEM((HALO + tm, POOL_WIDTH), F32), pltpu.VMEM((HALO + tm, CONV_WIDTH), F32)],
        compiler_params=pltpu.CompilerParams(dimension_semantics=("arbitrary",),
                                             vmem_limit_bytes=VMEM_LIMIT),
        name="mixers",
    )(upool, upool, hist_pool, uconv, uconv, hist_conv, pool_w_bd, pool_scale, dw, dw_b, ln_g, ln_b, pw_bf)


def _dense_kernel(x_ref, yp_ref, o_ref, yc_ref, wout_ref, g_ref, wgu_ref, wd_ref, fg_ref, out_ref, *, final):
    o1 = POOL_WIDTH
    o2 = o1 + ATTN_WIDTH
    mix = (jnp.dot(yp_ref[...], wout_ref[0:o1, :], preferred_element_type=F32)
           + jnp.dot(o_ref[...], wout_ref[o1:o2, :], preferred_element_type=F32)
           + jnp.dot(yc_ref[...], wout_ref[o2:, :], preferred_element_type=F32))
    x1 = x_ref[...] + mix
    h = _rms(x1, g_ref[...]).astype(BF16)
    gu = jnp.dot(h, wgu_ref[...], preferred_element_type=F32)
    gate = gu[:, :D_FF]
    a = (gate * jax.nn.sigmoid(gate) * gu[:, D_FF:]).astype(BF16)
    x2 = x1 + jnp.dot(a, wd_ref[...], preferred_element_type=F32)
    out_ref[...] = _rms(x2, fg_ref[...]) if final else x2


def _dense(x, ypool, o, yconv, wout_bf, g, wgu_bf, wd_bf, fg, tm, final):
    n = x.shape[0]
    row = lambda w: pl.BlockSpec((tm, w), lambda i: (i, 0))
    return pl.pallas_call(
        functools.partial(_dense_kernel, final=final),
        grid=(n // tm,),
        in_specs=[row(D_MODEL), row(POOL_WIDTH), row(ATTN_WIDTH), row(CONV_WIDTH),
                  _const_spec((MIX_WIDTH, D_MODEL)), _const_spec((1, D_MODEL)),
                  _const_spec((D_MODEL, 2 * D_FF)), _const_spec((D_FF, D_MODEL)), _const_spec((1, D_MODEL))],
        out_specs=row(D_MODEL),
        out_shape=jax.ShapeDtypeStruct((n, D_MODEL), F32),
        compiler_params=pltpu.CompilerParams(dimension_semantics=("arbitrary",),
                                             vmem_limit_bytes=VMEM_LIMIT),
        name="dense",
    )(x, ypool, o, yconv, wout_bf, g, wgu_bf, wd_bf, fg)


def _rope_tables(pos):
    half = HEAD_DIM // 2
    inv = ROPE_THETA ** (-jnp.arange(half, dtype=F32) / half)
    ang = pos.astype(F32)[:, None] * inv[None, :]
    reps = LANES // half
    cos = jnp.tile(jnp.cos(ang), (1, reps))
    sign = jnp.tile(jnp.concatenate([-jnp.ones((half,), F32), jnp.ones((half,), F32)]), LANES // HEAD_DIM)
    sin = jnp.tile(jnp.sin(ang), (1, reps)) * sign[None, :]
    return cos, sin


def _block_diag(pool_w):
    g, d, _ = pool_w.shape
    out = jnp.zeros((g * d, g * d), pool_w.dtype)
    for i in range(g):
        out = out.at[i * d:(i + 1) * d, i * d:(i + 1) * d].set(pool_w[i])
    return out


def _pad_hist(h):
    return jnp.pad(h, ((0, 0), (HALO - h.shape[1], 0), (0, 0)))


def kernel(x_prompt, x_sample, cache_k, cache_v, state_pool, state_conv, norm_mix_g, w_in, pool_w, pool_scale,
           lambda_qk, diff_norm_g, conv_dw, conv_dw_b, conv_ln_g, conv_ln_b, conv_pw, w_out, norm_ffn_g,
           w_gate_up, w_down, final_norm_g):
    B, S, _ = x_prompt.shape
    Bs, Ls, _ = x_sample.shape
    depth = w_in.shape[0]
    past_len = cache_k.shape[2]
    assert S % ATTN_TILE == 0 and S % ROW_TILE == 0 and (Bs * Ls) % 16 == 0 and Ls % 16 == 0 and Ls >= CONV_HIST

    cos_p, sin_p = _rope_tables(jnp.arange(S, dtype=jnp.int32))
    cos_s, sin_s = _rope_tables(past_len + jnp.arange(Ls, dtype=jnp.int32))
    cos_s, sin_s = jnp.tile(cos_s, (Bs, 1)), jnp.tile(sin_s, (Bs, 1))
    zero_pool = jnp.zeros((B, HALO, POOL_WIDTH), F32)
    zero_conv = jnp.zeros((B, HALO, CONV_WIDTH), F32)

    xp = x_prompt.reshape(B * S, D_MODEL)
    xs = x_sample.reshape(Bs * Ls, D_MODEL)
    row2 = lambda a: a.reshape(1, -1)
    fg = row2(final_norm_g)
    cache_kt = jnp.transpose(cache_k, (0, 1, 3, 4, 5, 2)).reshape(depth, Bs, ATTN_WIDTH, past_len)
    cache_v4 = cache_v.reshape(depth, Bs, past_len * N_HEADS, V_DIM)
    kv_stack = depth
    pp, cp = [], []
    ks_, vs_, ps_, cs_ = [], [], [], []
    for l in range(depth):
        lam_init = 0.8 - 0.6 * math.exp(-0.3 * l)
        w_in_bf = w_in[l].astype(BF16)
        w_out_bf = w_out[l].astype(BF16)
        w_gu_bf = w_gate_up[l].astype(BF16)
        w_d_bf = w_down[l].astype(BF16)
        pw_bf = conv_pw[l].astype(BF16)
        pool_bd = _block_diag(pool_w[l]).astype(BF16)
        mixer_w = (pool_bd, row2(pool_scale[l]), conv_dw[l], row2(conv_dw_b[l]), row2(conv_ln_g[l]),
                   row2(conv_ln_b[l]), pw_bf)
        final = l == depth - 1

        upool, q, k, kb, v, vb, uconv = _in_proj(xp, row2(norm_mix_g[l]), w_in_bf, cos_p, sin_p, ROW_TILE,
                                                 kv_stack=kv_stack, layer=l, seq_len=S)
        kv_stack = (k, v)
        o = _prompt_attention(lambda_qk[l], diff_norm_g[l].reshape(-1, 1), q.reshape(B, S, -1), kb.reshape(B, S, -1),
                              vb.reshape(B, S, -1), lam_init)
        ypool, yconv = _mixers(upool, uconv, zero_pool, zero_conv, *mixer_w, ROW_TILE, S, 0)
        xp = _dense(xp, ypool, o.reshape(B * S, -1), yconv, w_out_bf, row2(norm_ffn_g[l]), w_gu_bf, w_d_bf, fg,
                    ROW_TILE, final)
        pp.append(upool.reshape(B, S, -1)[:, S - POOL_HIST:])
        cp.append(uconv.reshape(B, S, -1)[:, S - CONV_HIST:])

        upool, q, k, kb, v, vb, uconv = _in_proj(xs, row2(norm_mix_g[l]), w_in_bf, cos_s, sin_s, Bs * Ls)
        o = _sample_attention(lambda_qk[l], row2(diff_norm_g[l]), q.reshape(Bs, Ls, -1), cache_kt, cache_v4,
                              kb.reshape(Bs, Ls, -1), vb.reshape(Bs, Ls, -1), lam_init, l)
        ypool, yconv = _mixers(upool, uconv, _pad_hist(state_pool[l]), _pad_hist(state_conv[l]), *mixer_w,
                               Ls, Ls, past_len)
        xs = _dense(xs, ypool, o.reshape(Bs * Ls, -1), yconv, w_out_bf, row2(norm_ffn_g[l]), w_gu_bf, w_d_bf, fg,
                    Bs * Ls, final)
        ks_.append(k.reshape(Bs, Ls, N_HEADS, 2, HEAD_DIM))
        vs_.append(v.reshape(Bs, Ls, N_HEADS, V_DIM))
        ps_.append(upool.reshape(Bs, Ls, -1)[:, Ls - POOL_HIST:])
        cs_.append(uconv.reshape(Bs, Ls, -1)[:, Ls - CONV_HIST:])

    kt_all, v4_all = kv_stack
    new_k = jnp.transpose(kt_all.reshape(depth, B, N_HEADS, 2, HEAD_DIM, S), (0, 1, 5, 2, 3, 4))
    new_v = v4_all.reshape(depth, B, S, N_HEADS, V_DIM)
    return (xp.reshape(B, S, D_MODEL), xs.reshape(Bs, Ls, D_MODEL),
            new_k, new_v, jnp.stack(pp), jnp.stack(cp),
            jnp.stack(ks_), jnp.stack(vs_), jnp.stack(ps_), jnp.stack(cs_))
```

```python
import functools
import math

import jax
import jax.numpy as jnp
from jax import lax
from jax.experimental import pallas as pl
from jax.experimental.pallas import tpu as pltpu

D_MODEL = 1024
CHUNK = 64
POOL_WIDTH = D_MODEL // 4
POOL_WINDOWS = (2, 4, 8, 16)
POOL_GROUP_DIM = POOL_WIDTH // len(POOL_WINDOWS)
POOL_HIST = max(POOL_WINDOWS) - 1
ATTN_WIDTH = D_MODEL // 2
N_HEADS = 4
HEAD_DIM = ATTN_WIDTH // (2 * N_HEADS)
V_DIM = 2 * HEAD_DIM
ROPE_THETA = 10000.0
CONV_WIDTH = D_MODEL // 4
CONV_K = 31
CONV_HIST = CONV_K - 1
MIX_WIDTH = POOL_WIDTH + ATTN_WIDTH + CONV_WIDTH
IN_WIDTH = POOL_WIDTH + 3 * ATTN_WIDTH + 2 * CONV_WIDTH
D_FF = ((-(-8 * D_MODEL // 3) + 255) // 256) * 256
EPS = 1e-6

LANES = 128
HALO = 32
ROW_TILE = 256
ATTN_TILE = 256
HEADS_PER_STEP = 4
VMEM_LIMIT = 56 * 1024 * 1024

F32 = jnp.float32
BF16 = jnp.bfloat16


def _rms(x, g):
    return x * lax.rsqrt(jnp.mean(x * x, axis=-1, keepdims=True) + EPS) * g


def _const_spec(shape):
    return pl.BlockSpec(shape, lambda *_: (0,) * len(shape), pipeline_mode=pl.Buffered(1))


def _in_proj_kernel(*refs, cache_layout, n_unused):
    x_ref, g_ref, w_ref, cos_ref, sin_ref = refs[:5]
    upool_ref, q_ref, k_ref, kb_ref, v_ref, vb_ref, uconv_ref = refs[5 + n_unused:]
    tm = x_ref.shape[0]
    h = _rms(x_ref[...], g_ref[...]).astype(BF16)
    z = jnp.dot(h, w_ref[...], preferred_element_type=F32)
    o1 = POOL_WIDTH
    o2 = o1 + ATTN_WIDTH
    o3 = o2 + ATTN_WIDTH
    o4 = o3 + ATTN_WIDTH
    o5 = o4 + CONV_WIDTH
    upool_ref[...] = z[:, :o1]

    cos = cos_ref[...]
    sin = sin_ref[...]
    lane = lax.broadcasted_iota(jnp.int32, cos.shape, 1)
    first_half = (lane & (HEAD_DIM // 2)) == 0

    def rope(t):
        up = pltpu.roll(t, LANES - HEAD_DIM // 2, axis=1)
        down = pltpu.roll(t, HEAD_DIM // 2, axis=1)
        return t * cos + jnp.where(first_half, up, down) * sin

    scale = HEAD_DIM ** -0.5 * math.log2(math.e)
    for j in range(ATTN_WIDTH // LANES):
        sl = slice(j * LANES, (j + 1) * LANES)
        qj = rope(z[:, o1 + j * LANES:o1 + (j + 1) * LANES])
        kj = rope(z[:, o2 + j * LANES:o2 + (j + 1) * LANES])
        vj = z[:, o3 + j * LANES:o3 + (j + 1) * LANES]
        q_ref[:, sl] = (qj * scale).astype(BF16)
        kb_ref[:, sl] = kj.astype(BF16)
        vb_ref[:, sl] = vj.astype(BF16)
        if cache_layout:
            k_ref[sl, :] = kj.T
            v_ref[pl.ds(j, tm, stride=N_HEADS), :] = vj
        else:
            k_ref[:, sl] = kj
            v_ref[:, sl] = vj
    uconv_ref[...] = z[:, o4:o5] * jax.nn.sigmoid(z[:, o5:])


def _in_proj(x, g, w_bf, cos_tab, sin_tab, tm, kv_stack=None, layer=0, seq_len=None):
    n = x.shape[0]
    tab_tiles = cos_tab.shape[0] // tm
    row = lambda w: pl.BlockSpec((tm, w), lambda i: (i, 0))
    tab = pl.BlockSpec((tm, LANES), lambda i: (i % tab_tiles, 0))
    flat = lambda dt: jax.ShapeDtypeStruct((n, ATTN_WIDTH), dt)
    cache_layout = seq_len is not None
    extra_in, extra_specs, aliases = [], [], {}
    if cache_layout:
        tps = seq_len // tm
        batch = n // seq_len
        if isinstance(kv_stack, int):
            k_shape = jax.ShapeDtypeStruct((kv_stack, batch, ATTN_WIDTH, seq_len), F32)
            v_shape = jax.ShapeDtypeStruct((kv_stack, batch, N_HEADS * seq_len, V_DIM), F32)
        else:
            k_shape, v_shape = (jax.ShapeDtypeStruct(a.shape, a.dtype) for a in kv_stack)
            extra_in = list(kv_stack)
            extra_specs = [pl.BlockSpec(memory_space=pl.ANY)] * 2
            aliases = {5: 2, 6: 4}
        k_spec = pl.BlockSpec((None, None, ATTN_WIDTH, tm), lambda i: (layer, i // tps, 0, i % tps))
        v_spec = pl.BlockSpec((None, None, N_HEADS * tm, V_DIM), lambda i: (layer, i // tps, i % tps, 0))
    else:
        k_shape, v_shape, k_spec, v_spec = flat(F32), flat(F32), row(ATTN_WIDTH), row(ATTN_WIDTH)
    outs = [jax.ShapeDtypeStruct((n, POOL_WIDTH), F32), flat(BF16), k_shape, flat(BF16), v_shape, flat(BF16),
            jax.ShapeDtypeStruct((n, CONV_WIDTH), F32)]
    return pl.pallas_call(
        functools.partial(_in_proj_kernel, cache_layout=cache_layout, n_unused=len(extra_in)),
        grid=(n // tm,),
        in_specs=[row(D_MODEL), _const_spec((1, D_MODEL)), _const_spec((D_MODEL, IN_WIDTH)), tab, tab] + extra_specs,
        out_specs=[row(POOL_WIDTH), row(ATTN_WIDTH), k_spec, row(ATTN_WIDTH), v_spec, row(ATTN_WIDTH),
                   row(CONV_WIDTH)],
        out_shape=outs,
        input_output_aliases=aliases,
        compiler_params=pltpu.CompilerParams(dimension_semantics=("arbitrary",),
                                             vmem_limit_bytes=VMEM_LIMIT),
        name="in_proj",
    )(x, g, w_bf, cos_tab, sin_tab, *extra_in)


def _lambda_scalar(lq, lam_init):
    a = jnp.sum(lq[0:1, :] * lq[1:2, :], axis=-1, keepdims=True)
    b = jnp.sum(lq[2:3, :] * lq[3:4, :], axis=-1, keepdims=True)
    return jnp.exp(a) - jnp.exp(b) + lam_init


def _stack_maps(q):
    lane = lax.broadcasted_iota(jnp.int32, q.shape, 1)
    zero = jnp.zeros_like(q)
    return jnp.concatenate([jnp.where(lane < HEAD_DIM, q, zero), jnp.where(lane >= HEAD_DIM, q, zero)], axis=0)


def _flash_step(carry, qq, kt, vt, mask):
    return _flash_update(carry, _flash_scores(qq, kt, mask), vt)


def _flash_scores(qq, kt, mask):
    s = lax.dot_general(kt, qq, (((1,), (1,)), ((), ())), preferred_element_type=F32)
    if mask is not None:
        s = jnp.where(mask, s, -1e30)
    return s


def _flash_update(carry, s, vt):
    m, l, acc = carry
    m_new = jnp.maximum(m, jnp.max(s, axis=0, keepdims=True))
    alpha = jnp.exp2(m - m_new)
    p = jnp.exp2(s - m_new)
    l = alpha * l + jnp.sum(p, axis=0, keepdims=True)
    pv = lax.dot_general(vt, p.astype(BF16), (((0,), (0,)), ((), ())), preferred_element_type=F32)
    return m_new, l, alpha * acc + pv


def _flash_init(cols):
    return (jnp.full((1, cols), -jnp.inf, F32), jnp.zeros((1, cols), F32), jnp.zeros((V_DIM, cols), F32))


def _attn_finish(carry, lam, g_col, lam_init, tq):
    _, l, acc = carry
    o = acc[:, :tq] / l[:, :tq] - lam * (acc[:, tq:] / l[:, tq:])
    o = o * lax.rsqrt(jnp.mean(o * o, axis=0, keepdims=True) + EPS) * g_col
    return (o * (1.0 - lam_init)).T


def _mask_diagonal_tile(s):
    t = s.shape[0]
    q_chunk = (lax.broadcasted_iota(jnp.int32, (1, s.shape[1]), 1) % t) // CHUNK
    blocks = [jnp.where(q_chunk >= a, s[a * CHUNK:(a + 1) * CHUNK, :], -1e30) for a in range(t // CHUNK)]
    return jnp.concatenate(blocks, axis=0)


def _prompt_attn_kernel(lq_ref, g_ref, q_ref, k_ref, v_ref, o_ref, qq_ref, *, lam_init):
    t = ATTN_TILE
    qi = pl.program_id(2)
    heads = [slice(h * LANES, (h + 1) * LANES) for h in range(HEADS_PER_STEP)]
    for h, c in enumerate(heads):
        qq_ref[h] = _stack_maps(q_ref[:, c])

    def step(j, carry, diagonal):
        r = pl.ds(pl.multiple_of(j * t, t), t)
        s = [_flash_scores(qq_ref[h], k_ref[r, c], None) for h, c in enumerate(heads)]
        if diagonal:
            s = [_mask_diagonal_tile(x) for x in s]
        return tuple(_flash_update(carry[h], s[h], v_ref[r, c]) for h, c in enumerate(heads))

    carry = lax.fori_loop(0, qi, lambda j, c: step(j, c, False), (_flash_init(2 * t),) * HEADS_PER_STEP)
    carry = step(qi, carry, True)
    lam = _lambda_scalar(lq_ref[...], lam_init)
    for h, c in enumerate(heads):
        o_ref[:, c] = _attn_finish(carry[h], lam, g_ref[...], lam_init, t).astype(BF16)


def _prompt_attention(lq, g, q, kb, vb, lam_init):
    b, s, _ = q.shape
    t = ATTN_TILE
    w = HEADS_PER_STEP * LANES
    return pl.pallas_call(
        functools.partial(_prompt_attn_kernel, lam_init=lam_init),
        grid=(b, N_HEADS // HEADS_PER_STEP, s // t),
        in_specs=[
            pl.BlockSpec((4, HEAD_DIM), lambda b_, h, i: (0, 0)),
            pl.BlockSpec((V_DIM, 1), lambda b_, h, i: (0, 0)),
            pl.BlockSpec((None, t, w), lambda b_, h, i: (b_, i, h)),
            pl.BlockSpec((None, s, w), lambda b_, h, i: (b_, 0, h)),
            pl.BlockSpec((None, s, w), lambda b_, h, i: (b_, 0, h)),
        ],
        out_specs=pl.BlockSpec((None, t, w), lambda b_, h, i: (b_, i, h)),
        out_shape=jax.ShapeDtypeStruct((b, s, ATTN_WIDTH), BF16),
        scratch_shapes=[pltpu.VMEM((HEADS_PER_STEP, 2 * t, LANES), BF16)],
        compiler_params=pltpu.CompilerParams(dimension_semantics=("arbitrary",) * 3,
                                             vmem_limit_bytes=VMEM_LIMIT),
        name="prompt_attention",
    )(lq, g, q, kb, vb)


def _row_scores(qq, kt, *, keys_on_rows):
    dims = (((1,), (1,)), ((), ())) if keys_on_rows else (((1,), (0,)), ((), ()))
    return lax.dot_general(qq, kt, dims, preferred_element_type=F32)


def _row_flash_update(carry, s, vt):
    m, l, acc = carry
    m_new = jnp.maximum(m, jnp.max(s, axis=-1, keepdims=True))
    alpha = jnp.exp2(m - m_new)
    p = jnp.exp2(s - m_new)
    l = alpha * l + jnp.sum(p, axis=-1, keepdims=True)
    acc = alpha * acc + jnp.dot(p.astype(BF16), vt, preferred_element_type=F32)
    return m_new, l, acc


def _sample_attn_kernel(lq_ref, g_ref, q_ref, ck_ref, cv_ref, k_ref, v_ref, o_ref, *, lam_init, past_len):
    t = ATTN_TILE
    tq = q_ref.shape[0]
    heads = [slice(h * LANES, (h + 1) * LANES) for h in range(N_HEADS)]
    qq = [_stack_maps(q_ref[:, c]) for c in heads]

    def body(j, carry):
        cols = pl.ds(pl.multiple_of(j * t, t), t)
        s = [_row_scores(qq[h], ck_ref[c, cols].astype(BF16), keys_on_rows=False) for h, c in enumerate(heads)]
        return tuple(
            _row_flash_update(carry[h], s[h], cv_ref[pl.ds(j * (t * N_HEADS) + h, t, stride=N_HEADS), :].astype(BF16))
            for h in range(N_HEADS))

    init = (jnp.full((2 * tq, 1), -jnp.inf, F32), jnp.zeros((2 * tq, 1), F32), jnp.zeros((2 * tq, V_DIM), F32))
    carry = lax.fori_loop(0, past_len // t, body, (init,) * N_HEADS)
    lam = _lambda_scalar(lq_ref[...], lam_init)
    for h, c in enumerate(heads):
        s = _row_scores(qq[h], k_ref[:, c], keys_on_rows=True)
        _, l, acc = _row_flash_update(carry[h], s, v_ref[:, c])
        o = acc[:tq] / l[:tq] - lam * (acc[tq:] / l[tq:])
        o_ref[:, c] = (_rms(o, g_ref[...]) * (1.0 - lam_init)).astype(BF16)


def _sample_attention(lq, g, q, cache_kt, cache_v4, kb, vb, lam_init, layer):
    b, ls, _ = q.shape
    past_len = cache_kt.shape[-1]
    assert past_len % ATTN_TILE == 0
    return pl.pallas_call(
        functools.partial(_sample_attn_kernel, lam_init=lam_init, past_len=past_len),
        grid=(b,),
        in_specs=[
            pl.BlockSpec((4, HEAD_DIM), lambda b_: (0, 0)),
            pl.BlockSpec((1, V_DIM), lambda b_: (0, 0)),
            pl.BlockSpec((None, ls, ATTN_WIDTH), lambda b_: (b_, 0, 0)),
            pl.BlockSpec((None, None, ATTN_WIDTH, past_len), lambda b_: (layer, b_, 0, 0)),
            pl.BlockSpec((None, None, N_HEADS * past_len, V_DIM), lambda b_: (layer, b_, 0, 0)),
            pl.BlockSpec((None, ls, ATTN_WIDTH), lambda b_: (b_, 0, 0)),
            pl.BlockSpec((None, ls, ATTN_WIDTH), lambda b_: (b_, 0, 0)),
        ],
        out_specs=pl.BlockSpec((None, ls, ATTN_WIDTH), lambda b_: (b_, 0, 0)),
        out_shape=jax.ShapeDtypeStruct((b, ls, ATTN_WIDTH), BF16),
        compiler_params=pltpu.CompilerParams(dimension_semantics=("arbitrary",),
                                             vmem_limit_bytes=VMEM_LIMIT),
        name="sample_attention",
    )(lq, g, q, cache_kt, cache_v4, kb, vb)


def _mixers_kernel(up_ref, up_halo_ref, up_hist_ref, uc_ref, uc_halo_ref, uc_hist_ref,
                   pool_w_ref, pool_scale_ref, dw_ref, dw_b_ref, ln_g_ref, ln_b_ref, pw_ref,
                   ypool_ref, yconv_ref, ext_p, ext_c, shift_c, conv_out, *, tiles_per_seq, pos_base):
    tm = up_ref.shape[0]
    i = pl.program_id(0)
    tile_in_seq = i % tiles_per_seq
    first = tile_in_seq == 0

    ext_p[0:HALO, :] = jnp.where(first, up_hist_ref[...], up_halo_ref[...])
    ext_p[HALO:, :] = up_ref[...]
    ext_c[0:HALO, :] = jnp.where(first, uc_hist_ref[...], uc_halo_ref[...])
    ext_c[HALO:, :] = uc_ref[...]

    assert POOL_WINDOWS == (2, 4, 8, 16) and HALO == 32
    e = ext_p[...]
    n = HALO + tm
    s2 = e[8:] + e[7:n - 1]
    s4 = s2[8:] + s2[6:n - 10]
    s8 = s4[8:] + s4[4:n - 20]
    sums = {2: s2[24:], 4: s4[16:], 8: s8[8:], 16: s8[8:] + s8[:n - 32]}
    u = e[HALO:]
    lane = lax.broadcasted_iota(jnp.int32, (tm, POOL_WIDTH), 1)
    group = lane // POOL_GROUP_DIM
    win = sums[POOL_WINDOWS[-1]]
    width = jnp.full((tm, POOL_WIDTH), POOL_WINDOWS[-1], jnp.int32)
    for gi in range(len(POOL_WINDOWS) - 2, -1, -1):
        win = jnp.where(group == gi, sums[POOL_WINDOWS[gi]], win)
        width = jnp.where(group == gi, POOL_WINDOWS[gi], width)
    pos = pos_base + tile_in_seq * tm + lax.broadcasted_iota(jnp.int32, (tm, POOL_WIDTH), 0)
    count = jnp.minimum(pos + 1, width).astype(F32)
    d = (win / count - u).astype(BF16)
    ypool = jnp.dot(d, pool_w_ref[...], preferred_element_type=F32) * pool_scale_ref[...]
    ypool_ref[...] = ypool.astype(BF16)

    first_row = HALO - CONV_HIST
    for r in range(1, 8):
        shift_c[r, 0:HALO - 8 + tm, :] = ext_c[pl.ds(r, HALO - 8 + tm), :]
    blk = min(tm, 64)
    for rb in range(tm // blk):
        yb = jnp.zeros((blk, CONV_WIDTH), F32)
        for k in range(CONV_K):
            r, a = (first_row + k) % 8, (first_row + k) // 8
            rows = pl.ds(8 * a + rb * blk, blk)
            src = ext_c[rows, :] if r == 0 else shift_c[r, rows, :]
            yb = yb + src * dw_ref[k:k + 1, :]
        conv_out[rb * blk:(rb + 1) * blk, :] = yb
    y = conv_out[...]
    y = y + dw_b_ref[...]
    mu = jnp.mean(y, axis=-1, keepdims=True)
    yc = y - mu
    yn = yc * lax.rsqrt(jnp.mean(yc * yc, axis=-1, keepdims=True) + EPS) * ln_g_ref[...] + ln_b_ref[...]
    act = (yn * jax.nn.sigmoid(yn)).astype(BF16)
    yconv_ref[...] = jnp.dot(act, pw_ref[...], preferred_element_type=F32).astype(BF16)


def _mixers(upool, uconv, hist_pool, hist_conv, pool_w_bd, pool_scale, dw, dw_b, ln_g, ln_b, pw_bf,
            tm, seq_len, pos_base):
    n = upool.shape[0]
    tiles_per_seq = seq_len // tm
    ratio = tm // HALO
    cur = pl.BlockSpec((tm, POOL_WIDTH), lambda i: (i, 0))
    halo = pl.BlockSpec((HALO, POOL_WIDTH), lambda i: (jnp.maximum(i * ratio - 1, 0), 0))
    hist = pl.BlockSpec((None, HALO, POOL_WIDTH), lambda i: (i // tiles_per_seq, 0, 0))
    return pl.pallas_call(
        functools.partial(_mixers_kernel, tiles_per_seq=tiles_per_seq, pos_base=pos_base),
        grid=(n // tm,),
        in_specs=[cur, halo, hist, cur, halo, hist,
                  _const_spec((POOL_WIDTH, POOL_WIDTH)), _const_spec((1, POOL_WIDTH)),
                  _const_spec((CONV_K, CONV_WIDTH)), _const_spec((1, CONV_WIDTH)),
                  _const_spec((1, CONV_WIDTH)), _const_spec((1, CONV_WIDTH)),
                  _const_spec((CONV_WIDTH, CONV_WIDTH))],
        out_specs=[cur, cur],
        out_shape=[jax.ShapeDtypeStruct((n, POOL_WIDTH), BF16), jax.ShapeDtypeStruct((n, CONV_WIDTH), BF16)],
        scratch_shapes=[pltpu.VMEM((HALO + tm, POOL_WIDTH), F32), pltpu.VMEM((HALO + tm, CONV_WIDTH), F32),
                        pltpu.VMEM((8, HALO + tm, CONV_WIDTH), F32), pltpu.VMEM((tm, CONV_WIDTH), F32)],
        compiler_params=pltpu.CompilerParams(dimension_semantics=("arbitrary",),
                                             vmem_limit_bytes=VMEM_LIMIT),
        name="mixers",
    )(upool, upool, hist_pool, uconv, uconv, hist_conv, pool_w_bd, pool_scale, dw, dw_b, ln_g, ln_b, pw_bf)


def _dense_kernel(x_ref, yp_ref, o_ref, yc_ref, wout_ref, g_ref, wgu_ref, wd_ref, fg_ref, out_ref, *, final):
    o1 = POOL_WIDTH
    o2 = o1 + ATTN_WIDTH
    mix = (jnp.dot(yp_ref[...], wout_ref[0:o1, :], preferred_element_type=F32)
           + jnp.dot(o_ref[...], wout_ref[o1:o2, :], preferred_element_type=F32)
           + jnp.dot(yc_ref[...], wout_ref[o2:, :], preferred_element_type=F32))
    x1 = x_ref[...] + mix
    h = _rms(x1, g_ref[...]).astype(BF16)
    gu = jnp.dot(h, wgu_ref[...], preferred_element_type=F32)
    gate = gu[:, :D_FF]
    a = (gate * jax.nn.sigmoid(gate) * gu[:, D_FF:]).astype(BF16)
    x2 = x1 + jnp.dot(a, wd_ref[...], preferred_element_type=F32)
    out_ref[...] = _rms(x2, fg_ref[...]) if final else x2


def _dense(x, ypool, o, yconv, wout_bf, g, wgu_bf, wd_bf, fg, tm, final):
    n = x.shape[0]
    row = lambda w: pl.BlockSpec((tm, w), lambda i: (i, 0))
    return pl.pallas_call(
        functools.partial(_dense_kernel, final=final),
        grid=(n // tm,),
        in_specs=[row(D_MODEL), row(POOL_WIDTH), row(ATTN_WIDTH), row(CONV_WIDTH),
                  _const_spec((MIX_WIDTH, D_MODEL)), _const_spec((1, D_MODEL)),
                  _const_spec((D_MODEL, 2 * D_FF)), _const_spec((D_FF, D_MODEL)), _const_spec((1, D_MODEL))],
        out_specs=row(D_MODEL),
        out_shape=jax.ShapeDtypeStruct((n, D_MODEL), F32),
        compiler_params=pltpu.CompilerParams(dimension_semantics=("arbitrary",),
                                             vmem_limit_bytes=VMEM_LIMIT),
        name="dense",
    )(x, ypool, o, yconv, wout_bf, g, wgu_bf, wd_bf, fg)


def _rope_tables(pos):
    half = HEAD_DIM // 2
    inv = ROPE_THETA ** (-jnp.arange(half, dtype=F32) / half)
    ang = pos.astype(F32)[:, None] * inv[None, :]
    reps = LANES // half
    cos = jnp.tile(jnp.cos(ang), (1, reps))
    sign = jnp.tile(jnp.concatenate([-jnp.ones((half,), F32), jnp.ones((half,), F32)]), LANES // HEAD_DIM)
    sin = jnp.tile(jnp.sin(ang), (1, reps)) * sign[None, :]
    return cos, sin


def _block_diag(pool_w):
    g, d, _ = pool_w.shape
    out = jnp.zeros((g * d, g * d), pool_w.dtype)
    for i in range(g):
        out = out.at[i * d:(i + 1) * d, i * d:(i + 1) * d].set(pool_w[i])
    return out


def _pad_hist(h):
    return jnp.pad(h, ((0, 0), (HALO - h.shape[1], 0), (0, 0)))


def kernel(x_prompt, x_sample, cache_k, cache_v, state_pool, state_conv, norm_mix_g, w_in, pool_w, pool_scale,
           lambda_qk, diff_norm_g, conv_dw, conv_dw_b, conv_ln_g, conv_ln_b, conv_pw, w_out, norm_ffn_g,
           w_gate_up, w_down, final_norm_g):
    B, S, _ = x_prompt.shape
    Bs, Ls, _ = x_sample.shape
    depth = w_in.shape[0]
    past_len = cache_k.shape[2]
    assert S % ATTN_TILE == 0 and S % ROW_TILE == 0 and (Bs * Ls) % 16 == 0 and Ls % 16 == 0 and Ls >= CONV_HIST

    cos_p, sin_p = _rope_tables(jnp.arange(S, dtype=jnp.int32))
    cos_s, sin_s = _rope_tables(past_len + jnp.arange(Ls, dtype=jnp.int32))
    cos_s, sin_s = jnp.tile(cos_s, (Bs, 1)), jnp.tile(sin_s, (Bs, 1))
    zero_pool = jnp.zeros((B, HALO, POOL_WIDTH), F32)
    zero_conv = jnp.zeros((B, HALO, CONV_WIDTH), F32)

    xp = x_prompt.reshape(B * S, D_MODEL)
    xs = x_sample.reshape(Bs * Ls, D_MODEL)
    row2 = lambda a: a.reshape(1, -1)
    fg = row2(final_norm_g)
    cache_kt = jnp.transpose(cache_k, (0, 1, 3, 4, 5, 2)).reshape(depth, Bs, ATTN_WIDTH, past_len)
    cache_v4 = cache_v.reshape(depth, Bs, past_len * N_HEADS, V_DIM)
    kv_stack = depth
    pp, cp = [], []
    ks_, vs_, ps_, cs_ = [], [], [], []
    for l in range(depth):
        lam_init = 0.8 - 0.6 * math.exp(-0.3 * l)
        w_in_bf = w_in[l].astype(BF16)
        w_out_bf = w_out[l].astype(BF16)
        w_gu_bf = w_gate_up[l].astype(BF16)
        w_d_bf = w_down[l].astype(BF16)
        pw_bf = conv_pw[l].astype(BF16)
        pool_bd = _block_diag(pool_w[l]).astype(BF16)
        mixer_w = (pool_bd, row2(pool_scale[l]), conv_dw[l], row2(conv_dw_b[l]), row2(conv_ln_g[l]),
                   row2(conv_ln_b[l]), pw_bf)
        final = l == depth - 1

        upool, q, k, kb, v, vb, uconv = _in_proj(xp, row2(norm_mix_g[l]), w_in_bf, cos_p, sin_p, ROW_TILE,
                                                 kv_stack=kv_stack, layer=l, seq_len=S)
        kv_stack = (k, v)
        o = _prompt_attention(lambda_qk[l], diff_norm_g[l].reshape(-1, 1), q.reshape(B, S, -1), kb.reshape(B, S, -1),
                              vb.reshape(B, S, -1), lam_init)
        ypool, yconv = _mixers(upool, uconv, zero_pool, zero_conv, *mixer_w, ROW_TILE, S, 0)
        xp = _dense(xp, ypool, o.reshape(B * S, -1), yconv, w_out_bf, row2(norm_ffn_g[l]), w_gu_bf, w_d_bf, fg,
                    ROW_TILE, final)
        pp.append(upool.reshape(B, S, -1)[:, S - POOL_HIST:])
        cp.append(uconv.reshape(B, S, -1)[:, S - CONV_HIST:])

        upool, q, k, kb, v, vb, uconv = _in_proj(xs, row2(norm_mix_g[l]), w_in_bf, cos_s, sin_s, Bs * Ls)
        o = _sample_attention(lambda_qk[l], row2(diff_norm_g[l]), q.reshape(Bs, Ls, -1), cache_kt, cache_v4,
                              kb.reshape(Bs, Ls, -1), vb.reshape(Bs, Ls, -1), lam_init, l)
        ypool, yconv = _mixers(upool, uconv, _pad_hist(state_pool[l]), _pad_hist(state_conv[l]), *mixer_w,
                               Ls, Ls, past_len)
        xs = _dense(xs, ypool, o.reshape(Bs * Ls, -1), yconv, w_out_bf, row2(norm_ffn_g[l]), w_gu_bf, w_d_bf, fg,
                    Bs * Ls, final)
        ks_.append(k.reshape(Bs, Ls, N_HEADS, 2, HEAD_DIM))
        vs_.append(v.reshape(Bs, Ls, N_HEADS, V_DIM))
        ps_.append(upool.reshape(Bs, Ls, -1)[:, Ls - POOL_HIST:])
        cs_.append(uconv.reshape(Bs, Ls, -1)[:, Ls - CONV_HIST:])

    kt_all, v4_all = kv_stack
    new_k = jnp.transpose(kt_all.reshape(depth, B, N_HEADS, 2, HEAD_DIM, S), (0, 1, 5, 2, 3, 4))
    new_v = v4_all.reshape(depth, B, S, N_HEADS, V_DIM)
    return (xp.reshape(B, S, D_MODEL), xs.reshape(Bs, Ls, D_MODEL),
            new_k, new_v, jnp.stack(pp), jnp.stack(cp),
            jnp.stack(ks_), jnp.stack(vs_), jnp.stack(ps_), jnp.stack(cs_))
```

```python
import functools
import math

import jax
import jax.numpy as jnp
from jax import lax
from jax.experimental import pallas as pl
from jax.experimental.pallas import tpu as pltpu

D_MODEL = 1024
CHUNK = 64
POOL_WIDTH = D_MODEL // 4
POOL_WINDOWS = (2, 4, 8, 16)
POOL_GROUP_DIM = POOL_WIDTH // len(POOL_WINDOWS)
POOL_HIST = max(POOL_WINDOWS) - 1
ATTN_WIDTH = D_MODEL // 2
N_HEADS = 4
HEAD_DIM = ATTN_WIDTH // (2 * N_HEADS)
V_DIM = 2 * HEAD_DIM
ROPE_THETA = 10000.0
CONV_WIDTH = D_MODEL // 4
CONV_K = 31
CONV_HIST = CONV_K - 1
MIX_WIDTH = POOL_WIDTH + ATTN_WIDTH + CONV_WIDTH
IN_WIDTH = POOL_WIDTH + 3 * ATTN_WIDTH + 2 * CONV_WIDTH
D_FF = ((-(-8 * D_MODEL // 3) + 255) // 256) * 256
EPS = 1e-6

LANES = 128
HALO = 32
ROW_TILE = 256
ATTN_TILE = 256
HEADS_PER_STEP = 4
VMEM_LIMIT = 56 * 1024 * 1024

F32 = jnp.float32
BF16 = jnp.bfloat16


def _rms(x, g):
    return x * lax.rsqrt(jnp.mean(x * x, axis=-1, keepdims=True) + EPS) * g


def _const_spec(shape):
    return pl.BlockSpec(shape, lambda *_: (0,) * len(shape), pipeline_mode=pl.Buffered(1))


O_Q = POOL_WIDTH
O_K = O_Q + ATTN_WIDTH
O_V = O_K + ATTN_WIDTH
O_A = O_V + ATTN_WIDTH
O_B = O_A + CONV_WIDTH


def _qkv_epilogue(z, cos_ref, sin_ref, q_ref, k_ref, kb_ref, v_ref, vb_ref, cache_layout):
    tm = z.shape[0]
    o1, o2, o3 = O_Q, O_K, O_V
    cos = cos_ref[...]
    sin = sin_ref[...]
    lane = lax.broadcasted_iota(jnp.int32, cos.shape, 1)
    first_half = (lane & (HEAD_DIM // 2)) == 0

    def rope(t):
        up = pltpu.roll(t, LANES - HEAD_DIM // 2, axis=1)
        down = pltpu.roll(t, HEAD_DIM // 2, axis=1)
        return t * cos + jnp.where(first_half, up, down) * sin

    scale = HEAD_DIM ** -0.5 * math.log2(math.e)
    for j in range(ATTN_WIDTH // LANES):
        sl = slice(j * LANES, (j + 1) * LANES)
        qj = rope(z[:, o1 + j * LANES:o1 + (j + 1) * LANES])
        kj = rope(z[:, o2 + j * LANES:o2 + (j + 1) * LANES])
        vj = z[:, o3 + j * LANES:o3 + (j + 1) * LANES]
        q_ref[:, sl] = (qj * scale).astype(BF16)
        kb_ref[:, sl] = kj.astype(BF16)
        vb_ref[:, sl] = vj.astype(BF16)
        if cache_layout:
            k_ref[sl, :] = kj.T
            v_ref[pl.ds(j, tm, stride=N_HEADS), :] = vj
        else:
            k_ref[:, sl] = kj
            v_ref[:, sl] = vj


def _sample_in_proj_kernel(x_ref, g_ref, w_ref, cos_ref, sin_ref,
                           upool_ref, q_ref, k_ref, kb_ref, v_ref, vb_ref, uconv_ref):
    h = _rms(x_ref[...], g_ref[...]).astype(BF16)
    z = jnp.dot(h, w_ref[...], preferred_element_type=F32)
    upool_ref[...] = z[:, :O_Q]
    uconv_ref[...] = z[:, O_A:O_B] * jax.nn.sigmoid(z[:, O_B:])
    _qkv_epilogue(z, cos_ref, sin_ref, q_ref, k_ref, kb_ref, v_ref, vb_ref, False)


def _sample_in_proj(x, g, w_bf, cos_tab, sin_tab):
    n = x.shape[0]
    full = lambda w: pl.BlockSpec((n, w), lambda i: (0, 0))
    flat = lambda dt: jax.ShapeDtypeStruct((n, ATTN_WIDTH), dt)
    return pl.pallas_call(
        _sample_in_proj_kernel,
        grid=(1,),
        in_specs=[full(D_MODEL), _const_spec((1, D_MODEL)), _const_spec((D_MODEL, IN_WIDTH)), full(LANES), full(LANES)],
        out_specs=[full(POOL_WIDTH)] + [full(ATTN_WIDTH)] * 5 + [full(CONV_WIDTH)],
        out_shape=[jax.ShapeDtypeStruct((n, POOL_WIDTH), F32), flat(BF16), flat(F32), flat(BF16), flat(F32),
                   flat(BF16), jax.ShapeDtypeStruct((n, CONV_WIDTH), F32)],
        compiler_params=pltpu.CompilerParams(dimension_semantics=("arbitrary",),
                                             vmem_limit_bytes=VMEM_LIMIT),
        name="sample_in_proj",
    )(x, g, w_bf, cos_tab, sin_tab)


def _prompt_in_proj_kernel(*refs, n_unused):
    x_ref, g_ref, w_ref, cos_ref, sin_ref = refs[:5]
    upool_ref, q_ref, k_ref, kb_ref, v_ref, vb_ref, uconv_ref = refs[5 + n_unused:]
    h = _rms(x_ref[...], g_ref[...]).astype(BF16)
    z = jnp.dot(h, w_ref[...], preferred_element_type=F32)
    upool_ref[...] = z[:, :O_Q]
    uconv_ref[...] = z[:, O_A:O_B] * jax.nn.sigmoid(z[:, O_B:])
    _qkv_epilogue(z, cos_ref, sin_ref, q_ref, k_ref, kb_ref, v_ref, vb_ref, True)


def _prompt_in_proj(x, g, w_bf, cos_tab, sin_tab, tm, seq_len, kv_stack, layer):
    n = x.shape[0]
    tps = seq_len // tm
    batch = n // seq_len
    row = lambda w: pl.BlockSpec((tm, w), lambda i: (i, 0))
    tab = pl.BlockSpec((tm, LANES), lambda i: (i % tps, 0))
    flat = lambda dt, w=ATTN_WIDTH: jax.ShapeDtypeStruct((n, w), dt)
    extra_in, extra_specs, aliases = [], [], {}
    if isinstance(kv_stack, int):
        k_shape = jax.ShapeDtypeStruct((kv_stack, batch, ATTN_WIDTH, seq_len), F32)
        v_shape = jax.ShapeDtypeStruct((kv_stack, batch, N_HEADS * seq_len, V_DIM), F32)
    else:
        k_shape, v_shape = (jax.ShapeDtypeStruct(a.shape, a.dtype) for a in kv_stack)
        extra_in = list(kv_stack)
        extra_specs = [pl.BlockSpec(memory_space=pl.ANY)] * 2
        aliases = {5: 2, 6: 4}
    k_spec = pl.BlockSpec((None, None, ATTN_WIDTH, tm), lambda i: (layer, i // tps, 0, i % tps))
    v_spec = pl.BlockSpec((None, None, N_HEADS * tm, V_DIM), lambda i: (layer, i // tps, i % tps, 0))
    return pl.pallas_call(
        functools.partial(_prompt_in_proj_kernel, n_unused=len(extra_in)),
        grid=(n // tm,),
        in_specs=[row(D_MODEL), _const_spec((1, D_MODEL)), _const_spec((D_MODEL, IN_WIDTH)), tab, tab] + extra_specs,
        out_specs=[row(POOL_WIDTH), row(ATTN_WIDTH), k_spec, row(ATTN_WIDTH), v_spec, row(ATTN_WIDTH),
                   row(CONV_WIDTH)],
        out_shape=[flat(F32, POOL_WIDTH), flat(BF16), k_shape, flat(BF16), v_shape, flat(BF16),
                   flat(F32, CONV_WIDTH)],
        input_output_aliases=aliases,
        compiler_params=pltpu.CompilerParams(dimension_semantics=("arbitrary",),
                                             vmem_limit_bytes=VMEM_LIMIT),
        name="prompt_in_proj",
    )(x, g, w_bf, cos_tab, sin_tab, *extra_in)


def _lambda_scalar(lq, lam_init):
    a = jnp.sum(lq[0:1, :] * lq[1:2, :], axis=-1, keepdims=True)
    b = jnp.sum(lq[2:3, :] * lq[3:4, :], axis=-1, keepdims=True)
    return jnp.exp(a) - jnp.exp(b) + lam_init


def _stack_maps(q):
    lane = lax.broadcasted_iota(jnp.int32, q.shape, 1)
    zero = jnp.zeros_like(q)
    return jnp.concatenate([jnp.where(lane < HEAD_DIM, q, zero), jnp.where(lane >= HEAD_DIM, q, zero)], axis=0)


def _flash_step(carry, qq, kt, vt, mask):
    return _flash_update(carry, _flash_scores(qq, kt, mask), vt)


def _flash_scores(qq, kt, mask):
    s = lax.dot_general(kt, qq, (((1,), (1,)), ((), ())), preferred_element_type=F32)
    if mask is not None:
        s = jnp.where(mask, s, -1e30)
    return s


def _flash_update(carry, s, vt):
    m, l, acc = carry
    m_new = jnp.maximum(m, jnp.max(s, axis=0, keepdims=True))
    alpha = jnp.exp2(m - m_new)
    p = jnp.exp2(s - m_new)
    l = alpha * l + jnp.sum(p, axis=0, keepdims=True)
    pv = lax.dot_general(vt, p.astype(BF16), (((0,), (0,)), ((), ())), preferred_element_type=F32)
    return m_new, l, alpha * acc + pv


def _flash_init(cols):
    return (jnp.full((1, cols), -jnp.inf, F32), jnp.zeros((1, cols), F32), jnp.zeros((V_DIM, cols), F32))


def _attn_finish(carry, lam, g_col, lam_init, tq):
    _, l, acc = carry
    o = acc[:, :tq] / l[:, :tq] - lam * (acc[:, tq:] / l[:, tq:])
    o = o * lax.rsqrt(jnp.mean(o * o, axis=0, keepdims=True) + EPS) * g_col
    return (o * (1.0 - lam_init)).T


def _mask_diagonal_tile(s):
    t = s.shape[0]
    q_chunk = (lax.broadcasted_iota(jnp.int32, (1, s.shape[1]), 1) % t) // CHUNK
    blocks = [jnp.where(q_chunk >= a, s[a * CHUNK:(a + 1) * CHUNK, :], -1e30) for a in range(t // CHUNK)]
    return jnp.concatenate(blocks, axis=0)


def _prompt_attn_kernel(lq_ref, g_ref, q_ref, k_ref, v_ref, o_ref, qq_ref, *, lam_init):
    t = ATTN_TILE
    qi = pl.program_id(2)
    heads = [slice(h * LANES, (h + 1) * LANES) for h in range(HEADS_PER_STEP)]
    for h, c in enumerate(heads):
        qq_ref[h] = _stack_maps(q_ref[:, c])

    def steps(tiles, carry):
        rows = [pl.ds(pl.multiple_of(j * t, t), t) for j, _ in tiles]
        s = [[_flash_scores(qq_ref[h], k_ref[r, c], None) for h, c in enumerate(heads)] for r in rows]
        for ti, (_, diagonal) in enumerate(tiles):
            if diagonal:
                s[ti] = [_mask_diagonal_tile(x) for x in s[ti]]
            carry = tuple(_flash_update(carry[h], s[ti][h], v_ref[rows[ti], c]) for h, c in enumerate(heads))
        return carry

    carry = lax.fori_loop(0, qi // 2, lambda p, c: steps([(2 * p, False), (2 * p + 1, False)], c),
                          (_flash_init(2 * t),) * HEADS_PER_STEP)
    carry = lax.cond(qi % 2 == 1,
                     lambda c: steps([(qi - 1, False), (qi, True)], c),
                     lambda c: steps([(qi, True)], c), carry)
    lam = _lambda_scalar(lq_ref[...], lam_init)
    for h, c in enumerate(heads):
        o_ref[:, c] = _attn_finish(carry[h], lam, g_ref[...], lam_init, t).astype(BF16)


def _prompt_attention(lq, g, q, kb, vb, lam_init):
    b, s, _ = q.shape
    t = ATTN_TILE
    w = HEADS_PER_STEP * LANES
    return pl.pallas_call(
        functools.partial(_prompt_attn_kernel, lam_init=lam_init),
        grid=(b, N_HEADS // HEADS_PER_STEP, s // t),
        in_specs=[
            pl.BlockSpec((4, HEAD_DIM), lambda b_, h, i: (0, 0)),
            pl.BlockSpec((V_DIM, 1), lambda b_, h, i: (0, 0)),
            pl.BlockSpec((None, t, w), lambda b_, h, i: (b_, i, h)),
            pl.BlockSpec((None, s, w), lambda b_, h, i: (b_, 0, h)),
            pl.BlockSpec((None, s, w), lambda b_, h, i: (b_, 0, h)),
        ],
        out_specs=pl.BlockSpec((None, t, w), lambda b_, h, i: (b_, i, h)),
        out_shape=jax.ShapeDtypeStruct((b, s, ATTN_WIDTH), BF16),
        scratch_shapes=[pltpu.VMEM((HEADS_PER_STEP, 2 * t, LANES), BF16)],
        compiler_params=pltpu.CompilerParams(dimension_semantics=("arbitrary",) * 3,
                                             vmem_limit_bytes=VMEM_LIMIT),
        name="prompt_attention",
    )(lq, g, q, kb, vb)


def _row_scores(qq, kt, *, keys_on_rows):
    dims = (((1,), (1,)), ((), ())) if keys_on_rows else (((1,), (0,)), ((), ()))
    return lax.dot_general(qq, kt, dims, preferred_element_type=F32)


def _row_flash_update(carry, s, vt):
    m, l, acc = carry
    m_new = jnp.maximum(m, jnp.max(s, axis=-1, keepdims=True))
    alpha = jnp.exp2(m - m_new)
    p = jnp.exp2(s - m_new)
    l = alpha * l + jnp.sum(p, axis=-1, keepdims=True)
    acc = alpha * acc + jnp.dot(p.astype(BF16), vt, preferred_element_type=F32)
    return m_new, l, acc


def _sample_attn_kernel(lq_ref, g_ref, q_ref, ck_ref, cv_ref, k_ref, v_ref, o_ref, *, lam_init, past_len):
    t = ATTN_TILE
    tq = q_ref.shape[0]
    heads = [slice(h * LANES, (h + 1) * LANES) for h in range(N_HEADS)]
    qq = [_stack_maps(q_ref[:, c]) for c in heads]

    def body(j, carry):
        cols = pl.ds(pl.multiple_of(j * t, t), t)
        s = [_row_scores(qq[h], ck_ref[c, cols].astype(BF16), keys_on_rows=False) for h, c in enumerate(heads)]
        return tuple(
            _row_flash_update(carry[h], s[h], cv_ref[pl.ds(j * (t * N_HEADS) + h, t, stride=N_HEADS), :].astype(BF16))
            for h in range(N_HEADS))

    init = (jnp.full((2 * tq, 1), -jnp.inf, F32), jnp.zeros((2 * tq, 1), F32), jnp.zeros((2 * tq, V_DIM), F32))
    carry = lax.fori_loop(0, past_len // t, body, (init,) * N_HEADS)
    lam = _lambda_scalar(lq_ref[...], lam_init)
    for h, c in enumerate(heads):
        s = _row_scores(qq[h], k_ref[:, c], keys_on_rows=True)
        _, l, acc = _row_flash_update(carry[h], s, v_ref[:, c])
        o = acc[:tq] / l[:tq] - lam * (acc[tq:] / l[tq:])
        o_ref[:, c] = (_rms(o, g_ref[...]) * (1.0 - lam_init)).astype(BF16)


def _sample_attention(lq, g, q, cache_kt, cache_v4, kb, vb, lam_init, layer):
    b, ls, _ = q.shape
    past_len = cache_kt.shape[-1]
    assert past_len % ATTN_TILE == 0
    return pl.pallas_call(
        functools.partial(_sample_attn_kernel, lam_init=lam_init, past_len=past_len),
        grid=(b,),
        in_specs=[
            pl.BlockSpec((4, HEAD_DIM), lambda b_: (0, 0)),
            pl.BlockSpec((1, V_DIM), lambda b_: (0, 0)),
            pl.BlockSpec((None, ls, ATTN_WIDTH), lambda b_: (b_, 0, 0)),
            pl.BlockSpec((None, None, ATTN_WIDTH, past_len), lambda b_: (layer, b_, 0, 0)),
            pl.BlockSpec((None, None, N_HEADS * past_len, V_DIM), lambda b_: (layer, b_, 0, 0)),
            pl.BlockSpec((None, ls, ATTN_WIDTH), lambda b_: (b_, 0, 0)),
            pl.BlockSpec((None, ls, ATTN_WIDTH), lambda b_: (b_, 0, 0)),
        ],
        out_specs=pl.BlockSpec((None, ls, ATTN_WIDTH), lambda b_: (b_, 0, 0)),
        out_shape=jax.ShapeDtypeStruct((b, ls, ATTN_WIDTH), BF16),
        compiler_params=pltpu.CompilerParams(dimension_semantics=("arbitrary",),
                                             vmem_limit_bytes=VMEM_LIMIT),
        name="sample_attention",
    )(lq, g, q, cache_kt, cache_v4, kb, vb)


def _pool_mixer(ext_p, pool_w_ref, pool_scale_ref, tm, pos0):
    assert POOL_WINDOWS == (2, 4, 8, 16) and HALO == 32
    e = ext_p[...]
    n = HALO + tm
    s2 = e[8:] + e[7:n - 1]
    s4 = s2[8:] + s2[6:n - 10]
    s8 = s4[8:] + s4[4:n - 20]
    sums = {2: s2[24:], 4: s4[16:], 8: s8[8:], 16: s8[8:] + s8[:n - 32]}
    u = e[HALO:]
    lane = lax.broadcasted_iota(jnp.int32, (tm, POOL_WIDTH), 1)
    group = lane // POOL_GROUP_DIM
    win = sums[POOL_WINDOWS[-1]]
    width = jnp.full((tm, POOL_WIDTH), POOL_WINDOWS[-1], jnp.int32)
    for gi in range(len(POOL_WINDOWS) - 2, -1, -1):
        win = jnp.where(group == gi, sums[POOL_WINDOWS[gi]], win)
        width = jnp.where(group == gi, POOL_WINDOWS[gi], width)
    pos = pos0 + lax.broadcasted_iota(jnp.int32, (tm, POOL_WIDTH), 0)
    count = jnp.minimum(pos + 1, width).astype(F32)
    d = (win / count - u).astype(BF16)
    return jnp.dot(d, pool_w_ref[...], preferred_element_type=F32) * pool_scale_ref[...]


CONV_ROW_BLOCK = 64


def _conv_mixer_steps(ext_c, shift_c, conv_out, dw_ref, dw_b_ref, ln_g_ref, ln_b_ref, pw_ref, tm, store):
    first_row = HALO - CONV_HIST
    blk = min(tm, CONV_ROW_BLOCK)

    def shift(residues):
        for r in residues:
            shift_c[r, 0:HALO - 8 + tm, :] = ext_c[pl.ds(r, HALO - 8 + tm), :]

    def taps(rb):
        yb = jnp.zeros((blk, CONV_WIDTH), F32)
        for k in range(CONV_K):
            r, a = (first_row + k) % 8, (first_row + k) // 8
            rows = pl.ds(8 * a + rb * blk, blk)
            src = ext_c[rows, :] if r == 0 else shift_c[r, rows, :]
            yb = yb + src * dw_ref[k:k + 1, :]
        conv_out[rb * blk:(rb + 1) * blk, :] = yb

    def finish():
        y = conv_out[...] + dw_b_ref[...]
        mu = jnp.mean(y, axis=-1, keepdims=True)
        yc = y - mu
        yn = yc * lax.rsqrt(jnp.mean(yc * yc, axis=-1, keepdims=True) + EPS) * ln_g_ref[...] + ln_b_ref[...]
        act = (yn * jax.nn.sigmoid(yn)).astype(BF16)
        store(jnp.dot(act, pw_ref[...], preferred_element_type=F32))

    steps = [functools.partial(shift, rs) for rs in ((1, 2), (3, 4, 5), (6, 7))]
    steps += [functools.partial(taps, rb) for rb in range(tm // blk)]
    return steps + [finish]


def _conv_mixer(ext_c, shift_c, conv_out, dw_ref, dw_b_ref, ln_g_ref, ln_b_ref, pw_ref, tm):
    out = []
    for step in _conv_mixer_steps(ext_c, shift_c, conv_out, dw_ref, dw_b_ref, ln_g_ref, ln_b_ref, pw_ref, tm,
                                  out.append):
        step()
    return out[0]


def _mixer_scratch(tm):
    return [pltpu.VMEM((HALO + tm, POOL_WIDTH), F32), pltpu.VMEM((HALO + tm, CONV_WIDTH), F32),
            pltpu.VMEM((8, HALO + tm, CONV_WIDTH), F32), pltpu.VMEM((tm, CONV_WIDTH), F32)]


def _mixer_weight_specs():
    return [_const_spec((POOL_WIDTH, POOL_WIDTH)), _const_spec((1, POOL_WIDTH)),
            _const_spec((CONV_K, CONV_WIDTH)), _const_spec((1, CONV_WIDTH)),
            _const_spec((1, CONV_WIDTH)), _const_spec((1, CONV_WIDTH)),
            _const_spec((CONV_WIDTH, CONV_WIDTH))]


def _sample_mixers_kernel(up_ref, up_hist_ref, uc_ref, uc_hist_ref,
                          pool_w_ref, pool_scale_ref, dw_ref, dw_b_ref, ln_g_ref, ln_b_ref, pw_ref,
                          ypool_ref, yconv_ref, ext_p, ext_c, shift_c, conv_out, *, pos_base):
    tm = up_ref.shape[0]
    ext_p[0:HALO, :] = up_hist_ref[...]
    ext_p[HALO:, :] = up_ref[...]
    ext_c[0:HALO, :] = uc_hist_ref[...]
    ext_c[HALO:, :] = uc_ref[...]
    ypool_ref[...] = _pool_mixer(ext_p, pool_w_ref, pool_scale_ref, tm, pos_base).astype(BF16)
    yconv_ref[...] = _conv_mixer(ext_c, shift_c, conv_out, dw_ref, dw_b_ref, ln_g_ref, ln_b_ref, pw_ref,
                                 tm).astype(BF16)


def _sample_mixers(upool, uconv, hist_pool, hist_conv, mixer_w, seq_len, pos_base):
    n = upool.shape[0]
    cur = pl.BlockSpec((seq_len, POOL_WIDTH), lambda i: (i, 0))
    hist = pl.BlockSpec((None, HALO, POOL_WIDTH), lambda i: (i, 0, 0))
    return pl.pallas_call(
        functools.partial(_sample_mixers_kernel, pos_base=pos_base),
        grid=(n // seq_len,),
        in_specs=[cur, hist, cur, hist] + _mixer_weight_specs(),
        out_specs=[cur, cur],
        out_shape=[jax.ShapeDtypeStruct((n, POOL_WIDTH), BF16), jax.ShapeDtypeStruct((n, CONV_WIDTH), BF16)],
        scratch_shapes=_mixer_scratch(seq_len),
        compiler_params=pltpu.CompilerParams(dimension_semantics=("arbitrary",),
                                             vmem_limit_bytes=VMEM_LIMIT),
        name="sample_mixers",
    )(upool, hist_pool, uconv, hist_conv, *mixer_w)


def _dense_math(x, ypool, o, yconv, wout_ref, g_ref, wgu_ref, wd_ref, fg_ref, final):
    o1 = POOL_WIDTH
    o2 = o1 + ATTN_WIDTH
    mix = (jnp.dot(ypool, wout_ref[0:o1, :], preferred_element_type=F32)
           + jnp.dot(o, wout_ref[o1:o2, :], preferred_element_type=F32)
           + jnp.dot(yconv, wout_ref[o2:, :], preferred_element_type=F32))
    x1 = x + mix
    h = _rms(x1, g_ref[...]).astype(BF16)
    gu = jnp.dot(h, wgu_ref[...], preferred_element_type=F32)
    gate = gu[:, :D_FF]
    a = (gate * jax.nn.sigmoid(gate) * gu[:, D_FF:]).astype(BF16)
    x2 = x1 + jnp.dot(a, wd_ref[...], preferred_element_type=F32)
    return _rms(x2, fg_ref[...]) if final else x2


def _dense_weight_specs():
    return [_const_spec((MIX_WIDTH, D_MODEL)), _const_spec((1, D_MODEL)),
            _const_spec((D_MODEL, 2 * D_FF)), _const_spec((D_FF, D_MODEL)), _const_spec((1, D_MODEL))]


def _prompt_dense_kernel(x_ref, o_ref, up0_ref, uc0_ref, up_ref, uc_ref, hist_p_ref, hist_c_ref,
                         pool_w_ref, pool_scale_ref, dw_ref, dw_b_ref, ln_g_ref, ln_b_ref, pw_ref,
                         wout_ref, g_ref, wgu_ref, wd_ref, fg_ref, out_ref,
                         ext_p, ext_c, shift_c, conv_out, y_pool, y_conv, *, final, tiles_per_seq):
    tm = x_ref.shape[0]
    i = pl.program_id(0)

    def mixer_steps(up_ref_, uc_ref_, tile, slot, first):
        def fill():
            if first is True:
                ext_p[0:HALO, :] = hist_p_ref[...]
                ext_c[0:HALO, :] = hist_c_ref[...]
            else:
                ext_p[0:HALO, :] = jnp.where(first, hist_p_ref[...], ext_p[tm:, :])
                ext_c[0:HALO, :] = jnp.where(first, hist_c_ref[...], ext_c[tm:, :])
            ext_p[HALO:, :] = up_ref_[...]
            ext_c[HALO:, :] = uc_ref_[...]

        def pool():
            pos0 = (tile % tiles_per_seq) * tm
            y_pool[slot] = _pool_mixer(ext_p, pool_w_ref, pool_scale_ref, tm, pos0).astype(BF16)

        def store_conv(y):
            y_conv[slot] = y.astype(BF16)

        return [fill] + _conv_mixer_steps(ext_c, shift_c, conv_out, dw_ref, dw_b_ref, ln_g_ref, ln_b_ref, pw_ref,
                                          tm, store_conv) + [pool]

    @pl.when(i == 0)
    def _():
        for step in mixer_steps(up0_ref, uc0_ref, 0, 0, True):
            step()

    slot = i % 2
    nxt = jnp.minimum(i + 1, pl.num_programs(0) - 1)
    out_ref[...] = _dense_math(x_ref[...], y_pool[slot], o_ref[...], y_conv[slot],
                               wout_ref, g_ref, wgu_ref, wd_ref, fg_ref, final)
    for step in mixer_steps(up_ref, uc_ref, nxt, 1 - slot, nxt % tiles_per_seq == 0):
        step()


def _prompt_dense(x, o, upool, uconv, hist_pool, hist_conv, mixer_w, dense_w, tm, seq_len, final):
    n = x.shape[0]
    tps = seq_len // tm
    last = n // tm - 1
    row = lambda w: pl.BlockSpec((tm, w), lambda i: (i, 0))
    first_tile = lambda w: pl.BlockSpec((tm, w), lambda i: (0, 0))
    next_tile = lambda w: pl.BlockSpec((tm, w), lambda i: (jnp.minimum(i + 1, last), 0))
    next_seq = lambda w: pl.BlockSpec((None, HALO, w), lambda i: (jnp.minimum(i + 1, last) // tps, 0, 0))
    return pl.pallas_call(
        functools.partial(_prompt_dense_kernel, final=final, tiles_per_seq=tps),
        grid=(n // tm,),
        in_specs=[row(D_MODEL), row(ATTN_WIDTH), first_tile(POOL_WIDTH), first_tile(CONV_WIDTH),
                  next_tile(POOL_WIDTH), next_tile(CONV_WIDTH), next_seq(POOL_WIDTH), next_seq(CONV_WIDTH)]
        + _mixer_weight_specs() + _dense_weight_specs(),
        out_specs=row(D_MODEL),
        out_shape=jax.ShapeDtypeStruct((n, D_MODEL), F32),
        scratch_shapes=_mixer_scratch(tm) + [pltpu.VMEM((2, tm, POOL_WIDTH), BF16),
                                             pltpu.VMEM((2, tm, CONV_WIDTH), BF16)],
        compiler_params=pltpu.CompilerParams(dimension_semantics=("arbitrary",),
                                             vmem_limit_bytes=VMEM_LIMIT),
        name="prompt_dense",
    )(x, o, upool, uconv, upool, uconv, hist_pool, hist_conv, *mixer_w, *dense_w)


def _sample_dense_kernel(x_ref, yp_ref, o_ref, yc_ref, wout_ref, g_ref, wgu_ref, wd_ref, fg_ref, out_ref, *, final):
    out_ref[...] = _dense_math(x_ref[...], yp_ref[...], o_ref[...], yc_ref[...],
                               wout_ref, g_ref, wgu_ref, wd_ref, fg_ref, final)


def _sample_dense(x, ypool, o, yconv, wout_bf, g, wgu_bf, wd_bf, fg, tm, final):
    n = x.shape[0]
    row = lambda w: pl.BlockSpec((tm, w), lambda i: (i, 0))
    return pl.pallas_call(
        functools.partial(_sample_dense_kernel, final=final),
        grid=(n // tm,),
        in_specs=[row(D_MODEL), row(POOL_WIDTH), row(ATTN_WIDTH), row(CONV_WIDTH)] + _dense_weight_specs(),
        out_specs=row(D_MODEL),
        out_shape=jax.ShapeDtypeStruct((n, D_MODEL), F32),
        compiler_params=pltpu.CompilerParams(dimension_semantics=("arbitrary",),
                                             vmem_limit_bytes=VMEM_LIMIT),
        name="sample_dense",
    )(x, ypool, o, yconv, wout_bf, g, wgu_bf, wd_bf, fg)


def _rope_tables(pos):
    half = HEAD_DIM // 2
    inv = ROPE_THETA ** (-jnp.arange(half, dtype=F32) / half)
    ang = pos.astype(F32)[:, None] * inv[None, :]
    reps = LANES // half
    cos = jnp.tile(jnp.cos(ang), (1, reps))
    sign = jnp.tile(jnp.concatenate([-jnp.ones((half,), F32), jnp.ones((half,), F32)]), LANES // HEAD_DIM)
    sin = jnp.tile(jnp.sin(ang), (1, reps)) * sign[None, :]
    return cos, sin


def _block_diag(pool_w):
    g, d, _ = pool_w.shape
    out = jnp.zeros((g * d, g * d), pool_w.dtype)
    for i in range(g):
        out = out.at[i * d:(i + 1) * d, i * d:(i + 1) * d].set(pool_w[i])
    return out


def _pad_hist(h):
    return jnp.pad(h, ((0, 0), (HALO - h.shape[1], 0), (0, 0)))


def kernel(x_prompt, x_sample, cache_k, cache_v, state_pool, state_conv, norm_mix_g, w_in, pool_w, pool_scale,
           lambda_qk, diff_norm_g, conv_dw, conv_dw_b, conv_ln_g, conv_ln_b, conv_pw, w_out, norm_ffn_g,
           w_gate_up, w_down, final_norm_g):
    B, S, _ = x_prompt.shape
    Bs, Ls, _ = x_sample.shape
    depth = w_in.shape[0]
    past_len = cache_k.shape[2]
    assert S % ATTN_TILE == 0 and S % ROW_TILE == 0 and (Bs * Ls) % 16 == 0 and Ls % 16 == 0 and Ls >= CONV_HIST

    cos_p, sin_p = _rope_tables(jnp.arange(S, dtype=jnp.int32))
    cos_s, sin_s = _rope_tables(past_len + jnp.arange(Ls, dtype=jnp.int32))
    cos_s, sin_s = jnp.tile(cos_s, (Bs, 1)), jnp.tile(sin_s, (Bs, 1))
    zero_pool = jnp.zeros((B, HALO, POOL_WIDTH), F32)
    zero_conv = jnp.zeros((B, HALO, CONV_WIDTH), F32)

    xp = x_prompt.reshape(B * S, D_MODEL)
    xs = x_sample.reshape(Bs * Ls, D_MODEL)
    row2 = lambda a: a.reshape(1, -1)
    fg = row2(final_norm_g)
    cache_kt = jnp.transpose(cache_k, (0, 1, 3, 4, 5, 2)).reshape(depth, Bs, ATTN_WIDTH, past_len)
    cache_v4 = cache_v.reshape(depth, Bs, past_len * N_HEADS, V_DIM)
    kv_stack = depth
    pp, cp = [], []
    ks_, vs_, ps_, cs_ = [], [], [], []
    for l in range(depth):
        lam_init = 0.8 - 0.6 * math.exp(-0.3 * l)
        w_in_bf = w_in[l].astype(BF16)
        w_out_bf = w_out[l].astype(BF16)
        w_gu_bf = w_gate_up[l].astype(BF16)
        w_d_bf = w_down[l].astype(BF16)
        pw_bf = conv_pw[l].astype(BF16)
        pool_bd = _block_diag(pool_w[l]).astype(BF16)
        mixer_w = (pool_bd, row2(pool_scale[l]), conv_dw[l], row2(conv_dw_b[l]), row2(conv_ln_g[l]),
                   row2(conv_ln_b[l]), pw_bf)
        dense_w = (w_out_bf, row2(norm_ffn_g[l]), w_gu_bf, w_d_bf, fg)
        final = l == depth - 1

        upool, q, k, kb, v, vb, uconv = _prompt_in_proj(xp, row2(norm_mix_g[l]), w_in_bf, cos_p, sin_p, ROW_TILE, S,
                                                        kv_stack, l)
        kv_stack = (k, v)
        o = _prompt_attention(lambda_qk[l], diff_norm_g[l].reshape(-1, 1), q.reshape(B, S, -1), kb.reshape(B, S, -1),
                              vb.reshape(B, S, -1), lam_init)
        xp = _prompt_dense(xp, o.reshape(B * S, -1), upool, uconv, zero_pool, zero_conv, mixer_w, dense_w,
                           ROW_TILE, S, final)
        pp.append(upool.reshape(B, S, -1)[:, S - POOL_HIST:])
        cp.append(uconv.reshape(B, S, -1)[:, S - CONV_HIST:])

        upool, q, k, kb, v, vb, uconv = _sample_in_proj(xs, row2(norm_mix_g[l]), w_in_bf, cos_s, sin_s)
        o = _sample_attention(lambda_qk[l], row2(diff_norm_g[l]), q.reshape(Bs, Ls, -1), cache_kt, cache_v4,
                              kb.reshape(Bs, Ls, -1), vb.reshape(Bs, Ls, -1), lam_init, l)
        ypool, yconv = _sample_mixers(upool, uconv, _pad_hist(state_pool[l]), _pad_hist(state_conv[l]), mixer_w,
                                      Ls, past_len)
        xs = _sample_dense(xs, ypool, o.reshape(Bs * Ls, -1), yconv, *dense_w, Bs * Ls, final)
        ks_.append(k.reshape(Bs, Ls, N_HEADS, 2, HEAD_DIM))
        vs_.append(v.reshape(Bs, Ls, N_HEADS, V_DIM))
        ps_.append(upool.reshape(Bs, Ls, -1)[:, Ls - POOL_HIST:])
        cs_.append(uconv.reshape(Bs, Ls, -1)[:, Ls - CONV_HIST:])

    kt_all, v4_all = kv_stack
    new_k = jnp.transpose(kt_all.reshape(depth, B, N_HEADS, 2, HEAD_DIM, S), (0, 1, 5, 2, 3, 4))
    new_v = v4_all.reshape(depth, B, S, N_HEADS, V_DIM)
    return (xp.reshape(B, S, D_MODEL), xs.reshape(Bs, Ls, D_MODEL),
            new_k, new_v, jnp.stack(pp), jnp.stack(cp),
            jnp.stack(ks_), jnp.stack(vs_), jnp.stack(ps_), jnp.stack(cs_))
```

```python
import functools
import math

import jax
import jax.numpy as jnp
from jax import lax
from jax.experimental import pallas as pl
from jax.experimental.pallas import tpu as pltpu

D_MODEL = 1024
CHUNK = 64
POOL_WIDTH = D_MODEL // 4
POOL_WINDOWS = (2, 4, 8, 16)
POOL_GROUP_DIM = POOL_WIDTH // len(POOL_WINDOWS)
POOL_HIST = max(POOL_WINDOWS) - 1
ATTN_WIDTH = D_MODEL // 2
N_HEADS = 4
HEAD_DIM = ATTN_WIDTH // (2 * N_HEADS)
V_DIM = 2 * HEAD_DIM
ROPE_THETA = 10000.0
CONV_WIDTH = D_MODEL // 4
CONV_K = 31
CONV_HIST = CONV_K - 1
MIX_WIDTH = POOL_WIDTH + ATTN_WIDTH + CONV_WIDTH
IN_WIDTH = POOL_WIDTH + 3 * ATTN_WIDTH + 2 * CONV_WIDTH
D_FF = ((-(-8 * D_MODEL // 3) + 255) // 256) * 256
EPS = 1e-6

LANES = 128
HALO = 32
ROW_TILE = 512
ATTN_TILE = 256
HEADS_PER_STEP = 4
VMEM_LIMIT = 56 * 1024 * 1024

F32 = jnp.float32
BF16 = jnp.bfloat16


def _rms(x, g):
    return x * lax.rsqrt(jnp.mean(x * x, axis=-1, keepdims=True) + EPS) * g


def _const_spec(shape):
    return pl.BlockSpec(shape, lambda *_: (0,) * len(shape), pipeline_mode=pl.Buffered(1))


def _layer_spec(shape, layer):
    return pl.BlockSpec((None,) + shape, lambda *_: (layer,) + (0,) * len(shape), pipeline_mode=pl.Buffered(1))


O_Q = POOL_WIDTH
O_K = O_Q + ATTN_WIDTH
O_V = O_K + ATTN_WIDTH
O_A = O_V + ATTN_WIDTH
O_B = O_A + CONV_WIDTH


def _qkv_epilogue(z, cos_ref, sin_ref, q_ref, k_ref, kb_ref, v_ref, vb_ref, cache_layout):
    tm = z.shape[0]
    o1, o2, o3 = O_Q, O_K, O_V
    cos = cos_ref[...]
    sin = sin_ref[...]
    lane = lax.broadcasted_iota(jnp.int32, cos.shape, 1)
    first_half = (lane & (HEAD_DIM // 2)) == 0

    def rope(t):
        up = pltpu.roll(t, LANES - HEAD_DIM // 2, axis=1)
        down = pltpu.roll(t, HEAD_DIM // 2, axis=1)
        return t * cos + jnp.where(first_half, up, down) * sin

    scale = HEAD_DIM ** -0.5 * math.log2(math.e)
    for j in range(ATTN_WIDTH // LANES):
        sl = slice(j * LANES, (j + 1) * LANES)
        qj = rope(z[:, o1 + j * LANES:o1 + (j + 1) * LANES])
        kj = rope(z[:, o2 + j * LANES:o2 + (j + 1) * LANES])
        vj = z[:, o3 + j * LANES:o3 + (j + 1) * LANES]
        q_ref[:, sl] = (qj * scale).astype(BF16)
        kb_ref[:, sl] = kj.astype(BF16)
        vb_ref[:, sl] = vj.astype(BF16)
        if cache_layout:
            k_ref[sl, :] = kj.T
            v_ref[pl.ds(j, tm, stride=N_HEADS), :] = vj
        else:
            k_ref[:, sl] = kj
            v_ref[:, sl] = vj


def _sample_in_proj_kernel(x_ref, g_ref, w_ref, cos_ref, sin_ref,
                           upool_ref, q_ref, k_ref, kb_ref, v_ref, vb_ref, uconv_ref):
    h = _rms(x_ref[...], g_ref[...]).astype(BF16)
    z = jnp.dot(h, w_ref[...], preferred_element_type=F32)
    upool_ref[...] = z[:, :O_Q]
    uconv_ref[...] = z[:, O_A:O_B] * jax.nn.sigmoid(z[:, O_B:])
    _qkv_epilogue(z, cos_ref, sin_ref, q_ref, k_ref, kb_ref, v_ref, vb_ref, False)


def _sample_in_proj(x, g, w_bf, cos_tab, sin_tab, layer):
    n = x.shape[0]
    full = lambda w: pl.BlockSpec((n, w), lambda i: (0, 0))
    flat = lambda dt: jax.ShapeDtypeStruct((n, ATTN_WIDTH), dt)
    return pl.pallas_call(
        _sample_in_proj_kernel,
        grid=(1,),
        in_specs=[full(D_MODEL), _layer_spec((1, D_MODEL), layer), _layer_spec((D_MODEL, IN_WIDTH), layer),
                  full(LANES), full(LANES)],
        out_specs=[full(POOL_WIDTH)] + [full(ATTN_WIDTH)] * 5 + [full(CONV_WIDTH)],
        out_shape=[jax.ShapeDtypeStruct((n, POOL_WIDTH), F32), flat(BF16), flat(F32), flat(BF16), flat(F32),
                   flat(BF16), jax.ShapeDtypeStruct((n, CONV_WIDTH), F32)],
        compiler_params=pltpu.CompilerParams(dimension_semantics=("arbitrary",),
                                             vmem_limit_bytes=VMEM_LIMIT),
        name="sample_in_proj",
    )(x, g, w_bf, cos_tab, sin_tab)


def _prompt_in_proj_kernel(*refs, n_unused):
    x_ref, g_ref, w_ref, cos_ref, sin_ref = refs[:5]
    upool_ref, q_ref, k_ref, kb_ref, v_ref, vb_ref, uconv_ref = refs[5 + n_unused:]
    h = _rms(x_ref[...], g_ref[...]).astype(BF16)
    z = jnp.dot(h, w_ref[...], preferred_element_type=F32)
    upool_ref[...] = z[:, :O_Q]
    uconv_ref[...] = z[:, O_A:O_B] * jax.nn.sigmoid(z[:, O_B:])
    _qkv_epilogue(z, cos_ref, sin_ref, q_ref, k_ref, kb_ref, v_ref, vb_ref, True)


def _prompt_in_proj(x, g, w_bf, cos_tab, sin_tab, tm, seq_len, kv_stack, layer):
    n = x.shape[0]
    tps = seq_len // tm
    batch = n // seq_len
    row = lambda w: pl.BlockSpec((tm, w), lambda i: (i, 0))
    tab = pl.BlockSpec((tm, LANES), lambda i: (i % tps, 0))
    flat = lambda dt, w=ATTN_WIDTH: jax.ShapeDtypeStruct((n, w), dt)
    extra_in, extra_specs, aliases = [], [], {}
    if isinstance(kv_stack, int):
        k_shape = jax.ShapeDtypeStruct((kv_stack, batch, ATTN_WIDTH, seq_len), F32)
        v_shape = jax.ShapeDtypeStruct((kv_stack, batch, N_HEADS * seq_len, V_DIM), F32)
    else:
        k_shape, v_shape = (jax.ShapeDtypeStruct(a.shape, a.dtype) for a in kv_stack)
        extra_in = list(kv_stack)
        extra_specs = [pl.BlockSpec(memory_space=pl.ANY)] * 2
        aliases = {5: 2, 6: 4}
    k_spec = pl.BlockSpec((None, None, ATTN_WIDTH, tm), lambda i: (layer, i // tps, 0, i % tps))
    v_spec = pl.BlockSpec((None, None, N_HEADS * tm, V_DIM), lambda i: (layer, i // tps, i % tps, 0))
    return pl.pallas_call(
        functools.partial(_prompt_in_proj_kernel, n_unused=len(extra_in)),
        grid=(n // tm,),
        in_specs=[row(D_MODEL), _layer_spec((1, D_MODEL), layer), _layer_spec((D_MODEL, IN_WIDTH), layer), tab, tab]
        + extra_specs,
        out_specs=[row(POOL_WIDTH), row(ATTN_WIDTH), k_spec, row(ATTN_WIDTH), v_spec, row(ATTN_WIDTH),
                   row(CONV_WIDTH)],
        out_shape=[flat(F32, POOL_WIDTH), flat(BF16), k_shape, flat(BF16), v_shape, flat(BF16),
                   flat(F32, CONV_WIDTH)],
        input_output_aliases=aliases,
        compiler_params=pltpu.CompilerParams(dimension_semantics=("arbitrary",),
                                             vmem_limit_bytes=VMEM_LIMIT),
        name="prompt_in_proj",
    )(x, g, w_bf, cos_tab, sin_tab, *extra_in)


def _lambda_scalar(lq, lam_init):
    a = jnp.sum(lq[0:1, :] * lq[1:2, :], axis=-1, keepdims=True)
    b = jnp.sum(lq[2:3, :] * lq[3:4, :], axis=-1, keepdims=True)
    return jnp.exp(a) - jnp.exp(b) + lam_init


def _stack_maps(q):
    lane = lax.broadcasted_iota(jnp.int32, q.shape, 1)
    zero = jnp.zeros_like(q)
    return jnp.concatenate([jnp.where(lane < HEAD_DIM, q, zero), jnp.where(lane >= HEAD_DIM, q, zero)], axis=0)


def _flash_step(carry, qq, kt, vt, mask):
    return _flash_update(carry, _flash_scores(qq, kt, mask), vt)


def _flash_scores(qq, kt, mask):
    s = lax.dot_general(kt, qq, (((1,), (1,)), ((), ())), preferred_element_type=F32)
    if mask is not None:
        s = jnp.where(mask, s, -1e30)
    return s


def _flash_update(carry, s, vt):
    m, l, acc = carry
    m_new = jnp.maximum(m, jnp.max(s, axis=0, keepdims=True))
    alpha = jnp.exp2(m - m_new)
    p = jnp.exp2(s - m_new)
    l = alpha * l + jnp.sum(p, axis=0, keepdims=True)
    pv = lax.dot_general(vt, p.astype(BF16), (((0,), (0,)), ((), ())), preferred_element_type=F32)
    return m_new, l, alpha * acc + pv


def _flash_init(cols):
    return (jnp.full((1, cols), -jnp.inf, F32), jnp.zeros((1, cols), F32), jnp.zeros((V_DIM, cols), F32))


def _attn_finish(carry, lam, g_col, lam_init, tq):
    _, l, acc = carry
    o = acc[:, :tq] / l[:, :tq] - lam * (acc[:, tq:] / l[:, tq:])
    o = o * lax.rsqrt(jnp.mean(o * o, axis=0, keepdims=True) + EPS) * g_col
    return (o * (1.0 - lam_init)).T


def _mask_diagonal_tile(s):
    t = s.shape[0]
    q_chunk = (lax.broadcasted_iota(jnp.int32, (1, s.shape[1]), 1) % t) // CHUNK
    blocks = [jnp.where(q_chunk >= a, s[a * CHUNK:(a + 1) * CHUNK, :], -1e30) for a in range(t // CHUNK)]
    return jnp.concatenate(blocks, axis=0)


def _prompt_attn_kernel(lq_ref, g_ref, q_ref, k_ref, v_ref, o_ref, qq_ref, *, lam_init):
    t = ATTN_TILE
    qi = pl.program_id(2)
    heads = [slice(h * LANES, (h + 1) * LANES) for h in range(HEADS_PER_STEP)]
    for h, c in enumerate(heads):
        qq_ref[h] = _stack_maps(q_ref[:, c])

    def steps(tiles, carry):
        rows = [pl.ds(pl.multiple_of(j * t, t), t) for j, _ in tiles]
        s = [[_flash_scores(qq_ref[h], k_ref[r, c], None) for h, c in enumerate(heads)] for r in rows]
        for ti, (_, diagonal) in enumerate(tiles):
            if diagonal:
                s[ti] = [_mask_diagonal_tile(x) for x in s[ti]]
            carry = tuple(_flash_update(carry[h], s[ti][h], v_ref[rows[ti], c]) for h, c in enumerate(heads))
        return carry

    carry = lax.fori_loop(0, qi // 2, lambda p, c: steps([(2 * p, False), (2 * p + 1, False)], c),
                          (_flash_init(2 * t),) * HEADS_PER_STEP)
    carry = lax.cond(qi % 2 == 1,
                     lambda c: steps([(qi - 1, False), (qi, True)], c),
                     lambda c: steps([(qi, True)], c), carry)
    lam = _lambda_scalar(lq_ref[...], lam_init)
    for h, c in enumerate(heads):
        o_ref[:, c] = _attn_finish(carry[h], lam, g_ref[...], lam_init, t).astype(BF16)


def _prompt_attention(lq, g, q, kb, vb, lam_init, layer):
    b, s, _ = q.shape
    t = ATTN_TILE
    w = HEADS_PER_STEP * LANES
    return pl.pallas_call(
        functools.partial(_prompt_attn_kernel, lam_init=lam_init),
        grid=(b, N_HEADS // HEADS_PER_STEP, s // t),
        in_specs=[
            _layer_spec((4, HEAD_DIM), layer),
            _layer_spec((V_DIM, 1), layer),
            pl.BlockSpec((None, t, w), lambda b_, h, i: (b_, i, h)),
            pl.BlockSpec((None, s, w), lambda b_, h, i: (b_, 0, h)),
            pl.BlockSpec((None, s, w), lambda b_, h, i: (b_, 0, h)),
        ],
        out_specs=pl.BlockSpec((None, t, w), lambda b_, h, i: (b_, i, h)),
        out_shape=jax.ShapeDtypeStruct((b, s, ATTN_WIDTH), BF16),
        scratch_shapes=[pltpu.VMEM((HEADS_PER_STEP, 2 * t, LANES), BF16)],
        compiler_params=pltpu.CompilerParams(dimension_semantics=("arbitrary",) * 3,
                                             vmem_limit_bytes=VMEM_LIMIT),
        name="prompt_attention",
    )(lq, g, q, kb, vb)


def _row_scores(qq, kt, *, keys_on_rows):
    dims = (((1,), (1,)), ((), ())) if keys_on_rows else (((1,), (0,)), ((), ()))
    return lax.dot_general(qq, kt, dims, preferred_element_type=F32)


def _row_flash_update(carry, s, vt):
    m, l, acc = carry
    m_new = jnp.maximum(m, jnp.max(s, axis=-1, keepdims=True))
    alpha = jnp.exp2(m - m_new)
    p = jnp.exp2(s - m_new)
    l = alpha * l + jnp.sum(p, axis=-1, keepdims=True)
    acc = alpha * acc + jnp.dot(p.astype(BF16), vt, preferred_element_type=F32)
    return m_new, l, acc


def _sample_attn_kernel(lq_ref, g_ref, q_ref, ck_ref, cv_ref, k_ref, v_ref, o_ref, *, lam_init, past_len):
    t = ATTN_TILE
    tq = q_ref.shape[0]
    heads = [slice(h * LANES, (h + 1) * LANES) for h in range(N_HEADS)]
    qq = [_stack_maps(q_ref[:, c]) for c in heads]

    def body(j, carry):
        cols = pl.ds(pl.multiple_of(j * t, t), t)
        s = [_row_scores(qq[h], ck_ref[c, cols].astype(BF16), keys_on_rows=False) for h, c in enumerate(heads)]
        return tuple(
            _row_flash_update(carry[h], s[h], cv_ref[pl.ds(j * (t * N_HEADS) + h, t, stride=N_HEADS), :].astype(BF16))
            for h in range(N_HEADS))

    init = (jnp.full((2 * tq, 1), -jnp.inf, F32), jnp.zeros((2 * tq, 1), F32), jnp.zeros((2 * tq, V_DIM), F32))
    carry = lax.fori_loop(0, past_len // t, body, (init,) * N_HEADS)
    lam = _lambda_scalar(lq_ref[...], lam_init)
    for h, c in enumerate(heads):
        s = _row_scores(qq[h], k_ref[:, c], keys_on_rows=True)
        _, l, acc = _row_flash_update(carry[h], s, v_ref[:, c])
        o = acc[:tq] / l[:tq] - lam * (acc[tq:] / l[tq:])
        o_ref[:, c] = (_rms(o, g_ref[...]) * (1.0 - lam_init)).astype(BF16)


def _sample_attention(lq, g, q, cache_kt, cache_v4, kb, vb, lam_init, layer):
    b, ls, _ = q.shape
    past_len = cache_kt.shape[-1]
    assert past_len % ATTN_TILE == 0
    return pl.pallas_call(
        functools.partial(_sample_attn_kernel, lam_init=lam_init, past_len=past_len),
        grid=(b,),
        in_specs=[
            _layer_spec((4, HEAD_DIM), layer),
            _layer_spec((1, V_DIM), layer),
            pl.BlockSpec((None, ls, ATTN_WIDTH), lambda b_: (b_, 0, 0)),
            pl.BlockSpec((None, None, ATTN_WIDTH, past_len), lambda b_: (layer, b_, 0, 0)),
            pl.BlockSpec((None, None, N_HEADS * past_len, V_DIM), lambda b_: (layer, b_, 0, 0)),
            pl.BlockSpec((None, ls, ATTN_WIDTH), lambda b_: (b_, 0, 0)),
            pl.BlockSpec((None, ls, ATTN_WIDTH), lambda b_: (b_, 0, 0)),
        ],
        out_specs=pl.BlockSpec((None, ls, ATTN_WIDTH), lambda b_: (b_, 0, 0)),
        out_shape=jax.ShapeDtypeStruct((b, ls, ATTN_WIDTH), BF16),
        compiler_params=pltpu.CompilerParams(dimension_semantics=("arbitrary",),
                                             vmem_limit_bytes=VMEM_LIMIT),
        name="sample_attention",
    )(lq, g, q, cache_kt, cache_v4, kb, vb)


def _pool_mixer(ext_p, pool_w_ref, pool_scale_ref, tm, pos0):
    assert POOL_WINDOWS == (2, 4, 8, 16) and HALO == 32
    e = ext_p[...]
    n = HALO + tm
    s2 = e[8:] + e[7:n - 1]
    s4 = s2[8:] + s2[6:n - 10]
    s8 = s4[8:] + s4[4:n - 20]
    sums = {2: s2[24:], 4: s4[16:], 8: s8[8:], 16: s8[8:] + s8[:n - 32]}
    u = e[HALO:]
    lane = lax.broadcasted_iota(jnp.int32, (tm, POOL_WIDTH), 1)
    group = lane // POOL_GROUP_DIM
    win = sums[POOL_WINDOWS[-1]]
    width = jnp.full((tm, POOL_WIDTH), POOL_WINDOWS[-1], jnp.int32)
    for gi in range(len(POOL_WINDOWS) - 2, -1, -1):
        win = jnp.where(group == gi, sums[POOL_WINDOWS[gi]], win)
        width = jnp.where(group == gi, POOL_WINDOWS[gi], width)
    pos = pos0 + lax.broadcasted_iota(jnp.int32, (tm, POOL_WIDTH), 0)
    count = jnp.minimum(pos + 1, width).astype(F32)
    d = (win / count - u).astype(BF16)
    return jnp.dot(d, pool_w_ref[...], preferred_element_type=F32) * pool_scale_ref[...]


CONV_ROW_BLOCK = 64


def _conv_mixer_steps(ext_c, shift_c, conv_out, dw_ref, dw_b_ref, ln_g_ref, ln_b_ref, pw_ref, tm, store):
    first_row = HALO - CONV_HIST
    blk = min(tm, CONV_ROW_BLOCK)

    def shift(residues):
        for r in residues:
            shift_c[r, 0:HALO - 8 + tm, :] = ext_c[pl.ds(r, HALO - 8 + tm), :]

    def taps(rb):
        yb = jnp.zeros((blk, CONV_WIDTH), F32)
        for k in range(CONV_K):
            r, a = (first_row + k) % 8, (first_row + k) // 8
            rows = pl.ds(8 * a + rb * blk, blk)
            src = ext_c[rows, :] if r == 0 else shift_c[r, rows, :]
            yb = yb + src * dw_ref[k:k + 1, :]
        conv_out[rb * blk:(rb + 1) * blk, :] = yb

    def finish():
        y = conv_out[...] + dw_b_ref[...]
        mu = jnp.mean(y, axis=-1, keepdims=True)
        yc = y - mu
        yn = yc * lax.rsqrt(jnp.mean(yc * yc, axis=-1, keepdims=True) + EPS) * ln_g_ref[...] + ln_b_ref[...]
        act = (yn * jax.nn.sigmoid(yn)).astype(BF16)
        store(jnp.dot(act, pw_ref[...], preferred_element_type=F32))

    steps = [functools.partial(shift, rs) for rs in ((1, 2), (3, 4, 5), (6, 7))]
    steps += [functools.partial(taps, rb) for rb in range(tm // blk)]
    return steps + [finish]


def _conv_mixer(ext_c, shift_c, conv_out, dw_ref, dw_b_ref, ln_g_ref, ln_b_ref, pw_ref, tm):
    out = []
    for step in _conv_mixer_steps(ext_c, shift_c, conv_out, dw_ref, dw_b_ref, ln_g_ref, ln_b_ref, pw_ref, tm,
                                  out.append):
        step()
    return out[0]


def _mixer_scratch(tm):
    return [pltpu.VMEM((HALO + tm, POOL_WIDTH), F32), pltpu.VMEM((HALO + tm, CONV_WIDTH), F32),
            pltpu.VMEM((8, HALO + tm, CONV_WIDTH), F32), pltpu.VMEM((tm, CONV_WIDTH), F32)]


def _mixer_weight_specs(layer):
    shapes = [(POOL_WIDTH, POOL_WIDTH), (1, POOL_WIDTH), (CONV_K, CONV_WIDTH), (1, CONV_WIDTH), (1, CONV_WIDTH),
              (1, CONV_WIDTH), (CONV_WIDTH, CONV_WIDTH)]
    return [_layer_spec(shape, layer) for shape in shapes]


def _sample_mixers_kernel(up_ref, up_hist_ref, uc_ref, uc_hist_ref,
                          pool_w_ref, pool_scale_ref, dw_ref, dw_b_ref, ln_g_ref, ln_b_ref, pw_ref,
                          ypool_ref, yconv_ref, ext_p, ext_c, shift_c, conv_out, *, pos_base):
    tm = up_ref.shape[0]
    ext_p[0:HALO, :] = up_hist_ref[...]
    ext_p[HALO:, :] = up_ref[...]
    ext_c[0:HALO, :] = uc_hist_ref[...]
    ext_c[HALO:, :] = uc_ref[...]
    ypool_ref[...] = _pool_mixer(ext_p, pool_w_ref, pool_scale_ref, tm, pos_base).astype(BF16)
    yconv_ref[...] = _conv_mixer(ext_c, shift_c, conv_out, dw_ref, dw_b_ref, ln_g_ref, ln_b_ref, pw_ref,
                                 tm).astype(BF16)


def _sample_mixers(upool, uconv, hist_pool, hist_conv, mixer_w, seq_len, pos_base, layer):
    n = upool.shape[0]
    cur = pl.BlockSpec((seq_len, POOL_WIDTH), lambda i: (i, 0))
    hist = pl.BlockSpec((None, None, HALO, POOL_WIDTH), lambda i: (layer, i, 0, 0))
    return pl.pallas_call(
        functools.partial(_sample_mixers_kernel, pos_base=pos_base),
        grid=(n // seq_len,),
        in_specs=[cur, hist, cur, hist] + _mixer_weight_specs(layer),
        out_specs=[cur, cur],
        out_shape=[jax.ShapeDtypeStruct((n, POOL_WIDTH), BF16), jax.ShapeDtypeStruct((n, CONV_WIDTH), BF16)],
        scratch_shapes=_mixer_scratch(seq_len),
        compiler_params=pltpu.CompilerParams(dimension_semantics=("arbitrary",),
                                             vmem_limit_bytes=VMEM_LIMIT),
        name="sample_mixers",
    )(upool, hist_pool, uconv, hist_conv, *mixer_w)


def _dense_math(x, ypool, o, yconv, wout_ref, g_ref, wgu_ref, wd_ref, fg_ref, final):
    o1 = POOL_WIDTH
    o2 = o1 + ATTN_WIDTH
    mix = (jnp.dot(ypool, wout_ref[0:o1, :], preferred_element_type=F32)
           + jnp.dot(o, wout_ref[o1:o2, :], preferred_element_type=F32)
           + jnp.dot(yconv, wout_ref[o2:, :], preferred_element_type=F32))
    x1 = x + mix
    h = _rms(x1, g_ref[...]).astype(BF16)
    gu = jnp.dot(h, wgu_ref[...], preferred_element_type=F32)
    gate = gu[:, :D_FF]
    a = (gate * jax.nn.sigmoid(gate) * gu[:, D_FF:]).astype(BF16)
    x2 = x1 + jnp.dot(a, wd_ref[...], preferred_element_type=F32)
    return _rms(x2, fg_ref[...]) if final else x2


def _dense_weight_specs(layer):
    shapes = [(MIX_WIDTH, D_MODEL), (1, D_MODEL), (D_MODEL, 2 * D_FF), (D_FF, D_MODEL)]
    return [_layer_spec(shape, layer) for shape in shapes] + [_const_spec((1, D_MODEL))]


def _prompt_dense_kernel(x_ref, o_ref, up0_ref, uc0_ref, up_ref, uc_ref, hist_p_ref, hist_c_ref,
                         pool_w_ref, pool_scale_ref, dw_ref, dw_b_ref, ln_g_ref, ln_b_ref, pw_ref,
                         wout_ref, g_ref, wgu_ref, wd_ref, fg_ref, out_ref,
                         ext_p, ext_c, shift_c, conv_out, y_pool, y_conv, *, final, tiles_per_seq):
    tm = x_ref.shape[0]
    i = pl.program_id(0)

    def mixer_steps(up_ref_, uc_ref_, tile, slot, first):
        def fill():
            if first is True:
                ext_p[0:HALO, :] = hist_p_ref[...]
                ext_c[0:HALO, :] = hist_c_ref[...]
            else:
                ext_p[0:HALO, :] = jnp.where(first, hist_p_ref[...], ext_p[tm:, :])
                ext_c[0:HALO, :] = jnp.where(first, hist_c_ref[...], ext_c[tm:, :])
            ext_p[HALO:, :] = up_ref_[...]
            ext_c[HALO:, :] = uc_ref_[...]

        def pool():
            pos0 = (tile % tiles_per_seq) * tm
            y_pool[slot] = _pool_mixer(ext_p, pool_w_ref, pool_scale_ref, tm, pos0).astype(BF16)

        def store_conv(y):
            y_conv[slot] = y.astype(BF16)

        return [fill] + _conv_mixer_steps(ext_c, shift_c, conv_out, dw_ref, dw_b_ref, ln_g_ref, ln_b_ref, pw_ref,
                                          tm, store_conv) + [pool]

    @pl.when(i == 0)
    def _():
        for step in mixer_steps(up0_ref, uc0_ref, 0, 0, True):
            step()

    slot = i % 2
    nxt = jnp.minimum(i + 1, pl.num_programs(0) - 1)
    out_ref[...] = _dense_math(x_ref[...], y_pool[slot], o_ref[...], y_conv[slot],
                               wout_ref, g_ref, wgu_ref, wd_ref, fg_ref, final)
    for step in mixer_steps(up_ref, uc_ref, nxt, 1 - slot, nxt % tiles_per_seq == 0):
        step()


def _prompt_dense(x, o, upool, uconv, hist_pool, hist_conv, mixer_w, dense_w, tm, seq_len, final, layer):
    n = x.shape[0]
    tps = seq_len // tm
    last = n // tm - 1
    row = lambda w: pl.BlockSpec((tm, w), lambda i: (i, 0))
    first_tile = lambda w: pl.BlockSpec((tm, w), lambda i: (0, 0))
    next_tile = lambda w: pl.BlockSpec((tm, w), lambda i: (jnp.minimum(i + 1, last), 0))
    next_seq = lambda w: pl.BlockSpec((None, HALO, w), lambda i: (jnp.minimum(i + 1, last) // tps, 0, 0))
    return pl.pallas_call(
        functools.partial(_prompt_dense_kernel, final=final, tiles_per_seq=tps),
        grid=(n // tm,),
        in_specs=[row(D_MODEL), row(ATTN_WIDTH), first_tile(POOL_WIDTH), first_tile(CONV_WIDTH),
                  next_tile(POOL_WIDTH), next_tile(CONV_WIDTH), next_seq(POOL_WIDTH), next_seq(CONV_WIDTH)]
        + _mixer_weight_specs(layer) + _dense_weight_specs(layer),
        out_specs=row(D_MODEL),
        out_shape=jax.ShapeDtypeStruct((n, D_MODEL), F32),
        scratch_shapes=_mixer_scratch(tm) + [pltpu.VMEM((2, tm, POOL_WIDTH), BF16),
                                             pltpu.VMEM((2, tm, CONV_WIDTH), BF16)],
        compiler_params=pltpu.CompilerParams(dimension_semantics=("arbitrary",),
                                             vmem_limit_bytes=VMEM_LIMIT),
        name="prompt_dense",
    )(x, o, upool, uconv, upool, uconv, hist_pool, hist_conv, *mixer_w, *dense_w)


def _sample_dense_kernel(x_ref, yp_ref, o_ref, yc_ref, wout_ref, g_ref, wgu_ref, wd_ref, fg_ref, out_ref, *, final):
    out_ref[...] = _dense_math(x_ref[...], yp_ref[...], o_ref[...], yc_ref[...],
                               wout_ref, g_ref, wgu_ref, wd_ref, fg_ref, final)


def _sample_dense(x, ypool, o, yconv, wout_bf, g, wgu_bf, wd_bf, fg, tm, final, layer):
    n = x.shape[0]
    row = lambda w: pl.BlockSpec((tm, w), lambda i: (i, 0))
    return pl.pallas_call(
        functools.partial(_sample_dense_kernel, final=final),
        grid=(n // tm,),
        in_specs=[row(D_MODEL), row(POOL_WIDTH), row(ATTN_WIDTH), row(CONV_WIDTH)] + _dense_weight_specs(layer),
        out_specs=row(D_MODEL),
        out_shape=jax.ShapeDtypeStruct((n, D_MODEL), F32),
        compiler_params=pltpu.CompilerParams(dimension_semantics=("arbitrary",),
                                             vmem_limit_bytes=VMEM_LIMIT),
        name="sample_dense",
    )(x, ypool, o, yconv, wout_bf, g, wgu_bf, wd_bf, fg)


def _rope_tables(pos):
    half = HEAD_DIM // 2
    inv = ROPE_THETA ** (-jnp.arange(half, dtype=F32) / half)
    ang = pos.astype(F32)[:, None] * inv[None, :]
    reps = LANES // half
    cos = jnp.tile(jnp.cos(ang), (1, reps))
    sign = jnp.tile(jnp.concatenate([-jnp.ones((half,), F32), jnp.ones((half,), F32)]), LANES // HEAD_DIM)
    sin = jnp.tile(jnp.sin(ang), (1, reps)) * sign[None, :]
    return cos, sin


def _block_diag(pool_w):
    depth, g, d, _ = pool_w.shape
    eye = jnp.eye(g, dtype=pool_w.dtype)
    return (pool_w[:, :, :, None, :] * eye[None, :, None, :, None]).reshape(depth, g * d, g * d)


def _pad_hist(h):
    return jnp.pad(h, ((0, 0), (0, 0), (HALO - h.shape[2], 0), (0, 0)))


def kernel(x_prompt, x_sample, cache_k, cache_v, state_pool, state_conv, norm_mix_g, w_in, pool_w, pool_scale,
           lambda_qk, diff_norm_g, conv_dw, conv_dw_b, conv_ln_g, conv_ln_b, conv_pw, w_out, norm_ffn_g,
           w_gate_up, w_down, final_norm_g):
    B, S, _ = x_prompt.shape
    Bs, Ls, _ = x_sample.shape
    depth = w_in.shape[0]
    past_len = cache_k.shape[2]
    assert S % ATTN_TILE == 0 and S % ROW_TILE == 0 and (Bs * Ls) % 16 == 0 and Ls % 16 == 0 and Ls >= CONV_HIST

    cos_p, sin_p = _rope_tables(jnp.arange(S, dtype=jnp.int32))
    cos_s, sin_s = _rope_tables(past_len + jnp.arange(Ls, dtype=jnp.int32))
    cos_s, sin_s = jnp.tile(cos_s, (Bs, 1)), jnp.tile(sin_s, (Bs, 1))
    zero_pool = jnp.zeros((B, HALO, POOL_WIDTH), F32)
    zero_conv = jnp.zeros((B, HALO, CONV_WIDTH), F32)

    xp = x_prompt.reshape(B * S, D_MODEL)
    xs = x_sample.reshape(Bs * Ls, D_MODEL)
    rows = lambda a: a.reshape(depth, 1, -1)
    norm_mix_rows, norm_ffn_rows = rows(norm_mix_g), rows(norm_ffn_g)
    diff_g_row, diff_g_col = rows(diff_norm_g), diff_norm_g.reshape(depth, -1, 1)
    w_in_bf, w_out_bf, w_gu_bf, w_d_bf = (w.astype(BF16) for w in (w_in, w_out, w_gate_up, w_down))
    mixer_w = (_block_diag(pool_w).astype(BF16), rows(pool_scale), conv_dw, rows(conv_dw_b), rows(conv_ln_g),
               rows(conv_ln_b), conv_pw.astype(BF16))
    dense_w = (w_out_bf, norm_ffn_rows, w_gu_bf, w_d_bf, final_norm_g.reshape(1, -1))
    hist_pool_s, hist_conv_s = _pad_hist(state_pool), _pad_hist(state_conv)
    cache_kt = jnp.transpose(cache_k, (0, 1, 3, 4, 5, 2)).reshape(depth, Bs, ATTN_WIDTH, past_len)
    cache_v4 = cache_v.reshape(depth, Bs, past_len * N_HEADS, V_DIM)
    kv_stack = depth
    pp, cp = [], []
    ks_, vs_, ps_, cs_ = [], [], [], []
    for l in range(depth):
        lam_init = 0.8 - 0.6 * math.exp(-0.3 * l)
        final = l == depth - 1

        upool, q, k, kb, v, vb, uconv = _prompt_in_proj(xp, norm_mix_rows, w_in_bf, cos_p, sin_p, ROW_TILE, S,
                                                        kv_stack, l)
        kv_stack = (k, v)
        o = _prompt_attention(lambda_qk, diff_g_col, q.reshape(B, S, -1), kb.reshape(B, S, -1),
                              vb.reshape(B, S, -1), lam_init, l)
        xp = _prompt_dense(xp, o.reshape(B * S, -1), upool, uconv, zero_pool, zero_conv, mixer_w, dense_w,
                           ROW_TILE, S, final, l)
        pp.append(upool.reshape(B, S, -1)[:, S - POOL_HIST:])
        cp.append(uconv.reshape(B, S, -1)[:, S - CONV_HIST:])

        upool, q, k, kb, v, vb, uconv = _sample_in_proj(xs, norm_mix_rows, w_in_bf, cos_s, sin_s, l)
        o = _sample_attention(lambda_qk, diff_g_row, q.reshape(Bs, Ls, -1), cache_kt, cache_v4,
                              kb.reshape(Bs, Ls, -1), vb.reshape(Bs, Ls, -1), lam_init, l)
        ypool, yconv = _sample_mixers(upool, uconv, hist_pool_s, hist_conv_s, mixer_w, Ls, past_len, l)
        xs = _sample_dense(xs, ypool, o.reshape(Bs * Ls, -1), yconv, *dense_w, Bs * Ls, final, l)
        ks_.append(k.reshape(Bs, Ls, N_HEADS, 2, HEAD_DIM))
        vs_.append(v.reshape(Bs, Ls, N_HEADS, V_DIM))
        ps_.append(upool.reshape(Bs, Ls, -1)[:, Ls - POOL_HIST:])
        cs_.append(uconv.reshape(Bs, Ls, -1)[:, Ls - CONV_HIST:])

    kt_all, v4_all = kv_stack
    new_k = jnp.transpose(kt_all.reshape(depth, B, N_HEADS, 2, HEAD_DIM, S), (0, 1, 5, 2, 3, 4))
    new_v = v4_all.reshape(depth, B, S, N_HEADS, V_DIM)
    return (xp.reshape(B, S, D_MODEL), xs.reshape(Bs, Ls, D_MODEL),
            new_k, new_v, jnp.stack(pp), jnp.stack(cp),
            jnp.stack(ks_), jnp.stack(vs_), jnp.stack(ps_), jnp.stack(cs_))
```

```python
import functools
import math

import jax
import jax.numpy as jnp
from jax import lax
from jax.experimental import pallas as pl
from jax.experimental.pallas import tpu as pltpu

D_MODEL = 1024
CHUNK = 64
POOL_WIDTH = D_MODEL // 4
POOL_WINDOWS = (2, 4, 8, 16)
POOL_GROUP_DIM = POOL_WIDTH // len(POOL_WINDOWS)
POOL_HIST = max(POOL_WINDOWS) - 1
ATTN_WIDTH = D_MODEL // 2
N_HEADS = 4
HEAD_DIM = ATTN_WIDTH // (2 * N_HEADS)
V_DIM = 2 * HEAD_DIM
ROPE_THETA = 10000.0
CONV_WIDTH = D_MODEL // 4
CONV_K = 31
CONV_HIST = CONV_K - 1
MIX_WIDTH = POOL_WIDTH + ATTN_WIDTH + CONV_WIDTH
IN_WIDTH = POOL_WIDTH + 3 * ATTN_WIDTH + 2 * CONV_WIDTH
D_FF = ((-(-8 * D_MODEL // 3) + 255) // 256) * 256
EPS = 1e-6

LANES = 128
HALO = 32
ROW_TILE = 512
ATTN_TILE = 256
HEADS_PER_STEP = 4
VMEM_LIMIT = 56 * 1024 * 1024

F32 = jnp.float32
BF16 = jnp.bfloat16


def _rms(x, g):
    return x * lax.rsqrt(jnp.mean(x * x, axis=-1, keepdims=True) + EPS) * g


def _const_spec(shape):
    return pl.BlockSpec(shape, lambda *_: (0,) * len(shape), pipeline_mode=pl.Buffered(1))


def _layer_spec(shape, layer):
    return pl.BlockSpec((None,) + shape, lambda *_: (layer,) + (0,) * len(shape), pipeline_mode=pl.Buffered(1))


O_Q = POOL_WIDTH
O_K = O_Q + ATTN_WIDTH
O_V = O_K + ATTN_WIDTH
O_A = O_V + ATTN_WIDTH
O_B = O_A + CONV_WIDTH


def _qkv_epilogue(z, cos_ref, sin_ref, q_ref, k_ref, kb_ref, v_ref, vb_ref, cache_layout):
    tm = z.shape[0]
    o1, o2, o3 = O_Q, O_K, O_V
    cos = cos_ref[...]
    sin = sin_ref[...]
    lane = lax.broadcasted_iota(jnp.int32, cos.shape, 1)
    first_half = (lane & (HEAD_DIM // 2)) == 0

    def rope(t):
        up = pltpu.roll(t, LANES - HEAD_DIM // 2, axis=1)
        down = pltpu.roll(t, HEAD_DIM // 2, axis=1)
        return t * cos + jnp.where(first_half, up, down) * sin

    scale = HEAD_DIM ** -0.5 * math.log2(math.e)
    for j in range(ATTN_WIDTH // LANES):
        sl = slice(j * LANES, (j + 1) * LANES)
        qj = rope(z[:, o1 + j * LANES:o1 + (j + 1) * LANES])
        kj = rope(z[:, o2 + j * LANES:o2 + (j + 1) * LANES])
        vj = z[:, o3 + j * LANES:o3 + (j + 1) * LANES]
        q_ref[:, sl] = (qj * scale).astype(BF16)
        kb_ref[:, sl] = kj.astype(BF16)
        if cache_layout:
            k_ref[sl, :] = kj.T
            v_ref[pl.ds(j, tm, stride=N_HEADS), :] = vj
            vb_ref[sl, :] = vj.T.astype(BF16)
        else:
            k_ref[:, sl] = kj
            v_ref[:, sl] = vj
            vb_ref[:, sl] = vj.astype(BF16)


def _sample_in_proj_kernel(x_ref, g_ref, w_ref, cos_ref, sin_ref,
                           upool_ref, q_ref, k_ref, kb_ref, v_ref, vb_ref, uconv_ref):
    h = _rms(x_ref[...], g_ref[...]).astype(BF16)
    z = jnp.dot(h, w_ref[...], preferred_element_type=F32)
    upool_ref[...] = z[:, :O_Q]
    uconv_ref[...] = z[:, O_A:O_B] * jax.nn.sigmoid(z[:, O_B:])
    _qkv_epilogue(z, cos_ref, sin_ref, q_ref, k_ref, kb_ref, v_ref, vb_ref, False)


def _sample_in_proj(x, g, w_bf, cos_tab, sin_tab, layer):
    n = x.shape[0]
    full = lambda w: pl.BlockSpec((n, w), lambda i: (0, 0))
    flat = lambda dt: jax.ShapeDtypeStruct((n, ATTN_WIDTH), dt)
    return pl.pallas_call(
        _sample_in_proj_kernel,
        grid=(1,),
        in_specs=[full(D_MODEL), _layer_spec((1, D_MODEL), layer), _layer_spec((D_MODEL, IN_WIDTH), layer),
                  full(LANES), full(LANES)],
        out_specs=[full(POOL_WIDTH)] + [full(ATTN_WIDTH)] * 5 + [full(CONV_WIDTH)],
        out_shape=[jax.ShapeDtypeStruct((n, POOL_WIDTH), F32), flat(BF16), flat(F32), flat(BF16), flat(F32),
                   flat(BF16), jax.ShapeDtypeStruct((n, CONV_WIDTH), F32)],
        compiler_params=pltpu.CompilerParams(dimension_semantics=("arbitrary",),
                                             vmem_limit_bytes=VMEM_LIMIT),
        name="sample_in_proj",
    )(x, g, w_bf, cos_tab, sin_tab)


def _prompt_in_proj_kernel(*refs, n_unused):
    x_ref, g_ref, w_ref, cos_ref, sin_ref = refs[:5]
    upool_ref, q_ref, k_ref, kb_ref, v_ref, vb_ref, uconv_ref = refs[5 + n_unused:]
    h = _rms(x_ref[...], g_ref[...]).astype(BF16)
    z = jnp.dot(h, w_ref[...], preferred_element_type=F32)
    upool_ref[...] = z[:, :O_Q]
    uconv_ref[...] = z[:, O_A:O_B] * jax.nn.sigmoid(z[:, O_B:])
    _qkv_epilogue(z, cos_ref, sin_ref, q_ref, k_ref, kb_ref, v_ref, vb_ref, True)


def _prompt_in_proj(x, g, w_bf, cos_tab, sin_tab, tm, seq_len, kv_stack, layer):
    n = x.shape[0]
    tps = seq_len // tm
    batch = n // seq_len
    row = lambda w: pl.BlockSpec((tm, w), lambda i: (i, 0))
    tab = pl.BlockSpec((tm, LANES), lambda i: (i % tps, 0))
    flat = lambda dt, w=ATTN_WIDTH: jax.ShapeDtypeStruct((n, w), dt)
    extra_in, extra_specs, aliases = [], [], {}
    if isinstance(kv_stack, int):
        k_shape = jax.ShapeDtypeStruct((kv_stack, batch, ATTN_WIDTH, seq_len), F32)
        v_shape = jax.ShapeDtypeStruct((kv_stack, batch, N_HEADS * seq_len, V_DIM), F32)
    else:
        k_shape, v_shape = (jax.ShapeDtypeStruct(a.shape, a.dtype) for a in kv_stack)
        extra_in = list(kv_stack)
        extra_specs = [pl.BlockSpec(memory_space=pl.ANY)] * 2
        aliases = {5: 2, 6: 4}
    k_spec = pl.BlockSpec((None, None, ATTN_WIDTH, tm), lambda i: (layer, i // tps, 0, i % tps))
    v_spec = pl.BlockSpec((None, None, N_HEADS * tm, V_DIM), lambda i: (layer, i // tps, i % tps, 0))
    vt_spec = pl.BlockSpec((None, ATTN_WIDTH, tm), lambda i: (i // tps, 0, i % tps))
    return pl.pallas_call(
        functools.partial(_prompt_in_proj_kernel, n_unused=len(extra_in)),
        grid=(n // tm,),
        in_specs=[row(D_MODEL), _layer_spec((1, D_MODEL), layer), _layer_spec((D_MODEL, IN_WIDTH), layer), tab, tab]
        + extra_specs,
        out_specs=[row(POOL_WIDTH), row(ATTN_WIDTH), k_spec, row(ATTN_WIDTH), v_spec, vt_spec, row(CONV_WIDTH)],
        out_shape=[flat(F32, POOL_WIDTH), flat(BF16), k_shape, flat(BF16), v_shape,
                   jax.ShapeDtypeStruct((batch, ATTN_WIDTH, seq_len), BF16), flat(F32, CONV_WIDTH)],
        input_output_aliases=aliases,
        compiler_params=pltpu.CompilerParams(dimension_semantics=("arbitrary",),
                                             vmem_limit_bytes=VMEM_LIMIT),
        name="prompt_in_proj",
    )(x, g, w_bf, cos_tab, sin_tab, *extra_in)


def _lambda_scalar(lq, lam_init):
    a = jnp.sum(lq[0:1, :] * lq[1:2, :], axis=-1, keepdims=True)
    b = jnp.sum(lq[2:3, :] * lq[3:4, :], axis=-1, keepdims=True)
    return jnp.exp(a) - jnp.exp(b) + lam_init


def _stack_maps(q):
    lane = lax.broadcasted_iota(jnp.int32, q.shape, 1)
    zero = jnp.zeros_like(q)
    return jnp.concatenate([jnp.where(lane < HEAD_DIM, q, zero), jnp.where(lane >= HEAD_DIM, q, zero)], axis=0)


ONES_ROWS = 16


def _flash_scores(qq, kt):
    return lax.dot_general(kt, qq, (((1,), (1,)), ((), ())), preferred_element_type=F32)


def _flash_update(carry, s, vt_ones):
    m, acc = carry
    m_new = jnp.maximum(m, jnp.max(s, axis=0, keepdims=True))
    alpha = jnp.exp2(m - m_new)
    p = jnp.exp2(s - m_new).astype(BF16)
    return m_new, alpha * acc + jnp.dot(vt_ones, p, preferred_element_type=F32)


def _flash_init(cols):
    return jnp.full((1, cols), -jnp.inf, F32), jnp.zeros((V_DIM + ONES_ROWS, cols), F32)


def _attn_finish(carry, lam, g_col, lam_init, tq):
    _, acc = carry
    scaled = acc[:V_DIM] * (1.0 / acc[V_DIM:V_DIM + 1])
    o = scaled[:, :tq] - lam * scaled[:, tq:]
    o = o * lax.rsqrt(jnp.mean(o * o, axis=0, keepdims=True) + EPS) * g_col
    return (o * (1.0 - lam_init)).T


def _mask_diagonal_tile(s):
    t = s.shape[0]
    q_chunk = (lax.broadcasted_iota(jnp.int32, (1, s.shape[1]), 1) % t) // CHUNK
    blocks = [jnp.where(q_chunk >= a, s[a * CHUNK:(a + 1) * CHUNK, :], -1e30) for a in range(t // CHUNK)]
    return jnp.concatenate(blocks, axis=0)


def _prompt_attn_kernel(lq_ref, g_ref, q_ref, k_ref, vt_ref, o_ref, qq_ref, *, lam_init):
    t = ATTN_TILE
    qi = pl.program_id(2)
    heads = [slice(h * LANES, (h + 1) * LANES) for h in range(HEADS_PER_STEP)]
    for h, c in enumerate(heads):
        qq_ref[h] = _stack_maps(q_ref[:, c])

    ones = jnp.ones((ONES_ROWS, t), BF16)

    def steps(tiles, carry):
        rows = [pl.ds(pl.multiple_of(j * t, t), t) for j, _ in tiles]
        s = [[_flash_scores(qq_ref[h], k_ref[r, c]) for h, c in enumerate(heads)] for r in rows]
        for ti, (_, diagonal) in enumerate(tiles):
            if diagonal:
                s[ti] = [_mask_diagonal_tile(x) for x in s[ti]]
            carry = tuple(
                _flash_update(carry[h], s[ti][h], jnp.concatenate([vt_ref[c, rows[ti]], ones], axis=0))
                for h, c in enumerate(heads))
        return carry

    carry = lax.fori_loop(0, qi // 2, lambda p, c: steps([(2 * p, False), (2 * p + 1, False)], c),
                          (_flash_init(2 * t),) * HEADS_PER_STEP)
    carry = lax.cond(qi % 2 == 1,
                     lambda c: steps([(qi - 1, False), (qi, True)], c),
                     lambda c: steps([(qi, True)], c), carry)
    lam = _lambda_scalar(lq_ref[...], lam_init)
    for h, c in enumerate(heads):
        o_ref[:, c] = _attn_finish(carry[h], lam, g_ref[...], lam_init, t).astype(BF16)


def _prompt_attention(lq, g, q, kb, vtb, lam_init, layer):
    b, s, _ = q.shape
    t = ATTN_TILE
    w = HEADS_PER_STEP * LANES
    return pl.pallas_call(
        functools.partial(_prompt_attn_kernel, lam_init=lam_init),
        grid=(b, N_HEADS // HEADS_PER_STEP, s // t),
        in_specs=[
            _layer_spec((4, HEAD_DIM), layer),
            _layer_spec((V_DIM, 1), layer),
            pl.BlockSpec((None, t, w), lambda b_, h, i: (b_, i, h)),
            pl.BlockSpec((None, s, w), lambda b_, h, i: (b_, 0, h)),
            pl.BlockSpec((None, w, s), lambda b_, h, i: (b_, h, 0)),
        ],
        out_specs=pl.BlockSpec((None, t, w), lambda b_, h, i: (b_, i, h)),
        out_shape=jax.ShapeDtypeStruct((b, s, ATTN_WIDTH), BF16),
        scratch_shapes=[pltpu.VMEM((HEADS_PER_STEP, 2 * t, LANES), BF16)],
        compiler_params=pltpu.CompilerParams(dimension_semantics=("arbitrary",) * 3,
                                             vmem_limit_bytes=VMEM_LIMIT),
        name="prompt_attention",
    )(lq, g, q, kb, vtb)


def _row_scores(qq, kt, *, keys_on_rows):
    dims = (((1,), (1,)), ((), ())) if keys_on_rows else (((1,), (0,)), ((), ()))
    return lax.dot_general(qq, kt, dims, preferred_element_type=F32)


def _row_flash_update(carry, s, vt):
    m, l, acc = carry
    m_new = jnp.maximum(m, jnp.max(s, axis=-1, keepdims=True))
    alpha = jnp.exp2(m - m_new)
    p = jnp.exp2(s - m_new)
    l = alpha * l + jnp.sum(p, axis=-1, keepdims=True)
    acc = alpha * acc + jnp.dot(p.astype(BF16), vt, preferred_element_type=F32)
    return m_new, l, acc


def _sample_attn_kernel(lq_ref, g_ref, q_ref, ck_ref, cv_ref, k_ref, v_ref, o_ref, *, lam_init, past_len):
    t = ATTN_TILE
    tq = q_ref.shape[0]
    heads = [slice(h * LANES, (h + 1) * LANES) for h in range(N_HEADS)]
    qq = [_stack_maps(q_ref[:, c]) for c in heads]

    def body(j, carry):
        cols = pl.ds(pl.multiple_of(j * t, t), t)
        s = [_row_scores(qq[h], ck_ref[c, cols].astype(BF16), keys_on_rows=False) for h, c in enumerate(heads)]
        return tuple(
            _row_flash_update(carry[h], s[h], cv_ref[pl.ds(j * (t * N_HEADS) + h, t, stride=N_HEADS), :].astype(BF16))
            for h in range(N_HEADS))

    init = (jnp.full((2 * tq, 1), -jnp.inf, F32), jnp.zeros((2 * tq, 1), F32), jnp.zeros((2 * tq, V_DIM), F32))
    carry = lax.fori_loop(0, past_len // t, body, (init,) * N_HEADS)
    lam = _lambda_scalar(lq_ref[...], lam_init)
    for h, c in enumerate(heads):
        s = _row_scores(qq[h], k_ref[:, c], keys_on_rows=True)
        _, l, acc = _row_flash_update(carry[h], s, v_ref[:, c])
        o = acc[:tq] / l[:tq] - lam * (acc[tq:] / l[tq:])
        o_ref[:, c] = (_rms(o, g_ref[...]) * (1.0 - lam_init)).astype(BF16)


def _sample_attention(lq, g, q, cache_kt, cache_v4, kb, vb, lam_init, layer):
    b, ls, _ = q.shape
    past_len = cache_kt.shape[-1]
    assert past_len % ATTN_TILE == 0
    return pl.pallas_call(
        functools.partial(_sample_attn_kernel, lam_init=lam_init, past_len=past_len),
        grid=(b,),
        in_specs=[
            _layer_spec((4, HEAD_DIM), layer),
            _layer_spec((1, V_DIM), layer),
            pl.BlockSpec((None, ls, ATTN_WIDTH), lambda b_: (b_, 0, 0)),
            pl.BlockSpec((None, None, ATTN_WIDTH, past_len), lambda b_: (layer, b_, 0, 0)),
            pl.BlockSpec((None, None, N_HEADS * past_len, V_DIM), lambda b_: (layer, b_, 0, 0)),
            pl.BlockSpec((None, ls, ATTN_WIDTH), lambda b_: (b_, 0, 0)),
            pl.BlockSpec((None, ls, ATTN_WIDTH), lambda b_: (b_, 0, 0)),
        ],
        out_specs=pl.BlockSpec((None, ls, ATTN_WIDTH), lambda b_: (b_, 0, 0)),
        out_shape=jax.ShapeDtypeStruct((b, ls, ATTN_WIDTH), BF16),
        compiler_params=pltpu.CompilerParams(dimension_semantics=("arbitrary",),
                                             vmem_limit_bytes=VMEM_LIMIT),
        name="sample_attention",
    )(lq, g, q, cache_kt, cache_v4, kb, vb)


def _pool_mixer(ext_p, pool_w_ref, pool_scale_ref, tm, pos0):
    assert POOL_WINDOWS == (2, 4, 8, 16) and HALO == 32
    e = ext_p[...]
    n = HALO + tm
    s2 = e[8:] + e[7:n - 1]
    s4 = s2[8:] + s2[6:n - 10]
    s8 = s4[8:] + s4[4:n - 20]
    sums = {2: s2[24:], 4: s4[16:], 8: s8[8:], 16: s8[8:] + s8[:n - 32]}
    u = e[HALO:]
    lane = lax.broadcasted_iota(jnp.int32, (tm, POOL_WIDTH), 1)
    group = lane // POOL_GROUP_DIM
    win = sums[POOL_WINDOWS[-1]]
    width = jnp.full((tm, POOL_WIDTH), POOL_WINDOWS[-1], jnp.int32)
    for gi in range(len(POOL_WINDOWS) - 2, -1, -1):
        win = jnp.where(group == gi, sums[POOL_WINDOWS[gi]], win)
        width = jnp.where(group == gi, POOL_WINDOWS[gi], width)
    pos = pos0 + lax.broadcasted_iota(jnp.int32, (tm, POOL_WIDTH), 0)
    count = jnp.minimum(pos + 1, width).astype(F32)
    d = (win / count - u).astype(BF16)
    return jnp.dot(d, pool_w_ref[...], preferred_element_type=F32) * pool_scale_ref[...]


CONV_ROW_BLOCK = 64


def _conv_mixer_steps(ext_c, shift_c, conv_out, dw_ref, dw_b_ref, ln_g_ref, ln_b_ref, pw_ref, tm, store):
    first_row = HALO - CONV_HIST
    blk = min(tm, CONV_ROW_BLOCK)

    def shift(residues):
        for r in residues:
            shift_c[r, 0:HALO - 8 + tm, :] = ext_c[pl.ds(r, HALO - 8 + tm), :]

    def taps(rb):
        yb = jnp.zeros((blk, CONV_WIDTH), F32)
        for k in range(CONV_K):
            r, a = (first_row + k) % 8, (first_row + k) // 8
            rows = pl.ds(8 * a + rb * blk, blk)
            src = ext_c[rows, :] if r == 0 else shift_c[r, rows, :]
            yb = yb + src * dw_ref[k:k + 1, :]
        conv_out[rb * blk:(rb + 1) * blk, :] = yb

    def finish():
        y = conv_out[...] + dw_b_ref[...]
        mu = jnp.mean(y, axis=-1, keepdims=True)
        yc = y - mu
        yn = yc * lax.rsqrt(jnp.mean(yc * yc, axis=-1, keepdims=True) + EPS) * ln_g_ref[...] + ln_b_ref[...]
        act = (yn * jax.nn.sigmoid(yn)).astype(BF16)
        store(jnp.dot(act, pw_ref[...], preferred_element_type=F32))

    steps = [functools.partial(shift, rs) for rs in ((1, 2), (3, 4, 5), (6, 7))]
    steps += [functools.partial(taps, rb) for rb in range(tm // blk)]
    return steps + [finish]


def _conv_mixer(ext_c, shift_c, conv_out, dw_ref, dw_b_ref, ln_g_ref, ln_b_ref, pw_ref, tm):
    out = []
    for step in _conv_mixer_steps(ext_c, shift_c, conv_out, dw_ref, dw_b_ref, ln_g_ref, ln_b_ref, pw_ref, tm,
                                  out.append):
        step()
    return out[0]


def _mixer_scratch(tm):
    return [pltpu.VMEM((HALO + tm, POOL_WIDTH), F32), pltpu.VMEM((HALO + tm, CONV_WIDTH), F32),
            pltpu.VMEM((8, HALO + tm, CONV_WIDTH), F32), pltpu.VMEM((tm, CONV_WIDTH), F32)]


def _mixer_weight_specs(layer):
    shapes = [(POOL_WIDTH, POOL_WIDTH), (1, POOL_WIDTH), (CONV_K, CONV_WIDTH), (1, CONV_WIDTH), (1, CONV_WIDTH),
              (1, CONV_WIDTH), (CONV_WIDTH, CONV_WIDTH)]
    return [_layer_spec(shape, layer) for shape in shapes]


def _sample_mixers_kernel(up_ref, up_hist_ref, uc_ref, uc_hist_ref,
                          pool_w_ref, pool_scale_ref, dw_ref, dw_b_ref, ln_g_ref, ln_b_ref, pw_ref,
                          ypool_ref, yconv_ref, ext_p, ext_c, shift_c, conv_out, *, pos_base):
    tm = up_ref.shape[0]
    ext_p[0:HALO, :] = up_hist_ref[...]
    ext_p[HALO:, :] = up_ref[...]
    ext_c[0:HALO, :] = uc_hist_ref[...]
    ext_c[HALO:, :] = uc_ref[...]
    ypool_ref[...] = _pool_mixer(ext_p, pool_w_ref, pool_scale_ref, tm, pos_base).astype(BF16)
    yconv_ref[...] = _conv_mixer(ext_c, shift_c, conv_out, dw_ref, dw_b_ref, ln_g_ref, ln_b_ref, pw_ref,
                                 tm).astype(BF16)


def _sample_mixers(upool, uconv, hist_pool, hist_conv, mixer_w, seq_len, pos_base, layer):
    n = upool.shape[0]
    cur = pl.BlockSpec((seq_len, POOL_WIDTH), lambda i: (i, 0))
    hist = pl.BlockSpec((None, None, HALO, POOL_WIDTH), lambda i: (layer, i, 0, 0))
    return pl.pallas_call(
        functools.partial(_sample_mixers_kernel, pos_base=pos_base),
        grid=(n // seq_len,),
        in_specs=[cur, hist, cur, hist] + _mixer_weight_specs(layer),
        out_specs=[cur, cur],
        out_shape=[jax.ShapeDtypeStruct((n, POOL_WIDTH), BF16), jax.ShapeDtypeStruct((n, CONV_WIDTH), BF16)],
        scratch_shapes=_mixer_scratch(seq_len),
        compiler_params=pltpu.CompilerParams(dimension_semantics=("arbitrary",),
                                             vmem_limit_bytes=VMEM_LIMIT),
        name="sample_mixers",
    )(upool, hist_pool, uconv, hist_conv, *mixer_w)


def _dense_math(x, ypool, o, yconv, wout_ref, g_ref, wgu_ref, wd_ref, fg_ref, final):
    o1 = POOL_WIDTH
    o2 = o1 + ATTN_WIDTH
    mix = (jnp.dot(ypool, wout_ref[0:o1, :], preferred_element_type=F32)
           + jnp.dot(o, wout_ref[o1:o2, :], preferred_element_type=F32)
           + jnp.dot(yconv, wout_ref[o2:, :], preferred_element_type=F32))
    x1 = x + mix
    h = _rms(x1, g_ref[...]).astype(BF16)
    gu = jnp.dot(h, wgu_ref[...], preferred_element_type=F32)
    gate = gu[:, :D_FF]
    a = (gate * jax.nn.sigmoid(gate) * gu[:, D_FF:]).astype(BF16)
    x2 = x1 + jnp.dot(a, wd_ref[...], preferred_element_type=F32)
    return _rms(x2, fg_ref[...]) if final else x2


def _dense_weight_specs(layer):
    shapes = [(MIX_WIDTH, D_MODEL), (1, D_MODEL), (D_MODEL, 2 * D_FF), (D_FF, D_MODEL)]
    return [_layer_spec(shape, layer) for shape in shapes] + [_const_spec((1, D_MODEL))]


def _prompt_dense_kernel(x_ref, o_ref, up0_ref, uc0_ref, up_ref, uc_ref, hist_p_ref, hist_c_ref,
                         pool_w_ref, pool_scale_ref, dw_ref, dw_b_ref, ln_g_ref, ln_b_ref, pw_ref,
                         wout_ref, g_ref, wgu_ref, wd_ref, fg_ref, out_ref,
                         ext_p, ext_c, shift_c, conv_out, y_pool, y_conv, *, final, tiles_per_seq):
    tm = x_ref.shape[0]
    i = pl.program_id(0)

    def mixer_steps(up_ref_, uc_ref_, tile, slot, first):
        def fill():
            if first is True:
                ext_p[0:HALO, :] = hist_p_ref[...]
                ext_c[0:HALO, :] = hist_c_ref[...]
            else:
                ext_p[0:HALO, :] = jnp.where(first, hist_p_ref[...], ext_p[tm:, :])
                ext_c[0:HALO, :] = jnp.where(first, hist_c_ref[...], ext_c[tm:, :])
            ext_p[HALO:, :] = up_ref_[...]
            ext_c[HALO:, :] = uc_ref_[...]

        def pool():
            pos0 = (tile % tiles_per_seq) * tm
            y_pool[slot] = _pool_mixer(ext_p, pool_w_ref, pool_scale_ref, tm, pos0).astype(BF16)

        def store_conv(y):
            y_conv[slot] = y.astype(BF16)

        return [fill] + _conv_mixer_steps(ext_c, shift_c, conv_out, dw_ref, dw_b_ref, ln_g_ref, ln_b_ref, pw_ref,
                                          tm, store_conv) + [pool]

    @pl.when(i == 0)
    def _():
        for step in mixer_steps(up0_ref, uc0_ref, 0, 0, True):
            step()

    slot = i % 2
    nxt = jnp.minimum(i + 1, pl.num_programs(0) - 1)
    out_ref[...] = _dense_math(x_ref[...], y_pool[slot], o_ref[...], y_conv[slot],
                               wout_ref, g_ref, wgu_ref, wd_ref, fg_ref, final)
    for step in mixer_steps(up_ref, uc_ref, nxt, 1 - slot, nxt % tiles_per_seq == 0):
        step()


def _prompt_dense(x, o, upool, uconv, hist_pool, hist_conv, mixer_w, dense_w, tm, seq_len, final, layer):
    n = x.shape[0]
    tps = seq_len // tm
    last = n // tm - 1
    row = lambda w: pl.BlockSpec((tm, w), lambda i: (i, 0))
    first_tile = lambda w: pl.BlockSpec((tm, w), lambda i: (0, 0))
    next_tile = lambda w: pl.BlockSpec((tm, w), lambda i: (jnp.minimum(i + 1, last), 0))
    next_seq = lambda w: pl.BlockSpec((None, HALO, w), lambda i: (jnp.minimum(i + 1, last) // tps, 0, 0))
    return pl.pallas_call(
        functools.partial(_prompt_dense_kernel, final=final, tiles_per_seq=tps),
        grid=(n // tm,),
        in_specs=[row(D_MODEL), row(ATTN_WIDTH), first_tile(POOL_WIDTH), first_tile(CONV_WIDTH),
                  next_tile(POOL_WIDTH), next_tile(CONV_WIDTH), next_seq(POOL_WIDTH), next_seq(CONV_WIDTH)]
        + _mixer_weight_specs(layer) + _dense_weight_specs(layer),
        out_specs=row(D_MODEL),
        out_shape=jax.ShapeDtypeStruct((n, D_MODEL), F32),
        scratch_shapes=_mixer_scratch(tm) + [pltpu.VMEM((2, tm, POOL_WIDTH), BF16),
                                             pltpu.VMEM((2, tm, CONV_WIDTH), BF16)],
        compiler_params=pltpu.CompilerParams(dimension_semantics=("arbitrary",),
                                             vmem_limit_bytes=VMEM_LIMIT),
        name="prompt_dense",
    )(x, o, upool, uconv, upool, uconv, hist_pool, hist_conv, *mixer_w, *dense_w)


def _sample_dense_kernel(x_ref, yp_ref, o_ref, yc_ref, wout_ref, g_ref, wgu_ref, wd_ref, fg_ref, out_ref, *, final):
    out_ref[...] = _dense_math(x_ref[...], yp_ref[...], o_ref[...], yc_ref[...],
                               wout_ref, g_ref, wgu_ref, wd_ref, fg_ref, final)


def _sample_dense(x, ypool, o, yconv, wout_bf, g, wgu_bf, wd_bf, fg, tm, final, layer):
    n = x.shape[0]
    row = lambda w: pl.BlockSpec((tm, w), lambda i: (i, 0))
    return pl.pallas_call(
        functools.partial(_sample_dense_kernel, final=final),
        grid=(n // tm,),
        in_specs=[row(D_MODEL), row(POOL_WIDTH), row(ATTN_WIDTH), row(CONV_WIDTH)] + _dense_weight_specs(layer),
        out_specs=row(D_MODEL),
        out_shape=jax.ShapeDtypeStruct((n, D_MODEL), F32),
        compiler_params=pltpu.CompilerParams(dimension_semantics=("arbitrary",),
                                             vmem_limit_bytes=VMEM_LIMIT),
        name="sample_dense",
    )(x, ypool, o, yconv, wout_bf, g, wgu_bf, wd_bf, fg)


def _rope_tables(pos):
    half = HEAD_DIM // 2
    inv = ROPE_THETA ** (-jnp.arange(half, dtype=F32) / half)
    ang = pos.astype(F32)[:, None] * inv[None, :]
    reps = LANES // half
    cos = jnp.tile(jnp.cos(ang), (1, reps))
    sign = jnp.tile(jnp.concatenate([-jnp.ones((half,), F32), jnp.ones((half,), F32)]), LANES // HEAD_DIM)
    sin = jnp.tile(jnp.sin(ang), (1, reps)) * sign[None, :]
    return cos, sin


def _block_diag(pool_w):
    depth, g, d, _ = pool_w.shape
    eye = jnp.eye(g, dtype=pool_w.dtype)
    return (pool_w[:, :, :, None, :] * eye[None, :, None, :, None]).reshape(depth, g * d, g * d)


def _pad_hist(h):
    return jnp.pad(h, ((0, 0), (0, 0), (HALO - h.shape[2], 0), (0, 0)))


def kernel(x_prompt, x_sample, cache_k, cache_v, state_pool, state_conv, norm_mix_g, w_in, pool_w, pool_scale,
           lambda_qk, diff_norm_g, conv_dw, conv_dw_b, conv_ln_g, conv_ln_b, conv_pw, w_out, norm_ffn_g,
           w_gate_up, w_down, final_norm_g):
    B, S, _ = x_prompt.shape
    Bs, Ls, _ = x_sample.shape
    depth = w_in.shape[0]
    past_len = cache_k.shape[2]
    assert S % ATTN_TILE == 0 and S % ROW_TILE == 0 and (Bs * Ls) % 16 == 0 and Ls % 16 == 0 and Ls >= CONV_HIST

    cos_p, sin_p = _rope_tables(jnp.arange(S, dtype=jnp.int32))
    cos_s, sin_s = _rope_tables(past_len + jnp.arange(Ls, dtype=jnp.int32))
    cos_s, sin_s = jnp.tile(cos_s, (Bs, 1)), jnp.tile(sin_s, (Bs, 1))
    zero_pool = jnp.zeros((B, HALO, POOL_WIDTH), F32)
    zero_conv = jnp.zeros((B, HALO, CONV_WIDTH), F32)

    xp = x_prompt.reshape(B * S, D_MODEL)
    xs = x_sample.reshape(Bs * Ls, D_MODEL)
    rows = lambda a: a.reshape(depth, 1, -1)
    norm_mix_rows, norm_ffn_rows = rows(norm_mix_g), rows(norm_ffn_g)
    diff_g_row, diff_g_col = rows(diff_norm_g), diff_norm_g.reshape(depth, -1, 1)
    w_in_bf, w_out_bf, w_gu_bf, w_d_bf = (w.astype(BF16) for w in (w_in, w_out, w_gate_up, w_down))
    mixer_w = (_block_diag(pool_w).astype(BF16), rows(pool_scale), conv_dw, rows(conv_dw_b), rows(conv_ln_g),
               rows(conv_ln_b), conv_pw.astype(BF16))
    dense_w = (w_out_bf, norm_ffn_rows, w_gu_bf, w_d_bf, final_norm_g.reshape(1, -1))
    hist_pool_s, hist_conv_s = _pad_hist(state_pool), _pad_hist(state_conv)
    cache_kt = jnp.transpose(cache_k, (0, 1, 3, 4, 5, 2)).reshape(depth, Bs, ATTN_WIDTH, past_len)
    cache_v4 = cache_v.reshape(depth, Bs, past_len * N_HEADS, V_DIM)
    kv_stack = depth
    pp, cp = [], []
    ks_, vs_, ps_, cs_ = [], [], [], []
    for l in range(depth):
        lam_init = 0.8 - 0.6 * math.exp(-0.3 * l)
        final = l == depth - 1

        upool, q, k, kb, v, vb, uconv = _prompt_in_proj(xp, norm_mix_rows, w_in_bf, cos_p, sin_p, ROW_TILE, S,
                                                        kv_stack, l)
        kv_stack = (k, v)
        o = _prompt_attention(lambda_qk, diff_g_col, q.reshape(B, S, -1), kb.reshape(B, S, -1),
                              vb, lam_init, l)
        xp = _prompt_dense(xp, o.reshape(B * S, -1), upool, uconv, zero_pool, zero_conv, mixer_w, dense_w,
                           ROW_TILE, S, final, l)
        pp.append(upool.reshape(B, S, -1)[:, S - POOL_HIST:])
        cp.append(uconv.reshape(B, S, -1)[:, S - CONV_HIST:])

        upool, q, k, kb, v, vb, uconv = _sample_in_proj(xs, norm_mix_rows, w_in_bf, cos_s, sin_s, l)
        o = _sample_attention(lambda_qk, diff_g_row, q.reshape(Bs, Ls, -1), cache_kt, cache_v4,
                              kb.reshape(Bs, Ls, -1), vb.reshape(Bs, Ls, -1), lam_init, l)
        ypool, yconv = _sample_mixers(upool, uconv, hist_pool_s, hist_conv_s, mixer_w, Ls, past_len, l)
        xs = _sample_dense(xs, ypool, o.reshape(Bs * Ls, -1), yconv, *dense_w, Bs * Ls, final, l)
        ks_.append(k.reshape(Bs, Ls, N_HEADS, 2, HEAD_DIM))
        vs_.append(v.reshape(Bs, Ls, N_HEADS, V_DIM))
        ps_.append(upool.reshape(Bs, Ls, -1)[:, Ls - POOL_HIST:])
        cs_.append(uconv.reshape(Bs, Ls, -1)[:, Ls - CONV_HIST:])

    kt_all, v4_all = kv_stack
    new_k = jnp.transpose(kt_all.reshape(depth, B, N_HEADS, 2, HEAD_DIM, S), (0, 1, 5, 2, 3, 4))
    new_v = v4_all.reshape(depth, B, S, N_HEADS, V_DIM)
    return (xp.reshape(B, S, D_MODEL), xs.reshape(Bs, Ls, D_MODEL),
            new_k, new_v, jnp.stack(pp), jnp.stack(cp),
            jnp.stack(ks_), jnp.stack(vs_), jnp.stack(ps_), jnp.stack(cs_))
```

```python
import functools
import math

import jax
import jax.numpy as jnp
from jax import lax
from jax.experimental import pallas as pl
from jax.experimental.pallas import tpu as pltpu

D_MODEL = 1024
CHUNK = 64
POOL_WIDTH = D_MODEL // 4
POOL_WINDOWS = (2, 4, 8, 16)
POOL_GROUP_DIM = POOL_WIDTH // len(POOL_WINDOWS)
POOL_HIST = max(POOL_WINDOWS) - 1
ATTN_WIDTH = D_MODEL // 2
N_HEADS = 4
HEAD_DIM = ATTN_WIDTH // (2 * N_HEADS)
V_DIM = 2 * HEAD_DIM
ROPE_THETA = 10000.0
CONV_WIDTH = D_MODEL // 4
CONV_K = 31
CONV_HIST = CONV_K - 1
MIX_WIDTH = POOL_WIDTH + ATTN_WIDTH + CONV_WIDTH
IN_WIDTH = POOL_WIDTH + 3 * ATTN_WIDTH + 2 * CONV_WIDTH
D_FF = ((-(-8 * D_MODEL // 3) + 255) // 256) * 256
EPS = 1e-6

LANES = 128
HALO = 32
ROW_TILE = 512
ATTN_TILE = 256
SAMPLE_KEY_TILE = 1024
HEADS_PER_STEP = 4
VMEM_LIMIT = 56 * 1024 * 1024

F32 = jnp.float32
BF16 = jnp.bfloat16


def _rms(x, g):
    return x * lax.rsqrt(jnp.mean(x * x, axis=-1, keepdims=True) + EPS) * g


def _const_spec(shape):
    return pl.BlockSpec(shape, lambda *_: (0,) * len(shape), pipeline_mode=pl.Buffered(1))


def _layer_spec(shape, layer):
    return pl.BlockSpec((None,) + shape, lambda *_: (layer,) + (0,) * len(shape), pipeline_mode=pl.Buffered(1))


O_Q = POOL_WIDTH
O_K = O_Q + ATTN_WIDTH
O_V = O_K + ATTN_WIDTH
O_A = O_V + ATTN_WIDTH
O_B = O_A + CONV_WIDTH


def _qkv_epilogue(z, cos_ref, sin_ref, q_ref, k_ref, kb_ref, v_ref, vb_ref, cache_layout):
    tm = z.shape[0]
    o1, o2, o3 = O_Q, O_K, O_V
    cos = cos_ref[...]
    sin = sin_ref[...]
    lane = lax.broadcasted_iota(jnp.int32, cos.shape, 1)
    first_half = (lane & (HEAD_DIM // 2)) == 0

    def rope(t):
        up = pltpu.roll(t, LANES - HEAD_DIM // 2, axis=1)
        down = pltpu.roll(t, HEAD_DIM // 2, axis=1)
        return t * cos + jnp.where(first_half, up, down) * sin

    scale = HEAD_DIM ** -0.5 * math.log2(math.e)
    for j in range(ATTN_WIDTH // LANES):
        sl = slice(j * LANES, (j + 1) * LANES)
        qj = rope(z[:, o1 + j * LANES:o1 + (j + 1) * LANES])
        kj = rope(z[:, o2 + j * LANES:o2 + (j + 1) * LANES])
        vj = z[:, o3 + j * LANES:o3 + (j + 1) * LANES]
        q_ref[:, sl] = (qj * scale).astype(BF16)
        kb_ref[:, sl] = kj.astype(BF16)
        if cache_layout:
            k_ref[sl, :] = kj.T
            v_ref[pl.ds(j, tm, stride=N_HEADS), :] = vj
            vb_ref[sl, :] = vj.T.astype(BF16)
        else:
            k_ref[:, sl] = kj
            v_ref[:, sl] = vj
            vb_ref[:, sl] = vj.astype(BF16)


def _sample_in_proj_kernel(x_ref, g_ref, w_ref, cos_ref, sin_ref,
                           upool_ref, q_ref, k_ref, kb_ref, v_ref, vb_ref, uconv_ref):
    h = _rms(x_ref[...], g_ref[...]).astype(BF16)
    z = jnp.dot(h, w_ref[...], preferred_element_type=F32)
    upool_ref[...] = z[:, :O_Q]
    uconv_ref[...] = z[:, O_A:O_B] * jax.nn.sigmoid(z[:, O_B:])
    _qkv_epilogue(z, cos_ref, sin_ref, q_ref, k_ref, kb_ref, v_ref, vb_ref, False)


def _sample_in_proj(x, g, w_bf, cos_tab, sin_tab, layer):
    n = x.shape[0]
    full = lambda w: pl.BlockSpec((n, w), lambda i: (0, 0))
    flat = lambda dt: jax.ShapeDtypeStruct((n, ATTN_WIDTH), dt)
    return pl.pallas_call(
        _sample_in_proj_kernel,
        grid=(1,),
        in_specs=[full(D_MODEL), _layer_spec((1, D_MODEL), layer), _layer_spec((D_MODEL, IN_WIDTH), layer),
                  full(LANES), full(LANES)],
        out_specs=[full(POOL_WIDTH)] + [full(ATTN_WIDTH)] * 5 + [full(CONV_WIDTH)],
        out_shape=[jax.ShapeDtypeStruct((n, POOL_WIDTH), F32), flat(BF16), flat(F32), flat(BF16), flat(F32),
                   flat(BF16), jax.ShapeDtypeStruct((n, CONV_WIDTH), F32)],
        compiler_params=pltpu.CompilerParams(dimension_semantics=("arbitrary",),
                                             vmem_limit_bytes=VMEM_LIMIT),
        name="sample_in_proj",
    )(x, g, w_bf, cos_tab, sin_tab)


def _prompt_in_proj_kernel(*refs, n_unused):
    x_ref, g_ref, w_ref, cos_ref, sin_ref = refs[:5]
    upool_ref, q_ref, k_ref, kb_ref, v_ref, vb_ref, uconv_ref = refs[5 + n_unused:]
    h = _rms(x_ref[...], g_ref[...]).astype(BF16)
    z = jnp.dot(h, w_ref[...], preferred_element_type=F32)
    upool_ref[...] = z[:, :O_Q]
    uconv_ref[...] = z[:, O_A:O_B] * jax.nn.sigmoid(z[:, O_B:])
    _qkv_epilogue(z, cos_ref, sin_ref, q_ref, k_ref, kb_ref, v_ref, vb_ref, True)


def _prompt_in_proj(x, g, w_bf, cos_tab, sin_tab, tm, seq_len, kv_stack, layer):
    n = x.shape[0]
    tps = seq_len // tm
    batch = n // seq_len
    row = lambda w: pl.BlockSpec((tm, w), lambda i: (i, 0))
    tab = pl.BlockSpec((tm, LANES), lambda i: (i % tps, 0))
    flat = lambda dt, w=ATTN_WIDTH: jax.ShapeDtypeStruct((n, w), dt)
    extra_in, extra_specs, aliases = [], [], {}
    if isinstance(kv_stack, int):
        k_shape = jax.ShapeDtypeStruct((kv_stack, batch, ATTN_WIDTH, seq_len), F32)
        v_shape = jax.ShapeDtypeStruct((kv_stack, batch, N_HEADS * seq_len, V_DIM), F32)
    else:
        k_shape, v_shape = (jax.ShapeDtypeStruct(a.shape, a.dtype) for a in kv_stack)
        extra_in = list(kv_stack)
        extra_specs = [pl.BlockSpec(memory_space=pl.ANY)] * 2
        aliases = {5: 2, 6: 4}
    k_spec = pl.BlockSpec((None, None, ATTN_WIDTH, tm), lambda i: (layer, i // tps, 0, i % tps))
    v_spec = pl.BlockSpec((None, None, N_HEADS * tm, V_DIM), lambda i: (layer, i // tps, i % tps, 0))
    vt_spec = pl.BlockSpec((None, ATTN_WIDTH, tm), lambda i: (i // tps, 0, i % tps))
    return pl.pallas_call(
        functools.partial(_prompt_in_proj_kernel, n_unused=len(extra_in)),
        grid=(n // tm,),
        in_specs=[row(D_MODEL), _layer_spec((1, D_MODEL), layer), _layer_spec((D_MODEL, IN_WIDTH), layer), tab, tab]
        + extra_specs,
        out_specs=[row(POOL_WIDTH), row(ATTN_WIDTH), k_spec, row(ATTN_WIDTH), v_spec, vt_spec, row(CONV_WIDTH)],
        out_shape=[flat(F32, POOL_WIDTH), flat(BF16), k_shape, flat(BF16), v_shape,
                   jax.ShapeDtypeStruct((batch, ATTN_WIDTH, seq_len), BF16), flat(F32, CONV_WIDTH)],
        input_output_aliases=aliases,
        compiler_params=pltpu.CompilerParams(dimension_semantics=("arbitrary",),
                                             vmem_limit_bytes=VMEM_LIMIT),
        name="prompt_in_proj",
    )(x, g, w_bf, cos_tab, sin_tab, *extra_in)


def _lambda_scalar(lq, lam_init):
    a = jnp.sum(lq[0:1, :] * lq[1:2, :], axis=-1, keepdims=True)
    b = jnp.sum(lq[2:3, :] * lq[3:4, :], axis=-1, keepdims=True)
    return jnp.exp(a) - jnp.exp(b) + lam_init


def _stack_maps(q):
    lane = lax.broadcasted_iota(jnp.int32, q.shape, 1)
    zero = jnp.zeros_like(q)
    return jnp.concatenate([jnp.where(lane < HEAD_DIM, q, zero), jnp.where(lane >= HEAD_DIM, q, zero)], axis=0)


ONES_ROWS = 16


def _flash_scores(qq, kt):
    return lax.dot_general(kt, qq, (((1,), (1,)), ((), ())), preferred_element_type=F32)


def _flash_update(carry, s, vt_ones, p_stage):
    m, acc = carry
    m_new = jnp.maximum(m, jnp.max(s, axis=0, keepdims=True))
    alpha = jnp.exp2(m - m_new)
    p_stage[...] = jnp.exp2(s - m_new).astype(BF16)
    return m_new, alpha * acc + jnp.dot(vt_ones, p_stage[...], preferred_element_type=F32)


def _flash_init(cols):
    return jnp.full((1, cols), -jnp.inf, F32), jnp.zeros((V_DIM + ONES_ROWS, cols), F32)


def _attn_finish(carry, lam, g_col, lam_init, tq):
    _, acc = carry
    scaled = acc[:V_DIM] * (1.0 / acc[V_DIM:V_DIM + 1])
    o = scaled[:, :tq] - lam * scaled[:, tq:]
    o = o * lax.rsqrt(jnp.mean(o * o, axis=0, keepdims=True) + EPS) * g_col
    return (o * (1.0 - lam_init)).T


def _mask_diagonal_tile(s):
    t = s.shape[0]
    q_chunk = (lax.broadcasted_iota(jnp.int32, (1, s.shape[1]), 1) % t) // CHUNK
    blocks = [jnp.where(q_chunk >= a, s[a * CHUNK:(a + 1) * CHUNK, :], -1e30) for a in range(t // CHUNK)]
    return jnp.concatenate(blocks, axis=0)


def _prompt_attn_kernel(lq_ref, g_ref, q_ref, k_ref, vt_ref, o_ref, qq_ref, p_ref, *, lam_init):
    t = ATTN_TILE
    qi = pl.program_id(2)
    heads = [slice(h * LANES, (h + 1) * LANES) for h in range(HEADS_PER_STEP)]
    for h, c in enumerate(heads):
        qq_ref[h] = _stack_maps(q_ref[:, c])

    ones = jnp.ones((ONES_ROWS, t), BF16)
    stage0 = jnp.minimum(qi, 0)

    def steps(tiles, carry):
        rows = [pl.ds(pl.multiple_of(j * t, t), t) for j, _ in tiles]
        s = [[_flash_scores(qq_ref[h], k_ref[r, c]) for h, c in enumerate(heads)] for r in rows]
        for ti, (_, diagonal) in enumerate(tiles):
            if diagonal:
                s[ti] = [_mask_diagonal_tile(x) for x in s[ti]]
            carry = tuple(
                _flash_update(carry[h], s[ti][h], jnp.concatenate([vt_ref[c, rows[ti]], ones], axis=0),
                              p_ref.at[stage0 + ti * HEADS_PER_STEP + h])
                for h, c in enumerate(heads))
        return carry

    carry = lax.fori_loop(0, qi // 2, lambda p, c: steps([(2 * p, False), (2 * p + 1, False)], c),
                          (_flash_init(2 * t),) * HEADS_PER_STEP)
    carry = lax.cond(qi % 2 == 1,
                     lambda c: steps([(qi - 1, False), (qi, True)], c),
                     lambda c: steps([(qi, True)], c), carry)
    lam = _lambda_scalar(lq_ref[...], lam_init)
    for h, c in enumerate(heads):
        o_ref[:, c] = _attn_finish(carry[h], lam, g_ref[...], lam_init, t).astype(BF16)


def _prompt_attention(lq, g, q, kb, vtb, lam_init, layer):
    b, s, _ = q.shape
    t = ATTN_TILE
    w = HEADS_PER_STEP * LANES
    return pl.pallas_call(
        functools.partial(_prompt_attn_kernel, lam_init=lam_init),
        grid=(b, N_HEADS // HEADS_PER_STEP, s // t),
        in_specs=[
            _layer_spec((4, HEAD_DIM), layer),
            _layer_spec((V_DIM, 1), layer),
            pl.BlockSpec((None, t, w), lambda b_, h, i: (b_, i, h)),
            pl.BlockSpec((None, s, w), lambda b_, h, i: (b_, 0, h)),
            pl.BlockSpec((None, w, s), lambda b_, h, i: (b_, h, 0)),
        ],
        out_specs=pl.BlockSpec((None, t, w), lambda b_, h, i: (b_, i, h)),
        out_shape=jax.ShapeDtypeStruct((b, s, ATTN_WIDTH), BF16),
        scratch_shapes=[pltpu.VMEM((HEADS_PER_STEP, 2 * t, LANES), BF16),
                        pltpu.VMEM((2 * HEADS_PER_STEP, t, 2 * t), BF16)],
        compiler_params=pltpu.CompilerParams(dimension_semantics=("arbitrary",) * 3,
                                             vmem_limit_bytes=VMEM_LIMIT),
        name="prompt_attention",
    )(lq, g, q, kb, vtb)


def _row_scores(qq, kt, *, keys_on_rows):
    dims = (((1,), (1,)), ((), ())) if keys_on_rows else (((1,), (0,)), ((), ()))
    return lax.dot_general(qq, kt, dims, preferred_element_type=F32)


def _row_flash_update(carry, s, vt):
    m, l, acc = carry
    m_new = jnp.maximum(m, jnp.max(s, axis=-1, keepdims=True))
    alpha = jnp.exp2(m - m_new)
    p = jnp.exp2(s - m_new)
    l = alpha * l + jnp.sum(p, axis=-1, keepdims=True)
    acc = alpha * acc + jnp.dot(p.astype(BF16), vt, preferred_element_type=F32)
    return m_new, l, acc


def _sample_attn_kernel(lq_ref, g_ref, q_ref, ck_ref, cv_ref, k_ref, v_ref, o_ref, *, lam_init, past_len):
    t = SAMPLE_KEY_TILE
    tq = q_ref.shape[0]
    heads = [slice(h * LANES, (h + 1) * LANES) for h in range(N_HEADS)]
    qq = [_stack_maps(q_ref[:, c]) for c in heads]

    def body(j, carry):
        cols = pl.ds(pl.multiple_of(j * t, t), t)
        s = [_row_scores(qq[h], ck_ref[c, cols].astype(BF16), keys_on_rows=False) for h, c in enumerate(heads)]
        return tuple(
            _row_flash_update(carry[h], s[h], cv_ref[pl.ds(j * (t * N_HEADS) + h, t, stride=N_HEADS), :].astype(BF16))
            for h in range(N_HEADS))

    init = (jnp.full((2 * tq, 1), -jnp.inf, F32), jnp.zeros((2 * tq, 1), F32), jnp.zeros((2 * tq, V_DIM), F32))
    carry = lax.fori_loop(0, past_len // t, body, (init,) * N_HEADS)
    lam = _lambda_scalar(lq_ref[...], lam_init)
    for h, c in enumerate(heads):
        s = _row_scores(qq[h], k_ref[:, c], keys_on_rows=True)
        _, l, acc = _row_flash_update(carry[h], s, v_ref[:, c])
        o = acc[:tq] / l[:tq] - lam * (acc[tq:] / l[tq:])
        o_ref[:, c] = (_rms(o, g_ref[...]) * (1.0 - lam_init)).astype(BF16)


def _sample_attention(lq, g, q, cache_kt, cache_v4, kb, vb, lam_init, layer):
    b, ls, _ = q.shape
    past_len = cache_kt.shape[-1]
    assert past_len % SAMPLE_KEY_TILE == 0
    return pl.pallas_call(
        functools.partial(_sample_attn_kernel, lam_init=lam_init, past_len=past_len),
        grid=(b,),
        in_specs=[
            _layer_spec((4, HEAD_DIM), layer),
            _layer_spec((1, V_DIM), layer),
            pl.BlockSpec((None, ls, ATTN_WIDTH), lambda b_: (b_, 0, 0)),
            pl.BlockSpec((None, None, ATTN_WIDTH, past_len), lambda b_: (layer, b_, 0, 0)),
            pl.BlockSpec((None, None, N_HEADS * past_len, V_DIM), lambda b_: (layer, b_, 0, 0)),
            pl.BlockSpec((None, ls, ATTN_WIDTH), lambda b_: (b_, 0, 0)),
            pl.BlockSpec((None, ls, ATTN_WIDTH), lambda b_: (b_, 0, 0)),
        ],
        out_specs=pl.BlockSpec((None, ls, ATTN_WIDTH), lambda b_: (b_, 0, 0)),
        out_shape=jax.ShapeDtypeStruct((b, ls, ATTN_WIDTH), BF16),
        compiler_params=pltpu.CompilerParams(dimension_semantics=("arbitrary",),
                                             vmem_limit_bytes=VMEM_LIMIT),
        name="sample_attention",
    )(lq, g, q, cache_kt, cache_v4, kb, vb)


def _pool_mixer(ext_p, pool_w_ref, pool_scale_ref, tm, pos0):
    assert POOL_WINDOWS == (2, 4, 8, 16) and HALO == 32
    e = ext_p[...]
    n = HALO + tm
    s2 = e[8:] + e[7:n - 1]
    s4 = s2[8:] + s2[6:n - 10]
    s8 = s4[8:] + s4[4:n - 20]
    sums = {2: s2[24:], 4: s4[16:], 8: s8[8:], 16: s8[8:] + s8[:n - 32]}
    u = e[HALO:]
    lane = lax.broadcasted_iota(jnp.int32, (tm, POOL_WIDTH), 1)
    group = lane // POOL_GROUP_DIM
    win = sums[POOL_WINDOWS[-1]]
    width = jnp.full((tm, POOL_WIDTH), POOL_WINDOWS[-1], jnp.int32)
    for gi in range(len(POOL_WINDOWS) - 2, -1, -1):
        win = jnp.where(group == gi, sums[POOL_WINDOWS[gi]], win)
        width = jnp.where(group == gi, POOL_WINDOWS[gi], width)
    pos = pos0 + lax.broadcasted_iota(jnp.int32, (tm, POOL_WIDTH), 0)
    count = jnp.minimum(pos + 1, width).astype(F32)
    d = (win / count - u).astype(BF16)
    return jnp.dot(d, pool_w_ref[...], preferred_element_type=F32) * pool_scale_ref[...]


CONV_ROW_BLOCK = 64


def _conv_mixer_steps(ext_c, shift_c, conv_out, dw_ref, dw_b_ref, ln_g_ref, ln_b_ref, pw_ref, tm, store):
    first_row = HALO - CONV_HIST
    blk = min(tm, CONV_ROW_BLOCK)

    def shift(residues):
        for r in residues:
            shift_c[r, 0:HALO - 8 + tm, :] = ext_c[pl.ds(r, HALO - 8 + tm), :]

    def taps(rb):
        yb = jnp.zeros((blk, CONV_WIDTH), F32)
        for k in range(CONV_K):
            r, a = (first_row + k) % 8, (first_row + k) // 8
            rows = pl.ds(8 * a + rb * blk, blk)
            src = ext_c[rows, :] if r == 0 else shift_c[r, rows, :]
            yb = yb + src * dw_ref[k:k + 1, :]
        conv_out[rb * blk:(rb + 1) * blk, :] = yb

    def finish():
        y = conv_out[...] + dw_b_ref[...]
        mu = jnp.mean(y, axis=-1, keepdims=True)
        yc = y - mu
        yn = yc * lax.rsqrt(jnp.mean(yc * yc, axis=-1, keepdims=True) + EPS) * ln_g_ref[...] + ln_b_ref[...]
        act = (yn * jax.nn.sigmoid(yn)).astype(BF16)
        store(jnp.dot(act, pw_ref[...], preferred_element_type=F32))

    steps = [functools.partial(shift, rs) for rs in ((1, 2), (3, 4, 5), (6, 7))]
    steps += [functools.partial(taps, rb) for rb in range(tm // blk)]
    return steps + [finish]


def _conv_mixer(ext_c, shift_c, conv_out, dw_ref, dw_b_ref, ln_g_ref, ln_b_ref, pw_ref, tm):
    out = []
    for step in _conv_mixer_steps(ext_c, shift_c, conv_out, dw_ref, dw_b_ref, ln_g_ref, ln_b_ref, pw_ref, tm,
                                  out.append):
        step()
    return out[0]


def _mixer_scratch(tm):
    return [pltpu.VMEM((HALO + tm, POOL_WIDTH), F32), pltpu.VMEM((HALO + tm, CONV_WIDTH), F32),
            pltpu.VMEM((8, HALO + tm, CONV_WIDTH), F32), pltpu.VMEM((tm, CONV_WIDTH), F32)]


def _mixer_weight_specs(layer):
    shapes = [(POOL_WIDTH, POOL_WIDTH), (1, POOL_WIDTH), (CONV_K, CONV_WIDTH), (1, CONV_WIDTH), (1, CONV_WIDTH),
              (1, CONV_WIDTH), (CONV_WIDTH, CONV_WIDTH)]
    return [_layer_spec(shape, layer) for shape in shapes]


def _sample_mixers_kernel(up_ref, up_hist_ref, uc_ref, uc_hist_ref,
                          pool_w_ref, pool_scale_ref, dw_ref, dw_b_ref, ln_g_ref, ln_b_ref, pw_ref,
                          ypool_ref, yconv_ref, ext_p, ext_c, shift_c, conv_out, *, pos_base):
    tm = up_ref.shape[0]
    ext_p[0:HALO, :] = up_hist_ref[...]
    ext_p[HALO:, :] = up_ref[...]
    ext_c[0:HALO, :] = uc_hist_ref[...]
    ext_c[HALO:, :] = uc_ref[...]
    ypool_ref[...] = _pool_mixer(ext_p, pool_w_ref, pool_scale_ref, tm, pos_base).astype(BF16)
    yconv_ref[...] = _conv_mixer(ext_c, shift_c, conv_out, dw_ref, dw_b_ref, ln_g_ref, ln_b_ref, pw_ref,
                                 tm).astype(BF16)


def _sample_mixers(upool, uconv, hist_pool, hist_conv, mixer_w, seq_len, pos_base, layer):
    n = upool.shape[0]
    cur = pl.BlockSpec((seq_len, POOL_WIDTH), lambda i: (i, 0))
    hist = pl.BlockSpec((None, None, HALO, POOL_WIDTH), lambda i: (layer, i, 0, 0))
    return pl.pallas_call(
        functools.partial(_sample_mixers_kernel, pos_base=pos_base),
        grid=(n // seq_len,),
        in_specs=[cur, hist, cur, hist] + _mixer_weight_specs(layer),
        out_specs=[cur, cur],
        out_shape=[jax.ShapeDtypeStruct((n, POOL_WIDTH), BF16), jax.ShapeDtypeStruct((n, CONV_WIDTH), BF16)],
        scratch_shapes=_mixer_scratch(seq_len),
        compiler_params=pltpu.CompilerParams(dimension_semantics=("arbitrary",),
                                             vmem_limit_bytes=VMEM_LIMIT),
        name="sample_mixers",
    )(upool, hist_pool, uconv, hist_conv, *mixer_w)


def _dense_math(x, ypool, o, yconv, wout_ref, g_ref, wgu_ref, wd_ref, fg_ref, final):
    o1 = POOL_WIDTH
    o2 = o1 + ATTN_WIDTH
    mix = (jnp.dot(ypool, wout_ref[0:o1, :], preferred_element_type=F32)
           + jnp.dot(o, wout_ref[o1:o2, :], preferred_element_type=F32)
           + jnp.dot(yconv, wout_ref[o2:, :], preferred_element_type=F32))
    x1 = x + mix
    h = _rms(x1, g_ref[...]).astype(BF16)
    gu = jnp.dot(h, wgu_ref[...], preferred_element_type=F32)
    gate = gu[:, :D_FF]
    a = (gate * jax.nn.sigmoid(gate) * gu[:, D_FF:]).astype(BF16)
    x2 = x1 + jnp.dot(a, wd_ref[...], preferred_element_type=F32)
    return _rms(x2, fg_ref[...]) if final else x2


def _dense_weight_specs(layer):
    shapes = [(MIX_WIDTH, D_MODEL), (1, D_MODEL), (D_MODEL, 2 * D_FF), (D_FF, D_MODEL)]
    return [_layer_spec(shape, layer) for shape in shapes] + [_const_spec((1, D_MODEL))]


def _prompt_dense_kernel(x_ref, o_ref, up0_ref, uc0_ref, up_ref, uc_ref, hist_p_ref, hist_c_ref,
                         pool_w_ref, pool_scale_ref, dw_ref, dw_b_ref, ln_g_ref, ln_b_ref, pw_ref,
                         wout_ref, g_ref, wgu_ref, wd_ref, fg_ref, out_ref,
                         ext_p, ext_c, shift_c, conv_out, y_pool, y_conv, *, final, tiles_per_seq):
    tm = x_ref.shape[0]
    i = pl.program_id(0)

    def mixer_steps(up_ref_, uc_ref_, tile, slot, first):
        def fill():
            if first is True:
                ext_p[0:HALO, :] = hist_p_ref[...]
                ext_c[0:HALO, :] = hist_c_ref[...]
            else:
                ext_p[0:HALO, :] = jnp.where(first, hist_p_ref[...], ext_p[tm:, :])
                ext_c[0:HALO, :] = jnp.where(first, hist_c_ref[...], ext_c[tm:, :])
            ext_p[HALO:, :] = up_ref_[...]
            ext_c[HALO:, :] = uc_ref_[...]

        def pool():
            pos0 = (tile % tiles_per_seq) * tm
            y_pool[slot] = _pool_mixer(ext_p, pool_w_ref, pool_scale_ref, tm, pos0).astype(BF16)

        def store_conv(y):
            y_conv[slot] = y.astype(BF16)

        return [fill] + _conv_mixer_steps(ext_c, shift_c, conv_out, dw_ref, dw_b_ref, ln_g_ref, ln_b_ref, pw_ref,
                                          tm, store_conv) + [pool]

    @pl.when(i == 0)
    def _():
        for step in mixer_steps(up0_ref, uc0_ref, 0, 0, True):
            step()

    slot = i % 2
    nxt = jnp.minimum(i + 1, pl.num_programs(0) - 1)
    out_ref[...] = _dense_math(x_ref[...], y_pool[slot], o_ref[...], y_conv[slot],
                               wout_ref, g_ref, wgu_ref, wd_ref, fg_ref, final)
    for step in mixer_steps(up_ref, uc_ref, nxt, 1 - slot, nxt % tiles_per_seq == 0):
        step()


def _prompt_dense(x, o, upool, uconv, hist_pool, hist_conv, mixer_w, dense_w, tm, seq_len, final, layer):
    n = x.shape[0]
    tps = seq_len // tm
    last = n // tm - 1
    row = lambda w: pl.BlockSpec((tm, w), lambda i: (i, 0))
    first_tile = lambda w: pl.BlockSpec((tm, w), lambda i: (0, 0))
    next_tile = lambda w: pl.BlockSpec((tm, w), lambda i: (jnp.minimum(i + 1, last), 0))
    next_seq = lambda w: pl.BlockSpec((None, HALO, w), lambda i: (jnp.minimum(i + 1, last) // tps, 0, 0))
    return pl.pallas_call(
        functools.partial(_prompt_dense_kernel, final=final, tiles_per_seq=tps),
        grid=(n // tm,),
        in_specs=[row(D_MODEL), row(ATTN_WIDTH), first_tile(POOL_WIDTH), first_tile(CONV_WIDTH),
                  next_tile(POOL_WIDTH), next_tile(CONV_WIDTH), next_seq(POOL_WIDTH), next_seq(CONV_WIDTH)]
        + _mixer_weight_specs(layer) + _dense_weight_specs(layer),
        out_specs=row(D_MODEL),
        out_shape=jax.ShapeDtypeStruct((n, D_MODEL), F32),
        scratch_shapes=_mixer_scratch(tm) + [pltpu.VMEM((2, tm, POOL_WIDTH), BF16),
                                             pltpu.VMEM((2, tm, CONV_WIDTH), BF16)],
        compiler_params=pltpu.CompilerParams(dimension_semantics=("arbitrary",),
                                             vmem_limit_bytes=VMEM_LIMIT),
        name="prompt_dense",
    )(x, o, upool, uconv, upool, uconv, hist_pool, hist_conv, *mixer_w, *dense_w)


def _sample_dense_kernel(x_ref, yp_ref, o_ref, yc_ref, wout_ref, g_ref, wgu_ref, wd_ref, fg_ref, out_ref, *, final):
    out_ref[...] = _dense_math(x_ref[...], yp_ref[...], o_ref[...], yc_ref[...],
                               wout_ref, g_ref, wgu_ref, wd_ref, fg_ref, final)


def _sample_dense(x, ypool, o, yconv, wout_bf, g, wgu_bf, wd_bf, fg, tm, final, layer):
    n = x.shape[0]
    row = lambda w: pl.BlockSpec((tm, w), lambda i: (i, 0))
    return pl.pallas_call(
        functools.partial(_sample_dense_kernel, final=final),
        grid=(n // tm,),
        in_specs=[row(D_MODEL), row(POOL_WIDTH), row(ATTN_WIDTH), row(CONV_WIDTH)] + _dense_weight_specs(layer),
        out_specs=row(D_MODEL),
        out_shape=jax.ShapeDtypeStruct((n, D_MODEL), F32),
        compiler_params=pltpu.CompilerParams(dimension_semantics=("arbitrary",),
                                             vmem_limit_bytes=VMEM_LIMIT),
        name="sample_dense",
    )(x, ypool, o, yconv, wout_bf, g, wgu_bf, wd_bf, fg)


def _rope_tables(pos):
    half = HEAD_DIM // 2
    inv = ROPE_THETA ** (-jnp.arange(half, dtype=F32) / half)
    ang = pos.astype(F32)[:, None] * inv[None, :]
    reps = LANES // half
    cos = jnp.tile(jnp.cos(ang), (1, reps))
    sign = jnp.tile(jnp.concatenate([-jnp.ones((half,), F32), jnp.ones((half,), F32)]), LANES // HEAD_DIM)
    sin = jnp.tile(jnp.sin(ang), (1, reps)) * sign[None, :]
    return cos, sin


def _block_diag(pool_w):
    depth, g, d, _ = pool_w.shape
    eye = jnp.eye(g, dtype=pool_w.dtype)
    return (pool_w[:, :, :, None, :] * eye[None, :, None, :, None]).reshape(depth, g * d, g * d)


def _pad_hist(h):
    return jnp.pad(h, ((0, 0), (0, 0), (HALO - h.shape[2], 0), (0, 0)))


def kernel(x_prompt, x_sample, cache_k, cache_v, state_pool, state_conv, norm_mix_g, w_in, pool_w, pool_scale,
           lambda_qk, diff_norm_g, conv_dw, conv_dw_b, conv_ln_g, conv_ln_b, conv_pw, w_out, norm_ffn_g,
           w_gate_up, w_down, final_norm_g):
    B, S, _ = x_prompt.shape
    Bs, Ls, _ = x_sample.shape
    depth = w_in.shape[0]
    past_len = cache_k.shape[2]
    assert S % ATTN_TILE == 0 and S % ROW_TILE == 0 and (Bs * Ls) % 16 == 0 and Ls % 16 == 0 and Ls >= CONV_HIST

    cos_p, sin_p = _rope_tables(jnp.arange(S, dtype=jnp.int32))
    cos_s, sin_s = _rope_tables(past_len + jnp.arange(Ls, dtype=jnp.int32))
    cos_s, sin_s = jnp.tile(cos_s, (Bs, 1)), jnp.tile(sin_s, (Bs, 1))
    zero_pool = jnp.zeros((B, HALO, POOL_WIDTH), F32)
    zero_conv = jnp.zeros((B, HALO, CONV_WIDTH), F32)

    xp = x_prompt.reshape(B * S, D_MODEL)
    xs = x_sample.reshape(Bs * Ls, D_MODEL)
    rows = lambda a: a.reshape(depth, 1, -1)
    norm_mix_rows, norm_ffn_rows = rows(norm_mix_g), rows(norm_ffn_g)
    diff_g_row, diff_g_col = rows(diff_norm_g), diff_norm_g.reshape(depth, -1, 1)
    w_in_bf, w_out_bf, w_gu_bf, w_d_bf = (w.astype(BF16) for w in (w_in, w_out, w_gate_up, w_down))
    mixer_w = (_block_diag(pool_w).astype(BF16), rows(pool_scale), conv_dw, rows(conv_dw_b), rows(conv_ln_g),
               rows(conv_ln_b), conv_pw.astype(BF16))
    dense_w = (w_out_bf, norm_ffn_rows, w_gu_bf, w_d_bf, final_norm_g.reshape(1, -1))
    hist_pool_s, hist_conv_s = _pad_hist(state_pool), _pad_hist(state_conv)
    cache_kt = jnp.transpose(cache_k, (0, 1, 3, 4, 5, 2)).reshape(depth, Bs, ATTN_WIDTH, past_len)
    cache_v4 = cache_v.reshape(depth, Bs, past_len * N_HEADS, V_DIM)
    kv_stack = depth
    pp, cp = [], []
    ks_, vs_, ps_, cs_ = [], [], [], []
    for l in range(depth):
        lam_init = 0.8 - 0.6 * math.exp(-0.3 * l)
        final = l == depth - 1

        upool, q, k, kb, v, vb, uconv = _prompt_in_proj(xp, norm_mix_rows, w_in_bf, cos_p, sin_p, ROW_TILE, S,
                                                        kv_stack, l)
        kv_stack = (k, v)
        o = _prompt_attention(lambda_qk, diff_g_col, q.reshape(B, S, -1), kb.reshape(B, S, -1),
                              vb, lam_init, l)
        xp = _prompt_dense(xp, o.reshape(B * S, -1), upool, uconv, zero_pool, zero_conv, mixer_w, dense_w,
                           ROW_TILE, S, final, l)
        pp.append(upool.reshape(B, S, -1)[:, S - POOL_HIST:])
        cp.append(uconv.reshape(B, S, -1)[:, S - CONV_HIST:])

        upool, q, k, kb, v, vb, uconv = _sample_in_proj(xs, norm_mix_rows, w_in_bf, cos_s, sin_s, l)
        o = _sample_attention(lambda_qk, diff_g_row, q.reshape(Bs, Ls, -1), cache_kt, cache_v4,
                              kb.reshape(Bs, Ls, -1), vb.reshape(Bs, Ls, -1), lam_init, l)
        ypool, yconv = _sample_mixers(upool, uconv, hist_pool_s, hist_conv_s, mixer_w, Ls, past_len, l)
        xs = _sample_dense(xs, ypool, o.reshape(Bs * Ls, -1), yconv, *dense_w, Bs * Ls, final, l)
        ks_.append(k.reshape(Bs, Ls, N_HEADS, 2, HEAD_DIM))
        vs_.append(v.reshape(Bs, Ls, N_HEADS, V_DIM))
        ps_.append(upool.reshape(Bs, Ls, -1)[:, Ls - POOL_HIST:])
        cs_.append(uconv.reshape(Bs, Ls, -1)[:, Ls - CONV_HIST:])

    kt_all, v4_all = kv_stack
    new_k = jnp.transpose(kt_all.reshape(depth, B, N_HEADS, 2, HEAD_DIM, S), (0, 1, 5, 2, 3, 4))
    new_v = v4_all.reshape(depth, B, S, N_HEADS, V_DIM)
    return (xp.reshape(B, S, D_MODEL), xs.reshape(Bs, Ls, D_MODEL),
            new_k, new_v, jnp.stack(pp), jnp.stack(cp),
            jnp.stack(ks_), jnp.stack(vs_), jnp.stack(ps_), jnp.stack(cs_))
```

```python
import functools
import math

import jax
import jax.numpy as jnp
from jax import lax
from jax.experimental import pallas as pl
from jax.experimental.pallas import tpu as pltpu

D_MODEL = 1024
CHUNK = 64
POOL_WIDTH = D_MODEL // 4
POOL_WINDOWS = (2, 4, 8, 16)
POOL_GROUP_DIM = POOL_WIDTH // len(POOL_WINDOWS)
POOL_HIST = max(POOL_WINDOWS) - 1
ATTN_WIDTH = D_MODEL // 2
N_HEADS = 4
HEAD_DIM = ATTN_WIDTH // (2 * N_HEADS)
V_DIM = 2 * HEAD_DIM
ROPE_THETA = 10000.0
CONV_WIDTH = D_MODEL // 4
CONV_K = 31
CONV_HIST = CONV_K - 1
MIX_WIDTH = POOL_WIDTH + ATTN_WIDTH + CONV_WIDTH
IN_WIDTH = POOL_WIDTH + 3 * ATTN_WIDTH + 2 * CONV_WIDTH
D_FF = ((-(-8 * D_MODEL // 3) + 255) // 256) * 256
EPS = 1e-6

LANES = 128
HALO = 32
ROW_TILE = 512
ATTN_TILE = 256
SAMPLE_KEY_TILE = 1024
HEADS_PER_STEP = 4
VMEM_LIMIT = 56 * 1024 * 1024

F32 = jnp.float32
BF16 = jnp.bfloat16


def _rms(x, g):
    return x * lax.rsqrt(jnp.mean(x * x, axis=-1, keepdims=True) + EPS) * g


def _const_spec(shape):
    return pl.BlockSpec(shape, lambda *_: (0,) * len(shape), pipeline_mode=pl.Buffered(1))


def _layer_spec(shape, layer):
    return pl.BlockSpec((None,) + shape, lambda *_: (layer,) + (0,) * len(shape), pipeline_mode=pl.Buffered(1))


O_Q = POOL_WIDTH
O_K = O_Q + ATTN_WIDTH
O_V = O_K + ATTN_WIDTH
O_A = O_V + ATTN_WIDTH
O_B = O_A + CONV_WIDTH


def _qkv_epilogue(z, cos_ref, sin_ref, q_ref, k_ref, kb_ref, v_ref, vb_ref, cache_layout):
    tm = z.shape[0]
    o1, o2, o3 = O_Q, O_K, O_V
    cos = cos_ref[...]
    sin = sin_ref[...]
    lane = lax.broadcasted_iota(jnp.int32, cos.shape, 1)
    first_half = (lane & (HEAD_DIM // 2)) == 0

    def rope(t):
        up = pltpu.roll(t, LANES - HEAD_DIM // 2, axis=1)
        down = pltpu.roll(t, HEAD_DIM // 2, axis=1)
        return t * cos + jnp.where(first_half, up, down) * sin

    scale = HEAD_DIM ** -0.5 * math.log2(math.e)
    for j in range(ATTN_WIDTH // LANES):
        sl = slice(j * LANES, (j + 1) * LANES)
        qj = rope(z[:, o1 + j * LANES:o1 + (j + 1) * LANES])
        kj = rope(z[:, o2 + j * LANES:o2 + (j + 1) * LANES])
        vj = z[:, o3 + j * LANES:o3 + (j + 1) * LANES]
        q_ref[:, sl] = (qj * scale).astype(BF16)
        kb_ref[:, sl] = kj.astype(BF16)
        if cache_layout:
            k_ref[sl, :] = kj.T
            v_ref[pl.ds(j, tm, stride=N_HEADS), :] = vj
            vb_ref[sl, :] = vj.T.astype(BF16)
        else:
            k_ref[:, sl] = kj
            v_ref[:, sl] = vj
            vb_ref[:, sl] = vj.astype(BF16)


def _sample_in_proj_kernel(x_ref, g_ref, w_ref, cos_ref, sin_ref,
                           upool_ref, q_ref, k_ref, kb_ref, v_ref, vb_ref, uconv_ref):
    h = _rms(x_ref[...], g_ref[...]).astype(BF16)
    z = jnp.dot(h, w_ref[...], preferred_element_type=F32)
    upool_ref[...] = z[:, :O_Q]
    uconv_ref[...] = z[:, O_A:O_B] * jax.nn.sigmoid(z[:, O_B:])
    _qkv_epilogue(z, cos_ref, sin_ref, q_ref, k_ref, kb_ref, v_ref, vb_ref, False)


def _sample_in_proj(x, g, w_bf, cos_tab, sin_tab, layer):
    n = x.shape[0]
    full = lambda w: pl.BlockSpec((n, w), lambda i: (0, 0))
    flat = lambda dt: jax.ShapeDtypeStruct((n, ATTN_WIDTH), dt)
    return pl.pallas_call(
        _sample_in_proj_kernel,
        grid=(1,),
        in_specs=[full(D_MODEL), _layer_spec((1, D_MODEL), layer), _layer_spec((D_MODEL, IN_WIDTH), layer),
                  full(LANES), full(LANES)],
        out_specs=[full(POOL_WIDTH)] + [full(ATTN_WIDTH)] * 5 + [full(CONV_WIDTH)],
        out_shape=[jax.ShapeDtypeStruct((n, POOL_WIDTH), F32), flat(BF16), flat(F32), flat(BF16), flat(F32),
                   flat(BF16), jax.ShapeDtypeStruct((n, CONV_WIDTH), F32)],
        compiler_params=pltpu.CompilerParams(dimension_semantics=("arbitrary",),
                                             vmem_limit_bytes=VMEM_LIMIT),
        name="sample_in_proj",
    )(x, g, w_bf, cos_tab, sin_tab)


def _prompt_in_proj_kernel(*refs, n_unused):
    x_ref, g_ref, w_ref, cos_ref, sin_ref = refs[:5]
    upool_ref, q_ref, k_ref, kb_ref, v_ref, vb_ref, uconv_ref = refs[5 + n_unused:]
    h = _rms(x_ref[...], g_ref[...]).astype(BF16)
    z = jnp.dot(h, w_ref[...], preferred_element_type=F32)
    upool_ref[...] = z[:, :O_Q]
    uconv_ref[...] = z[:, O_A:O_B] * jax.nn.sigmoid(z[:, O_B:])
    _qkv_epilogue(z, cos_ref, sin_ref, q_ref, k_ref, kb_ref, v_ref, vb_ref, True)


def _prompt_in_proj(x, g, w_bf, cos_tab, sin_tab, tm, seq_len, kv_stack, layer):
    n = x.shape[0]
    tps = seq_len // tm
    batch = n // seq_len
    row = lambda w: pl.BlockSpec((tm, w), lambda i: (i, 0))
    tab = pl.BlockSpec((tm, LANES), lambda i: (i % tps, 0))
    flat = lambda dt, w=ATTN_WIDTH: jax.ShapeDtypeStruct((n, w), dt)
    extra_in, extra_specs, aliases = [], [], {}
    if isinstance(kv_stack, int):
        k_shape = jax.ShapeDtypeStruct((kv_stack, batch, ATTN_WIDTH, seq_len), F32)
        v_shape = jax.ShapeDtypeStruct((kv_stack, batch, N_HEADS * seq_len, V_DIM), F32)
    else:
        k_shape, v_shape = (jax.ShapeDtypeStruct(a.shape, a.dtype) for a in kv_stack)
        extra_in = list(kv_stack)
        extra_specs = [pl.BlockSpec(memory_space=pl.ANY)] * 2
        aliases = {5: 2, 6: 4}
    k_spec = pl.BlockSpec((None, None, ATTN_WIDTH, tm), lambda i: (layer, i // tps, 0, i % tps))
    v_spec = pl.BlockSpec((None, None, N_HEADS * tm, V_DIM), lambda i: (layer, i // tps, i % tps, 0))
    vt_spec = pl.BlockSpec((None, ATTN_WIDTH, tm), lambda i: (i // tps, 0, i % tps))
    return pl.pallas_call(
        functools.partial(_prompt_in_proj_kernel, n_unused=len(extra_in)),
        grid=(n // tm,),
        in_specs=[row(D_MODEL), _layer_spec((1, D_MODEL), layer), _layer_spec((D_MODEL, IN_WIDTH), layer), tab, tab]
        + extra_specs,
        out_specs=[row(POOL_WIDTH), row(ATTN_WIDTH), k_spec, row(ATTN_WIDTH), v_spec, vt_spec, row(CONV_WIDTH)],
        out_shape=[flat(F32, POOL_WIDTH), flat(BF16), k_shape, flat(BF16), v_shape,
                   jax.ShapeDtypeStruct((batch, ATTN_WIDTH, seq_len), BF16), flat(F32, CONV_WIDTH)],
        input_output_aliases=aliases,
        compiler_params=pltpu.CompilerParams(dimension_semantics=("arbitrary",),
                                             vmem_limit_bytes=VMEM_LIMIT),
        name="prompt_in_proj",
    )(x, g, w_bf, cos_tab, sin_tab, *extra_in)


def _lambda_scalar(lq, lam_init):
    a = jnp.sum(lq[0:1, :] * lq[1:2, :], axis=-1, keepdims=True)
    b = jnp.sum(lq[2:3, :] * lq[3:4, :], axis=-1, keepdims=True)
    return jnp.exp(a) - jnp.exp(b) + lam_init


def _stack_maps(q):
    lane = lax.broadcasted_iota(jnp.int32, q.shape, 1)
    zero = jnp.zeros_like(q)
    return jnp.concatenate([jnp.where(lane < HEAD_DIM, q, zero), jnp.where(lane >= HEAD_DIM, q, zero)], axis=0)


ONES_ROWS = 16


def _flash_scores(qq, kt):
    return lax.dot_general(kt, qq, (((1,), (1,)), ((), ())), preferred_element_type=F32)


def _flash_update(carry, s, vt_ones, p_stage):
    m, acc = carry
    m_new = jnp.maximum(m, jnp.max(s, axis=0, keepdims=True))
    alpha = jnp.exp2(m - m_new)
    p_stage[...] = jnp.exp2(s - m_new).astype(BF16)
    return m_new, alpha * acc + jnp.dot(vt_ones, p_stage[...], preferred_element_type=F32)


def _flash_init(cols):
    return jnp.full((1, cols), -jnp.inf, F32), jnp.zeros((V_DIM + ONES_ROWS, cols), F32)


def _attn_finish(carry, lam, g_col, lam_init, tq):
    _, acc = carry
    scaled = acc[:V_DIM] * (1.0 / acc[V_DIM:V_DIM + 1])
    o = scaled[:, :tq] - lam * scaled[:, tq:]
    o = o * lax.rsqrt(jnp.mean(o * o, axis=0, keepdims=True) + EPS) * g_col
    return (o * (1.0 - lam_init)).T


def _mask_diagonal_tile(s):
    t = s.shape[0]
    q_chunk = (lax.broadcasted_iota(jnp.int32, (1, s.shape[1]), 1) % t) // CHUNK
    blocks = [jnp.where(q_chunk >= a, s[a * CHUNK:(a + 1) * CHUNK, :], -1e30) for a in range(t // CHUNK)]
    return jnp.concatenate(blocks, axis=0)


def _prompt_attn_kernel(lq_ref, g_ref, q_ref, k_ref, vt_ref, o_ref, qq_ref, p_ref, *, lam_init):
    t = ATTN_TILE
    qi = pl.program_id(2)
    heads = [slice(h * LANES, (h + 1) * LANES) for h in range(HEADS_PER_STEP)]
    for h, c in enumerate(heads):
        qq_ref[h] = _stack_maps(q_ref[:, c])

    ones = jnp.ones((ONES_ROWS, t), BF16)
    stage0 = jnp.minimum(qi, 0)

    def steps(tiles, carry):
        rows = [pl.ds(pl.multiple_of(j * t, t), t) for j, _ in tiles]
        s = [[_flash_scores(qq_ref[h], k_ref[r, c]) for h, c in enumerate(heads)] for r in rows]
        for ti, (_, diagonal) in enumerate(tiles):
            if diagonal:
                s[ti] = [_mask_diagonal_tile(x) for x in s[ti]]
            carry = tuple(
                _flash_update(carry[h], s[ti][h], jnp.concatenate([vt_ref[c, rows[ti]], ones], axis=0),
                              p_ref.at[stage0 + ti * HEADS_PER_STEP + h])
                for h, c in enumerate(heads))
        return carry

    carry = lax.fori_loop(0, qi // 2, lambda p, c: steps([(2 * p, False), (2 * p + 1, False)], c),
                          (_flash_init(2 * t),) * HEADS_PER_STEP)
    carry = lax.cond(qi % 2 == 1,
                     lambda c: steps([(qi - 1, False), (qi, True)], c),
                     lambda c: steps([(qi, True)], c), carry)
    lam = _lambda_scalar(lq_ref[...], lam_init)
    for h, c in enumerate(heads):
        o_ref[:, c] = _attn_finish(carry[h], lam, g_ref[...], lam_init, t).astype(BF16)


def _prompt_attention(lq, g, q, kb, vtb, lam_init, layer):
    b, s, _ = q.shape
    t = ATTN_TILE
    w = HEADS_PER_STEP * LANES
    return pl.pallas_call(
        functools.partial(_prompt_attn_kernel, lam_init=lam_init),
        grid=(b, N_HEADS // HEADS_PER_STEP, s // t),
        in_specs=[
            _layer_spec((4, HEAD_DIM), layer),
            _layer_spec((V_DIM, 1), layer),
            pl.BlockSpec((None, t, w), lambda b_, h, i: (b_, i, h)),
            pl.BlockSpec((None, s, w), lambda b_, h, i: (b_, 0, h)),
            pl.BlockSpec((None, w, s), lambda b_, h, i: (b_, h, 0)),
        ],
        out_specs=pl.BlockSpec((None, t, w), lambda b_, h, i: (b_, i, h)),
        out_shape=jax.ShapeDtypeStruct((b, s, ATTN_WIDTH), BF16),
        scratch_shapes=[pltpu.VMEM((HEADS_PER_STEP, 2 * t, LANES), BF16),
                        pltpu.VMEM((2 * HEADS_PER_STEP, t, 2 * t), BF16)],
        compiler_params=pltpu.CompilerParams(dimension_semantics=("arbitrary",) * 3,
                                             vmem_limit_bytes=VMEM_LIMIT),
        name="prompt_attention",
    )(lq, g, q, kb, vtb)


def _row_scores(qq, kt, *, keys_on_rows):
    dims = (((1,), (1,)), ((), ())) if keys_on_rows else (((1,), (0,)), ((), ()))
    return lax.dot_general(qq, kt, dims, preferred_element_type=F32)


def _row_flash_update(carry, s, vt):
    m, l, acc = carry
    m_new = jnp.maximum(m, jnp.max(s, axis=-1, keepdims=True))
    alpha = jnp.exp2(m - m_new)
    p = jnp.exp2(s - m_new)
    l = alpha * l + jnp.sum(p, axis=-1, keepdims=True)
    acc = alpha * acc + jnp.dot(p.astype(BF16), vt, preferred_element_type=F32)
    return m_new, l, acc


def _sample_attn_kernel(lq_ref, g_ref, q_ref, ck_ref, cv_ref, k_ref, v_ref, o_ref, *, lam_init, past_len):
    t = SAMPLE_KEY_TILE
    tq = q_ref.shape[0]
    heads = [slice(h * LANES, (h + 1) * LANES) for h in range(N_HEADS)]
    qq = [_stack_maps(q_ref[:, c]) for c in heads]

    def body(j, carry):
        cols = pl.ds(pl.multiple_of(j * t, t), t)
        s = [_row_scores(qq[h], ck_ref[c, cols].astype(BF16), keys_on_rows=False) for h, c in enumerate(heads)]
        return tuple(
            _row_flash_update(carry[h], s[h], cv_ref[pl.ds(j * (t * N_HEADS) + h, t, stride=N_HEADS), :].astype(BF16))
            for h in range(N_HEADS))

    init = (jnp.full((2 * tq, 1), -jnp.inf, F32), jnp.zeros((2 * tq, 1), F32), jnp.zeros((2 * tq, V_DIM), F32))
    carry = lax.fori_loop(0, past_len // t, body, (init,) * N_HEADS)
    lam = _lambda_scalar(lq_ref[...], lam_init)
    for h, c in enumerate(heads):
        s = _row_scores(qq[h], k_ref[:, c], keys_on_rows=True)
        _, l, acc = _row_flash_update(carry[h], s, v_ref[:, c])
        o = acc[:tq] / l[:tq] - lam * (acc[tq:] / l[tq:])
        o_ref[:, c] = (_rms(o, g_ref[...]) * (1.0 - lam_init)).astype(BF16)


def _sample_attention(lq, g, q, cache_kt, cache_v4, kb, vb, lam_init, layer):
    b, ls, _ = q.shape
    past_len = cache_kt.shape[-1]
    assert past_len % SAMPLE_KEY_TILE == 0
    return pl.pallas_call(
        functools.partial(_sample_attn_kernel, lam_init=lam_init, past_len=past_len),
        grid=(b,),
        in_specs=[
            _layer_spec((4, HEAD_DIM), layer),
            _layer_spec((1, V_DIM), layer),
            pl.BlockSpec((None, ls, ATTN_WIDTH), lambda b_: (b_, 0, 0)),
            pl.BlockSpec((None, None, ATTN_WIDTH, past_len), lambda b_: (layer, b_, 0, 0)),
            pl.BlockSpec((None, None, N_HEADS * past_len, V_DIM), lambda b_: (layer, b_, 0, 0)),
            pl.BlockSpec((None, ls, ATTN_WIDTH), lambda b_: (b_, 0, 0)),
            pl.BlockSpec((None, ls, ATTN_WIDTH), lambda b_: (b_, 0, 0)),
        ],
        out_specs=pl.BlockSpec((None, ls, ATTN_WIDTH), lambda b_: (b_, 0, 0)),
        out_shape=jax.ShapeDtypeStruct((b, ls, ATTN_WIDTH), BF16),
        compiler_params=pltpu.CompilerParams(dimension_semantics=("arbitrary",),
                                             vmem_limit_bytes=VMEM_LIMIT),
        name="sample_attention",
    )(lq, g, q, cache_kt, cache_v4, kb, vb)


def _pool_mixer(ext_p, pool_w_ref, pool_scale_ref, tm, pos0):
    assert POOL_WINDOWS == (2, 4, 8, 16) and HALO == 32
    e = ext_p[...]
    n = HALO + tm
    s2 = e[8:] + e[7:n - 1]
    s4 = s2[8:] + s2[6:n - 10]
    s8 = s4[8:] + s4[4:n - 20]
    sums = {2: s2[24:], 4: s4[16:], 8: s8[8:], 16: s8[8:] + s8[:n - 32]}
    u = e[HALO:]
    lane = lax.broadcasted_iota(jnp.int32, (tm, POOL_WIDTH), 1)
    group = lane // POOL_GROUP_DIM
    win = sums[POOL_WINDOWS[-1]]
    width = jnp.full((tm, POOL_WIDTH), POOL_WINDOWS[-1], jnp.int32)
    for gi in range(len(POOL_WINDOWS) - 2, -1, -1):
        win = jnp.where(group == gi, sums[POOL_WINDOWS[gi]], win)
        width = jnp.where(group == gi, POOL_WINDOWS[gi], width)
    pos = pos0 + lax.broadcasted_iota(jnp.int32, (tm, POOL_WIDTH), 0)
    count = jnp.minimum(pos + 1, width).astype(F32)
    d = (win / count - u).astype(BF16)
    return jnp.dot(d, pool_w_ref[...], preferred_element_type=F32) * pool_scale_ref[...]


CONV_ROW_BLOCK = 64


def _conv_mixer_steps(ext_c, shift_c, conv_out, dw_ref, dw_b_ref, ln_g_ref, ln_b_ref, pw_ref, tm, store):
    first_row = HALO - CONV_HIST
    blk = min(tm, CONV_ROW_BLOCK)

    def shift(residues):
        e = ext_c[...]
        for r in residues:
            shift_c[r, 0:HALO - 8 + tm, :] = pltpu.roll(e, HALO + tm - r, axis=0)[0:HALO - 8 + tm, :]

    def taps(rb):
        yb = jnp.zeros((blk, CONV_WIDTH), F32)
        for k in range(CONV_K):
            r, a = (first_row + k) % 8, (first_row + k) // 8
            rows = pl.ds(8 * a + rb * blk, blk)
            src = ext_c[rows, :] if r == 0 else shift_c[r, rows, :]
            yb = yb + src * dw_ref[k:k + 1, :]
        conv_out[rb * blk:(rb + 1) * blk, :] = yb

    def finish():
        y = conv_out[...] + dw_b_ref[...]
        mu = jnp.mean(y, axis=-1, keepdims=True)
        yc = y - mu
        yn = yc * lax.rsqrt(jnp.mean(yc * yc, axis=-1, keepdims=True) + EPS) * ln_g_ref[...] + ln_b_ref[...]
        act = (yn * jax.nn.sigmoid(yn)).astype(BF16)
        store(jnp.dot(act, pw_ref[...], preferred_element_type=F32))

    steps = [functools.partial(shift, rs) for rs in ((1, 2), (3, 4, 5), (6, 7))]
    steps += [functools.partial(taps, rb) for rb in range(tm // blk)]
    return steps + [finish]


def _conv_mixer(ext_c, shift_c, conv_out, dw_ref, dw_b_ref, ln_g_ref, ln_b_ref, pw_ref, tm):
    out = []
    for step in _conv_mixer_steps(ext_c, shift_c, conv_out, dw_ref, dw_b_ref, ln_g_ref, ln_b_ref, pw_ref, tm,
                                  out.append):
        step()
    return out[0]


def _mixer_scratch(tm):
    return [pltpu.VMEM((HALO + tm, POOL_WIDTH), F32), pltpu.VMEM((HALO + tm, CONV_WIDTH), F32),
            pltpu.VMEM((8, HALO + tm, CONV_WIDTH), F32), pltpu.VMEM((tm, CONV_WIDTH), F32)]


def _mixer_weight_specs(layer):
    shapes = [(POOL_WIDTH, POOL_WIDTH), (1, POOL_WIDTH), (CONV_K, CONV_WIDTH), (1, CONV_WIDTH), (1, CONV_WIDTH),
              (1, CONV_WIDTH), (CONV_WIDTH, CONV_WIDTH)]
    return [_layer_spec(shape, layer) for shape in shapes]


def _sample_mixers_kernel(up_ref, up_hist_ref, uc_ref, uc_hist_ref,
                          pool_w_ref, pool_scale_ref, dw_ref, dw_b_ref, ln_g_ref, ln_b_ref, pw_ref,
                          ypool_ref, yconv_ref, ext_p, ext_c, shift_c, conv_out, *, pos_base):
    tm = up_ref.shape[0]
    ext_p[0:HALO, :] = up_hist_ref[...]
    ext_p[HALO:, :] = up_ref[...]
    ext_c[0:HALO, :] = uc_hist_ref[...]
    ext_c[HALO:, :] = uc_ref[...]
    ypool_ref[...] = _pool_mixer(ext_p, pool_w_ref, pool_scale_ref, tm, pos_base).astype(BF16)
    yconv_ref[...] = _conv_mixer(ext_c, shift_c, conv_out, dw_ref, dw_b_ref, ln_g_ref, ln_b_ref, pw_ref,
                                 tm).astype(BF16)


def _sample_mixers(upool, uconv, hist_pool, hist_conv, mixer_w, seq_len, pos_base, layer):
    n = upool.shape[0]
    cur = pl.BlockSpec((seq_len, POOL_WIDTH), lambda i: (i, 0))
    hist = pl.BlockSpec((None, None, HALO, POOL_WIDTH), lambda i: (layer, i, 0, 0))
    return pl.pallas_call(
        functools.partial(_sample_mixers_kernel, pos_base=pos_base),
        grid=(n // seq_len,),
        in_specs=[cur, hist, cur, hist] + _mixer_weight_specs(layer),
        out_specs=[cur, cur],
        out_shape=[jax.ShapeDtypeStruct((n, POOL_WIDTH), BF16), jax.ShapeDtypeStruct((n, CONV_WIDTH), BF16)],
        scratch_shapes=_mixer_scratch(seq_len),
        compiler_params=pltpu.CompilerParams(dimension_semantics=("arbitrary",),
                                             vmem_limit_bytes=VMEM_LIMIT),
        name="sample_mixers",
    )(upool, hist_pool, uconv, hist_conv, *mixer_w)


def _dense_math(x, ypool, o, yconv, wout_ref, g_ref, wgu_ref, wd_ref, fg_ref, final):
    o1 = POOL_WIDTH
    o2 = o1 + ATTN_WIDTH
    mix = (jnp.dot(ypool, wout_ref[0:o1, :], preferred_element_type=F32)
           + jnp.dot(o, wout_ref[o1:o2, :], preferred_element_type=F32)
           + jnp.dot(yconv, wout_ref[o2:, :], preferred_element_type=F32))
    x1 = x + mix
    h = _rms(x1, g_ref[...]).astype(BF16)
    gu = jnp.dot(h, wgu_ref[...], preferred_element_type=F32)
    gate = gu[:, :D_FF]
    a = (gate * jax.nn.sigmoid(gate) * gu[:, D_FF:]).astype(BF16)
    x2 = x1 + jnp.dot(a, wd_ref[...], preferred_element_type=F32)
    return _rms(x2, fg_ref[...]) if final else x2


def _dense_weight_specs(layer):
    shapes = [(MIX_WIDTH, D_MODEL), (1, D_MODEL), (D_MODEL, 2 * D_FF), (D_FF, D_MODEL)]
    return [_layer_spec(shape, layer) for shape in shapes] + [_const_spec((1, D_MODEL))]


def _prompt_dense_kernel(x_ref, o_ref, up0_ref, uc0_ref, up_ref, uc_ref, hist_p_ref, hist_c_ref,
                         pool_w_ref, pool_scale_ref, dw_ref, dw_b_ref, ln_g_ref, ln_b_ref, pw_ref,
                         wout_ref, g_ref, wgu_ref, wd_ref, fg_ref, out_ref,
                         ext_p, ext_c, shift_c, conv_out, y_pool, y_conv, *, final, tiles_per_seq):
    tm = x_ref.shape[0]
    i = pl.program_id(0)

    def mixer_steps(up_ref_, uc_ref_, tile, slot, first):
        def fill():
            if first is True:
                ext_p[0:HALO, :] = hist_p_ref[...]
                ext_c[0:HALO, :] = hist_c_ref[...]
            else:
                ext_p[0:HALO, :] = jnp.where(first, hist_p_ref[...], ext_p[tm:, :])
                ext_c[0:HALO, :] = jnp.where(first, hist_c_ref[...], ext_c[tm:, :])
            ext_p[HALO:, :] = up_ref_[...]
            ext_c[HALO:, :] = uc_ref_[...]

        def pool():
            pos0 = (tile % tiles_per_seq) * tm
            y_pool[slot] = _pool_mixer(ext_p, pool_w_ref, pool_scale_ref, tm, pos0).astype(BF16)

        def store_conv(y):
            y_conv[slot] = y.astype(BF16)

        return [fill] + _conv_mixer_steps(ext_c, shift_c, conv_out, dw_ref, dw_b_ref, ln_g_ref, ln_b_ref, pw_ref,
                                          tm, store_conv) + [pool]

    @pl.when(i == 0)
    def _():
        for step in mixer_steps(up0_ref, uc0_ref, 0, 0, True):
            step()

    slot = i % 2
    nxt = jnp.minimum(i + 1, pl.num_programs(0) - 1)
    out_ref[...] = _dense_math(x_ref[...], y_pool[slot], o_ref[...], y_conv[slot],
                               wout_ref, g_ref, wgu_ref, wd_ref, fg_ref, final)
    for step in mixer_steps(up_ref, uc_ref, nxt, 1 - slot, nxt % tiles_per_seq == 0):
        step()


def _prompt_dense(x, o, upool, uconv, hist_pool, hist_conv, mixer_w, dense_w, tm, seq_len, final, layer):
    n = x.shape[0]
    tps = seq_len // tm
    last = n // tm - 1
    row = lambda w: pl.BlockSpec((tm, w), lambda i: (i, 0))
    first_tile = lambda w: pl.BlockSpec((tm, w), lambda i: (0, 0))
    next_tile = lambda w: pl.BlockSpec((tm, w), lambda i: (jnp.minimum(i + 1, last), 0))
    next_seq = lambda w: pl.BlockSpec((None, HALO, w), lambda i: (jnp.minimum(i + 1, last) // tps, 0, 0))
    return pl.pallas_call(
        functools.partial(_prompt_dense_kernel, final=final, tiles_per_seq=tps),
        grid=(n // tm,),
        in_specs=[row(D_MODEL), row(ATTN_WIDTH), first_tile(POOL_WIDTH), first_tile(CONV_WIDTH),
                  next_tile(POOL_WIDTH), next_tile(CONV_WIDTH), next_seq(POOL_WIDTH), next_seq(CONV_WIDTH)]
        + _mixer_weight_specs(layer) + _dense_weight_specs(layer),
        out_specs=row(D_MODEL),
        out_shape=jax.ShapeDtypeStruct((n, D_MODEL), F32),
        scratch_shapes=_mixer_scratch(tm) + [pltpu.VMEM((2, tm, POOL_WIDTH), BF16),
                                             pltpu.VMEM((2, tm, CONV_WIDTH), BF16)],
        compiler_params=pltpu.CompilerParams(dimension_semantics=("arbitrary",),
                                             vmem_limit_bytes=VMEM_LIMIT),
        name="prompt_dense",
    )(x, o, upool, uconv, upool, uconv, hist_pool, hist_conv, *mixer_w, *dense_w)


def _sample_dense_kernel(x_ref, yp_ref, o_ref, yc_ref, wout_ref, g_ref, wgu_ref, wd_ref, fg_ref, out_ref, *, final):
    out_ref[...] = _dense_math(x_ref[...], yp_ref[...], o_ref[...], yc_ref[...],
                               wout_ref, g_ref, wgu_ref, wd_ref, fg_ref, final)


def _sample_dense(x, ypool, o, yconv, wout_bf, g, wgu_bf, wd_bf, fg, tm, final, layer):
    n = x.shape[0]
    row = lambda w: pl.BlockSpec((tm, w), lambda i: (i, 0))
    return pl.pallas_call(
        functools.partial(_sample_dense_kernel, final=final),
        grid=(n // tm,),
        in_specs=[row(D_MODEL), row(POOL_WIDTH), row(ATTN_WIDTH), row(CONV_WIDTH)] + _dense_weight_specs(layer),
        out_specs=row(D_MODEL),
        out_shape=jax.ShapeDtypeStruct((n, D_MODEL), F32),
        compiler_params=pltpu.CompilerParams(dimension_semantics=("arbitrary",),
                                             vmem_limit_bytes=VMEM_LIMIT),
        name="sample_dense",
    )(x, ypool, o, yconv, wout_bf, g, wgu_bf, wd_bf, fg)


def _rope_tables(pos):
    half = HEAD_DIM // 2
    inv = ROPE_THETA ** (-jnp.arange(half, dtype=F32) / half)
    ang = pos.astype(F32)[:, None] * inv[None, :]
    reps = LANES // half
    cos = jnp.tile(jnp.cos(ang), (1, reps))
    sign = jnp.tile(jnp.concatenate([-jnp.ones((half,), F32), jnp.ones((half,), F32)]), LANES // HEAD_DIM)
    sin = jnp.tile(jnp.sin(ang), (1, reps)) * sign[None, :]
    return cos, sin


def _block_diag(pool_w):
    depth, g, d, _ = pool_w.shape
    eye = jnp.eye(g, dtype=pool_w.dtype)
    return (pool_w[:, :, :, None, :] * eye[None, :, None, :, None]).reshape(depth, g * d, g * d)


def _pad_hist(h):
    return jnp.pad(h, ((0, 0), (0, 0), (HALO - h.shape[2], 0), (0, 0)))


def kernel(x_prompt, x_sample, cache_k, cache_v, state_pool, state_conv, norm_mix_g, w_in, pool_w, pool_scale,
           lambda_qk, diff_norm_g, conv_dw, conv_dw_b, conv_ln_g, conv_ln_b, conv_pw, w_out, norm_ffn_g,
           w_gate_up, w_down, final_norm_g):
    B, S, _ = x_prompt.shape
    Bs, Ls, _ = x_sample.shape
    depth = w_in.shape[0]
    past_len = cache_k.shape[2]
    assert S % ATTN_TILE == 0 and S % ROW_TILE == 0 and (Bs * Ls) % 16 == 0 and Ls % 16 == 0 and Ls >= CONV_HIST

    cos_p, sin_p = _rope_tables(jnp.arange(S, dtype=jnp.int32))
    cos_s, sin_s = _rope_tables(past_len + jnp.arange(Ls, dtype=jnp.int32))
    cos_s, sin_s = jnp.tile(cos_s, (Bs, 1)), jnp.tile(sin_s, (Bs, 1))
    zero_pool = jnp.zeros((B, HALO, POOL_WIDTH), F32)
    zero_conv = jnp.zeros((B, HALO, CONV_WIDTH), F32)

    xp = x_prompt.reshape(B * S, D_MODEL)
    xs = x_sample.reshape(Bs * Ls, D_MODEL)
    rows = lambda a: a.reshape(depth, 1, -1)
    norm_mix_rows, norm_ffn_rows = rows(norm_mix_g), rows(norm_ffn_g)
    diff_g_row, diff_g_col = rows(diff_norm_g), diff_norm_g.reshape(depth, -1, 1)
    w_in_bf, w_out_bf, w_gu_bf, w_d_bf = (w.astype(BF16) for w in (w_in, w_out, w_gate_up, w_down))
    mixer_w = (_block_diag(pool_w).astype(BF16), rows(pool_scale), conv_dw, rows(conv_dw_b), rows(conv_ln_g),
               rows(conv_ln_b), conv_pw.astype(BF16))
    dense_w = (w_out_bf, norm_ffn_rows, w_gu_bf, w_d_bf, final_norm_g.reshape(1, -1))
    hist_pool_s, hist_conv_s = _pad_hist(state_pool), _pad_hist(state_conv)
    cache_kt = jnp.transpose(cache_k, (0, 1, 3, 4, 5, 2)).reshape(depth, Bs, ATTN_WIDTH, past_len)
    cache_v4 = cache_v.reshape(depth, Bs, past_len * N_HEADS, V_DIM)
    kv_stack = depth
    pp, cp = [], []
    ks_, vs_, ps_, cs_ = [], [], [], []
    for l in range(depth):
        lam_init = 0.8 - 0.6 * math.exp(-0.3 * l)
        final = l == depth - 1

        upool, q, k, kb, v, vb, uconv = _prompt_in_proj(xp, norm_mix_rows, w_in_bf, cos_p, sin_p, ROW_TILE, S,
                                                        kv_stack, l)
        kv_stack = (k, v)
        o = _prompt_attention(lambda_qk, diff_g_col, q.reshape(B, S, -1), kb.reshape(B, S, -1),
                              vb, lam_init, l)
        xp = _prompt_dense(xp, o.reshape(B * S, -1), upool, uconv, zero_pool, zero_conv, mixer_w, dense_w,
                           ROW_TILE, S, final, l)
        pp.append(upool.reshape(B, S, -1)[:, S - POOL_HIST:])
        cp.append(uconv.reshape(B, S, -1)[:, S - CONV_HIST:])

        upool, q, k, kb, v, vb, uconv = _sample_in_proj(xs, norm_mix_rows, w_in_bf, cos_s, sin_s, l)
        o = _sample_attention(lambda_qk, diff_g_row, q.reshape(Bs, Ls, -1), cache_kt, cache_v4,
                              kb.reshape(Bs, Ls, -1), vb.reshape(Bs, Ls, -1), lam_init, l)
        ypool, yconv = _sample_mixers(upool, uconv, hist_pool_s, hist_conv_s, mixer_w, Ls, past_len, l)
        xs = _sample_dense(xs, ypool, o.reshape(Bs * Ls, -1), yconv, *dense_w, Bs * Ls, final, l)
        ks_.append(k.reshape(Bs, Ls, N_HEADS, 2, HEAD_DIM))
        vs_.append(v.reshape(Bs, Ls, N_HEADS, V_DIM))
        ps_.append(upool.reshape(Bs, Ls, -1)[:, Ls - POOL_HIST:])
        cs_.append(uconv.reshape(Bs, Ls, -1)[:, Ls - CONV_HIST:])

    kt_all, v4_all = kv_stack
    new_k = jnp.transpose(kt_all.reshape(depth, B, N_HEADS, 2, HEAD_DIM, S), (0, 1, 5, 2, 3, 4))
    new_v = v4_all.reshape(depth, B, S, N_HEADS, V_DIM)
    return (xp.reshape(B, S, D_MODEL), xs.reshape(Bs, Ls, D_MODEL),
            new_k, new_v, jnp.stack(pp), jnp.stack(cp),
            jnp.stack(ks_), jnp.stack(vs_), jnp.stack(ps_), jnp.stack(cs_))
```

```python
import functools
import math

import jax
import jax.numpy as jnp
from jax import lax
from jax.experimental import pallas as pl
from jax.experimental.pallas import tpu as pltpu

D_MODEL = 1024
CHUNK = 64
POOL_WIDTH = D_MODEL // 4
POOL_WINDOWS = (2, 4, 8, 16)
POOL_GROUP_DIM = POOL_WIDTH // len(POOL_WINDOWS)
POOL_HIST = max(POOL_WINDOWS) - 1
ATTN_WIDTH = D_MODEL // 2
N_HEADS = 4
HEAD_DIM = ATTN_WIDTH // (2 * N_HEADS)
V_DIM = 2 * HEAD_DIM
ROPE_THETA = 10000.0
CONV_WIDTH = D_MODEL // 4
CONV_K = 31
CONV_HIST = CONV_K - 1
MIX_WIDTH = POOL_WIDTH + ATTN_WIDTH + CONV_WIDTH
IN_WIDTH = POOL_WIDTH + 3 * ATTN_WIDTH + 2 * CONV_WIDTH
D_FF = ((-(-8 * D_MODEL // 3) + 255) // 256) * 256
EPS = 1e-6

LANES = 128
HALO = 32
ROW_TILE = 512
IN_PROJ_TILE = 1024
ATTN_TILE = 256
SAMPLE_KEY_TILE = 1024
HEADS_PER_STEP = 4
VMEM_LIMIT = 56 * 1024 * 1024

F32 = jnp.float32
BF16 = jnp.bfloat16


def _rms(x, g):
    return x * lax.rsqrt(jnp.mean(x * x, axis=-1, keepdims=True) + EPS) * g


def _const_spec(shape):
    return pl.BlockSpec(shape, lambda *_: (0,) * len(shape), pipeline_mode=pl.Buffered(1))


def _layer_spec(shape, layer):
    return pl.BlockSpec((None,) + shape, lambda *_: (layer,) + (0,) * len(shape), pipeline_mode=pl.Buffered(1))


O_Q = POOL_WIDTH
O_K = O_Q + ATTN_WIDTH
O_V = O_K + ATTN_WIDTH
O_A = O_V + ATTN_WIDTH
O_B = O_A + CONV_WIDTH


def _qkv_epilogue(z, cos_ref, sin_ref, q_ref, k_ref, kb_ref, v_ref, vb_ref, cache_layout):
    tm = z.shape[0]
    o1, o2, o3 = O_Q, O_K, O_V
    cos = cos_ref[...]
    sin = sin_ref[...]
    lane = lax.broadcasted_iota(jnp.int32, cos.shape, 1)
    first_half = (lane & (HEAD_DIM // 2)) == 0

    def rope(t):
        up = pltpu.roll(t, LANES - HEAD_DIM // 2, axis=1)
        down = pltpu.roll(t, HEAD_DIM // 2, axis=1)
        return t * cos + jnp.where(first_half, up, down) * sin

    scale = HEAD_DIM ** -0.5 * math.log2(math.e)
    for j in range(ATTN_WIDTH // LANES):
        sl = slice(j * LANES, (j + 1) * LANES)
        qj = rope(z[:, o1 + j * LANES:o1 + (j + 1) * LANES])
        kj = rope(z[:, o2 + j * LANES:o2 + (j + 1) * LANES])
        vj = z[:, o3 + j * LANES:o3 + (j + 1) * LANES]
        q_ref[:, sl] = (qj * scale).astype(BF16)
        kb_ref[:, sl] = kj.astype(BF16)
        if cache_layout:
            k_ref[sl, :] = kj.T
            v_ref[pl.ds(j, tm, stride=N_HEADS), :] = vj
            vb_ref[sl, :] = vj.T.astype(BF16)
        else:
            k_ref[:, sl] = kj
            v_ref[:, sl] = vj
            vb_ref[:, sl] = vj.astype(BF16)


def _sample_in_proj_kernel(x_ref, g_ref, w_ref, cos_ref, sin_ref,
                           upool_ref, q_ref, k_ref, kb_ref, v_ref, vb_ref, uconv_ref):
    h = _rms(x_ref[...], g_ref[...]).astype(BF16)
    z = jnp.dot(h, w_ref[...], preferred_element_type=F32)
    upool_ref[...] = z[:, :O_Q]
    uconv_ref[...] = z[:, O_A:O_B] * jax.nn.sigmoid(z[:, O_B:])
    _qkv_epilogue(z, cos_ref, sin_ref, q_ref, k_ref, kb_ref, v_ref, vb_ref, False)


def _sample_in_proj(x, g, w_bf, cos_tab, sin_tab, layer):
    n = x.shape[0]
    full = lambda w: pl.BlockSpec((n, w), lambda i: (0, 0))
    flat = lambda dt: jax.ShapeDtypeStruct((n, ATTN_WIDTH), dt)
    return pl.pallas_call(
        _sample_in_proj_kernel,
        grid=(1,),
        in_specs=[full(D_MODEL), _layer_spec((1, D_MODEL), layer), _layer_spec((D_MODEL, IN_WIDTH), layer),
                  full(LANES), full(LANES)],
        out_specs=[full(POOL_WIDTH)] + [full(ATTN_WIDTH)] * 5 + [full(CONV_WIDTH)],
        out_shape=[jax.ShapeDtypeStruct((n, POOL_WIDTH), F32), flat(BF16), flat(F32), flat(BF16), flat(F32),
                   flat(BF16), jax.ShapeDtypeStruct((n, CONV_WIDTH), F32)],
        compiler_params=pltpu.CompilerParams(dimension_semantics=("arbitrary",),
                                             vmem_limit_bytes=VMEM_LIMIT),
        name="sample_in_proj",
    )(x, g, w_bf, cos_tab, sin_tab)


def _prompt_in_proj_kernel(*refs, n_unused):
    x_ref, g_ref, w_ref, cos_ref, sin_ref = refs[:5]
    upool_ref, q_ref, k_ref, kb_ref, v_ref, vb_ref, uconv_ref = refs[5 + n_unused:]
    h = _rms(x_ref[...], g_ref[...]).astype(BF16)
    z = jnp.dot(h, w_ref[...], preferred_element_type=F32)
    upool_ref[...] = z[:, :O_Q]
    uconv_ref[...] = z[:, O_A:O_B] * jax.nn.sigmoid(z[:, O_B:])
    _qkv_epilogue(z, cos_ref, sin_ref, q_ref, k_ref, kb_ref, v_ref, vb_ref, True)


def _prompt_in_proj(x, g, w_bf, cos_tab, sin_tab, tm, seq_len, kv_stack, layer):
    n = x.shape[0]
    tps = seq_len // tm
    batch = n // seq_len
    row = lambda w: pl.BlockSpec((tm, w), lambda i: (i, 0))
    tab = pl.BlockSpec((tm, LANES), lambda i: (i % tps, 0))
    flat = lambda dt, w=ATTN_WIDTH: jax.ShapeDtypeStruct((n, w), dt)
    extra_in, extra_specs, aliases = [], [], {}
    if isinstance(kv_stack, int):
        k_shape = jax.ShapeDtypeStruct((kv_stack, batch, ATTN_WIDTH, seq_len), F32)
        v_shape = jax.ShapeDtypeStruct((kv_stack, batch, N_HEADS * seq_len, V_DIM), F32)
    else:
        k_shape, v_shape = (jax.ShapeDtypeStruct(a.shape, a.dtype) for a in kv_stack)
        extra_in = list(kv_stack)
        extra_specs = [pl.BlockSpec(memory_space=pl.ANY)] * 2
        aliases = {5: 2, 6: 4}
    k_spec = pl.BlockSpec((None, None, ATTN_WIDTH, tm), lambda i: (layer, i // tps, 0, i % tps))
    v_spec = pl.BlockSpec((None, None, N_HEADS * tm, V_DIM), lambda i: (layer, i // tps, i % tps, 0))
    vt_spec = pl.BlockSpec((None, ATTN_WIDTH, tm), lambda i: (i // tps, 0, i % tps))
    return pl.pallas_call(
        functools.partial(_prompt_in_proj_kernel, n_unused=len(extra_in)),
        grid=(n // tm,),
        in_specs=[row(D_MODEL), _layer_spec((1, D_MODEL), layer), _layer_spec((D_MODEL, IN_WIDTH), layer), tab, tab]
        + extra_specs,
        out_specs=[row(POOL_WIDTH), row(ATTN_WIDTH), k_spec, row(ATTN_WIDTH), v_spec, vt_spec, row(CONV_WIDTH)],
        out_shape=[flat(F32, POOL_WIDTH), flat(BF16), k_shape, flat(BF16), v_shape,
                   jax.ShapeDtypeStruct((batch, ATTN_WIDTH, seq_len), BF16), flat(F32, CONV_WIDTH)],
        input_output_aliases=aliases,
        compiler_params=pltpu.CompilerParams(dimension_semantics=("arbitrary",),
                                             vmem_limit_bytes=VMEM_LIMIT),
        name="prompt_in_proj",
    )(x, g, w_bf, cos_tab, sin_tab, *extra_in)


def _lambda_scalar(lq, lam_init):
    a = jnp.sum(lq[0:1, :] * lq[1:2, :], axis=-1, keepdims=True)
    b = jnp.sum(lq[2:3, :] * lq[3:4, :], axis=-1, keepdims=True)
    return jnp.exp(a) - jnp.exp(b) + lam_init


def _stack_maps(q):
    lane = lax.broadcasted_iota(jnp.int32, q.shape, 1)
    zero = jnp.zeros_like(q)
    return jnp.concatenate([jnp.where(lane < HEAD_DIM, q, zero), jnp.where(lane >= HEAD_DIM, q, zero)], axis=0)


ONES_ROWS = 16


def _flash_scores(qq, kt):
    return lax.dot_general(kt, qq, (((1,), (1,)), ((), ())), preferred_element_type=F32)


def _flash_update(carry, s, vt_ones, p_stage):
    m, acc = carry
    m_new = jnp.maximum(m, jnp.max(s, axis=0, keepdims=True))
    alpha = jnp.exp2(m - m_new)
    p_stage[...] = jnp.exp2(s - m_new).astype(BF16)
    return m_new, alpha * acc + jnp.dot(vt_ones, p_stage[...], preferred_element_type=F32)


def _flash_init(cols):
    return jnp.full((1, cols), -jnp.inf, F32), jnp.zeros((V_DIM + ONES_ROWS, cols), F32)


def _attn_finish(carry, lam, g_col, lam_init, tq):
    _, acc = carry
    scaled = acc[:V_DIM] * (1.0 / acc[V_DIM:V_DIM + 1])
    o = scaled[:, :tq] - lam * scaled[:, tq:]
    o = o * lax.rsqrt(jnp.mean(o * o, axis=0, keepdims=True) + EPS) * g_col
    return (o * (1.0 - lam_init)).T


def _mask_diagonal_tile(s):
    t = s.shape[0]
    q_chunk = (lax.broadcasted_iota(jnp.int32, (1, s.shape[1]), 1) % t) // CHUNK
    blocks = [jnp.where(q_chunk >= a, s[a * CHUNK:(a + 1) * CHUNK, :], -1e30) for a in range(t // CHUNK)]
    return jnp.concatenate(blocks, axis=0)


def _prompt_attn_kernel(lq_ref, g_ref, q_ref, k_ref, vt_ref, o_ref, qq_ref, p_ref, *, lam_init):
    t = ATTN_TILE
    qi = pl.program_id(2)
    heads = [slice(h * LANES, (h + 1) * LANES) for h in range(HEADS_PER_STEP)]
    for h, c in enumerate(heads):
        qq_ref[h] = _stack_maps(q_ref[:, c])

    ones = jnp.ones((ONES_ROWS, t), BF16)
    stage0 = jnp.minimum(qi, 0)

    def steps(tiles, carry):
        rows = [pl.ds(pl.multiple_of(j * t, t), t) for j, _ in tiles]
        s = [[_flash_scores(qq_ref[h], k_ref[r, c]) for h, c in enumerate(heads)] for r in rows]
        for ti, (_, diagonal) in enumerate(tiles):
            if diagonal:
                s[ti] = [_mask_diagonal_tile(x) for x in s[ti]]
            carry = tuple(
                _flash_update(carry[h], s[ti][h], jnp.concatenate([vt_ref[c, rows[ti]], ones], axis=0),
                              p_ref.at[stage0 + ti * HEADS_PER_STEP + h])
                for h, c in enumerate(heads))
        return carry

    carry = lax.fori_loop(0, qi // 2, lambda p, c: steps([(2 * p, False), (2 * p + 1, False)], c),
                          (_flash_init(2 * t),) * HEADS_PER_STEP)
    carry = lax.cond(qi % 2 == 1,
                     lambda c: steps([(qi - 1, False), (qi, True)], c),
                     lambda c: steps([(qi, True)], c), carry)
    lam = _lambda_scalar(lq_ref[...], lam_init)
    for h, c in enumerate(heads):
        o_ref[:, c] = _attn_finish(carry[h], lam, g_ref[...], lam_init, t).astype(BF16)


def _prompt_attention(lq, g, q, kb, vtb, lam_init, layer):
    b, s, _ = q.shape
    t = ATTN_TILE
    w = HEADS_PER_STEP * LANES
    return pl.pallas_call(
        functools.partial(_prompt_attn_kernel, lam_init=lam_init),
        grid=(b, N_HEADS // HEADS_PER_STEP, s // t),
        in_specs=[
            _layer_spec((4, HEAD_DIM), layer),
            _layer_spec((V_DIM, 1), layer),
            pl.BlockSpec((None, t, w), lambda b_, h, i: (b_, i, h)),
            pl.BlockSpec((None, s, w), lambda b_, h, i: (b_, 0, h)),
            pl.BlockSpec((None, w, s), lambda b_, h, i: (b_, h, 0)),
        ],
        out_specs=pl.BlockSpec((None, t, w), lambda b_, h, i: (b_, i, h)),
        out_shape=jax.ShapeDtypeStruct((b, s, ATTN_WIDTH), BF16),
        scratch_shapes=[pltpu.VMEM((HEADS_PER_STEP, 2 * t, LANES), BF16),
                        pltpu.VMEM((2 * HEADS_PER_STEP, t, 2 * t), BF16)],
        compiler_params=pltpu.CompilerParams(dimension_semantics=("arbitrary",) * 3,
                                             vmem_limit_bytes=VMEM_LIMIT),
        name="prompt_attention",
    )(lq, g, q, kb, vtb)


def _row_scores(qq, kt, *, keys_on_rows):
    dims = (((1,), (1,)), ((), ())) if keys_on_rows else (((1,), (0,)), ((), ()))
    return lax.dot_general(qq, kt, dims, preferred_element_type=F32)


def _row_flash_update(carry, s, vt):
    m, l, acc = carry
    m_new = jnp.maximum(m, jnp.max(s, axis=-1, keepdims=True))
    alpha = jnp.exp2(m - m_new)
    p = jnp.exp2(s - m_new)
    l = alpha * l + jnp.sum(p, axis=-1, keepdims=True)
    acc = alpha * acc + jnp.dot(p.astype(BF16), vt, preferred_element_type=F32)
    return m_new, l, acc


def _sample_attn_kernel(lq_ref, g_ref, q_ref, ck_ref, cv_ref, k_ref, v_ref, o_ref, *, lam_init, past_len):
    t = SAMPLE_KEY_TILE
    tq = q_ref.shape[0]
    heads = [slice(h * LANES, (h + 1) * LANES) for h in range(N_HEADS)]
    qq = [_stack_maps(q_ref[:, c]) for c in heads]

    def body(j, carry):
        cols = pl.ds(pl.multiple_of(j * t, t), t)
        s = [_row_scores(qq[h], ck_ref[c, cols].astype(BF16), keys_on_rows=False) for h, c in enumerate(heads)]
        return tuple(
            _row_flash_update(carry[h], s[h], cv_ref[pl.ds(j * (t * N_HEADS) + h, t, stride=N_HEADS), :].astype(BF16))
            for h in range(N_HEADS))

    init = (jnp.full((2 * tq, 1), -jnp.inf, F32), jnp.zeros((2 * tq, 1), F32), jnp.zeros((2 * tq, V_DIM), F32))
    carry = lax.fori_loop(0, past_len // t, body, (init,) * N_HEADS)
    lam = _lambda_scalar(lq_ref[...], lam_init)
    for h, c in enumerate(heads):
        s = _row_scores(qq[h], k_ref[:, c], keys_on_rows=True)
        _, l, acc = _row_flash_update(carry[h], s, v_ref[:, c])
        o = acc[:tq] / l[:tq] - lam * (acc[tq:] / l[tq:])
        o_ref[:, c] = (_rms(o, g_ref[...]) * (1.0 - lam_init)).astype(BF16)


def _sample_attention(lq, g, q, cache_kt, cache_v4, kb, vb, lam_init, layer):
    b, ls, _ = q.shape
    past_len = cache_kt.shape[-1]
    assert past_len % SAMPLE_KEY_TILE == 0
    return pl.pallas_call(
        functools.partial(_sample_attn_kernel, lam_init=lam_init, past_len=past_len),
        grid=(b,),
        in_specs=[
            _layer_spec((4, HEAD_DIM), layer),
            _layer_spec((1, V_DIM), layer),
            pl.BlockSpec((None, ls, ATTN_WIDTH), lambda b_: (b_, 0, 0)),
            pl.BlockSpec((None, None, ATTN_WIDTH, past_len), lambda b_: (layer, b_, 0, 0)),
            pl.BlockSpec((None, None, N_HEADS * past_len, V_DIM), lambda b_: (layer, b_, 0, 0)),
            pl.BlockSpec((None, ls, ATTN_WIDTH), lambda b_: (b_, 0, 0)),
            pl.BlockSpec((None, ls, ATTN_WIDTH), lambda b_: (b_, 0, 0)),
        ],
        out_specs=pl.BlockSpec((None, ls, ATTN_WIDTH), lambda b_: (b_, 0, 0)),
        out_shape=jax.ShapeDtypeStruct((b, ls, ATTN_WIDTH), BF16),
        compiler_params=pltpu.CompilerParams(dimension_semantics=("arbitrary",),
                                             vmem_limit_bytes=VMEM_LIMIT),
        name="sample_attention",
    )(lq, g, q, cache_kt, cache_v4, kb, vb)


def _pool_mixer(ext_p, pool_w_ref, pool_scale_ref, tm, pos0):
    assert POOL_WINDOWS == (2, 4, 8, 16) and HALO == 32
    e = ext_p[...]
    n = HALO + tm
    s2 = e[8:] + e[7:n - 1]
    s4 = s2[8:] + s2[6:n - 10]
    s8 = s4[8:] + s4[4:n - 20]
    sums = {2: s2[24:], 4: s4[16:], 8: s8[8:], 16: s8[8:] + s8[:n - 32]}
    u = e[HALO:]
    lane = lax.broadcasted_iota(jnp.int32, (tm, POOL_WIDTH), 1)
    group = lane // POOL_GROUP_DIM
    win = sums[POOL_WINDOWS[-1]]
    width = jnp.full((tm, POOL_WIDTH), POOL_WINDOWS[-1], jnp.int32)
    for gi in range(len(POOL_WINDOWS) - 2, -1, -1):
        win = jnp.where(group == gi, sums[POOL_WINDOWS[gi]], win)
        width = jnp.where(group == gi, POOL_WINDOWS[gi], width)
    pos = pos0 + lax.broadcasted_iota(jnp.int32, (tm, POOL_WIDTH), 0)
    count = jnp.minimum(pos + 1, width).astype(F32)
    d = (win / count - u).astype(BF16)
    return jnp.dot(d, pool_w_ref[...], preferred_element_type=F32) * pool_scale_ref[...]


CONV_ROW_BLOCK = 64


def _conv_mixer_steps(ext_c, shift_c, conv_out, dw_ref, dw_b_ref, ln_g_ref, ln_b_ref, pw_ref, tm, store):
    first_row = HALO - CONV_HIST
    blk = min(tm, CONV_ROW_BLOCK)

    def shift(residues):
        for r in residues:
            shift_c[r, 0:HALO - 8 + tm, :] = ext_c[pl.ds(r, HALO - 8 + tm), :]

    def taps(rb):
        yb = jnp.zeros((blk, CONV_WIDTH), F32)
        for k in range(CONV_K):
            r, a = (first_row + k) % 8, (first_row + k) // 8
            rows = pl.ds(8 * a + rb * blk, blk)
            src = ext_c[rows, :] if r == 0 else shift_c[r, rows, :]
            yb = yb + src * dw_ref[k:k + 1, :]
        conv_out[rb * blk:(rb + 1) * blk, :] = yb

    def finish():
        y = conv_out[...] + dw_b_ref[...]
        mu = jnp.mean(y, axis=-1, keepdims=True)
        yc = y - mu
        yn = yc * lax.rsqrt(jnp.mean(yc * yc, axis=-1, keepdims=True) + EPS) * ln_g_ref[...] + ln_b_ref[...]
        act = (yn * jax.nn.sigmoid(yn)).astype(BF16)
        store(jnp.dot(act, pw_ref[...], preferred_element_type=F32))

    steps = [functools.partial(shift, rs) for rs in ((1, 2), (3, 4, 5), (6, 7))]
    steps += [functools.partial(taps, rb) for rb in range(tm // blk)]
    return steps + [finish]


def _conv_mixer(ext_c, shift_c, conv_out, dw_ref, dw_b_ref, ln_g_ref, ln_b_ref, pw_ref, tm):
    out = []
    for step in _conv_mixer_steps(ext_c, shift_c, conv_out, dw_ref, dw_b_ref, ln_g_ref, ln_b_ref, pw_ref, tm,
                                  out.append):
        step()
    return out[0]


def _mixer_scratch(tm):
    return [pltpu.VMEM((HALO + tm, POOL_WIDTH), F32), pltpu.VMEM((HALO + tm, CONV_WIDTH), F32),
            pltpu.VMEM((8, HALO + tm, CONV_WIDTH), F32), pltpu.VMEM((tm, CONV_WIDTH), F32)]


def _mixer_weight_specs(layer):
    shapes = [(POOL_WIDTH, POOL_WIDTH), (1, POOL_WIDTH), (CONV_K, CONV_WIDTH), (1, CONV_WIDTH), (1, CONV_WIDTH),
              (1, CONV_WIDTH), (CONV_WIDTH, CONV_WIDTH)]
    return [_layer_spec(shape, layer) for shape in shapes]


def _sample_mixers_kernel(up_ref, up_hist_ref, uc_ref, uc_hist_ref,
                          pool_w_ref, pool_scale_ref, dw_ref, dw_b_ref, ln_g_ref, ln_b_ref, pw_ref,
                          ypool_ref, yconv_ref, ext_p, ext_c, shift_c, conv_out, *, pos_base):
    tm = up_ref.shape[0]
    ext_p[0:HALO, :] = up_hist_ref[...]
    ext_p[HALO:, :] = up_ref[...]
    ext_c[0:HALO, :] = uc_hist_ref[...]
    ext_c[HALO:, :] = uc_ref[...]
    ypool_ref[...] = _pool_mixer(ext_p, pool_w_ref, pool_scale_ref, tm, pos_base).astype(BF16)
    yconv_ref[...] = _conv_mixer(ext_c, shift_c, conv_out, dw_ref, dw_b_ref, ln_g_ref, ln_b_ref, pw_ref,
                                 tm).astype(BF16)


def _sample_mixers(upool, uconv, hist_pool, hist_conv, mixer_w, seq_len, pos_base, layer):
    n = upool.shape[0]
    cur = pl.BlockSpec((seq_len, POOL_WIDTH), lambda i: (i, 0))
    hist = pl.BlockSpec((None, None, HALO, POOL_WIDTH), lambda i: (layer, i, 0, 0))
    return pl.pallas_call(
        functools.partial(_sample_mixers_kernel, pos_base=pos_base),
        grid=(n // seq_len,),
        in_specs=[cur, hist, cur, hist] + _mixer_weight_specs(layer),
        out_specs=[cur, cur],
        out_shape=[jax.ShapeDtypeStruct((n, POOL_WIDTH), BF16), jax.ShapeDtypeStruct((n, CONV_WIDTH), BF16)],
        scratch_shapes=_mixer_scratch(seq_len),
        compiler_params=pltpu.CompilerParams(dimension_semantics=("arbitrary",),
                                             vmem_limit_bytes=VMEM_LIMIT),
        name="sample_mixers",
    )(upool, hist_pool, uconv, hist_conv, *mixer_w)


def _dense_math(x, ypool, o, yconv, wout_ref, g_ref, wgu_ref, wd_ref, fg_ref, final):
    o1 = POOL_WIDTH
    o2 = o1 + ATTN_WIDTH
    mix = (jnp.dot(ypool, wout_ref[0:o1, :], preferred_element_type=F32)
           + jnp.dot(o, wout_ref[o1:o2, :], preferred_element_type=F32)
           + jnp.dot(yconv, wout_ref[o2:, :], preferred_element_type=F32))
    x1 = x + mix
    h = _rms(x1, g_ref[...]).astype(BF16)
    gu = jnp.dot(h, wgu_ref[...], preferred_element_type=F32)
    gate = gu[:, :D_FF]
    a = (gate * jax.nn.sigmoid(gate) * gu[:, D_FF:]).astype(BF16)
    x2 = x1 + jnp.dot(a, wd_ref[...], preferred_element_type=F32)
    return _rms(x2, fg_ref[...]) if final else x2


def _dense_weight_specs(layer):
    shapes = [(MIX_WIDTH, D_MODEL), (1, D_MODEL), (D_MODEL, 2 * D_FF), (D_FF, D_MODEL)]
    return [_layer_spec(shape, layer) for shape in shapes] + [_const_spec((1, D_MODEL))]


def _prompt_dense_kernel(x_ref, o_ref, up0_ref, uc0_ref, up_ref, uc_ref, hist_p_ref, hist_c_ref,
                         pool_w_ref, pool_scale_ref, dw_ref, dw_b_ref, ln_g_ref, ln_b_ref, pw_ref,
                         wout_ref, g_ref, wgu_ref, wd_ref, fg_ref, out_ref,
                         ext_p, ext_c, shift_c, conv_out, y_pool, y_conv, *, final, tiles_per_seq):
    tm = x_ref.shape[0]
    i = pl.program_id(0)

    def mixer_steps(up_ref_, uc_ref_, tile, slot, first):
        def fill():
            if first is True:
                ext_p[0:HALO, :] = hist_p_ref[...]
                ext_c[0:HALO, :] = hist_c_ref[...]
            else:
                ext_p[0:HALO, :] = jnp.where(first, hist_p_ref[...], ext_p[tm:, :])
                ext_c[0:HALO, :] = jnp.where(first, hist_c_ref[...], ext_c[tm:, :])
            ext_p[HALO:, :] = up_ref_[...]
            ext_c[HALO:, :] = uc_ref_[...]

        def pool():
            pos0 = (tile % tiles_per_seq) * tm
            y_pool[slot] = _pool_mixer(ext_p, pool_w_ref, pool_scale_ref, tm, pos0).astype(BF16)

        def store_conv(y):
            y_conv[slot] = y.astype(BF16)

        return [fill] + _conv_mixer_steps(ext_c, shift_c, conv_out, dw_ref, dw_b_ref, ln_g_ref, ln_b_ref, pw_ref,
                                          tm, store_conv) + [pool]

    @pl.when(i == 0)
    def _():
        for step in mixer_steps(up0_ref, uc0_ref, 0, 0, True):
            step()

    slot = i % 2
    nxt = jnp.minimum(i + 1, pl.num_programs(0) - 1)
    out_ref[...] = _dense_math(x_ref[...], y_pool[slot], o_ref[...], y_conv[slot],
                               wout_ref, g_ref, wgu_ref, wd_ref, fg_ref, final)
    for step in mixer_steps(up_ref, uc_ref, nxt, 1 - slot, nxt % tiles_per_seq == 0):
        step()


def _prompt_dense(x, o, upool, uconv, hist_pool, hist_conv, mixer_w, dense_w, tm, seq_len, final, layer):
    n = x.shape[0]
    tps = seq_len // tm
    last = n // tm - 1
    row = lambda w: pl.BlockSpec((tm, w), lambda i: (i, 0))
    first_tile = lambda w: pl.BlockSpec((tm, w), lambda i: (0, 0))
    next_tile = lambda w: pl.BlockSpec((tm, w), lambda i: (jnp.minimum(i + 1, last), 0))
    next_seq = lambda w: pl.BlockSpec((None, HALO, w), lambda i: (jnp.minimum(i + 1, last) // tps, 0, 0))
    return pl.pallas_call(
        functools.partial(_prompt_dense_kernel, final=final, tiles_per_seq=tps),
        grid=(n // tm,),
        in_specs=[row(D_MODEL), row(ATTN_WIDTH), first_tile(POOL_WIDTH), first_tile(CONV_WIDTH),
                  next_tile(POOL_WIDTH), next_tile(CONV_WIDTH), next_seq(POOL_WIDTH), next_seq(CONV_WIDTH)]
        + _mixer_weight_specs(layer) + _dense_weight_specs(layer),
        out_specs=row(D_MODEL),
        out_shape=jax.ShapeDtypeStruct((n, D_MODEL), F32),
        scratch_shapes=_mixer_scratch(tm) + [pltpu.VMEM((2, tm, POOL_WIDTH), BF16),
                                             pltpu.VMEM((2, tm, CONV_WIDTH), BF16)],
        compiler_params=pltpu.CompilerParams(dimension_semantics=("arbitrary",),
                                             vmem_limit_bytes=VMEM_LIMIT),
        name="prompt_dense",
    )(x, o, upool, uconv, upool, uconv, hist_pool, hist_conv, *mixer_w, *dense_w)


def _sample_dense_kernel(x_ref, yp_ref, o_ref, yc_ref, wout_ref, g_ref, wgu_ref, wd_ref, fg_ref, out_ref, *, final):
    out_ref[...] = _dense_math(x_ref[...], yp_ref[...], o_ref[...], yc_ref[...],
                               wout_ref, g_ref, wgu_ref, wd_ref, fg_ref, final)


def _sample_dense(x, ypool, o, yconv, wout_bf, g, wgu_bf, wd_bf, fg, tm, final, layer):
    n = x.shape[0]
    row = lambda w: pl.BlockSpec((tm, w), lambda i: (i, 0))
    return pl.pallas_call(
        functools.partial(_sample_dense_kernel, final=final),
        grid=(n // tm,),
        in_specs=[row(D_MODEL), row(POOL_WIDTH), row(ATTN_WIDTH), row(CONV_WIDTH)] + _dense_weight_specs(layer),
        out_specs=row(D_MODEL),
        out_shape=jax.ShapeDtypeStruct((n, D_MODEL), F32),
        compiler_params=pltpu.CompilerParams(dimension_semantics=("arbitrary",),
                                             vmem_limit_bytes=VMEM_LIMIT),
        name="sample_dense",
    )(x, ypool, o, yconv, wout_bf, g, wgu_bf, wd_bf, fg)


def _rope_tables(pos):
    half = HEAD_DIM // 2
    inv = ROPE_THETA ** (-jnp.arange(half, dtype=F32) / half)
    ang = pos.astype(F32)[:, None] * inv[None, :]
    reps = LANES // half
    cos = jnp.tile(jnp.cos(ang), (1, reps))
    sign = jnp.tile(jnp.concatenate([-jnp.ones((half,), F32), jnp.ones((half,), F32)]), LANES // HEAD_DIM)
    sin = jnp.tile(jnp.sin(ang), (1, reps)) * sign[None, :]
    return cos, sin


def _block_diag(pool_w):
    depth, g, d, _ = pool_w.shape
    eye = jnp.eye(g, dtype=pool_w.dtype)
    return (pool_w[:, :, :, None, :] * eye[None, :, None, :, None]).reshape(depth, g * d, g * d)


def _pad_hist(h):
    return jnp.pad(h, ((0, 0), (0, 0), (HALO - h.shape[2], 0), (0, 0)))


def kernel(x_prompt, x_sample, cache_k, cache_v, state_pool, state_conv, norm_mix_g, w_in, pool_w, pool_scale,
           lambda_qk, diff_norm_g, conv_dw, conv_dw_b, conv_ln_g, conv_ln_b, conv_pw, w_out, norm_ffn_g,
           w_gate_up, w_down, final_norm_g):
    B, S, _ = x_prompt.shape
    Bs, Ls, _ = x_sample.shape
    depth = w_in.shape[0]
    past_len = cache_k.shape[2]
    assert S % ATTN_TILE == 0 and S % ROW_TILE == 0 and S % IN_PROJ_TILE == 0 and S // ROW_TILE >= 2
    assert (Bs * Ls) % 16 == 0 and Ls % 16 == 0 and Ls >= CONV_HIST

    cos_p, sin_p = _rope_tables(jnp.arange(S, dtype=jnp.int32))
    cos_s, sin_s = _rope_tables(past_len + jnp.arange(Ls, dtype=jnp.int32))
    cos_s, sin_s = jnp.tile(cos_s, (Bs, 1)), jnp.tile(sin_s, (Bs, 1))
    zero_pool = jnp.zeros((B, HALO, POOL_WIDTH), F32)
    zero_conv = jnp.zeros((B, HALO, CONV_WIDTH), F32)

    xp = x_prompt.reshape(B * S, D_MODEL)
    xs = x_sample.reshape(Bs * Ls, D_MODEL)
    rows = lambda a: a.reshape(depth, 1, -1)
    norm_mix_rows, norm_ffn_rows = rows(norm_mix_g), rows(norm_ffn_g)
    diff_g_row, diff_g_col = rows(diff_norm_g), diff_norm_g.reshape(depth, -1, 1)
    w_in_bf, w_out_bf, w_gu_bf, w_d_bf = (w.astype(BF16) for w in (w_in, w_out, w_gate_up, w_down))
    mixer_w = (_block_diag(pool_w).astype(BF16), rows(pool_scale), conv_dw, rows(conv_dw_b), rows(conv_ln_g),
               rows(conv_ln_b), conv_pw.astype(BF16))
    dense_w = (w_out_bf, norm_ffn_rows, w_gu_bf, w_d_bf, final_norm_g.reshape(1, -1))
    hist_pool_s, hist_conv_s = _pad_hist(state_pool), _pad_hist(state_conv)
    cache_kt = jnp.transpose(cache_k, (0, 1, 3, 4, 5, 2)).reshape(depth, Bs, ATTN_WIDTH, past_len)
    cache_v4 = cache_v.reshape(depth, Bs, past_len * N_HEADS, V_DIM)
    kv_stack = depth
    pp, cp = [], []
    ks_, vs_, ps_, cs_ = [], [], [], []
    for l in range(depth):
        lam_init = 0.8 - 0.6 * math.exp(-0.3 * l)
        final = l == depth - 1

        upool, q, k, kb, v, vb, uconv = _prompt_in_proj(xp, norm_mix_rows, w_in_bf, cos_p, sin_p, IN_PROJ_TILE, S,
                                                        kv_stack, l)
        kv_stack = (k, v)
        o = _prompt_attention(lambda_qk, diff_g_col, q.reshape(B, S, -1), kb.reshape(B, S, -1),
                              vb, lam_init, l)
        xp = _prompt_dense(xp, o.reshape(B * S, -1), upool, uconv, zero_pool, zero_conv, mixer_w, dense_w,
                           ROW_TILE, S, final, l)
        pp.append(upool.reshape(B, S, -1)[:, S - POOL_HIST:])
        cp.append(uconv.reshape(B, S, -1)[:, S - CONV_HIST:])

        upool, q, k, kb, v, vb, uconv = _sample_in_proj(xs, norm_mix_rows, w_in_bf, cos_s, sin_s, l)
        o = _sample_attention(lambda_qk, diff_g_row, q.reshape(Bs, Ls, -1), cache_kt, cache_v4,
                              kb.reshape(Bs, Ls, -1), vb.reshape(Bs, Ls, -1), lam_init, l)
        ypool, yconv = _sample_mixers(upool, uconv, hist_pool_s, hist_conv_s, mixer_w, Ls, past_len, l)
        xs = _sample_dense(xs, ypool, o.reshape(Bs * Ls, -1), yconv, *dense_w, Bs * Ls, final, l)
        ks_.append(k.reshape(Bs, Ls, N_HEADS, 2, HEAD_DIM))
        vs_.append(v.reshape(Bs, Ls, N_HEADS, V_DIM))
        ps_.append(upool.reshape(Bs, Ls, -1)[:, Ls - POOL_HIST:])
        cs_.append(uconv.reshape(Bs, Ls, -1)[:, Ls - CONV_HIST:])

    kt_all, v4_all = kv_stack
    new_k = jnp.transpose(kt_all.reshape(depth, B, N_HEADS, 2, HEAD_DIM, S), (0, 1, 5, 2, 3, 4))
    new_v = v4_all.reshape(depth, B, S, N_HEADS, V_DIM)
    return (xp.reshape(B, S, D_MODEL), xs.reshape(Bs, Ls, D_MODEL),
            new_k, new_v, jnp.stack(pp), jnp.stack(cp),
            jnp.stack(ks_), jnp.stack(vs_), jnp.stack(ps_), jnp.stack(cs_))
```

```python
import functools
import math

import jax
import jax.numpy as jnp
from jax import lax
from jax.experimental import pallas as pl
from jax.experimental.pallas import tpu as pltpu

D_MODEL = 1024
CHUNK = 64
POOL_WIDTH = D_MODEL // 4
POOL_WINDOWS = (2, 4, 8, 16)
POOL_GROUP_DIM = POOL_WIDTH // len(POOL_WINDOWS)
POOL_HIST = max(POOL_WINDOWS) - 1
ATTN_WIDTH = D_MODEL // 2
N_HEADS = 4
HEAD_DIM = ATTN_WIDTH // (2 * N_HEADS)
V_DIM = 2 * HEAD_DIM
ROPE_THETA = 10000.0
CONV_WIDTH = D_MODEL // 4
CONV_K = 31
CONV_HIST = CONV_K - 1
MIX_WIDTH = POOL_WIDTH + ATTN_WIDTH + CONV_WIDTH
IN_WIDTH = POOL_WIDTH + 3 * ATTN_WIDTH + 2 * CONV_WIDTH
D_FF = ((-(-8 * D_MODEL // 3) + 255) // 256) * 256
EPS = 1e-6

LANES = 128
HALO = 32
ROW_TILE = 512
IN_PROJ_TILE = 1024
ATTN_TILE = 256
SAMPLE_KEY_TILE = 1024
HEADS_PER_STEP = 2
VMEM_LIMIT = 56 * 1024 * 1024

F32 = jnp.float32
BF16 = jnp.bfloat16


def _rms(x, g):
    return x * lax.rsqrt(jnp.mean(x * x, axis=-1, keepdims=True) + EPS) * g


def _const_spec(shape):
    return pl.BlockSpec(shape, lambda *_: (0,) * len(shape), pipeline_mode=pl.Buffered(1))


def _layer_spec(shape, layer):
    return pl.BlockSpec((None,) + shape, lambda *_: (layer,) + (0,) * len(shape), pipeline_mode=pl.Buffered(1))


O_Q = POOL_WIDTH
O_K = O_Q + ATTN_WIDTH
O_V = O_K + ATTN_WIDTH
O_A = O_V + ATTN_WIDTH
O_B = O_A + CONV_WIDTH


def _qkv_epilogue(z, cos_ref, sin_ref, q_ref, k_ref, kb_ref, v_ref, vb_ref, cache_layout):
    tm = z.shape[0]
    o1, o2, o3 = O_Q, O_K, O_V
    cos = cos_ref[...]
    sin = sin_ref[...]
    lane = lax.broadcasted_iota(jnp.int32, cos.shape, 1)
    first_half = (lane & (HEAD_DIM // 2)) == 0

    def rope(t):
        up = pltpu.roll(t, LANES - HEAD_DIM // 2, axis=1)
        down = pltpu.roll(t, HEAD_DIM // 2, axis=1)
        return t * cos + jnp.where(first_half, up, down) * sin

    scale = HEAD_DIM ** -0.5 * math.log2(math.e)
    for j in range(ATTN_WIDTH // LANES):
        sl = slice(j * LANES, (j + 1) * LANES)
        qj = rope(z[:, o1 + j * LANES:o1 + (j + 1) * LANES])
        kj = rope(z[:, o2 + j * LANES:o2 + (j + 1) * LANES])
        vj = z[:, o3 + j * LANES:o3 + (j + 1) * LANES]
        q_ref[:, sl] = (qj * scale).astype(BF16)
        kb_ref[:, sl] = kj.astype(BF16)
        if cache_layout:
            k_ref[sl, :] = kj.T
            v_ref[pl.ds(j, tm, stride=N_HEADS), :] = vj
            vb_ref[sl, :] = vj.T.astype(BF16)
        else:
            k_ref[:, sl] = kj
            v_ref[:, sl] = vj
            vb_ref[:, sl] = vj.astype(BF16)


def _sample_in_proj_kernel(x_ref, g_ref, w_ref, cos_ref, sin_ref,
                           upool_ref, q_ref, k_ref, kb_ref, v_ref, vb_ref, uconv_ref):
    h = _rms(x_ref[...], g_ref[...]).astype(BF16)
    z = jnp.dot(h, w_ref[...], preferred_element_type=F32)
    upool_ref[...] = z[:, :O_Q]
    uconv_ref[...] = z[:, O_A:O_B] * jax.nn.sigmoid(z[:, O_B:])
    _qkv_epilogue(z, cos_ref, sin_ref, q_ref, k_ref, kb_ref, v_ref, vb_ref, False)


def _sample_in_proj(x, g, w_bf, cos_tab, sin_tab, layer):
    n = x.shape[0]
    full = lambda w: pl.BlockSpec((n, w), lambda i: (0, 0))
    flat = lambda dt: jax.ShapeDtypeStruct((n, ATTN_WIDTH), dt)
    return pl.pallas_call(
        _sample_in_proj_kernel,
        grid=(1,),
        in_specs=[full(D_MODEL), _layer_spec((1, D_MODEL), layer), _layer_spec((D_MODEL, IN_WIDTH), layer),
                  full(LANES), full(LANES)],
        out_specs=[full(POOL_WIDTH)] + [full(ATTN_WIDTH)] * 5 + [full(CONV_WIDTH)],
        out_shape=[jax.ShapeDtypeStruct((n, POOL_WIDTH), F32), flat(BF16), flat(F32), flat(BF16), flat(F32),
                   flat(BF16), jax.ShapeDtypeStruct((n, CONV_WIDTH), F32)],
        compiler_params=pltpu.CompilerParams(dimension_semantics=("arbitrary",),
                                             vmem_limit_bytes=VMEM_LIMIT),
        name="sample_in_proj",
    )(x, g, w_bf, cos_tab, sin_tab)


def _prompt_in_proj_kernel(*refs, n_unused):
    x_ref, g_ref, w_ref, cos_ref, sin_ref = refs[:5]
    upool_ref, q_ref, k_ref, kb_ref, v_ref, vb_ref, uconv_ref = refs[5 + n_unused:]
    h = _rms(x_ref[...], g_ref[...]).astype(BF16)
    z = jnp.dot(h, w_ref[...], preferred_element_type=F32)
    upool_ref[...] = z[:, :O_Q]
    uconv_ref[...] = z[:, O_A:O_B] * jax.nn.sigmoid(z[:, O_B:])
    _qkv_epilogue(z, cos_ref, sin_ref, q_ref, k_ref, kb_ref, v_ref, vb_ref, True)


def _prompt_in_proj(x, g, w_bf, cos_tab, sin_tab, tm, seq_len, kv_stack, layer):
    n = x.shape[0]
    tps = seq_len // tm
    batch = n // seq_len
    row = lambda w: pl.BlockSpec((tm, w), lambda i: (i, 0))
    tab = pl.BlockSpec((tm, LANES), lambda i: (i % tps, 0))
    flat = lambda dt, w=ATTN_WIDTH: jax.ShapeDtypeStruct((n, w), dt)
    extra_in, extra_specs, aliases = [], [], {}
    if isinstance(kv_stack, int):
        k_shape = jax.ShapeDtypeStruct((kv_stack, batch, ATTN_WIDTH, seq_len), F32)
        v_shape = jax.ShapeDtypeStruct((kv_stack, batch, N_HEADS * seq_len, V_DIM), F32)
    else:
        k_shape, v_shape = (jax.ShapeDtypeStruct(a.shape, a.dtype) for a in kv_stack)
        extra_in = list(kv_stack)
        extra_specs = [pl.BlockSpec(memory_space=pl.ANY)] * 2
        aliases = {5: 2, 6: 4}
    k_spec = pl.BlockSpec((None, None, ATTN_WIDTH, tm), lambda i: (layer, i // tps, 0, i % tps))
    v_spec = pl.BlockSpec((None, None, N_HEADS * tm, V_DIM), lambda i: (layer, i // tps, i % tps, 0))
    vt_spec = pl.BlockSpec((None, ATTN_WIDTH, tm), lambda i: (i // tps, 0, i % tps))
    return pl.pallas_call(
        functools.partial(_prompt_in_proj_kernel, n_unused=len(extra_in)),
        grid=(n // tm,),
        in_specs=[row(D_MODEL), _layer_spec((1, D_MODEL), layer), _layer_spec((D_MODEL, IN_WIDTH), layer), tab, tab]
        + extra_specs,
        out_specs=[row(POOL_WIDTH), row(ATTN_WIDTH), k_spec, row(ATTN_WIDTH), v_spec, vt_spec, row(CONV_WIDTH)],
        out_shape=[flat(F32, POOL_WIDTH), flat(BF16), k_shape, flat(BF16), v_shape,
                   jax.ShapeDtypeStruct((batch, ATTN_WIDTH, seq_len), BF16), flat(F32, CONV_WIDTH)],
        input_output_aliases=aliases,
        compiler_params=pltpu.CompilerParams(dimension_semantics=("arbitrary",),
                                             vmem_limit_bytes=VMEM_LIMIT),
        name="prompt_in_proj",
    )(x, g, w_bf, cos_tab, sin_tab, *extra_in)


def _lambda_scalar(lq, lam_init):
    a = jnp.sum(lq[0:1, :] * lq[1:2, :], axis=-1, keepdims=True)
    b = jnp.sum(lq[2:3, :] * lq[3:4, :], axis=-1, keepdims=True)
    return jnp.exp(a) - jnp.exp(b) + lam_init


def _stack_maps(q):
    lane = lax.broadcasted_iota(jnp.int32, q.shape, 1)
    zero = jnp.zeros_like(q)
    return jnp.concatenate([jnp.where(lane < HEAD_DIM, q, zero), jnp.where(lane >= HEAD_DIM, q, zero)], axis=0)


ONES_ROWS = 16


def _flash_scores(qq, kt):
    return lax.dot_general(kt, qq, (((1,), (1,)), ((), ())), preferred_element_type=F32)


def _flash_update(carry, s, vt_ones, p_stage):
    m, acc = carry
    m_new = jnp.maximum(m, jnp.max(s, axis=0, keepdims=True))
    alpha = jnp.exp2(m - m_new)
    p_stage[...] = jnp.exp2(s - m_new).astype(BF16)
    return m_new, alpha * acc + jnp.dot(vt_ones, p_stage[...], preferred_element_type=F32)


def _flash_init(cols):
    return jnp.full((1, cols), -jnp.inf, F32), jnp.zeros((V_DIM + ONES_ROWS, cols), F32)


def _attn_finish(carry, lam, g_col, lam_init, tq):
    _, acc = carry
    scaled = acc[:V_DIM] * (1.0 / acc[V_DIM:V_DIM + 1])
    o = scaled[:, :tq] - lam * scaled[:, tq:]
    o = o * lax.rsqrt(jnp.mean(o * o, axis=0, keepdims=True) + EPS) * g_col
    return (o * (1.0 - lam_init)).T


def _mask_diagonal_tile(s):
    t = s.shape[0]
    q_chunk = (lax.broadcasted_iota(jnp.int32, (1, s.shape[1]), 1) % t) // CHUNK
    blocks = [jnp.where(q_chunk >= a, s[a * CHUNK:(a + 1) * CHUNK, :], -1e30) for a in range(t // CHUNK)]
    return jnp.concatenate(blocks, axis=0)


def _prompt_attn_kernel(lq_ref, g_ref, q_ref, k_ref, vt_ref, o_ref, qq_ref, p_ref, *, lam_init):
    t = ATTN_TILE
    qi = pl.program_id(2)
    heads = [slice(h * LANES, (h + 1) * LANES) for h in range(HEADS_PER_STEP)]
    for h, c in enumerate(heads):
        qq_ref[h] = _stack_maps(q_ref[:, c])

    ones = jnp.ones((ONES_ROWS, t), BF16)
    stage0 = jnp.minimum(qi, 0)

    def steps(tiles, carry):
        rows = [pl.ds(pl.multiple_of(j * t, t), t) for j, _ in tiles]
        s = [[_flash_scores(qq_ref[h], k_ref[r, c]) for h, c in enumerate(heads)] for r in rows]
        for ti, (_, diagonal) in enumerate(tiles):
            if diagonal:
                s[ti] = [_mask_diagonal_tile(x) for x in s[ti]]
            carry = tuple(
                _flash_update(carry[h], s[ti][h], jnp.concatenate([vt_ref[c, rows[ti]], ones], axis=0),
                              p_ref.at[stage0 + ti * HEADS_PER_STEP + h])
                for h, c in enumerate(heads))
        return carry

    carry = lax.fori_loop(0, qi // 2, lambda p, c: steps([(2 * p, False), (2 * p + 1, False)], c),
                          (_flash_init(2 * t),) * HEADS_PER_STEP)
    carry = lax.cond(qi % 2 == 1,
                     lambda c: steps([(qi - 1, False), (qi, True)], c),
                     lambda c: steps([(qi, True)], c), carry)
    lam = _lambda_scalar(lq_ref[...], lam_init)
    for h, c in enumerate(heads):
        o_ref[:, c] = _attn_finish(carry[h], lam, g_ref[...], lam_init, t).astype(BF16)


def _prompt_attention(lq, g, q, kb, vtb, lam_init, layer):
    b, s, _ = q.shape
    t = ATTN_TILE
    w = HEADS_PER_STEP * LANES
    return pl.pallas_call(
        functools.partial(_prompt_attn_kernel, lam_init=lam_init),
        grid=(b, N_HEADS // HEADS_PER_STEP, s // t),
        in_specs=[
            _layer_spec((4, HEAD_DIM), layer),
            _layer_spec((V_DIM, 1), layer),
            pl.BlockSpec((None, t, w), lambda b_, h, i: (b_, i, h)),
            pl.BlockSpec((None, s, w), lambda b_, h, i: (b_, 0, h)),
            pl.BlockSpec((None, w, s), lambda b_, h, i: (b_, h, 0)),
        ],
        out_specs=pl.BlockSpec((None, t, w), lambda b_, h, i: (b_, i, h)),
        out_shape=jax.ShapeDtypeStruct((b, s, ATTN_WIDTH), BF16),
        scratch_shapes=[pltpu.VMEM((HEADS_PER_STEP, 2 * t, LANES), BF16),
                        pltpu.VMEM((2 * HEADS_PER_STEP, t, 2 * t), BF16)],
        compiler_params=pltpu.CompilerParams(dimension_semantics=("arbitrary",) * 3,
                                             vmem_limit_bytes=VMEM_LIMIT),
        name="prompt_attention",
    )(lq, g, q, kb, vtb)


def _row_scores(qq, kt, *, keys_on_rows):
    dims = (((1,), (1,)), ((), ())) if keys_on_rows else (((1,), (0,)), ((), ()))
    return lax.dot_general(qq, kt, dims, preferred_element_type=F32)


def _row_flash_update(carry, s, vt):
    m, l, acc = carry
    m_new = jnp.maximum(m, jnp.max(s, axis=-1, keepdims=True))
    alpha = jnp.exp2(m - m_new)
    p = jnp.exp2(s - m_new)
    l = alpha * l + jnp.sum(p, axis=-1, keepdims=True)
    acc = alpha * acc + jnp.dot(p.astype(BF16), vt, preferred_element_type=F32)
    return m_new, l, acc


def _sample_attn_kernel(lq_ref, g_ref, q_ref, ck_ref, cv_ref, k_ref, v_ref, o_ref, *, lam_init, past_len):
    t = SAMPLE_KEY_TILE
    tq = q_ref.shape[0]
    heads = [slice(h * LANES, (h + 1) * LANES) for h in range(N_HEADS)]
    qq = [_stack_maps(q_ref[:, c]) for c in heads]

    def body(j, carry):
        cols = pl.ds(pl.multiple_of(j * t, t), t)
        s = [_row_scores(qq[h], ck_ref[c, cols].astype(BF16), keys_on_rows=False) for h, c in enumerate(heads)]
        return tuple(
            _row_flash_update(carry[h], s[h], cv_ref[pl.ds(j * (t * N_HEADS) + h, t, stride=N_HEADS), :].astype(BF16))
            for h in range(N_HEADS))

    init = (jnp.full((2 * tq, 1), -jnp.inf, F32), jnp.zeros((2 * tq, 1), F32), jnp.zeros((2 * tq, V_DIM), F32))
    carry = lax.fori_loop(0, past_len // t, body, (init,) * N_HEADS)
    lam = _lambda_scalar(lq_ref[...], lam_init)
    for h, c in enumerate(heads):
        s = _row_scores(qq[h], k_ref[:, c], keys_on_rows=True)
        _, l, acc = _row_flash_update(carry[h], s, v_ref[:, c])
        o = acc[:tq] / l[:tq] - lam * (acc[tq:] / l[tq:])
        o_ref[:, c] = (_rms(o, g_ref[...]) * (1.0 - lam_init)).astype(BF16)


def _sample_attention(lq, g, q, cache_kt, cache_v4, kb, vb, lam_init, layer):
    b, ls, _ = q.shape
    past_len = cache_kt.shape[-1]
    assert past_len % SAMPLE_KEY_TILE == 0
    return pl.pallas_call(
        functools.partial(_sample_attn_kernel, lam_init=lam_init, past_len=past_len),
        grid=(b,),
        in_specs=[
            _layer_spec((4, HEAD_DIM), layer),
            _layer_spec((1, V_DIM), layer),
            pl.BlockSpec((None, ls, ATTN_WIDTH), lambda b_: (b_, 0, 0)),
            pl.BlockSpec((None, None, ATTN_WIDTH, past_len), lambda b_: (layer, b_, 0, 0)),
            pl.BlockSpec((None, None, N_HEADS * past_len, V_DIM), lambda b_: (layer, b_, 0, 0)),
            pl.BlockSpec((None, ls, ATTN_WIDTH), lambda b_: (b_, 0, 0)),
            pl.BlockSpec((None, ls, ATTN_WIDTH), lambda b_: (b_, 0, 0)),
        ],
        out_specs=pl.BlockSpec((None, ls, ATTN_WIDTH), lambda b_: (b_, 0, 0)),
        out_shape=jax.ShapeDtypeStruct((b, ls, ATTN_WIDTH), BF16),
        compiler_params=pltpu.CompilerParams(dimension_semantics=("arbitrary",),
                                             vmem_limit_bytes=VMEM_LIMIT),
        name="sample_attention",
    )(lq, g, q, cache_kt, cache_v4, kb, vb)


def _pool_mixer(ext_p, pool_w_ref, pool_scale_ref, tm, pos0):
    assert POOL_WINDOWS == (2, 4, 8, 16) and HALO == 32
    e = ext_p[...]
    n = HALO + tm
    s2 = e[8:] + e[7:n - 1]
    s4 = s2[8:] + s2[6:n - 10]
    s8 = s4[8:] + s4[4:n - 20]
    sums = {2: s2[24:], 4: s4[16:], 8: s8[8:], 16: s8[8:] + s8[:n - 32]}
    u = e[HALO:]
    lane = lax.broadcasted_iota(jnp.int32, (tm, POOL_WIDTH), 1)
    group = lane // POOL_GROUP_DIM
    win = sums[POOL_WINDOWS[-1]]
    width = jnp.full((tm, POOL_WIDTH), POOL_WINDOWS[-1], jnp.int32)
    for gi in range(len(POOL_WINDOWS) - 2, -1, -1):
        win = jnp.where(group == gi, sums[POOL_WINDOWS[gi]], win)
        width = jnp.where(group == gi, POOL_WINDOWS[gi], width)
    pos = pos0 + lax.broadcasted_iota(jnp.int32, (tm, POOL_WIDTH), 0)
    count = jnp.minimum(pos + 1, width).astype(F32)
    d = (win / count - u).astype(BF16)
    return jnp.dot(d, pool_w_ref[...], preferred_element_type=F32) * pool_scale_ref[...]


CONV_ROW_BLOCK = 64


def _conv_mixer_steps(ext_c, shift_c, conv_out, dw_ref, dw_b_ref, ln_g_ref, ln_b_ref, pw_ref, tm, store):
    first_row = HALO - CONV_HIST
    blk = min(tm, CONV_ROW_BLOCK)

    def shift(residues):
        for r in residues:
            shift_c[r, 0:HALO - 8 + tm, :] = ext_c[pl.ds(r, HALO - 8 + tm), :]

    def taps(rb):
        yb = jnp.zeros((blk, CONV_WIDTH), F32)
        for k in range(CONV_K):
            r, a = (first_row + k) % 8, (first_row + k) // 8
            rows = pl.ds(8 * a + rb * blk, blk)
            src = ext_c[rows, :] if r == 0 else shift_c[r, rows, :]
            yb = yb + src * dw_ref[k:k + 1, :]
        conv_out[rb * blk:(rb + 1) * blk, :] = yb

    def finish():
        y = conv_out[...] + dw_b_ref[...]
        mu = jnp.mean(y, axis=-1, keepdims=True)
        yc = y - mu
        yn = yc * lax.rsqrt(jnp.mean(yc * yc, axis=-1, keepdims=True) + EPS) * ln_g_ref[...] + ln_b_ref[...]
        act = (yn * jax.nn.sigmoid(yn)).astype(BF16)
        store(jnp.dot(act, pw_ref[...], preferred_element_type=F32))

    steps = [functools.partial(shift, rs) for rs in ((1, 2), (3, 4, 5), (6, 7))]
    steps += [functools.partial(taps, rb) for rb in range(tm // blk)]
    return steps + [finish]


def _conv_mixer(ext_c, shift_c, conv_out, dw_ref, dw_b_ref, ln_g_ref, ln_b_ref, pw_ref, tm):
    out = []
    for step in _conv_mixer_steps(ext_c, shift_c, conv_out, dw_ref, dw_b_ref, ln_g_ref, ln_b_ref, pw_ref, tm,
                                  out.append):
        step()
    return out[0]


def _mixer_scratch(tm):
    return [pltpu.VMEM((HALO + tm, POOL_WIDTH), F32), pltpu.VMEM((HALO + tm, CONV_WIDTH), F32),
            pltpu.VMEM((8, HALO + tm, CONV_WIDTH), F32), pltpu.VMEM((tm, CONV_WIDTH), F32)]


def _mixer_weight_specs(layer):
    shapes = [(POOL_WIDTH, POOL_WIDTH), (1, POOL_WIDTH), (CONV_K, CONV_WIDTH), (1, CONV_WIDTH), (1, CONV_WIDTH),
              (1, CONV_WIDTH), (CONV_WIDTH, CONV_WIDTH)]
    return [_layer_spec(shape, layer) for shape in shapes]


def _sample_mixers_kernel(up_ref, up_hist_ref, uc_ref, uc_hist_ref,
                          pool_w_ref, pool_scale_ref, dw_ref, dw_b_ref, ln_g_ref, ln_b_ref, pw_ref,
                          ypool_ref, yconv_ref, ext_p, ext_c, shift_c, conv_out, *, pos_base):
    tm = up_ref.shape[0]
    ext_p[0:HALO, :] = up_hist_ref[...]
    ext_p[HALO:, :] = up_ref[...]
    ext_c[0:HALO, :] = uc_hist_ref[...]
    ext_c[HALO:, :] = uc_ref[...]
    ypool_ref[...] = _pool_mixer(ext_p, pool_w_ref, pool_scale_ref, tm, pos_base).astype(BF16)
    yconv_ref[...] = _conv_mixer(ext_c, shift_c, conv_out, dw_ref, dw_b_ref, ln_g_ref, ln_b_ref, pw_ref,
                                 tm).astype(BF16)


def _sample_mixers(upool, uconv, hist_pool, hist_conv, mixer_w, seq_len, pos_base, layer):
    n = upool.shape[0]
    cur = pl.BlockSpec((seq_len, POOL_WIDTH), lambda i: (i, 0))
    hist = pl.BlockSpec((None, None, HALO, POOL_WIDTH), lambda i: (layer, i, 0, 0))
    return pl.pallas_call(
        functools.partial(_sample_mixers_kernel, pos_base=pos_base),
        grid=(n // seq_len,),
        in_specs=[cur, hist, cur, hist] + _mixer_weight_specs(layer),
        out_specs=[cur, cur],
        out_shape=[jax.ShapeDtypeStruct((n, POOL_WIDTH), BF16), jax.ShapeDtypeStruct((n, CONV_WIDTH), BF16)],
        scratch_shapes=_mixer_scratch(seq_len),
        compiler_params=pltpu.CompilerParams(dimension_semantics=("arbitrary",),
                                             vmem_limit_bytes=VMEM_LIMIT),
        name="sample_mixers",
    )(upool, hist_pool, uconv, hist_conv, *mixer_w)


def _dense_math(x, ypool, o, yconv, wout_ref, g_ref, wgu_ref, wd_ref, fg_ref, final):
    o1 = POOL_WIDTH
    o2 = o1 + ATTN_WIDTH
    mix = (jnp.dot(ypool, wout_ref[0:o1, :], preferred_element_type=F32)
           + jnp.dot(o, wout_ref[o1:o2, :], preferred_element_type=F32)
           + jnp.dot(yconv, wout_ref[o2:, :], preferred_element_type=F32))
    x1 = x + mix
    h = _rms(x1, g_ref[...]).astype(BF16)
    gu = jnp.dot(h, wgu_ref[...], preferred_element_type=F32)
    gate = gu[:, :D_FF]
    a = (gate * jax.nn.sigmoid(gate) * gu[:, D_FF:]).astype(BF16)
    x2 = x1 + jnp.dot(a, wd_ref[...], preferred_element_type=F32)
    return _rms(x2, fg_ref[...]) if final else x2


def _dense_weight_specs(layer):
    shapes = [(MIX_WIDTH, D_MODEL), (1, D_MODEL), (D_MODEL, 2 * D_FF), (D_FF, D_MODEL)]
    return [_layer_spec(shape, layer) for shape in shapes] + [_const_spec((1, D_MODEL))]


def _prompt_dense_kernel(x_ref, o_ref, up0_ref, uc0_ref, up_ref, uc_ref, hist_p_ref, hist_c_ref,
                         pool_w_ref, pool_scale_ref, dw_ref, dw_b_ref, ln_g_ref, ln_b_ref, pw_ref,
                         wout_ref, g_ref, wgu_ref, wd_ref, fg_ref, out_ref,
                         ext_p, ext_c, shift_c, conv_out, y_pool, y_conv, *, final, tiles_per_seq):
    tm = x_ref.shape[0]
    i = pl.program_id(0)

    def mixer_steps(up_ref_, uc_ref_, tile, slot, first):
        def fill():
            if first is True:
                ext_p[0:HALO, :] = hist_p_ref[...]
                ext_c[0:HALO, :] = hist_c_ref[...]
            else:
                ext_p[0:HALO, :] = jnp.where(first, hist_p_ref[...], ext_p[tm:, :])
                ext_c[0:HALO, :] = jnp.where(first, hist_c_ref[...], ext_c[tm:, :])
            ext_p[HALO:, :] = up_ref_[...]
            ext_c[HALO:, :] = uc_ref_[...]

        def pool():
            pos0 = (tile % tiles_per_seq) * tm
            y_pool[slot] = _pool_mixer(ext_p, pool_w_ref, pool_scale_ref, tm, pos0).astype(BF16)

        def store_conv(y):
            y_conv[slot] = y.astype(BF16)

        return [fill] + _conv_mixer_steps(ext_c, shift_c, conv_out, dw_ref, dw_b_ref, ln_g_ref, ln_b_ref, pw_ref,
                                          tm, store_conv) + [pool]

    @pl.when(i == 0)
    def _():
        for step in mixer_steps(up0_ref, uc0_ref, 0, 0, True):
            step()

    slot = i % 2
    nxt = jnp.minimum(i + 1, pl.num_programs(0) - 1)
    out_ref[...] = _dense_math(x_ref[...], y_pool[slot], o_ref[...], y_conv[slot],
                               wout_ref, g_ref, wgu_ref, wd_ref, fg_ref, final)
    for step in mixer_steps(up_ref, uc_ref, nxt, 1 - slot, nxt % tiles_per_seq == 0):
        step()


def _prompt_dense(x, o, upool, uconv, hist_pool, hist_conv, mixer_w, dense_w, tm, seq_len, final, layer):
    n = x.shape[0]
    tps = seq_len // tm
    last = n // tm - 1
    row = lambda w: pl.BlockSpec((tm, w), lambda i: (i, 0))
    first_tile = lambda w: pl.BlockSpec((tm, w), lambda i: (0, 0))
    next_tile = lambda w: pl.BlockSpec((tm, w), lambda i: (jnp.minimum(i + 1, last), 0))
    next_seq = lambda w: pl.BlockSpec((None, HALO, w), lambda i: (jnp.minimum(i + 1, last) // tps, 0, 0))
    return pl.pallas_call(
        functools.partial(_prompt_dense_kernel, final=final, tiles_per_seq=tps),
        grid=(n // tm,),
        in_specs=[row(D_MODEL), row(ATTN_WIDTH), first_tile(POOL_WIDTH), first_tile(CONV_WIDTH),
                  next_tile(POOL_WIDTH), next_tile(CONV_WIDTH), next_seq(POOL_WIDTH), next_seq(CONV_WIDTH)]
        + _mixer_weight_specs(layer) + _dense_weight_specs(layer),
        out_specs=row(D_MODEL),
        out_shape=jax.ShapeDtypeStruct((n, D_MODEL), F32),
        scratch_shapes=_mixer_scratch(tm) + [pltpu.VMEM((2, tm, POOL_WIDTH), BF16),
                                             pltpu.VMEM((2, tm, CONV_WIDTH), BF16)],
        compiler_params=pltpu.CompilerParams(dimension_semantics=("arbitrary",),
                                             vmem_limit_bytes=VMEM_LIMIT),
        name="prompt_dense",
    )(x, o, upool, uconv, upool, uconv, hist_pool, hist_conv, *mixer_w, *dense_w)


def _sample_dense_kernel(x_ref, yp_ref, o_ref, yc_ref, wout_ref, g_ref, wgu_ref, wd_ref, fg_ref, out_ref, *, final):
    out_ref[...] = _dense_math(x_ref[...], yp_ref[...], o_ref[...], yc_ref[...],
                               wout_ref, g_ref, wgu_ref, wd_ref, fg_ref, final)


def _sample_dense(x, ypool, o, yconv, wout_bf, g, wgu_bf, wd_bf, fg, tm, final, layer):
    n = x.shape[0]
    row = lambda w: pl.BlockSpec((tm, w), lambda i: (i, 0))
    return pl.pallas_call(
        functools.partial(_sample_dense_kernel, final=final),
        grid=(n // tm,),
        in_specs=[row(D_MODEL), row(POOL_WIDTH), row(ATTN_WIDTH), row(CONV_WIDTH)] + _dense_weight_specs(layer),
        out_specs=row(D_MODEL),
        out_shape=jax.ShapeDtypeStruct((n, D_MODEL), F32),
        compiler_params=pltpu.CompilerParams(dimension_semantics=("arbitrary",),
                                             vmem_limit_bytes=VMEM_LIMIT),
        name="sample_dense",
    )(x, ypool, o, yconv, wout_bf, g, wgu_bf, wd_bf, fg)


def _rope_tables(pos):
    half = HEAD_DIM // 2
    inv = ROPE_THETA ** (-jnp.arange(half, dtype=F32) / half)
    ang = pos.astype(F32)[:, None] * inv[None, :]
    reps = LANES // half
    cos = jnp.tile(jnp.cos(ang), (1, reps))
    sign = jnp.tile(jnp.concatenate([-jnp.ones((half,), F32), jnp.ones((half,), F32)]), LANES // HEAD_DIM)
    sin = jnp.tile(jnp.sin(ang), (1, reps)) * sign[None, :]
    return cos, sin


def _block_diag(pool_w):
    depth, g, d, _ = pool_w.shape
    eye = jnp.eye(g, dtype=pool_w.dtype)
    return (pool_w[:, :, :, None, :] * eye[None, :, None, :, None]).reshape(depth, g * d, g * d)


def _pad_hist(h):
    return jnp.pad(h, ((0, 0), (0, 0), (HALO - h.shape[2], 0), (0, 0)))


def kernel(x_prompt, x_sample, cache_k, cache_v, state_pool, state_conv, norm_mix_g, w_in, pool_w, pool_scale,
           lambda_qk, diff_norm_g, conv_dw, conv_dw_b, conv_ln_g, conv_ln_b, conv_pw, w_out, norm_ffn_g,
           w_gate_up, w_down, final_norm_g):
    B, S, _ = x_prompt.shape
    Bs, Ls, _ = x_sample.shape
    depth = w_in.shape[0]
    past_len = cache_k.shape[2]
    assert S % ATTN_TILE == 0 and S % ROW_TILE == 0 and S % IN_PROJ_TILE == 0 and S // ROW_TILE >= 2
    assert (Bs * Ls) % 16 == 0 and Ls % 16 == 0 and Ls >= CONV_HIST

    cos_p, sin_p = _rope_tables(jnp.arange(S, dtype=jnp.int32))
    cos_s, sin_s = _rope_tables(past_len + jnp.arange(Ls, dtype=jnp.int32))
    cos_s, sin_s = jnp.tile(cos_s, (Bs, 1)), jnp.tile(sin_s, (Bs, 1))
    zero_pool = jnp.zeros((B, HALO, POOL_WIDTH), F32)
    zero_conv = jnp.zeros((B, HALO, CONV_WIDTH), F32)

    xp = x_prompt.reshape(B * S, D_MODEL)
    xs = x_sample.reshape(Bs * Ls, D_MODEL)
    rows = lambda a: a.reshape(depth, 1, -1)
    norm_mix_rows, norm_ffn_rows = rows(norm_mix_g), rows(norm_ffn_g)
    diff_g_row, diff_g_col = rows(diff_norm_g), diff_norm_g.reshape(depth, -1, 1)
    w_in_bf, w_out_bf, w_gu_bf, w_d_bf = (w.astype(BF16) for w in (w_in, w_out, w_gate_up, w_down))
    mixer_w = (_block_diag(pool_w).astype(BF16), rows(pool_scale), conv_dw, rows(conv_dw_b), rows(conv_ln_g),
               rows(conv_ln_b), conv_pw.astype(BF16))
    dense_w = (w_out_bf, norm_ffn_rows, w_gu_bf, w_d_bf, final_norm_g.reshape(1, -1))
    hist_pool_s, hist_conv_s = _pad_hist(state_pool), _pad_hist(state_conv)
    cache_kt = jnp.transpose(cache_k, (0, 1, 3, 4, 5, 2)).reshape(depth, Bs, ATTN_WIDTH, past_len)
    cache_v4 = cache_v.reshape(depth, Bs, past_len * N_HEADS, V_DIM)
    kv_stack = depth
    pp, cp = [], []
    ks_, vs_, ps_, cs_ = [], [], [], []
    for l in range(depth):
        lam_init = 0.8 - 0.6 * math.exp(-0.3 * l)
        final = l == depth - 1

        upool, q, k, kb, v, vb, uconv = _prompt_in_proj(xp, norm_mix_rows, w_in_bf, cos_p, sin_p, IN_PROJ_TILE, S,
                                                        kv_stack, l)
        kv_stack = (k, v)
        o = _prompt_attention(lambda_qk, diff_g_col, q.reshape(B, S, -1), kb.reshape(B, S, -1),
                              vb, lam_init, l)
        xp = _prompt_dense(xp, o.reshape(B * S, -1), upool, uconv, zero_pool, zero_conv, mixer_w, dense_w,
                           ROW_TILE, S, final, l)
        pp.append(upool.reshape(B, S, -1)[:, S - POOL_HIST:])
        cp.append(uconv.reshape(B, S, -1)[:, S - CONV_HIST:])

        upool, q, k, kb, v, vb, uconv = _sample_in_proj(xs, norm_mix_rows, w_in_bf, cos_s, sin_s, l)
        o = _sample_attention(lambda_qk, diff_g_row, q.reshape(Bs, Ls, -1), cache_kt, cache_v4,
                              kb.reshape(Bs, Ls, -1), vb.reshape(Bs, Ls, -1), lam_init, l)
        ypool, yconv = _sample_mixers(upool, uconv, hist_pool_s, hist_conv_s, mixer_w, Ls, past_len, l)
        xs = _sample_dense(xs, ypool, o.reshape(Bs * Ls, -1), yconv, *dense_w, Bs * Ls, final, l)
        ks_.append(k.reshape(Bs, Ls, N_HEADS, 2, HEAD_DIM))
        vs_.append(v.reshape(Bs, Ls, N_HEADS, V_DIM))
        ps_.append(upool.reshape(Bs, Ls, -1)[:, Ls - POOL_HIST:])
        cs_.append(uconv.reshape(Bs, Ls, -1)[:, Ls - CONV_HIST:])

    kt_all, v4_all = kv_stack
    new_k = jnp.transpose(kt_all.reshape(depth, B, N_HEADS, 2, HEAD_DIM, S), (0, 1, 5, 2, 3, 4))
    new_v = v4_all.reshape(depth, B, S, N_HEADS, V_DIM)
    return (xp.reshape(B, S, D_MODEL), xs.reshape(Bs, Ls, D_MODEL),
            new_k, new_v, jnp.stack(pp), jnp.stack(cp),
            jnp.stack(ks_), jnp.stack(vs_), jnp.stack(ps_), jnp.stack(cs_))
```

```python
import functools
import math

import jax
import jax.numpy as jnp
from jax import lax
from jax.experimental import pallas as pl
from jax.experimental.pallas import tpu as pltpu

D_MODEL = 1024
CHUNK = 64
POOL_WIDTH = D_MODEL // 4
POOL_WINDOWS = (2, 4, 8, 16)
POOL_GROUP_DIM = POOL_WIDTH // len(POOL_WINDOWS)
POOL_HIST = max(POOL_WINDOWS) - 1
ATTN_WIDTH = D_MODEL // 2
N_HEADS = 4
HEAD_DIM = ATTN_WIDTH // (2 * N_HEADS)
V_DIM = 2 * HEAD_DIM
ROPE_THETA = 10000.0
CONV_WIDTH = D_MODEL // 4
CONV_K = 31
CONV_HIST = CONV_K - 1
MIX_WIDTH = POOL_WIDTH + ATTN_WIDTH + CONV_WIDTH
IN_WIDTH = POOL_WIDTH + 3 * ATTN_WIDTH + 2 * CONV_WIDTH
D_FF = ((-(-8 * D_MODEL // 3) + 255) // 256) * 256
EPS = 1e-6

LANES = 128
HALO = 32
ROW_TILE = 512
IN_PROJ_TILE = 1024
ATTN_TILE = 256
SAMPLE_KEY_TILE = 1024
Q_TILES_PER_STEP = 8
HEADS_PER_STEP = 4
VMEM_LIMIT = 56 * 1024 * 1024

F32 = jnp.float32
BF16 = jnp.bfloat16


def _rms(x, g):
    return x * lax.rsqrt(jnp.mean(x * x, axis=-1, keepdims=True) + EPS) * g


def _const_spec(shape):
    return pl.BlockSpec(shape, lambda *_: (0,) * len(shape), pipeline_mode=pl.Buffered(1))


def _layer_spec(shape, layer):
    return pl.BlockSpec((None,) + shape, lambda *_: (layer,) + (0,) * len(shape), pipeline_mode=pl.Buffered(1))


O_Q = POOL_WIDTH
O_K = O_Q + ATTN_WIDTH
O_V = O_K + ATTN_WIDTH
O_A = O_V + ATTN_WIDTH
O_B = O_A + CONV_WIDTH


def _qkv_epilogue(z, cos_ref, sin_ref, q_ref, k_ref, kb_ref, v_ref, vb_ref, cache_layout):
    tm = z.shape[0]
    o1, o2, o3 = O_Q, O_K, O_V
    cos = cos_ref[...]
    sin = sin_ref[...]
    lane = lax.broadcasted_iota(jnp.int32, cos.shape, 1)
    first_half = (lane & (HEAD_DIM // 2)) == 0

    def rope(t):
        up = pltpu.roll(t, LANES - HEAD_DIM // 2, axis=1)
        down = pltpu.roll(t, HEAD_DIM // 2, axis=1)
        return t * cos + jnp.where(first_half, up, down) * sin

    scale = HEAD_DIM ** -0.5 * math.log2(math.e)
    for j in range(ATTN_WIDTH // LANES):
        sl = slice(j * LANES, (j + 1) * LANES)
        qj = rope(z[:, o1 + j * LANES:o1 + (j + 1) * LANES])
        kj = rope(z[:, o2 + j * LANES:o2 + (j + 1) * LANES])
        vj = z[:, o3 + j * LANES:o3 + (j + 1) * LANES]
        q_ref[:, sl] = (qj * scale).astype(BF16)
        kb_ref[:, sl] = kj.astype(BF16)
        if cache_layout:
            k_ref[sl, :] = kj.T
            v_ref[pl.ds(j, tm, stride=N_HEADS), :] = vj
            vb_ref[sl, :] = vj.T.astype(BF16)
        else:
            k_ref[:, sl] = kj
            v_ref[:, sl] = vj
            vb_ref[:, sl] = vj.astype(BF16)


def _sample_in_proj_kernel(x_ref, g_ref, w_ref, cos_ref, sin_ref,
                           upool_ref, q_ref, k_ref, kb_ref, v_ref, vb_ref, uconv_ref):
    h = _rms(x_ref[...], g_ref[...]).astype(BF16)
    z = jnp.dot(h, w_ref[...], preferred_element_type=F32)
    upool_ref[...] = z[:, :O_Q]
    uconv_ref[...] = z[:, O_A:O_B] * jax.nn.sigmoid(z[:, O_B:])
    _qkv_epilogue(z, cos_ref, sin_ref, q_ref, k_ref, kb_ref, v_ref, vb_ref, False)


def _sample_in_proj(x, g, w_bf, cos_tab, sin_tab, layer):
    n = x.shape[0]
    full = lambda w: pl.BlockSpec((n, w), lambda i: (0, 0))
    flat = lambda dt: jax.ShapeDtypeStruct((n, ATTN_WIDTH), dt)
    return pl.pallas_call(
        _sample_in_proj_kernel,
        grid=(1,),
        in_specs=[full(D_MODEL), _layer_spec((1, D_MODEL), layer), _layer_spec((D_MODEL, IN_WIDTH), layer),
                  full(LANES), full(LANES)],
        out_specs=[full(POOL_WIDTH)] + [full(ATTN_WIDTH)] * 5 + [full(CONV_WIDTH)],
        out_shape=[jax.ShapeDtypeStruct((n, POOL_WIDTH), F32), flat(BF16), flat(F32), flat(BF16), flat(F32),
                   flat(BF16), jax.ShapeDtypeStruct((n, CONV_WIDTH), F32)],
        compiler_params=pltpu.CompilerParams(dimension_semantics=("arbitrary",),
                                             vmem_limit_bytes=VMEM_LIMIT),
        name="sample_in_proj",
    )(x, g, w_bf, cos_tab, sin_tab)


def _prompt_in_proj_kernel(*refs, n_unused):
    x_ref, g_ref, w_ref, cos_ref, sin_ref = refs[:5]
    upool_ref, q_ref, k_ref, kb_ref, v_ref, vb_ref, uconv_ref = refs[5 + n_unused:]
    h = _rms(x_ref[...], g_ref[...]).astype(BF16)
    z = jnp.dot(h, w_ref[...], preferred_element_type=F32)
    upool_ref[...] = z[:, :O_Q]
    uconv_ref[...] = z[:, O_A:O_B] * jax.nn.sigmoid(z[:, O_B:])
    _qkv_epilogue(z, cos_ref, sin_ref, q_ref, k_ref, kb_ref, v_ref, vb_ref, True)


def _prompt_in_proj(x, g, w_bf, cos_tab, sin_tab, tm, seq_len, kv_stack, layer):
    n = x.shape[0]
    tps = seq_len // tm
    batch = n // seq_len
    row = lambda w: pl.BlockSpec((tm, w), lambda i: (i, 0))
    tab = pl.BlockSpec((tm, LANES), lambda i: (i % tps, 0))
    flat = lambda dt, w=ATTN_WIDTH: jax.ShapeDtypeStruct((n, w), dt)
    extra_in, extra_specs, aliases = [], [], {}
    if isinstance(kv_stack, int):
        k_shape = jax.ShapeDtypeStruct((kv_stack, batch, ATTN_WIDTH, seq_len), F32)
        v_shape = jax.ShapeDtypeStruct((kv_stack, batch, N_HEADS * seq_len, V_DIM), F32)
    else:
        k_shape, v_shape = (jax.ShapeDtypeStruct(a.shape, a.dtype) for a in kv_stack)
        extra_in = list(kv_stack)
        extra_specs = [pl.BlockSpec(memory_space=pl.ANY)] * 2
        aliases = {5: 2, 6: 4}
    k_spec = pl.BlockSpec((None, None, ATTN_WIDTH, tm), lambda i: (layer, i // tps, 0, i % tps))
    v_spec = pl.BlockSpec((None, None, N_HEADS * tm, V_DIM), lambda i: (layer, i // tps, i % tps, 0))
    vt_spec = pl.BlockSpec((None, ATTN_WIDTH, tm), lambda i: (i // tps, 0, i % tps))
    return pl.pallas_call(
        functools.partial(_prompt_in_proj_kernel, n_unused=len(extra_in)),
        grid=(n // tm,),
        in_specs=[row(D_MODEL), _layer_spec((1, D_MODEL), layer), _layer_spec((D_MODEL, IN_WIDTH), layer), tab, tab]
        + extra_specs,
        out_specs=[row(POOL_WIDTH), row(ATTN_WIDTH), k_spec, row(ATTN_WIDTH), v_spec, vt_spec, row(CONV_WIDTH)],
        out_shape=[flat(F32, POOL_WIDTH), flat(BF16), k_shape, flat(BF16), v_shape,
                   jax.ShapeDtypeStruct((batch, ATTN_WIDTH, seq_len), BF16), flat(F32, CONV_WIDTH)],
        input_output_aliases=aliases,
        compiler_params=pltpu.CompilerParams(dimension_semantics=("arbitrary",),
                                             vmem_limit_bytes=VMEM_LIMIT),
        name="prompt_in_proj",
    )(x, g, w_bf, cos_tab, sin_tab, *extra_in)


def _lambda_scalar(lq, lam_init):
    a = jnp.sum(lq[0:1, :] * lq[1:2, :], axis=-1, keepdims=True)
    b = jnp.sum(lq[2:3, :] * lq[3:4, :], axis=-1, keepdims=True)
    return jnp.exp(a) - jnp.exp(b) + lam_init


def _stack_maps(q):
    lane = lax.broadcasted_iota(jnp.int32, q.shape, 1)
    zero = jnp.zeros_like(q)
    return jnp.concatenate([jnp.where(lane < HEAD_DIM, q, zero), jnp.where(lane >= HEAD_DIM, q, zero)], axis=0)


ONES_ROWS = 16


def _flash_scores(qq, kt):
    return lax.dot_general(kt, qq, (((1,), (1,)), ((), ())), preferred_element_type=F32)


def _flash_update(carry, s, vt_ones, p_stage):
    m, acc = carry
    m_new = jnp.maximum(m, jnp.max(s, axis=0, keepdims=True))
    alpha = jnp.exp2(m - m_new)
    p_stage[...] = jnp.exp2(s - m_new).astype(BF16)
    return m_new, alpha * acc + jnp.dot(vt_ones, p_stage[...], preferred_element_type=F32)


def _flash_init(cols):
    return jnp.full((1, cols), -jnp.inf, F32), jnp.zeros((V_DIM + ONES_ROWS, cols), F32)


def _attn_finish(carry, lam, g_col, lam_init, tq):
    _, acc = carry
    scaled = acc[:V_DIM] * (1.0 / acc[V_DIM:V_DIM + 1])
    o = scaled[:, :tq] - lam * scaled[:, tq:]
    o = o * lax.rsqrt(jnp.mean(o * o, axis=0, keepdims=True) + EPS) * g_col
    return (o * (1.0 - lam_init)).T


def _mask_diagonal_tile(s):
    t = s.shape[0]
    q_chunk = (lax.broadcasted_iota(jnp.int32, (1, s.shape[1]), 1) % t) // CHUNK
    blocks = [jnp.where(q_chunk >= a, s[a * CHUNK:(a + 1) * CHUNK, :], -1e30) for a in range(t // CHUNK)]
    return jnp.concatenate(blocks, axis=0)


def _prompt_attn_kernel(lq_ref, g_ref, q_ref, k_ref, vt_ref, o_ref, qq_ref, p_ref, *, lam_init):
    t = ATTN_TILE
    heads = [slice(h * LANES, (h + 1) * LANES) for h in range(HEADS_PER_STEP)]
    ones = jnp.ones((ONES_ROWS, t), BF16)
    lam = _lambda_scalar(lq_ref[...], lam_init)
    q_tiles = q_ref.shape[0] // t

    def one_query_tile(sub, _):
        qi = pl.program_id(2) * q_tiles + sub
        q_rows = pl.ds(pl.multiple_of(sub * t, t), t)
        for h, c in enumerate(heads):
            qq_ref[h] = _stack_maps(q_ref[q_rows, c])
        stage0 = jnp.minimum(qi, 0)

        def steps(tiles, carry):
            rows = [pl.ds(pl.multiple_of(j * t, t), t) for j, _ in tiles]
            s = [[_flash_scores(qq_ref[h], k_ref[r, c]) for h, c in enumerate(heads)] for r in rows]
            for ti, (_, diagonal) in enumerate(tiles):
                if diagonal:
                    s[ti] = [_mask_diagonal_tile(x) for x in s[ti]]
                carry = tuple(
                    _flash_update(carry[h], s[ti][h], jnp.concatenate([vt_ref[c, rows[ti]], ones], axis=0),
                                  p_ref.at[stage0 + ti * HEADS_PER_STEP + h])
                    for h, c in enumerate(heads))
            return carry

        carry = lax.fori_loop(0, qi // 2, lambda p, c: steps([(2 * p, False), (2 * p + 1, False)], c),
                              (_flash_init(2 * t),) * HEADS_PER_STEP)
        carry = lax.cond(qi % 2 == 1,
                         lambda c: steps([(qi - 1, False), (qi, True)], c),
                         lambda c: steps([(qi, True)], c), carry)
        for h, c in enumerate(heads):
            o_ref[q_rows, c] = _attn_finish(carry[h], lam, g_ref[...], lam_init, t).astype(BF16)
        return 0

    lax.fori_loop(0, q_tiles, one_query_tile, 0)


def _prompt_attention(lq, g, q, kb, vtb, lam_init, layer):
    b, s, _ = q.shape
    t = ATTN_TILE
    w = HEADS_PER_STEP * LANES
    tq = Q_TILES_PER_STEP * t
    return pl.pallas_call(
        functools.partial(_prompt_attn_kernel, lam_init=lam_init),
        grid=(b, N_HEADS // HEADS_PER_STEP, s // tq),
        in_specs=[
            _layer_spec((4, HEAD_DIM), layer),
            _layer_spec((V_DIM, 1), layer),
            pl.BlockSpec((None, tq, w), lambda b_, h, i: (b_, i, h)),
            pl.BlockSpec((None, s, w), lambda b_, h, i: (b_, 0, h)),
            pl.BlockSpec((None, w, s), lambda b_, h, i: (b_, h, 0)),
        ],
        out_specs=pl.BlockSpec((None, tq, w), lambda b_, h, i: (b_, i, h)),
        out_shape=jax.ShapeDtypeStruct((b, s, ATTN_WIDTH), BF16),
        scratch_shapes=[pltpu.VMEM((HEADS_PER_STEP, 2 * t, LANES), BF16),
                        pltpu.VMEM((2 * HEADS_PER_STEP, t, 2 * t), BF16)],
        compiler_params=pltpu.CompilerParams(dimension_semantics=("arbitrary",) * 3,
                                             vmem_limit_bytes=VMEM_LIMIT),
        name="prompt_attention",
    )(lq, g, q, kb, vtb)


def _row_scores(qq, kt, *, keys_on_rows):
    dims = (((1,), (1,)), ((), ())) if keys_on_rows else (((1,), (0,)), ((), ()))
    return lax.dot_general(qq, kt, dims, preferred_element_type=F32)


def _row_flash_update(carry, s, vt):
    m, l, acc = carry
    m_new = jnp.maximum(m, jnp.max(s, axis=-1, keepdims=True))
    alpha = jnp.exp2(m - m_new)
    p = jnp.exp2(s - m_new)
    l = alpha * l + jnp.sum(p, axis=-1, keepdims=True)
    acc = alpha * acc + jnp.dot(p.astype(BF16), vt, preferred_element_type=F32)
    return m_new, l, acc


def _sample_attn_kernel(lq_ref, g_ref, q_ref, ck_ref, cv_ref, k_ref, v_ref, o_ref, *, lam_init, past_len):
    t = SAMPLE_KEY_TILE
    tq = q_ref.shape[0]
    heads = [slice(h * LANES, (h + 1) * LANES) for h in range(N_HEADS)]
    qq = [_stack_maps(q_ref[:, c]) for c in heads]

    def body(j, carry):
        cols = pl.ds(pl.multiple_of(j * t, t), t)
        s = [_row_scores(qq[h], ck_ref[c, cols].astype(BF16), keys_on_rows=False) for h, c in enumerate(heads)]
        return tuple(
            _row_flash_update(carry[h], s[h], cv_ref[pl.ds(j * (t * N_HEADS) + h, t, stride=N_HEADS), :].astype(BF16))
            for h in range(N_HEADS))

    init = (jnp.full((2 * tq, 1), -jnp.inf, F32), jnp.zeros((2 * tq, 1), F32), jnp.zeros((2 * tq, V_DIM), F32))
    carry = lax.fori_loop(0, past_len // t, body, (init,) * N_HEADS)
    lam = _lambda_scalar(lq_ref[...], lam_init)
    for h, c in enumerate(heads):
        s = _row_scores(qq[h], k_ref[:, c], keys_on_rows=True)
        _, l, acc = _row_flash_update(carry[h], s, v_ref[:, c])
        o = acc[:tq] / l[:tq] - lam * (acc[tq:] / l[tq:])
        o_ref[:, c] = (_rms(o, g_ref[...]) * (1.0 - lam_init)).astype(BF16)


def _sample_attention(lq, g, q, cache_kt, cache_v4, kb, vb, lam_init, layer):
    b, ls, _ = q.shape
    past_len = cache_kt.shape[-1]
    assert past_len % SAMPLE_KEY_TILE == 0
    return pl.pallas_call(
        functools.partial(_sample_attn_kernel, lam_init=lam_init, past_len=past_len),
        grid=(b,),
        in_specs=[
            _layer_spec((4, HEAD_DIM), layer),
            _layer_spec((1, V_DIM), layer),
            pl.BlockSpec((None, ls, ATTN_WIDTH), lambda b_: (b_, 0, 0)),
            pl.BlockSpec((None, None, ATTN_WIDTH, past_len), lambda b_: (layer, b_, 0, 0)),
            pl.BlockSpec((None, None, N_HEADS * past_len, V_DIM), lambda b_: (layer, b_, 0, 0)),
            pl.BlockSpec((None, ls, ATTN_WIDTH), lambda b_: (b_, 0, 0)),
            pl.BlockSpec((None, ls, ATTN_WIDTH), lambda b_: (b_, 0, 0)),
        ],
        out_specs=pl.BlockSpec((None, ls, ATTN_WIDTH), lambda b_: (b_, 0, 0)),
        out_shape=jax.ShapeDtypeStruct((b, ls, ATTN_WIDTH), BF16),
        compiler_params=pltpu.CompilerParams(dimension_semantics=("arbitrary",),
                                             vmem_limit_bytes=VMEM_LIMIT),
        name="sample_attention",
    )(lq, g, q, cache_kt, cache_v4, kb, vb)


def _pool_mixer(ext_p, pool_w_ref, pool_scale_ref, tm, pos0):
    assert POOL_WINDOWS == (2, 4, 8, 16) and HALO == 32
    e = ext_p[...]
    n = HALO + tm
    s2 = e[8:] + e[7:n - 1]
    s4 = s2[8:] + s2[6:n - 10]
    s8 = s4[8:] + s4[4:n - 20]
    sums = {2: s2[24:], 4: s4[16:], 8: s8[8:], 16: s8[8:] + s8[:n - 32]}
    u = e[HALO:]
    lane = lax.broadcasted_iota(jnp.int32, (tm, POOL_WIDTH), 1)
    group = lane // POOL_GROUP_DIM
    win = sums[POOL_WINDOWS[-1]]
    width = jnp.full((tm, POOL_WIDTH), POOL_WINDOWS[-1], jnp.int32)
    for gi in range(len(POOL_WINDOWS) - 2, -1, -1):
        win = jnp.where(group == gi, sums[POOL_WINDOWS[gi]], win)
        width = jnp.where(group == gi, POOL_WINDOWS[gi], width)
    pos = pos0 + lax.broadcasted_iota(jnp.int32, (tm, POOL_WIDTH), 0)
    count = jnp.minimum(pos + 1, width).astype(F32)
    d = (win / count - u).astype(BF16)
    return jnp.dot(d, pool_w_ref[...], preferred_element_type=F32) * pool_scale_ref[...]


CONV_ROW_BLOCK = 64


def _conv_mixer_steps(ext_c, shift_c, conv_out, dw_ref, dw_b_ref, ln_g_ref, ln_b_ref, pw_ref, tm, store):
    first_row = HALO - CONV_HIST
    blk = min(tm, CONV_ROW_BLOCK)

    def shift(residues):
        for r in residues:
            shift_c[r, 0:HALO - 8 + tm, :] = ext_c[pl.ds(r, HALO - 8 + tm), :]

    def taps(rb):
        yb = jnp.zeros((blk, CONV_WIDTH), F32)
        for k in range(CONV_K):
            r, a = (first_row + k) % 8, (first_row + k) // 8
            rows = pl.ds(8 * a + rb * blk, blk)
            src = ext_c[rows, :] if r == 0 else shift_c[r, rows, :]
            yb = yb + src * dw_ref[k:k + 1, :]
        conv_out[rb * blk:(rb + 1) * blk, :] = yb

    def finish():
        y = conv_out[...] + dw_b_ref[...]
        mu = jnp.mean(y, axis=-1, keepdims=True)
        yc = y - mu
        yn = yc * lax.rsqrt(jnp.mean(yc * yc, axis=-1, keepdims=True) + EPS) * ln_g_ref[...] + ln_b_ref[...]
        act = (yn * jax.nn.sigmoid(yn)).astype(BF16)
        store(jnp.dot(act, pw_ref[...], preferred_element_type=F32))

    steps = [functools.partial(shift, rs) for rs in ((1, 2), (3, 4, 5), (6, 7))]
    steps += [functools.partial(taps, rb) for rb in range(tm // blk)]
    return steps + [finish]


def _conv_mixer(ext_c, shift_c, conv_out, dw_ref, dw_b_ref, ln_g_ref, ln_b_ref, pw_ref, tm):
    out = []
    for step in _conv_mixer_steps(ext_c, shift_c, conv_out, dw_ref, dw_b_ref, ln_g_ref, ln_b_ref, pw_ref, tm,
                                  out.append):
        step()
    return out[0]


def _mixer_scratch(tm):
    return [pltpu.VMEM((HALO + tm, POOL_WIDTH), F32), pltpu.VMEM((HALO + tm, CONV_WIDTH), F32),
            pltpu.VMEM((8, HALO + tm, CONV_WIDTH), F32), pltpu.VMEM((tm, CONV_WIDTH), F32)]


def _mixer_weight_specs(layer):
    shapes = [(POOL_WIDTH, POOL_WIDTH), (1, POOL_WIDTH), (CONV_K, CONV_WIDTH), (1, CONV_WIDTH), (1, CONV_WIDTH),
              (1, CONV_WIDTH), (CONV_WIDTH, CONV_WIDTH)]
    return [_layer_spec(shape, layer) for shape in shapes]


def _sample_mixers_kernel(up_ref, up_hist_ref, uc_ref, uc_hist_ref,
                          pool_w_ref, pool_scale_ref, dw_ref, dw_b_ref, ln_g_ref, ln_b_ref, pw_ref,
                          ypool_ref, yconv_ref, ext_p, ext_c, shift_c, conv_out, *, pos_base):
    tm = up_ref.shape[0]
    ext_p[0:HALO, :] = up_hist_ref[...]
    ext_p[HALO:, :] = up_ref[...]
    ext_c[0:HALO, :] = uc_hist_ref[...]
    ext_c[HALO:, :] = uc_ref[...]
    ypool_ref[...] = _pool_mixer(ext_p, pool_w_ref, pool_scale_ref, tm, pos_base).astype(BF16)
    yconv_ref[...] = _conv_mixer(ext_c, shift_c, conv_out, dw_ref, dw_b_ref, ln_g_ref, ln_b_ref, pw_ref,
                                 tm).astype(BF16)


def _sample_mixers(upool, uconv, hist_pool, hist_conv, mixer_w, seq_len, pos_base, layer):
    n = upool.shape[0]
    cur = pl.BlockSpec((seq_len, POOL_WIDTH), lambda i: (i, 0))
    hist = pl.BlockSpec((None, None, HALO, POOL_WIDTH), lambda i: (layer, i, 0, 0))
    return pl.pallas_call(
        functools.partial(_sample_mixers_kernel, pos_base=pos_base),
        grid=(n // seq_len,),
        in_specs=[cur, hist, cur, hist] + _mixer_weight_specs(layer),
        out_specs=[cur, cur],
        out_shape=[jax.ShapeDtypeStruct((n, POOL_WIDTH), BF16), jax.ShapeDtypeStruct((n, CONV_WIDTH), BF16)],
        scratch_shapes=_mixer_scratch(seq_len),
        compiler_params=pltpu.CompilerParams(dimension_semantics=("arbitrary",),
                                             vmem_limit_bytes=VMEM_LIMIT),
        name="sample_mixers",
    )(upool, hist_pool, uconv, hist_conv, *mixer_w)


def _dense_math(x, ypool, o, yconv, wout_ref, g_ref, wgu_ref, wd_ref, fg_ref, final):
    o1 = POOL_WIDTH
    o2 = o1 + ATTN_WIDTH
    mix = (jnp.dot(ypool, wout_ref[0:o1, :], preferred_element_type=F32)
           + jnp.dot(o, wout_ref[o1:o2, :], preferred_element_type=F32)
           + jnp.dot(yconv, wout_ref[o2:, :], preferred_element_type=F32))
    x1 = x + mix
    h = _rms(x1, g_ref[...]).astype(BF16)
    gu = jnp.dot(h, wgu_ref[...], preferred_element_type=F32)
    gate = gu[:, :D_FF]
    a = (gate * jax.nn.sigmoid(gate) * gu[:, D_FF:]).astype(BF16)
    x2 = x1 + jnp.dot(a, wd_ref[...], preferred_element_type=F32)
    return _rms(x2, fg_ref[...]) if final else x2


def _dense_weight_specs(layer):
    shapes = [(MIX_WIDTH, D_MODEL), (1, D_MODEL), (D_MODEL, 2 * D_FF), (D_FF, D_MODEL)]
    return [_layer_spec(shape, layer) for shape in shapes] + [_const_spec((1, D_MODEL))]


def _prompt_dense_kernel(x_ref, o_ref, up0_ref, uc0_ref, up_ref, uc_ref, hist_p_ref, hist_c_ref,
                         pool_w_ref, pool_scale_ref, dw_ref, dw_b_ref, ln_g_ref, ln_b_ref, pw_ref,
                         wout_ref, g_ref, wgu_ref, wd_ref, fg_ref, out_ref,
                         ext_p, ext_c, shift_c, conv_out, y_pool, y_conv, *, final, tiles_per_seq):
    tm = x_ref.shape[0]
    i = pl.program_id(0)

    def mixer_steps(up_ref_, uc_ref_, tile, slot, first):
        def fill():
            if first is True:
                ext_p[0:HALO, :] = hist_p_ref[...]
                ext_c[0:HALO, :] = hist_c_ref[...]
            else:
                ext_p[0:HALO, :] = jnp.where(first, hist_p_ref[...], ext_p[tm:, :])
                ext_c[0:HALO, :] = jnp.where(first, hist_c_ref[...], ext_c[tm:, :])
            ext_p[HALO:, :] = up_ref_[...]
            ext_c[HALO:, :] = uc_ref_[...]

        def pool():
            pos0 = (tile % tiles_per_seq) * tm
            y_pool[slot] = _pool_mixer(ext_p, pool_w_ref, pool_scale_ref, tm, pos0).astype(BF16)

        def store_conv(y):
            y_conv[slot] = y.astype(BF16)

        return [fill] + _conv_mixer_steps(ext_c, shift_c, conv_out, dw_ref, dw_b_ref, ln_g_ref, ln_b_ref, pw_ref,
                                          tm, store_conv) + [pool]

    @pl.when(i == 0)
    def _():
        for step in mixer_steps(up0_ref, uc0_ref, 0, 0, True):
            step()

    slot = i % 2
    nxt = jnp.minimum(i + 1, pl.num_programs(0) - 1)
    out_ref[...] = _dense_math(x_ref[...], y_pool[slot], o_ref[...], y_conv[slot],
                               wout_ref, g_ref, wgu_ref, wd_ref, fg_ref, final)
    for step in mixer_steps(up_ref, uc_ref, nxt, 1 - slot, nxt % tiles_per_seq == 0):
        step()


def _prompt_dense(x, o, upool, uconv, hist_pool, hist_conv, mixer_w, dense_w, tm, seq_len, final, layer):
    n = x.shape[0]
    tps = seq_len // tm
    last = n // tm - 1
    row = lambda w: pl.BlockSpec((tm, w), lambda i: (i, 0))
    first_tile = lambda w: pl.BlockSpec((tm, w), lambda i: (0, 0))
    next_tile = lambda w: pl.BlockSpec((tm, w), lambda i: (jnp.minimum(i + 1, last), 0))
    next_seq = lambda w: pl.BlockSpec((None, HALO, w), lambda i: (jnp.minimum(i + 1, last) // tps, 0, 0))
    return pl.pallas_call(
        functools.partial(_prompt_dense_kernel, final=final, tiles_per_seq=tps),
        grid=(n // tm,),
        in_specs=[row(D_MODEL), row(ATTN_WIDTH), first_tile(POOL_WIDTH), first_tile(CONV_WIDTH),
                  next_tile(POOL_WIDTH), next_tile(CONV_WIDTH), next_seq(POOL_WIDTH), next_seq(CONV_WIDTH)]
        + _mixer_weight_specs(layer) + _dense_weight_specs(layer),
        out_specs=row(D_MODEL),
        out_shape=jax.ShapeDtypeStruct((n, D_MODEL), F32),
        scratch_shapes=_mixer_scratch(tm) + [pltpu.VMEM((2, tm, POOL_WIDTH), BF16),
                                             pltpu.VMEM((2, tm, CONV_WIDTH), BF16)],
        compiler_params=pltpu.CompilerParams(dimension_semantics=("arbitrary",),
                                             vmem_limit_bytes=VMEM_LIMIT),
        name="prompt_dense",
    )(x, o, upool, uconv, upool, uconv, hist_pool, hist_conv, *mixer_w, *dense_w)


def _sample_dense_kernel(x_ref, yp_ref, o_ref, yc_ref, wout_ref, g_ref, wgu_ref, wd_ref, fg_ref, out_ref, *, final):
    out_ref[...] = _dense_math(x_ref[...], yp_ref[...], o_ref[...], yc_ref[...],
                               wout_ref, g_ref, wgu_ref, wd_ref, fg_ref, final)


def _sample_dense(x, ypool, o, yconv, wout_bf, g, wgu_bf, wd_bf, fg, tm, final, layer):
    n = x.shape[0]
    row = lambda w: pl.BlockSpec((tm, w), lambda i: (i, 0))
    return pl.pallas_call(
        functools.partial(_sample_dense_kernel, final=final),
        grid=(n // tm,),
        in_specs=[row(D_MODEL), row(POOL_WIDTH), row(ATTN_WIDTH), row(CONV_WIDTH)] + _dense_weight_specs(layer),
        out_specs=row(D_MODEL),
        out_shape=jax.ShapeDtypeStruct((n, D_MODEL), F32),
        compiler_params=pltpu.CompilerParams(dimension_semantics=("arbitrary",),
                                             vmem_limit_bytes=VMEM_LIMIT),
        name="sample_dense",
    )(x, ypool, o, yconv, wout_bf, g, wgu_bf, wd_bf, fg)


def _rope_tables(pos):
    half = HEAD_DIM // 2
    inv = ROPE_THETA ** (-jnp.arange(half, dtype=F32) / half)
    ang = pos.astype(F32)[:, None] * inv[None, :]
    reps = LANES // half
    cos = jnp.tile(jnp.cos(ang), (1, reps))
    sign = jnp.tile(jnp.concatenate([-jnp.ones((half,), F32), jnp.ones((half,), F32)]), LANES // HEAD_DIM)
    sin = jnp.tile(jnp.sin(ang), (1, reps)) * sign[None, :]
    return cos, sin


def _block_diag(pool_w):
    depth, g, d, _ = pool_w.shape
    eye = jnp.eye(g, dtype=pool_w.dtype)
    return (pool_w[:, :, :, None, :] * eye[None, :, None, :, None]).reshape(depth, g * d, g * d)


def _pad_hist(h):
    return jnp.pad(h, ((0, 0), (0, 0), (HALO - h.shape[2], 0), (0, 0)))


def kernel(x_prompt, x_sample, cache_k, cache_v, state_pool, state_conv, norm_mix_g, w_in, pool_w, pool_scale,
           lambda_qk, diff_norm_g, conv_dw, conv_dw_b, conv_ln_g, conv_ln_b, conv_pw, w_out, norm_ffn_g,
           w_gate_up, w_down, final_norm_g):
    B, S, _ = x_prompt.shape
    Bs, Ls, _ = x_sample.shape
    depth = w_in.shape[0]
    past_len = cache_k.shape[2]
    assert S % (Q_TILES_PER_STEP * ATTN_TILE) == 0 and S % ROW_TILE == 0 and S % IN_PROJ_TILE == 0
    assert S // ROW_TILE >= 2
    assert (Bs * Ls) % 16 == 0 and Ls % 16 == 0 and Ls >= CONV_HIST

    cos_p, sin_p = _rope_tables(jnp.arange(S, dtype=jnp.int32))
    cos_s, sin_s = _rope_tables(past_len + jnp.arange(Ls, dtype=jnp.int32))
    cos_s, sin_s = jnp.tile(cos_s, (Bs, 1)), jnp.tile(sin_s, (Bs, 1))
    zero_pool = jnp.zeros((B, HALO, POOL_WIDTH), F32)
    zero_conv = jnp.zeros((B, HALO, CONV_WIDTH), F32)

    xp = x_prompt.reshape(B * S, D_MODEL)
    xs = x_sample.reshape(Bs * Ls, D_MODEL)
    rows = lambda a: a.reshape(depth, 1, -1)
    norm_mix_rows, norm_ffn_rows = rows(norm_mix_g), rows(norm_ffn_g)
    diff_g_row, diff_g_col = rows(diff_norm_g), diff_norm_g.reshape(depth, -1, 1)
    w_in_bf, w_out_bf, w_gu_bf, w_d_bf = (w.astype(BF16) for w in (w_in, w_out, w_gate_up, w_down))
    mixer_w = (_block_diag(pool_w).astype(BF16), rows(pool_scale), conv_dw, rows(conv_dw_b), rows(conv_ln_g),
               rows(conv_ln_b), conv_pw.astype(BF16))
    dense_w = (w_out_bf, norm_ffn_rows, w_gu_bf, w_d_bf, final_norm_g.reshape(1, -1))
    hist_pool_s, hist_conv_s = _pad_hist(state_pool), _pad_hist(state_conv)
    cache_kt = jnp.transpose(cache_k, (0, 1, 3, 4, 5, 2)).reshape(depth, Bs, ATTN_WIDTH, past_len)
    cache_v4 = cache_v.reshape(depth, Bs, past_len * N_HEADS, V_DIM)
    kv_stack = depth
    pp, cp = [], []
    ks_, vs_, ps_, cs_ = [], [], [], []
    for l in range(depth):
        lam_init = 0.8 - 0.6 * math.exp(-0.3 * l)
        final = l == depth - 1

        upool, q, k, kb, v, vb, uconv = _prompt_in_proj(xp, norm_mix_rows, w_in_bf, cos_p, sin_p, IN_PROJ_TILE, S,
                                                        kv_stack, l)
        kv_stack = (k, v)
        o = _prompt_attention(lambda_qk, diff_g_col, q.reshape(B, S, -1), kb.reshape(B, S, -1),
                              vb, lam_init, l)
        xp = _prompt_dense(xp, o.reshape(B * S, -1), upool, uconv, zero_pool, zero_conv, mixer_w, dense_w,
                           ROW_TILE, S, final, l)
        pp.append(upool.reshape(B, S, -1)[:, S - POOL_HIST:])
        cp.append(uconv.reshape(B, S, -1)[:, S - CONV_HIST:])

        upool, q, k, kb, v, vb, uconv = _sample_in_proj(xs, norm_mix_rows, w_in_bf, cos_s, sin_s, l)
        o = _sample_attention(lambda_qk, diff_g_row, q.reshape(Bs, Ls, -1), cache_kt, cache_v4,
                              kb.reshape(Bs, Ls, -1), vb.reshape(Bs, Ls, -1), lam_init, l)
        ypool, yconv = _sample_mixers(upool, uconv, hist_pool_s, hist_conv_s, mixer_w, Ls, past_len, l)
        xs = _sample_dense(xs, ypool, o.reshape(Bs * Ls, -1), yconv, *dense_w, Bs * Ls, final, l)
        ks_.append(k.reshape(Bs, Ls, N_HEADS, 2, HEAD_DIM))
        vs_.append(v.reshape(Bs, Ls, N_HEADS, V_DIM))
        ps_.append(upool.reshape(Bs, Ls, -1)[:, Ls - POOL_HIST:])
        cs_.append(uconv.reshape(Bs, Ls, -1)[:, Ls - CONV_HIST:])

    kt_all, v4_all = kv_stack
    new_k = jnp.transpose(kt_all.reshape(depth, B, N_HEADS, 2, HEAD_DIM, S), (0, 1, 5, 2, 3, 4))
    new_v = v4_all.reshape(depth, B, S, N_HEADS, V_DIM)
    return (xp.reshape(B, S, D_MODEL), xs.reshape(Bs, Ls, D_MODEL),
            new_k, new_v, jnp.stack(pp), jnp.stack(cp),
            jnp.stack(ks_), jnp.stack(vs_), jnp.stack(ps_), jnp.stack(cs_))
```

```python
import functools
import math

import jax
import jax.numpy as jnp
from jax import lax
from jax.experimental import pallas as pl
from jax.experimental.pallas import tpu as pltpu

D_MODEL = 1024
CHUNK = 64
POOL_WIDTH = D_MODEL // 4
POOL_WINDOWS = (2, 4, 8, 16)
POOL_GROUP_DIM = POOL_WIDTH // len(POOL_WINDOWS)
POOL_HIST = max(POOL_WINDOWS) - 1
ATTN_WIDTH = D_MODEL // 2
N_HEADS = 4
HEAD_DIM = ATTN_WIDTH // (2 * N_HEADS)
V_DIM = 2 * HEAD_DIM
ROPE_THETA = 10000.0
CONV_WIDTH = D_MODEL // 4
CONV_K = 31
CONV_HIST = CONV_K - 1
MIX_WIDTH = POOL_WIDTH + ATTN_WIDTH + CONV_WIDTH
IN_WIDTH = POOL_WIDTH + 3 * ATTN_WIDTH + 2 * CONV_WIDTH
D_FF = ((-(-8 * D_MODEL // 3) + 255) // 256) * 256
EPS = 1e-6

LANES = 128
HALO = 32
ROW_TILE = 512
IN_PROJ_TILE = 1024
ATTN_TILE = 256
SAMPLE_KEY_TILE = 1024
KEY_TILES_PER_BLOCK = 4
Q_TILES_PER_STEP = 8
HEADS_PER_STEP = 4
VMEM_LIMIT = 56 * 1024 * 1024

F32 = jnp.float32
BF16 = jnp.bfloat16


def _rms(x, g):
    return x * lax.rsqrt(jnp.mean(x * x, axis=-1, keepdims=True) + EPS) * g


def _const_spec(shape):
    return pl.BlockSpec(shape, lambda *_: (0,) * len(shape), pipeline_mode=pl.Buffered(1))


def _layer_spec(shape, layer):
    return pl.BlockSpec((None,) + shape, lambda *_: (layer,) + (0,) * len(shape), pipeline_mode=pl.Buffered(1))


O_Q = POOL_WIDTH
O_K = O_Q + ATTN_WIDTH
O_V = O_K + ATTN_WIDTH
O_A = O_V + ATTN_WIDTH
O_B = O_A + CONV_WIDTH


def _qkv_epilogue(z, cos_ref, sin_ref, q_ref, k_ref, kb_ref, v_ref, vb_ref, cache_layout):
    tm = z.shape[0]
    o1, o2, o3 = O_Q, O_K, O_V
    cos = cos_ref[...]
    sin = sin_ref[...]
    lane = lax.broadcasted_iota(jnp.int32, cos.shape, 1)
    first_half = (lane & (HEAD_DIM // 2)) == 0

    def rope(t):
        up = pltpu.roll(t, LANES - HEAD_DIM // 2, axis=1)
        down = pltpu.roll(t, HEAD_DIM // 2, axis=1)
        return t * cos + jnp.where(first_half, up, down) * sin

    scale = HEAD_DIM ** -0.5 * math.log2(math.e)
    for j in range(ATTN_WIDTH // LANES):
        sl = slice(j * LANES, (j + 1) * LANES)
        qj = rope(z[:, o1 + j * LANES:o1 + (j + 1) * LANES])
        kj = rope(z[:, o2 + j * LANES:o2 + (j + 1) * LANES])
        vj = z[:, o3 + j * LANES:o3 + (j + 1) * LANES]
        q_ref[:, sl] = (qj * scale).astype(BF16)
        kb_ref[:, sl] = kj.astype(BF16)
        if cache_layout:
            k_ref[sl, :] = kj.T
            v_ref[pl.ds(j, tm, stride=N_HEADS), :] = vj
            vb_ref[sl, :] = vj.T.astype(BF16)
        else:
            k_ref[:, sl] = kj
            v_ref[:, sl] = vj
            vb_ref[:, sl] = vj.astype(BF16)


def _sample_in_proj_kernel(x_ref, g_ref, w_ref, cos_ref, sin_ref,
                           upool_ref, q_ref, k_ref, kb_ref, v_ref, vb_ref, uconv_ref):
    h = _rms(x_ref[...], g_ref[...]).astype(BF16)
    z = jnp.dot(h, w_ref[...], preferred_element_type=F32)
    upool_ref[...] = z[:, :O_Q]
    uconv_ref[...] = z[:, O_A:O_B] * jax.nn.sigmoid(z[:, O_B:])
    _qkv_epilogue(z, cos_ref, sin_ref, q_ref, k_ref, kb_ref, v_ref, vb_ref, False)


def _sample_in_proj(x, g, w_bf, cos_tab, sin_tab, layer):
    n = x.shape[0]
    full = lambda w: pl.BlockSpec((n, w), lambda i: (0, 0))
    flat = lambda dt: jax.ShapeDtypeStruct((n, ATTN_WIDTH), dt)
    return pl.pallas_call(
        _sample_in_proj_kernel,
        grid=(1,),
        in_specs=[full(D_MODEL), _layer_spec((1, D_MODEL), layer), _layer_spec((D_MODEL, IN_WIDTH), layer),
                  full(LANES), full(LANES)],
        out_specs=[full(POOL_WIDTH)] + [full(ATTN_WIDTH)] * 5 + [full(CONV_WIDTH)],
        out_shape=[jax.ShapeDtypeStruct((n, POOL_WIDTH), F32), flat(BF16), flat(F32), flat(BF16), flat(F32),
                   flat(BF16), jax.ShapeDtypeStruct((n, CONV_WIDTH), F32)],
        compiler_params=pltpu.CompilerParams(dimension_semantics=("arbitrary",),
                                             vmem_limit_bytes=VMEM_LIMIT),
        name="sample_in_proj",
    )(x, g, w_bf, cos_tab, sin_tab)


def _prompt_in_proj_kernel(*refs, n_unused):
    x_ref, g_ref, w_ref, cos_ref, sin_ref = refs[:5]
    upool_ref, q_ref, k_ref, kb_ref, v_ref, vb_ref, uconv_ref = refs[5 + n_unused:]
    h = _rms(x_ref[...], g_ref[...]).astype(BF16)
    z = jnp.dot(h, w_ref[...], preferred_element_type=F32)
    upool_ref[...] = z[:, :O_Q]
    uconv_ref[...] = z[:, O_A:O_B] * jax.nn.sigmoid(z[:, O_B:])
    _qkv_epilogue(z, cos_ref, sin_ref, q_ref, k_ref, kb_ref, v_ref, vb_ref, True)


def _prompt_in_proj(x, g, w_bf, cos_tab, sin_tab, tm, seq_len, kv_stack, layer):
    n = x.shape[0]
    tps = seq_len // tm
    batch = n // seq_len
    row = lambda w: pl.BlockSpec((tm, w), lambda i: (i, 0))
    tab = pl.BlockSpec((tm, LANES), lambda i: (i % tps, 0))
    flat = lambda dt, w=ATTN_WIDTH: jax.ShapeDtypeStruct((n, w), dt)
    extra_in, extra_specs, aliases = [], [], {}
    if isinstance(kv_stack, int):
        k_shape = jax.ShapeDtypeStruct((kv_stack, batch, ATTN_WIDTH, seq_len), F32)
        v_shape = jax.ShapeDtypeStruct((kv_stack, batch, N_HEADS * seq_len, V_DIM), F32)
    else:
        k_shape, v_shape = (jax.ShapeDtypeStruct(a.shape, a.dtype) for a in kv_stack)
        extra_in = list(kv_stack)
        extra_specs = [pl.BlockSpec(memory_space=pl.ANY)] * 2
        aliases = {5: 2, 6: 4}
    k_spec = pl.BlockSpec((None, None, ATTN_WIDTH, tm), lambda i: (layer, i // tps, 0, i % tps))
    v_spec = pl.BlockSpec((None, None, N_HEADS * tm, V_DIM), lambda i: (layer, i // tps, i % tps, 0))
    vt_spec = pl.BlockSpec((None, ATTN_WIDTH, tm), lambda i: (i // tps, 0, i % tps))
    return pl.pallas_call(
        functools.partial(_prompt_in_proj_kernel, n_unused=len(extra_in)),
        grid=(n // tm,),
        in_specs=[row(D_MODEL), _layer_spec((1, D_MODEL), layer), _layer_spec((D_MODEL, IN_WIDTH), layer), tab, tab]
        + extra_specs,
        out_specs=[row(POOL_WIDTH), row(ATTN_WIDTH), k_spec, row(ATTN_WIDTH), v_spec, vt_spec, row(CONV_WIDTH)],
        out_shape=[flat(F32, POOL_WIDTH), flat(BF16), k_shape, flat(BF16), v_shape,
                   jax.ShapeDtypeStruct((batch, ATTN_WIDTH, seq_len), BF16), flat(F32, CONV_WIDTH)],
        input_output_aliases=aliases,
        compiler_params=pltpu.CompilerParams(dimension_semantics=("arbitrary",),
                                             vmem_limit_bytes=VMEM_LIMIT),
        name="prompt_in_proj",
    )(x, g, w_bf, cos_tab, sin_tab, *extra_in)


def _lambda_scalar(lq, lam_init):
    a = jnp.sum(lq[0:1, :] * lq[1:2, :], axis=-1, keepdims=True)
    b = jnp.sum(lq[2:3, :] * lq[3:4, :], axis=-1, keepdims=True)
    return jnp.exp(a) - jnp.exp(b) + lam_init


def _stack_maps(q):
    lane = lax.broadcasted_iota(jnp.int32, q.shape, 1)
    zero = jnp.zeros_like(q)
    return jnp.concatenate([jnp.where(lane < HEAD_DIM, q, zero), jnp.where(lane >= HEAD_DIM, q, zero)], axis=0)


ONES_ROWS = 16


def _flash_scores(qq, kt):
    return lax.dot_general(kt, qq, (((1,), (1,)), ((), ())), preferred_element_type=F32)


def _flash_update(carry, s, vt_ones, p_stage):
    m, acc = carry
    m_new = jnp.maximum(m, jnp.max(s, axis=0, keepdims=True))
    alpha = jnp.exp2(m - m_new)
    p_stage[...] = jnp.exp2(s - m_new).astype(BF16)
    return m_new, alpha * acc + jnp.dot(vt_ones, p_stage[...], preferred_element_type=F32)


def _flash_init(cols):
    return jnp.full((1, cols), -jnp.inf, F32), jnp.zeros((V_DIM + ONES_ROWS, cols), F32)


def _attn_finish(carry, lam, g_col, lam_init, tq):
    _, acc = carry
    scaled = acc[:V_DIM] * (1.0 / acc[V_DIM:V_DIM + 1])
    o = scaled[:, :tq] - lam * scaled[:, tq:]
    o = o * lax.rsqrt(jnp.mean(o * o, axis=0, keepdims=True) + EPS) * g_col
    return (o * (1.0 - lam_init)).T


def _mask_diagonal_tile(s):
    t = s.shape[0]
    q_chunk = (lax.broadcasted_iota(jnp.int32, (1, s.shape[1]), 1) % t) // CHUNK
    blocks = [jnp.where(q_chunk >= a, s[a * CHUNK:(a + 1) * CHUNK, :], -1e30) for a in range(t // CHUNK)]
    return jnp.concatenate(blocks, axis=0)


def _prompt_attn_kernel(lq_ref, g_ref, q_ref, k_ref, vt_ref, o_ref, qq_ref, p_ref, *, lam_init):
    t = ATTN_TILE
    heads = [slice(h * LANES, (h + 1) * LANES) for h in range(HEADS_PER_STEP)]
    ones = jnp.ones((ONES_ROWS, t), BF16)
    lam = _lambda_scalar(lq_ref[...], lam_init)
    q_tiles = q_ref.shape[0] // t

    def one_query_tile(sub, _):
        qi = pl.program_id(2) * q_tiles + sub
        q_rows = pl.ds(pl.multiple_of(sub * t, t), t)
        for h, c in enumerate(heads):
            qq_ref[h] = _stack_maps(q_ref[q_rows, c])
        stage0 = jnp.minimum(qi, 0)

        def steps(tiles, carry):
            rows = [pl.ds(pl.multiple_of(j * t, t), t) for j, _ in tiles]
            s = [[_flash_scores(qq_ref[h], k_ref[r, c]) for h, c in enumerate(heads)] for r in rows]
            for ti, (_, diagonal) in enumerate(tiles):
                if diagonal:
                    s[ti] = [_mask_diagonal_tile(x) for x in s[ti]]
                carry = tuple(
                    _flash_update(carry[h], s[ti][h], jnp.concatenate([vt_ref[c, rows[ti]], ones], axis=0),
                                  p_ref.at[stage0 + ti * HEADS_PER_STEP + h])
                    for h, c in enumerate(heads))
            return carry

        nb = KEY_TILES_PER_BLOCK
        carry = lax.fori_loop(0, qi // nb, lambda p, c: steps([(nb * p + i, False) for i in range(nb)], c),
                              (_flash_init(2 * t),) * HEADS_PER_STEP)
        left = qi % nb

        def last_block(n_full):
            return lambda c: steps([(qi - n_full + i, False) for i in range(n_full)] + [(qi, True)], c)

        carry = lax.switch(left, [last_block(n) for n in range(nb)], carry)
        for h, c in enumerate(heads):
            o_ref[q_rows, c] = _attn_finish(carry[h], lam, g_ref[...], lam_init, t).astype(BF16)
        return 0

    lax.fori_loop(0, q_tiles, one_query_tile, 0)


def _prompt_attention(lq, g, q, kb, vtb, lam_init, layer):
    b, s, _ = q.shape
    t = ATTN_TILE
    w = HEADS_PER_STEP * LANES
    tq = Q_TILES_PER_STEP * t
    return pl.pallas_call(
        functools.partial(_prompt_attn_kernel, lam_init=lam_init),
        grid=(b, N_HEADS // HEADS_PER_STEP, s // tq),
        in_specs=[
            _layer_spec((4, HEAD_DIM), layer),
            _layer_spec((V_DIM, 1), layer),
            pl.BlockSpec((None, tq, w), lambda b_, h, i: (b_, i, h)),
            pl.BlockSpec((None, s, w), lambda b_, h, i: (b_, 0, h)),
            pl.BlockSpec((None, w, s), lambda b_, h, i: (b_, h, 0)),
        ],
        out_specs=pl.BlockSpec((None, tq, w), lambda b_, h, i: (b_, i, h)),
        out_shape=jax.ShapeDtypeStruct((b, s, ATTN_WIDTH), BF16),
        scratch_shapes=[pltpu.VMEM((HEADS_PER_STEP, 2 * t, LANES), BF16),
                        pltpu.VMEM((KEY_TILES_PER_BLOCK * HEADS_PER_STEP, t, 2 * t), BF16)],
        compiler_params=pltpu.CompilerParams(dimension_semantics=("arbitrary",) * 3,
                                             vmem_limit_bytes=VMEM_LIMIT),
        name="prompt_attention",
    )(lq, g, q, kb, vtb)


def _row_scores(qq, kt, *, keys_on_rows):
    dims = (((1,), (1,)), ((), ())) if keys_on_rows else (((1,), (0,)), ((), ()))
    return lax.dot_general(qq, kt, dims, preferred_element_type=F32)


def _row_flash_update(carry, s, vt):
    m, l, acc = carry
    m_new = jnp.maximum(m, jnp.max(s, axis=-1, keepdims=True))
    alpha = jnp.exp2(m - m_new)
    p = jnp.exp2(s - m_new)
    l = alpha * l + jnp.sum(p, axis=-1, keepdims=True)
    acc = alpha * acc + jnp.dot(p.astype(BF16), vt, preferred_element_type=F32)
    return m_new, l, acc


def _sample_attn_kernel(lq_ref, g_ref, q_ref, ck_ref, cv_ref, k_ref, v_ref, o_ref, *, lam_init, past_len):
    t = SAMPLE_KEY_TILE
    tq = q_ref.shape[0]
    heads = [slice(h * LANES, (h + 1) * LANES) for h in range(N_HEADS)]
    qq = [_stack_maps(q_ref[:, c]) for c in heads]

    def body(j, carry):
        cols = pl.ds(pl.multiple_of(j * t, t), t)
        s = [_row_scores(qq[h], ck_ref[c, cols].astype(BF16), keys_on_rows=False) for h, c in enumerate(heads)]
        return tuple(
            _row_flash_update(carry[h], s[h], cv_ref[pl.ds(j * (t * N_HEADS) + h, t, stride=N_HEADS), :].astype(BF16))
            for h in range(N_HEADS))

    init = (jnp.full((2 * tq, 1), -jnp.inf, F32), jnp.zeros((2 * tq, 1), F32), jnp.zeros((2 * tq, V_DIM), F32))
    carry = lax.fori_loop(0, past_len // t, body, (init,) * N_HEADS)
    lam = _lambda_scalar(lq_ref[...], lam_init)
    for h, c in enumerate(heads):
        s = _row_scores(qq[h], k_ref[:, c], keys_on_rows=True)
        _, l, acc = _row_flash_update(carry[h], s, v_ref[:, c])
        o = acc[:tq] / l[:tq] - lam * (acc[tq:] / l[tq:])
        o_ref[:, c] = (_rms(o, g_ref[...]) * (1.0 - lam_init)).astype(BF16)


def _sample_attention(lq, g, q, cache_kt, cache_v4, kb, vb, lam_init, layer):
    b, ls, _ = q.shape
    past_len = cache_kt.shape[-1]
    assert past_len % SAMPLE_KEY_TILE == 0
    return pl.pallas_call(
        functools.partial(_sample_attn_kernel, lam_init=lam_init, past_len=past_len),
        grid=(b,),
        in_specs=[
            _layer_spec((4, HEAD_DIM), layer),
            _layer_spec((1, V_DIM), layer),
            pl.BlockSpec((None, ls, ATTN_WIDTH), lambda b_: (b_, 0, 0)),
            pl.BlockSpec((None, None, ATTN_WIDTH, past_len), lambda b_: (layer, b_, 0, 0)),
            pl.BlockSpec((None, None, N_HEADS * past_len, V_DIM), lambda b_: (layer, b_, 0, 0)),
            pl.BlockSpec((None, ls, ATTN_WIDTH), lambda b_: (b_, 0, 0)),
            pl.BlockSpec((None, ls, ATTN_WIDTH), lambda b_: (b_, 0, 0)),
        ],
        out_specs=pl.BlockSpec((None, ls, ATTN_WIDTH), lambda b_: (b_, 0, 0)),
        out_shape=jax.ShapeDtypeStruct((b, ls, ATTN_WIDTH), BF16),
        compiler_params=pltpu.CompilerParams(dimension_semantics=("arbitrary",),
                                             vmem_limit_bytes=VMEM_LIMIT),
        name="sample_attention",
    )(lq, g, q, cache_kt, cache_v4, kb, vb)


def _pool_mixer(ext_p, pool_w_ref, pool_scale_ref, tm, pos0):
    assert POOL_WINDOWS == (2, 4, 8, 16) and HALO == 32
    e = ext_p[...]
    n = HALO + tm
    s2 = e[8:] + e[7:n - 1]
    s4 = s2[8:] + s2[6:n - 10]
    s8 = s4[8:] + s4[4:n - 20]
    sums = {2: s2[24:], 4: s4[16:], 8: s8[8:], 16: s8[8:] + s8[:n - 32]}
    u = e[HALO:]
    lane = lax.broadcasted_iota(jnp.int32, (tm, POOL_WIDTH), 1)
    group = lane // POOL_GROUP_DIM
    win = sums[POOL_WINDOWS[-1]]
    width = jnp.full((tm, POOL_WIDTH), POOL_WINDOWS[-1], jnp.int32)
    for gi in range(len(POOL_WINDOWS) - 2, -1, -1):
        win = jnp.where(group == gi, sums[POOL_WINDOWS[gi]], win)
        width = jnp.where(group == gi, POOL_WINDOWS[gi], width)
    pos = pos0 + lax.broadcasted_iota(jnp.int32, (tm, POOL_WIDTH), 0)
    count = jnp.minimum(pos + 1, width).astype(F32)
    d = (win / count - u).astype(BF16)
    return jnp.dot(d, pool_w_ref[...], preferred_element_type=F32) * pool_scale_ref[...]


CONV_ROW_BLOCK = 64


def _conv_mixer_steps(ext_c, shift_c, conv_out, dw_ref, dw_b_ref, ln_g_ref, ln_b_ref, pw_ref, tm, store):
    first_row = HALO - CONV_HIST
    blk = min(tm, CONV_ROW_BLOCK)

    def shift(residues):
        for r in residues:
            shift_c[r, 0:HALO - 8 + tm, :] = ext_c[pl.ds(r, HALO - 8 + tm), :]

    def taps(rb):
        yb = jnp.zeros((blk, CONV_WIDTH), F32)
        for k in range(CONV_K):
            r, a = (first_row + k) % 8, (first_row + k) // 8
            rows = pl.ds(8 * a + rb * blk, blk)
            src = ext_c[rows, :] if r == 0 else shift_c[r, rows, :]
            yb = yb + src * dw_ref[k:k + 1, :]
        conv_out[rb * blk:(rb + 1) * blk, :] = yb

    def finish():
        y = conv_out[...] + dw_b_ref[...]
        mu = jnp.mean(y, axis=-1, keepdims=True)
        yc = y - mu
        yn = yc * lax.rsqrt(jnp.mean(yc * yc, axis=-1, keepdims=True) + EPS) * ln_g_ref[...] + ln_b_ref[...]
        act = (yn * jax.nn.sigmoid(yn)).astype(BF16)
        store(jnp.dot(act, pw_ref[...], preferred_element_type=F32))

    steps = [functools.partial(shift, rs) for rs in ((1, 2), (3, 4, 5), (6, 7))]
    steps += [functools.partial(taps, rb) for rb in range(tm // blk)]
    return steps + [finish]


def _conv_mixer(ext_c, shift_c, conv_out, dw_ref, dw_b_ref, ln_g_ref, ln_b_ref, pw_ref, tm):
    out = []
    for step in _conv_mixer_steps(ext_c, shift_c, conv_out, dw_ref, dw_b_ref, ln_g_ref, ln_b_ref, pw_ref, tm,
                                  out.append):
        step()
    return out[0]


def _mixer_scratch(tm):
    return [pltpu.VMEM((HALO + tm, POOL_WIDTH), F32), pltpu.VMEM((HALO + tm, CONV_WIDTH), F32),
            pltpu.VMEM((8, HALO + tm, CONV_WIDTH), F32), pltpu.VMEM((tm, CONV_WIDTH), F32)]


def _mixer_weight_specs(layer):
    shapes = [(POOL_WIDTH, POOL_WIDTH), (1, POOL_WIDTH), (CONV_K, CONV_WIDTH), (1, CONV_WIDTH), (1, CONV_WIDTH),
              (1, CONV_WIDTH), (CONV_WIDTH, CONV_WIDTH)]
    return [_layer_spec(shape, layer) for shape in shapes]


def _sample_mixers_kernel(up_ref, up_hist_ref, uc_ref, uc_hist_ref,
                          pool_w_ref, pool_scale_ref, dw_ref, dw_b_ref, ln_g_ref, ln_b_ref, pw_ref,
                          ypool_ref, yconv_ref, ext_p, ext_c, shift_c, conv_out, *, pos_base):
    tm = up_ref.shape[0]
    ext_p[0:HALO, :] = up_hist_ref[...]
    ext_p[HALO:, :] = up_ref[...]
    ext_c[0:HALO, :] = uc_hist_ref[...]
    ext_c[HALO:, :] = uc_ref[...]
    ypool_ref[...] = _pool_mixer(ext_p, pool_w_ref, pool_scale_ref, tm, pos_base).astype(BF16)
    yconv_ref[...] = _conv_mixer(ext_c, shift_c, conv_out, dw_ref, dw_b_ref, ln_g_ref, ln_b_ref, pw_ref,
                                 tm).astype(BF16)


def _sample_mixers(upool, uconv, hist_pool, hist_conv, mixer_w, seq_len, pos_base, layer):
    n = upool.shape[0]
    cur = pl.BlockSpec((seq_len, POOL_WIDTH), lambda i: (i, 0))
    hist = pl.BlockSpec((None, None, HALO, POOL_WIDTH), lambda i: (layer, i, 0, 0))
    return pl.pallas_call(
        functools.partial(_sample_mixers_kernel, pos_base=pos_base),
        grid=(n // seq_len,),
        in_specs=[cur, hist, cur, hist] + _mixer_weight_specs(layer),
        out_specs=[cur, cur],
        out_shape=[jax.ShapeDtypeStruct((n, POOL_WIDTH), BF16), jax.ShapeDtypeStruct((n, CONV_WIDTH), BF16)],
        scratch_shapes=_mixer_scratch(seq_len),
        compiler_params=pltpu.CompilerParams(dimension_semantics=("arbitrary",),
                                             vmem_limit_bytes=VMEM_LIMIT),
        name="sample_mixers",
    )(upool, hist_pool, uconv, hist_conv, *mixer_w)


def _dense_math(x, ypool, o, yconv, wout_ref, g_ref, wgu_ref, wd_ref, fg_ref, final):
    o1 = POOL_WIDTH
    o2 = o1 + ATTN_WIDTH
    mix = (jnp.dot(ypool, wout_ref[0:o1, :], preferred_element_type=F32)
           + jnp.dot(o, wout_ref[o1:o2, :], preferred_element_type=F32)
           + jnp.dot(yconv, wout_ref[o2:, :], preferred_element_type=F32))
    x1 = x + mix
    h = _rms(x1, g_ref[...]).astype(BF16)
    gu = jnp.dot(h, wgu_ref[...], preferred_element_type=F32)
    gate = gu[:, :D_FF]
    a = (gate * jax.nn.sigmoid(gate) * gu[:, D_FF:]).astype(BF16)
    x2 = x1 + jnp.dot(a, wd_ref[...], preferred_element_type=F32)
    return _rms(x2, fg_ref[...]) if final else x2


def _dense_weight_specs(layer):
    shapes = [(MIX_WIDTH, D_MODEL), (1, D_MODEL), (D_MODEL, 2 * D_FF), (D_FF, D_MODEL)]
    return [_layer_spec(shape, layer) for shape in shapes] + [_const_spec((1, D_MODEL))]


def _prompt_dense_kernel(x_ref, o_ref, up0_ref, uc0_ref, up_ref, uc_ref, hist_p_ref, hist_c_ref,
                         pool_w_ref, pool_scale_ref, dw_ref, dw_b_ref, ln_g_ref, ln_b_ref, pw_ref,
                         wout_ref, g_ref, wgu_ref, wd_ref, fg_ref, out_ref,
                         ext_p, ext_c, shift_c, conv_out, y_pool, y_conv, *, final, tiles_per_seq):
    tm = x_ref.shape[0]
    i = pl.program_id(0)

    def mixer_steps(up_ref_, uc_ref_, tile, slot, first):
        def fill():
            if first is True:
                ext_p[0:HALO, :] = hist_p_ref[...]
                ext_c[0:HALO, :] = hist_c_ref[...]
            else:
                ext_p[0:HALO, :] = jnp.where(first, hist_p_ref[...], ext_p[tm:, :])
                ext_c[0:HALO, :] = jnp.where(first, hist_c_ref[...], ext_c[tm:, :])
            ext_p[HALO:, :] = up_ref_[...]
            ext_c[HALO:, :] = uc_ref_[...]

        def pool():
            pos0 = (tile % tiles_per_seq) * tm
            y_pool[slot] = _pool_mixer(ext_p, pool_w_ref, pool_scale_ref, tm, pos0).astype(BF16)

        def store_conv(y):
            y_conv[slot] = y.astype(BF16)

        return [fill] + _conv_mixer_steps(ext_c, shift_c, conv_out, dw_ref, dw_b_ref, ln_g_ref, ln_b_ref, pw_ref,
                                          tm, store_conv) + [pool]

    @pl.when(i == 0)
    def _():
        for step in mixer_steps(up0_ref, uc0_ref, 0, 0, True):
            step()

    slot = i % 2
    nxt = jnp.minimum(i + 1, pl.num_programs(0) - 1)
    out_ref[...] = _dense_math(x_ref[...], y_pool[slot], o_ref[...], y_conv[slot],
                               wout_ref, g_ref, wgu_ref, wd_ref, fg_ref, final)
    for step in mixer_steps(up_ref, uc_ref, nxt, 1 - slot, nxt % tiles_per_seq == 0):
        step()


def _prompt_dense(x, o, upool, uconv, hist_pool, hist_conv, mixer_w, dense_w, tm, seq_len, final, layer):
    n = x.shape[0]
    tps = seq_len // tm
    last = n // tm - 1
    row = lambda w: pl.BlockSpec((tm, w), lambda i: (i, 0))
    first_tile = lambda w: pl.BlockSpec((tm, w), lambda i: (0, 0))
    next_tile = lambda w: pl.BlockSpec((tm, w), lambda i: (jnp.minimum(i + 1, last), 0))
    next_seq = lambda w: pl.BlockSpec((None, HALO, w), lambda i: (jnp.minimum(i + 1, last) // tps, 0, 0))
    return pl.pallas_call(
        functools.partial(_prompt_dense_kernel, final=final, tiles_per_seq=tps),
        grid=(n // tm,),
        in_specs=[row(D_MODEL), row(ATTN_WIDTH), first_tile(POOL_WIDTH), first_tile(CONV_WIDTH),
                  next_tile(POOL_WIDTH), next_tile(CONV_WIDTH), next_seq(POOL_WIDTH), next_seq(CONV_WIDTH)]
        + _mixer_weight_specs(layer) + _dense_weight_specs(layer),
        out_specs=row(D_MODEL),
        out_shape=jax.ShapeDtypeStruct((n, D_MODEL), F32),
        scratch_shapes=_mixer_scratch(tm) + [pltpu.VMEM((2, tm, POOL_WIDTH), BF16),
                                             pltpu.VMEM((2, tm, CONV_WIDTH), BF16)],
        compiler_params=pltpu.CompilerParams(dimension_semantics=("arbitrary",),
                                             vmem_limit_bytes=VMEM_LIMIT),
        name="prompt_dense",
    )(x, o, upool, uconv, upool, uconv, hist_pool, hist_conv, *mixer_w, *dense_w)


def _sample_dense_kernel(x_ref, yp_ref, o_ref, yc_ref, wout_ref, g_ref, wgu_ref, wd_ref, fg_ref, out_ref, *, final):
    out_ref[...] = _dense_math(x_ref[...], yp_ref[...], o_ref[...], yc_ref[...],
                               wout_ref, g_ref, wgu_ref, wd_ref, fg_ref, final)


def _sample_dense(x, ypool, o, yconv, wout_bf, g, wgu_bf, wd_bf, fg, tm, final, layer):
    n = x.shape[0]
    row = lambda w: pl.BlockSpec((tm, w), lambda i: (i, 0))
    return pl.pallas_call(
        functools.partial(_sample_dense_kernel, final=final),
        grid=(n // tm,),
        in_specs=[row(D_MODEL), row(POOL_WIDTH), row(ATTN_WIDTH), row(CONV_WIDTH)] + _dense_weight_specs(layer),
        out_specs=row(D_MODEL),
        out_shape=jax.ShapeDtypeStruct((n, D_MODEL), F32),
        compiler_params=pltpu.CompilerParams(dimension_semantics=("arbitrary",),
                                             vmem_limit_bytes=VMEM_LIMIT),
        name="sample_dense",
    )(x, ypool, o, yconv, wout_bf, g, wgu_bf, wd_bf, fg)


def _rope_tables(pos):
    half = HEAD_DIM // 2
    inv = ROPE_THETA ** (-jnp.arange(half, dtype=F32) / half)
    ang = pos.astype(F32)[:, None] * inv[None, :]
    reps = LANES // half
    cos = jnp.tile(jnp.cos(ang), (1, reps))
    sign = jnp.tile(jnp.concatenate([-jnp.ones((half,), F32), jnp.ones((half,), F32)]), LANES // HEAD_DIM)
    sin = jnp.tile(jnp.sin(ang), (1, reps)) * sign[None, :]
    return cos, sin


def _block_diag(pool_w):
    depth, g, d, _ = pool_w.shape
    eye = jnp.eye(g, dtype=pool_w.dtype)
    return (pool_w[:, :, :, None, :] * eye[None, :, None, :, None]).reshape(depth, g * d, g * d)


def _pad_hist(h):
    return jnp.pad(h, ((0, 0), (0, 0), (HALO - h.shape[2], 0), (0, 0)))


def kernel(x_prompt, x_sample, cache_k, cache_v, state_pool, state_conv, norm_mix_g, w_in, pool_w, pool_scale,
           lambda_qk, diff_norm_g, conv_dw, conv_dw_b, conv_ln_g, conv_ln_b, conv_pw, w_out, norm_ffn_g,
           w_gate_up, w_down, final_norm_g):
    B, S, _ = x_prompt.shape
    Bs, Ls, _ = x_sample.shape
    depth = w_in.shape[0]
    past_len = cache_k.shape[2]
    assert S % (Q_TILES_PER_STEP * ATTN_TILE) == 0 and S % ROW_TILE == 0 and S % IN_PROJ_TILE == 0
    assert S // ROW_TILE >= 2
    assert (Bs * Ls) % 16 == 0 and Ls % 16 == 0 and Ls >= CONV_HIST

    cos_p, sin_p = _rope_tables(jnp.arange(S, dtype=jnp.int32))
    cos_s, sin_s = _rope_tables(past_len + jnp.arange(Ls, dtype=jnp.int32))
    cos_s, sin_s = jnp.tile(cos_s, (Bs, 1)), jnp.tile(sin_s, (Bs, 1))
    zero_pool = jnp.zeros((B, HALO, POOL_WIDTH), F32)
    zero_conv = jnp.zeros((B, HALO, CONV_WIDTH), F32)

    xp = x_prompt.reshape(B * S, D_MODEL)
    xs = x_sample.reshape(Bs * Ls, D_MODEL)
    rows = lambda a: a.reshape(depth, 1, -1)
    norm_mix_rows, norm_ffn_rows = rows(norm_mix_g), rows(norm_ffn_g)
    diff_g_row, diff_g_col = rows(diff_norm_g), diff_norm_g.reshape(depth, -1, 1)
    w_in_bf, w_out_bf, w_gu_bf, w_d_bf = (w.astype(BF16) for w in (w_in, w_out, w_gate_up, w_down))
    mixer_w = (_block_diag(pool_w).astype(BF16), rows(pool_scale), conv_dw, rows(conv_dw_b), rows(conv_ln_g),
               rows(conv_ln_b), conv_pw.astype(BF16))
    dense_w = (w_out_bf, norm_ffn_rows, w_gu_bf, w_d_bf, final_norm_g.reshape(1, -1))
    hist_pool_s, hist_conv_s = _pad_hist(state_pool), _pad_hist(state_conv)
    cache_kt = jnp.transpose(cache_k, (0, 1, 3, 4, 5, 2)).reshape(depth, Bs, ATTN_WIDTH, past_len)
    cache_v4 = cache_v.reshape(depth, Bs, past_len * N_HEADS, V_DIM)
    kv_stack = depth
    pp, cp = [], []
    ks_, vs_, ps_, cs_ = [], [], [], []
    for l in range(depth):
        lam_init = 0.8 - 0.6 * math.exp(-0.3 * l)
        final = l == depth - 1

        upool, q, k, kb, v, vb, uconv = _prompt_in_proj(xp, norm_mix_rows, w_in_bf, cos_p, sin_p, IN_PROJ_TILE, S,
                                                        kv_stack, l)
        kv_stack = (k, v)
        o = _prompt_attention(lambda_qk, diff_g_col, q.reshape(B, S, -1), kb.reshape(B, S, -1),
                              vb, lam_init, l)
        xp = _prompt_dense(xp, o.reshape(B * S, -1), upool, uconv, zero_pool, zero_conv, mixer_w, dense_w,
                           ROW_TILE, S, final, l)
        pp.append(upool.reshape(B, S, -1)[:, S - POOL_HIST:])
        cp.append(uconv.reshape(B, S, -1)[:, S - CONV_HIST:])

        upool, q, k, kb, v, vb, uconv = _sample_in_proj(xs, norm_mix_rows, w_in_bf, cos_s, sin_s, l)
        o = _sample_attention(lambda_qk, diff_g_row, q.reshape(Bs, Ls, -1), cache_kt, cache_v4,
                              kb.reshape(Bs, Ls, -1), vb.reshape(Bs, Ls, -1), lam_init, l)
        ypool, yconv = _sample_mixers(upool, uconv, hist_pool_s, hist_conv_s, mixer_w, Ls, past_len, l)
        xs = _sample_dense(xs, ypool, o.reshape(Bs * Ls, -1), yconv, *dense_w, Bs * Ls, final, l)
        ks_.append(k.reshape(Bs, Ls, N_HEADS, 2, HEAD_DIM))
        vs_.append(v.reshape(Bs, Ls, N_HEADS, V_DIM))
        ps_.append(upool.reshape(Bs, Ls, -1)[:, Ls - POOL_HIST:])
        cs_.append(uconv.reshape(Bs, Ls, -1)[:, Ls - CONV_HIST:])

    kt_all, v4_all = kv_stack
    new_k = jnp.transpose(kt_all.reshape(depth, B, N_HEADS, 2, HEAD_DIM, S), (0, 1, 5, 2, 3, 4))
    new_v = v4_all.reshape(depth, B, S, N_HEADS, V_DIM)
    return (xp.reshape(B, S, D_MODEL), xs.reshape(Bs, Ls, D_MODEL),
            new_k, new_v, jnp.stack(pp), jnp.stack(cp),
            jnp.stack(ks_), jnp.stack(vs_), jnp.stack(ps_), jnp.stack(cs_))
```

```python
import functools
import math

import jax
import jax.numpy as jnp
from jax import lax
from jax.experimental import pallas as pl
from jax.experimental.pallas import tpu as pltpu

D_MODEL = 1024
CHUNK = 64
POOL_WIDTH = D_MODEL // 4
POOL_WINDOWS = (2, 4, 8, 16)
POOL_GROUP_DIM = POOL_WIDTH // len(POOL_WINDOWS)
POOL_HIST = max(POOL_WINDOWS) - 1
ATTN_WIDTH = D_MODEL // 2
N_HEADS = 4
HEAD_DIM = ATTN_WIDTH // (2 * N_HEADS)
V_DIM = 2 * HEAD_DIM
ROPE_THETA = 10000.0
CONV_WIDTH = D_MODEL // 4
CONV_K = 31
CONV_HIST = CONV_K - 1
MIX_WIDTH = POOL_WIDTH + ATTN_WIDTH + CONV_WIDTH
IN_WIDTH = POOL_WIDTH + 3 * ATTN_WIDTH + 2 * CONV_WIDTH
D_FF = ((-(-8 * D_MODEL // 3) + 255) // 256) * 256
EPS = 1e-6

LANES = 128
HALO = 32
ROW_TILE = 512
IN_PROJ_TILE = 1024
ATTN_TILE = 256
SAMPLE_KEY_TILE = 1024
KEY_TILES_PER_BLOCK = 8
Q_TILES_PER_STEP = 8
HEADS_PER_STEP = 4
VMEM_LIMIT = 56 * 1024 * 1024

F32 = jnp.float32
BF16 = jnp.bfloat16


def _rms(x, g):
    return x * lax.rsqrt(jnp.mean(x * x, axis=-1, keepdims=True) + EPS) * g


def _const_spec(shape):
    return pl.BlockSpec(shape, lambda *_: (0,) * len(shape), pipeline_mode=pl.Buffered(1))


def _layer_spec(shape, layer):
    return pl.BlockSpec((None,) + shape, lambda *_: (layer,) + (0,) * len(shape), pipeline_mode=pl.Buffered(1))


O_Q = POOL_WIDTH
O_K = O_Q + ATTN_WIDTH
O_V = O_K + ATTN_WIDTH
O_A = O_V + ATTN_WIDTH
O_B = O_A + CONV_WIDTH


def _qkv_epilogue(z, cos_ref, sin_ref, q_ref, k_ref, kb_ref, v_ref, vb_ref, cache_layout):
    tm = z.shape[0]
    o1, o2, o3 = O_Q, O_K, O_V
    cos = cos_ref[...]
    sin = sin_ref[...]
    lane = lax.broadcasted_iota(jnp.int32, cos.shape, 1)
    first_half = (lane & (HEAD_DIM // 2)) == 0

    def rope(t):
        up = pltpu.roll(t, LANES - HEAD_DIM // 2, axis=1)
        down = pltpu.roll(t, HEAD_DIM // 2, axis=1)
        return t * cos + jnp.where(first_half, up, down) * sin

    scale = HEAD_DIM ** -0.5 * math.log2(math.e)
    for j in range(ATTN_WIDTH // LANES):
        sl = slice(j * LANES, (j + 1) * LANES)
        qj = rope(z[:, o1 + j * LANES:o1 + (j + 1) * LANES])
        kj = rope(z[:, o2 + j * LANES:o2 + (j + 1) * LANES])
        vj = z[:, o3 + j * LANES:o3 + (j + 1) * LANES]
        q_ref[:, sl] = (qj * scale).astype(BF16)
        kb_ref[:, sl] = kj.astype(BF16)
        if cache_layout:
            k_ref[sl, :] = kj.T
            v_ref[pl.ds(j, tm, stride=N_HEADS), :] = vj
            vb_ref[sl, :] = vj.T.astype(BF16)
        else:
            k_ref[:, sl] = kj
            v_ref[:, sl] = vj
            vb_ref[:, sl] = vj.astype(BF16)


def _sample_in_proj_kernel(x_ref, g_ref, w_ref, cos_ref, sin_ref,
                           upool_ref, q_ref, k_ref, kb_ref, v_ref, vb_ref, uconv_ref):
    h = _rms(x_ref[...], g_ref[...]).astype(BF16)
    z = jnp.dot(h, w_ref[...], preferred_element_type=F32)
    upool_ref[...] = z[:, :O_Q]
    uconv_ref[...] = z[:, O_A:O_B] * jax.nn.sigmoid(z[:, O_B:])
    _qkv_epilogue(z, cos_ref, sin_ref, q_ref, k_ref, kb_ref, v_ref, vb_ref, False)


def _sample_in_proj(x, g, w_bf, cos_tab, sin_tab, layer):
    n = x.shape[0]
    full = lambda w: pl.BlockSpec((n, w), lambda i: (0, 0))
    flat = lambda dt: jax.ShapeDtypeStruct((n, ATTN_WIDTH), dt)
    return pl.pallas_call(
        _sample_in_proj_kernel,
        grid=(1,),
        in_specs=[full(D_MODEL), _layer_spec((1, D_MODEL), layer), _layer_spec((D_MODEL, IN_WIDTH), layer),
                  full(LANES), full(LANES)],
        out_specs=[full(POOL_WIDTH)] + [full(ATTN_WIDTH)] * 5 + [full(CONV_WIDTH)],
        out_shape=[jax.ShapeDtypeStruct((n, POOL_WIDTH), F32), flat(BF16), flat(F32), flat(BF16), flat(F32),
                   flat(BF16), jax.ShapeDtypeStruct((n, CONV_WIDTH), F32)],
        compiler_params=pltpu.CompilerParams(dimension_semantics=("arbitrary",),
                                             vmem_limit_bytes=VMEM_LIMIT),
        name="sample_in_proj",
    )(x, g, w_bf, cos_tab, sin_tab)


def _prompt_in_proj_kernel(*refs, n_unused):
    x_ref, g_ref, w_ref, cos_ref, sin_ref = refs[:5]
    upool_ref, q_ref, k_ref, kb_ref, v_ref, vb_ref, uconv_ref = refs[5 + n_unused:]
    h = _rms(x_ref[...], g_ref[...]).astype(BF16)
    z = jnp.dot(h, w_ref[...], preferred_element_type=F32)
    upool_ref[...] = z[:, :O_Q]
    uconv_ref[...] = z[:, O_A:O_B] * jax.nn.sigmoid(z[:, O_B:])
    _qkv_epilogue(z, cos_ref, sin_ref, q_ref, k_ref, kb_ref, v_ref, vb_ref, True)


def _prompt_in_proj(x, g, w_bf, cos_tab, sin_tab, tm, seq_len, kv_stack, layer):
    n = x.shape[0]
    tps = seq_len // tm
    batch = n // seq_len
    row = lambda w: pl.BlockSpec((tm, w), lambda i: (i, 0))
    tab = pl.BlockSpec((tm, LANES), lambda i: (i % tps, 0))
    flat = lambda dt, w=ATTN_WIDTH: jax.ShapeDtypeStruct((n, w), dt)
    extra_in, extra_specs, aliases = [], [], {}
    if isinstance(kv_stack, int):
        k_shape = jax.ShapeDtypeStruct((kv_stack, batch, ATTN_WIDTH, seq_len), F32)
        v_shape = jax.ShapeDtypeStruct((kv_stack, batch, N_HEADS * seq_len, V_DIM), F32)
    else:
        k_shape, v_shape = (jax.ShapeDtypeStruct(a.shape, a.dtype) for a in kv_stack)
        extra_in = list(kv_stack)
        extra_specs = [pl.BlockSpec(memory_space=pl.ANY)] * 2
        aliases = {5: 2, 6: 4}
    k_spec = pl.BlockSpec((None, None, ATTN_WIDTH, tm), lambda i: (layer, i // tps, 0, i % tps))
    v_spec = pl.BlockSpec((None, None, N_HEADS * tm, V_DIM), lambda i: (layer, i // tps, i % tps, 0))
    vt_spec = pl.BlockSpec((None, ATTN_WIDTH, tm), lambda i: (i // tps, 0, i % tps))
    return pl.pallas_call(
        functools.partial(_prompt_in_proj_kernel, n_unused=len(extra_in)),
        grid=(n // tm,),
        in_specs=[row(D_MODEL), _layer_spec((1, D_MODEL), layer), _layer_spec((D_MODEL, IN_WIDTH), layer), tab, tab]
        + extra_specs,
        out_specs=[row(POOL_WIDTH), row(ATTN_WIDTH), k_spec, row(ATTN_WIDTH), v_spec, vt_spec, row(CONV_WIDTH)],
        out_shape=[flat(F32, POOL_WIDTH), flat(BF16), k_shape, flat(BF16), v_shape,
                   jax.ShapeDtypeStruct((batch, ATTN_WIDTH, seq_len), BF16), flat(F32, CONV_WIDTH)],
        input_output_aliases=aliases,
        compiler_params=pltpu.CompilerParams(dimension_semantics=("arbitrary",),
                                             vmem_limit_bytes=VMEM_LIMIT),
        name="prompt_in_proj",
    )(x, g, w_bf, cos_tab, sin_tab, *extra_in)


def _lambda_scalar(lq, lam_init):
    a = jnp.sum(lq[0:1, :] * lq[1:2, :], axis=-1, keepdims=True)
    b = jnp.sum(lq[2:3, :] * lq[3:4, :], axis=-1, keepdims=True)
    return jnp.exp(a) - jnp.exp(b) + lam_init


def _stack_maps(q):
    lane = lax.broadcasted_iota(jnp.int32, q.shape, 1)
    zero = jnp.zeros_like(q)
    return jnp.concatenate([jnp.where(lane < HEAD_DIM, q, zero), jnp.where(lane >= HEAD_DIM, q, zero)], axis=0)


ONES_ROWS = 16


def _flash_scores(qq, kt):
    return lax.dot_general(kt, qq, (((1,), (1,)), ((), ())), preferred_element_type=F32)


def _flash_update(carry, s, vt_ones, p_stage):
    m, acc = carry
    m_new = jnp.maximum(m, jnp.max(s, axis=0, keepdims=True))
    alpha = jnp.exp2(m - m_new)
    p_stage[...] = jnp.exp2(s - m_new).astype(BF16)
    return m_new, alpha * acc + jnp.dot(vt_ones, p_stage[...], preferred_element_type=F32)


def _flash_init(cols):
    return jnp.full((1, cols), -jnp.inf, F32), jnp.zeros((V_DIM + ONES_ROWS, cols), F32)


def _attn_finish(carry, lam, g_col, lam_init, tq):
    _, acc = carry
    scaled = acc[:V_DIM] * (1.0 / acc[V_DIM:V_DIM + 1])
    o = scaled[:, :tq] - lam * scaled[:, tq:]
    o = o * lax.rsqrt(jnp.mean(o * o, axis=0, keepdims=True) + EPS) * g_col
    return (o * (1.0 - lam_init)).T


def _mask_diagonal_tile(s):
    t = s.shape[0]
    q_chunk = (lax.broadcasted_iota(jnp.int32, (1, s.shape[1]), 1) % t) // CHUNK
    blocks = [jnp.where(q_chunk >= a, s[a * CHUNK:(a + 1) * CHUNK, :], -1e30) for a in range(t // CHUNK)]
    return jnp.concatenate(blocks, axis=0)


def _prompt_attn_kernel(lq_ref, g_ref, q_ref, k_ref, vt_ref, o_ref, qq_ref, p_ref, *, lam_init):
    t = ATTN_TILE
    heads = [slice(h * LANES, (h + 1) * LANES) for h in range(HEADS_PER_STEP)]
    ones = jnp.ones((ONES_ROWS, t), BF16)
    lam = _lambda_scalar(lq_ref[...], lam_init)
    q_tiles = q_ref.shape[0] // t

    def one_query_tile(sub, _):
        qi = pl.program_id(2) * q_tiles + sub
        q_rows = pl.ds(pl.multiple_of(sub * t, t), t)
        for h, c in enumerate(heads):
            qq_ref[h] = _stack_maps(q_ref[q_rows, c])
        stage0 = jnp.minimum(qi, 0)

        def steps(tiles, carry):
            rows = [pl.ds(pl.multiple_of(j * t, t), t) for j, _ in tiles]
            s = [[_flash_scores(qq_ref[h], k_ref[r, c]) for h, c in enumerate(heads)] for r in rows]
            for ti, (_, diagonal) in enumerate(tiles):
                if diagonal:
                    s[ti] = [_mask_diagonal_tile(x) for x in s[ti]]
                carry = tuple(
                    _flash_update(carry[h], s[ti][h], jnp.concatenate([vt_ref[c, rows[ti]], ones], axis=0),
                                  p_ref.at[stage0 + ti * HEADS_PER_STEP + h])
                    for h, c in enumerate(heads))
            return carry

        nb = min(KEY_TILES_PER_BLOCK, k_ref.shape[0] // t)
        carry = (_flash_init(2 * t),) * HEADS_PER_STEP
        if nb < k_ref.shape[0] // t:
            carry = lax.fori_loop(0, qi // nb, lambda p, c: steps([(nb * p + i, False) for i in range(nb)], c),
                                  carry)
        left = qi % nb

        def last_block(n_full):
            return lambda c: steps([(qi - n_full + i, False) for i in range(n_full)] + [(qi, True)], c)

        carry = lax.switch(left, [last_block(n) for n in range(nb)], carry)
        for h, c in enumerate(heads):
            o_ref[q_rows, c] = _attn_finish(carry[h], lam, g_ref[...], lam_init, t).astype(BF16)
        return 0

    lax.fori_loop(0, q_tiles, one_query_tile, 0)


def _prompt_attention(lq, g, q, kb, vtb, lam_init, layer):
    b, s, _ = q.shape
    t = ATTN_TILE
    w = HEADS_PER_STEP * LANES
    tq = Q_TILES_PER_STEP * t
    return pl.pallas_call(
        functools.partial(_prompt_attn_kernel, lam_init=lam_init),
        grid=(b, N_HEADS // HEADS_PER_STEP, s // tq),
        in_specs=[
            _layer_spec((4, HEAD_DIM), layer),
            _layer_spec((V_DIM, 1), layer),
            pl.BlockSpec((None, tq, w), lambda b_, h, i: (b_, i, h)),
            pl.BlockSpec((None, s, w), lambda b_, h, i: (b_, 0, h)),
            pl.BlockSpec((None, w, s), lambda b_, h, i: (b_, h, 0)),
        ],
        out_specs=pl.BlockSpec((None, tq, w), lambda b_, h, i: (b_, i, h)),
        out_shape=jax.ShapeDtypeStruct((b, s, ATTN_WIDTH), BF16),
        scratch_shapes=[pltpu.VMEM((HEADS_PER_STEP, 2 * t, LANES), BF16),
                        pltpu.VMEM((KEY_TILES_PER_BLOCK * HEADS_PER_STEP, t, 2 * t), BF16)],
        compiler_params=pltpu.CompilerParams(dimension_semantics=("arbitrary",) * 3,
                                             vmem_limit_bytes=VMEM_LIMIT),
        name="prompt_attention",
    )(lq, g, q, kb, vtb)


def _row_scores(qq, kt, *, keys_on_rows):
    dims = (((1,), (1,)), ((), ())) if keys_on_rows else (((1,), (0,)), ((), ()))
    return lax.dot_general(qq, kt, dims, preferred_element_type=F32)


def _row_flash_update(carry, s, vt):
    m, l, acc = carry
    m_new = jnp.maximum(m, jnp.max(s, axis=-1, keepdims=True))
    alpha = jnp.exp2(m - m_new)
    p = jnp.exp2(s - m_new)
    l = alpha * l + jnp.sum(p, axis=-1, keepdims=True)
    acc = alpha * acc + jnp.dot(p.astype(BF16), vt, preferred_element_type=F32)
    return m_new, l, acc


def _sample_attn_kernel(lq_ref, g_ref, q_ref, ck_ref, cv_ref, k_ref, v_ref, o_ref, *, lam_init, past_len):
    t = SAMPLE_KEY_TILE
    tq = q_ref.shape[0]
    heads = [slice(h * LANES, (h + 1) * LANES) for h in range(N_HEADS)]
    qq = [_stack_maps(q_ref[:, c]) for c in heads]

    def body(j, carry):
        cols = pl.ds(pl.multiple_of(j * t, t), t)
        s = [_row_scores(qq[h], ck_ref[c, cols].astype(BF16), keys_on_rows=False) for h, c in enumerate(heads)]
        return tuple(
            _row_flash_update(carry[h], s[h], cv_ref[pl.ds(j * (t * N_HEADS) + h, t, stride=N_HEADS), :].astype(BF16))
            for h in range(N_HEADS))

    init = (jnp.full((2 * tq, 1), -jnp.inf, F32), jnp.zeros((2 * tq, 1), F32), jnp.zeros((2 * tq, V_DIM), F32))
    carry = lax.fori_loop(0, past_len // t, body, (init,) * N_HEADS)
    lam = _lambda_scalar(lq_ref[...], lam_init)
    for h, c in enumerate(heads):
        s = _row_scores(qq[h], k_ref[:, c], keys_on_rows=True)
        _, l, acc = _row_flash_update(carry[h], s, v_ref[:, c])
        o = acc[:tq] / l[:tq] - lam * (acc[tq:] / l[tq:])
        o_ref[:, c] = (_rms(o, g_ref[...]) * (1.0 - lam_init)).astype(BF16)


def _sample_attention(lq, g, q, cache_kt, cache_v4, kb, vb, lam_init, layer):
    b, ls, _ = q.shape
    past_len = cache_kt.shape[-1]
    assert past_len % SAMPLE_KEY_TILE == 0
    return pl.pallas_call(
        functools.partial(_sample_attn_kernel, lam_init=lam_init, past_len=past_len),
        grid=(b,),
        in_specs=[
            _layer_spec((4, HEAD_DIM), layer),
            _layer_spec((1, V_DIM), layer),
            pl.BlockSpec((None, ls, ATTN_WIDTH), lambda b_: (b_, 0, 0)),
            pl.BlockSpec((None, None, ATTN_WIDTH, past_len), lambda b_: (layer, b_, 0, 0)),
            pl.BlockSpec((None, None, N_HEADS * past_len, V_DIM), lambda b_: (layer, b_, 0, 0)),
            pl.BlockSpec((None, ls, ATTN_WIDTH), lambda b_: (b_, 0, 0)),
            pl.BlockSpec((None, ls, ATTN_WIDTH), lambda b_: (b_, 0, 0)),
        ],
        out_specs=pl.BlockSpec((None, ls, ATTN_WIDTH), lambda b_: (b_, 0, 0)),
        out_shape=jax.ShapeDtypeStruct((b, ls, ATTN_WIDTH), BF16),
        compiler_params=pltpu.CompilerParams(dimension_semantics=("arbitrary",),
                                             vmem_limit_bytes=VMEM_LIMIT),
        name="sample_attention",
    )(lq, g, q, cache_kt, cache_v4, kb, vb)


def _pool_mixer(ext_p, pool_w_ref, pool_scale_ref, tm, pos0):
    assert POOL_WINDOWS == (2, 4, 8, 16) and HALO == 32
    e = ext_p[...]
    n = HALO + tm
    s2 = e[8:] + e[7:n - 1]
    s4 = s2[8:] + s2[6:n - 10]
    s8 = s4[8:] + s4[4:n - 20]
    sums = {2: s2[24:], 4: s4[16:], 8: s8[8:], 16: s8[8:] + s8[:n - 32]}
    u = e[HALO:]
    lane = lax.broadcasted_iota(jnp.int32, (tm, POOL_WIDTH), 1)
    group = lane // POOL_GROUP_DIM
    win = sums[POOL_WINDOWS[-1]]
    width = jnp.full((tm, POOL_WIDTH), POOL_WINDOWS[-1], jnp.int32)
    for gi in range(len(POOL_WINDOWS) - 2, -1, -1):
        win = jnp.where(group == gi, sums[POOL_WINDOWS[gi]], win)
        width = jnp.where(group == gi, POOL_WINDOWS[gi], width)
    pos = pos0 + lax.broadcasted_iota(jnp.int32, (tm, POOL_WIDTH), 0)
    count = jnp.minimum(pos + 1, width).astype(F32)
    d = (win / count - u).astype(BF16)
    return jnp.dot(d, pool_w_ref[...], preferred_element_type=F32) * pool_scale_ref[...]


CONV_ROW_BLOCK = 64


def _conv_mixer_steps(ext_c, shift_c, conv_out, dw_ref, dw_b_ref, ln_g_ref, ln_b_ref, pw_ref, tm, store):
    first_row = HALO - CONV_HIST
    blk = min(tm, CONV_ROW_BLOCK)

    def shift(residues):
        for r in residues:
            shift_c[r, 0:HALO - 8 + tm, :] = ext_c[pl.ds(r, HALO - 8 + tm), :]

    def taps(rb):
        yb = jnp.zeros((blk, CONV_WIDTH), F32)
        for k in range(CONV_K):
            r, a = (first_row + k) % 8, (first_row + k) // 8
            rows = pl.ds(8 * a + rb * blk, blk)
            src = ext_c[rows, :] if r == 0 else shift_c[r, rows, :]
            yb = yb + src * dw_ref[k:k + 1, :]
        conv_out[rb * blk:(rb + 1) * blk, :] = yb

    def finish():
        y = conv_out[...] + dw_b_ref[...]
        mu = jnp.mean(y, axis=-1, keepdims=True)
        yc = y - mu
        yn = yc * lax.rsqrt(jnp.mean(yc * yc, axis=-1, keepdims=True) + EPS) * ln_g_ref[...] + ln_b_ref[...]
        act = (yn * jax.nn.sigmoid(yn)).astype(BF16)
        store(jnp.dot(act, pw_ref[...], preferred_element_type=F32))

    steps = [functools.partial(shift, rs) for rs in ((1, 2), (3, 4, 5), (6, 7))]
    steps += [functools.partial(taps, rb) for rb in range(tm // blk)]
    return steps + [finish]


def _conv_mixer(ext_c, shift_c, conv_out, dw_ref, dw_b_ref, ln_g_ref, ln_b_ref, pw_ref, tm):
    out = []
    for step in _conv_mixer_steps(ext_c, shift_c, conv_out, dw_ref, dw_b_ref, ln_g_ref, ln_b_ref, pw_ref, tm,
                                  out.append):
        step()
    return out[0]


def _mixer_scratch(tm):
    return [pltpu.VMEM((HALO + tm, POOL_WIDTH), F32), pltpu.VMEM((HALO + tm, CONV_WIDTH), F32),
            pltpu.VMEM((8, HALO + tm, CONV_WIDTH), F32), pltpu.VMEM((tm, CONV_WIDTH), F32)]


def _mixer_weight_specs(layer):
    shapes = [(POOL_WIDTH, POOL_WIDTH), (1, POOL_WIDTH), (CONV_K, CONV_WIDTH), (1, CONV_WIDTH), (1, CONV_WIDTH),
              (1, CONV_WIDTH), (CONV_WIDTH, CONV_WIDTH)]
    return [_layer_spec(shape, layer) for shape in shapes]


def _sample_mixers_kernel(up_ref, up_hist_ref, uc_ref, uc_hist_ref,
                          pool_w_ref, pool_scale_ref, dw_ref, dw_b_ref, ln_g_ref, ln_b_ref, pw_ref,
                          ypool_ref, yconv_ref, ext_p, ext_c, shift_c, conv_out, *, pos_base):
    tm = up_ref.shape[0]
    ext_p[0:HALO, :] = up_hist_ref[...]
    ext_p[HALO:, :] = up_ref[...]
    ext_c[0:HALO, :] = uc_hist_ref[...]
    ext_c[HALO:, :] = uc_ref[...]
    ypool_ref[...] = _pool_mixer(ext_p, pool_w_ref, pool_scale_ref, tm, pos_base).astype(BF16)
    yconv_ref[...] = _conv_mixer(ext_c, shift_c, conv_out, dw_ref, dw_b_ref, ln_g_ref, ln_b_ref, pw_ref,
                                 tm).astype(BF16)


def _sample_mixers(upool, uconv, hist_pool, hist_conv, mixer_w, seq_len, pos_base, layer):
    n = upool.shape[0]
    cur = pl.BlockSpec((seq_len, POOL_WIDTH), lambda i: (i, 0))
    hist = pl.BlockSpec((None, None, HALO, POOL_WIDTH), lambda i: (layer, i, 0, 0))
    return pl.pallas_call(
        functools.partial(_sample_mixers_kernel, pos_base=pos_base),
        grid=(n // seq_len,),
        in_specs=[cur, hist, cur, hist] + _mixer_weight_specs(layer),
        out_specs=[cur, cur],
        out_shape=[jax.ShapeDtypeStruct((n, POOL_WIDTH), BF16), jax.ShapeDtypeStruct((n, CONV_WIDTH), BF16)],
        scratch_shapes=_mixer_scratch(seq_len),
        compiler_params=pltpu.CompilerParams(dimension_semantics=("arbitrary",),
                                             vmem_limit_bytes=VMEM_LIMIT),
        name="sample_mixers",
    )(upool, hist_pool, uconv, hist_conv, *mixer_w)


def _dense_math(x, ypool, o, yconv, wout_ref, g_ref, wgu_ref, wd_ref, fg_ref, final):
    o1 = POOL_WIDTH
    o2 = o1 + ATTN_WIDTH
    mix = (jnp.dot(ypool, wout_ref[0:o1, :], preferred_element_type=F32)
           + jnp.dot(o, wout_ref[o1:o2, :], preferred_element_type=F32)
           + jnp.dot(yconv, wout_ref[o2:, :], preferred_element_type=F32))
    x1 = x + mix
    h = _rms(x1, g_ref[...]).astype(BF16)
    gu = jnp.dot(h, wgu_ref[...], preferred_element_type=F32)
    gate = gu[:, :D_FF]
    a = (gate * jax.nn.sigmoid(gate) * gu[:, D_FF:]).astype(BF16)
    x2 = x1 + jnp.dot(a, wd_ref[...], preferred_element_type=F32)
    return _rms(x2, fg_ref[...]) if final else x2


def _dense_weight_specs(layer):
    shapes = [(MIX_WIDTH, D_MODEL), (1, D_MODEL), (D_MODEL, 2 * D_FF), (D_FF, D_MODEL)]
    return [_layer_spec(shape, layer) for shape in shapes] + [_const_spec((1, D_MODEL))]


def _prompt_dense_kernel(x_ref, o_ref, up0_ref, uc0_ref, up_ref, uc_ref, hist_p_ref, hist_c_ref,
                         pool_w_ref, pool_scale_ref, dw_ref, dw_b_ref, ln_g_ref, ln_b_ref, pw_ref,
                         wout_ref, g_ref, wgu_ref, wd_ref, fg_ref, out_ref,
                         ext_p, ext_c, shift_c, conv_out, y_pool, y_conv, *, final, tiles_per_seq):
    tm = x_ref.shape[0]
    i = pl.program_id(0)

    def mixer_steps(up_ref_, uc_ref_, tile, slot, first):
        def fill():
            if first is True:
                ext_p[0:HALO, :] = hist_p_ref[...]
                ext_c[0:HALO, :] = hist_c_ref[...]
            else:
                ext_p[0:HALO, :] = jnp.where(first, hist_p_ref[...], ext_p[tm:, :])
                ext_c[0:HALO, :] = jnp.where(first, hist_c_ref[...], ext_c[tm:, :])
            ext_p[HALO:, :] = up_ref_[...]
            ext_c[HALO:, :] = uc_ref_[...]

        def pool():
            pos0 = (tile % tiles_per_seq) * tm
            y_pool[slot] = _pool_mixer(ext_p, pool_w_ref, pool_scale_ref, tm, pos0).astype(BF16)

        def store_conv(y):
            y_conv[slot] = y.astype(BF16)

        return [fill] + _conv_mixer_steps(ext_c, shift_c, conv_out, dw_ref, dw_b_ref, ln_g_ref, ln_b_ref, pw_ref,
                                          tm, store_conv) + [pool]

    @pl.when(i == 0)
    def _():
        for step in mixer_steps(up0_ref, uc0_ref, 0, 0, True):
            step()

    slot = i % 2
    nxt = jnp.minimum(i + 1, pl.num_programs(0) - 1)
    out_ref[...] = _dense_math(x_ref[...], y_pool[slot], o_ref[...], y_conv[slot],
                               wout_ref, g_ref, wgu_ref, wd_ref, fg_ref, final)
    for step in mixer_steps(up_ref, uc_ref, nxt, 1 - slot, nxt % tiles_per_seq == 0):
        step()


def _prompt_dense(x, o, upool, uconv, hist_pool, hist_conv, mixer_w, dense_w, tm, seq_len, final, layer):
    n = x.shape[0]
    tps = seq_len // tm
    last = n // tm - 1
    row = lambda w: pl.BlockSpec((tm, w), lambda i: (i, 0))
    first_tile = lambda w: pl.BlockSpec((tm, w), lambda i: (0, 0))
    next_tile = lambda w: pl.BlockSpec((tm, w), lambda i: (jnp.minimum(i + 1, last), 0))
    next_seq = lambda w: pl.BlockSpec((None, HALO, w), lambda i: (jnp.minimum(i + 1, last) // tps, 0, 0))
    return pl.pallas_call(
        functools.partial(_prompt_dense_kernel, final=final, tiles_per_seq=tps),
        grid=(n // tm,),
        in_specs=[row(D_MODEL), row(ATTN_WIDTH), first_tile(POOL_WIDTH), first_tile(CONV_WIDTH),
                  next_tile(POOL_WIDTH), next_tile(CONV_WIDTH), next_seq(POOL_WIDTH), next_seq(CONV_WIDTH)]
        + _mixer_weight_specs(layer) + _dense_weight_specs(layer),
        out_specs=row(D_MODEL),
        out_shape=jax.ShapeDtypeStruct((n, D_MODEL), F32),
        scratch_shapes=_mixer_scratch(tm) + [pltpu.VMEM((2, tm, POOL_WIDTH), BF16),
                                             pltpu.VMEM((2, tm, CONV_WIDTH), BF16)],
        compiler_params=pltpu.CompilerParams(dimension_semantics=("arbitrary",),
                                             vmem_limit_bytes=VMEM_LIMIT),
        name="prompt_dense",
    )(x, o, upool, uconv, upool, uconv, hist_pool, hist_conv, *mixer_w, *dense_w)


def _sample_dense_kernel(x_ref, yp_ref, o_ref, yc_ref, wout_ref, g_ref, wgu_ref, wd_ref, fg_ref, out_ref, *, final):
    out_ref[...] = _dense_math(x_ref[...], yp_ref[...], o_ref[...], yc_ref[...],
                               wout_ref, g_ref, wgu_ref, wd_ref, fg_ref, final)


def _sample_dense(x, ypool, o, yconv, wout_bf, g, wgu_bf, wd_bf, fg, tm, final, layer):
    n = x.shape[0]
    row = lambda w: pl.BlockSpec((tm, w), lambda i: (i, 0))
    return pl.pallas_call(
        functools.partial(_sample_dense_kernel, final=final),
        grid=(n // tm,),
        in_specs=[row(D_MODEL), row(POOL_WIDTH), row(ATTN_WIDTH), row(CONV_WIDTH)] + _dense_weight_specs(layer),
        out_specs=row(D_MODEL),
        out_shape=jax.ShapeDtypeStruct((n, D_MODEL), F32),
        compiler_params=pltpu.CompilerParams(dimension_semantics=("arbitrary",),
                                             vmem_limit_bytes=VMEM_LIMIT),
        name="sample_dense",
    )(x, ypool, o, yconv, wout_bf, g, wgu_bf, wd_bf, fg)


def _rope_tables(pos):
    half = HEAD_DIM // 2
    inv = ROPE_THETA ** (-jnp.arange(half, dtype=F32) / half)
    ang = pos.astype(F32)[:, None] * inv[None, :]
    reps = LANES // half
    cos = jnp.tile(jnp.cos(ang), (1, reps))
    sign = jnp.tile(jnp.concatenate([-jnp.ones((half,), F32), jnp.ones((half,), F32)]), LANES // HEAD_DIM)
    sin = jnp.tile(jnp.sin(ang), (1, reps)) * sign[None, :]
    return cos, sin


def _block_diag(pool_w):
    depth, g, d, _ = pool_w.shape
    eye = jnp.eye(g, dtype=pool_w.dtype)
    return (pool_w[:, :, :, None, :] * eye[None, :, None, :, None]).reshape(depth, g * d, g * d)


def _pad_hist(h):
    return jnp.pad(h, ((0, 0), (0, 0), (HALO - h.shape[2], 0), (0, 0)))


def kernel(x_prompt, x_sample, cache_k, cache_v, state_pool, state_conv, norm_mix_g, w_in, pool_w, pool_scale,
           lambda_qk, diff_norm_g, conv_dw, conv_dw_b, conv_ln_g, conv_ln_b, conv_pw, w_out, norm_ffn_g,
           w_gate_up, w_down, final_norm_g):
    B, S, _ = x_prompt.shape
    Bs, Ls, _ = x_sample.shape
    depth = w_in.shape[0]
    past_len = cache_k.shape[2]
    assert S % (Q_TILES_PER_STEP * ATTN_TILE) == 0 and S % ROW_TILE == 0 and S % IN_PROJ_TILE == 0
    assert S // ROW_TILE >= 2
    assert (Bs * Ls) % 16 == 0 and Ls % 16 == 0 and Ls >= CONV_HIST

    cos_p, sin_p = _rope_tables(jnp.arange(S, dtype=jnp.int32))
    cos_s, sin_s = _rope_tables(past_len + jnp.arange(Ls, dtype=jnp.int32))
    cos_s, sin_s = jnp.tile(cos_s, (Bs, 1)), jnp.tile(sin_s, (Bs, 1))
    zero_pool = jnp.zeros((B, HALO, POOL_WIDTH), F32)
    zero_conv = jnp.zeros((B, HALO, CONV_WIDTH), F32)

    xp = x_prompt.reshape(B * S, D_MODEL)
    xs = x_sample.reshape(Bs * Ls, D_MODEL)
    rows = lambda a: a.reshape(depth, 1, -1)
    norm_mix_rows, norm_ffn_rows = rows(norm_mix_g), rows(norm_ffn_g)
    diff_g_row, diff_g_col = rows(diff_norm_g), diff_norm_g.reshape(depth, -1, 1)
    w_in_bf, w_out_bf, w_gu_bf, w_d_bf = (w.astype(BF16) for w in (w_in, w_out, w_gate_up, w_down))
    mixer_w = (_block_diag(pool_w).astype(BF16), rows(pool_scale), conv_dw, rows(conv_dw_b), rows(conv_ln_g),
               rows(conv_ln_b), conv_pw.astype(BF16))
    dense_w = (w_out_bf, norm_ffn_rows, w_gu_bf, w_d_bf, final_norm_g.reshape(1, -1))
    hist_pool_s, hist_conv_s = _pad_hist(state_pool), _pad_hist(state_conv)
    cache_kt = jnp.transpose(cache_k, (0, 1, 3, 4, 5, 2)).reshape(depth, Bs, ATTN_WIDTH, past_len)
    cache_v4 = cache_v.reshape(depth, Bs, past_len * N_HEADS, V_DIM)
    kv_stack = depth
    pp, cp = [], []
    ks_, vs_, ps_, cs_ = [], [], [], []
    for l in range(depth):
        lam_init = 0.8 - 0.6 * math.exp(-0.3 * l)
        final = l == depth - 1

        upool, q, k, kb, v, vb, uconv = _prompt_in_proj(xp, norm_mix_rows, w_in_bf, cos_p, sin_p, IN_PROJ_TILE, S,
                                                        kv_stack, l)
        kv_stack = (k, v)
        o = _prompt_attention(lambda_qk, diff_g_col, q.reshape(B, S, -1), kb.reshape(B, S, -1),
                              vb, lam_init, l)
        xp = _prompt_dense(xp, o.reshape(B * S, -1), upool, uconv, zero_pool, zero_conv, mixer_w, dense_w,
                           ROW_TILE, S, final, l)
        pp.append(upool.reshape(B, S, -1)[:, S - POOL_HIST:])
        cp.append(uconv.reshape(B, S, -1)[:, S - CONV_HIST:])

        upool, q, k, kb, v, vb, uconv = _sample_in_proj(xs, norm_mix_rows, w_in_bf, cos_s, sin_s, l)
        o = _sample_attention(lambda_qk, diff_g_row, q.reshape(Bs, Ls, -1), cache_kt, cache_v4,
                              kb.reshape(Bs, Ls, -1), vb.reshape(Bs, Ls, -1), lam_init, l)
        ypool, yconv = _sample_mixers(upool, uconv, hist_pool_s, hist_conv_s, mixer_w, Ls, past_len, l)
        xs = _sample_dense(xs, ypool, o.reshape(Bs * Ls, -1), yconv, *dense_w, Bs * Ls, final, l)
        ks_.append(k.reshape(Bs, Ls, N_HEADS, 2, HEAD_DIM))
        vs_.append(v.reshape(Bs, Ls, N_HEADS, V_DIM))
        ps_.append(upool.reshape(Bs, Ls, -1)[:, Ls - POOL_HIST:])
        cs_.append(uconv.reshape(Bs, Ls, -1)[:, Ls - CONV_HIST:])

    kt_all, v4_all = kv_stack
    new_k = jnp.transpose(kt_all.reshape(depth, B, N_HEADS, 2, HEAD_DIM, S), (0, 1, 5, 2, 3, 4))
    new_v = v4_all.reshape(depth, B, S, N_HEADS, V_DIM)
    return (xp.reshape(B, S, D_MODEL), xs.reshape(Bs, Ls, D_MODEL),
            new_k, new_v, jnp.stack(pp), jnp.stack(cp),
            jnp.stack(ks_), jnp.stack(vs_), jnp.stack(ps_), jnp.stack(cs_))
```

```python
import functools
import math

import jax
import jax.numpy as jnp
from jax import lax
from jax.experimental import pallas as pl
from jax.experimental.pallas import tpu as pltpu

D_MODEL = 1024
CHUNK = 64
POOL_WIDTH = D_MODEL // 4
POOL_WINDOWS = (2, 4, 8, 16)
POOL_GROUP_DIM = POOL_WIDTH // len(POOL_WINDOWS)
POOL_HIST = max(POOL_WINDOWS) - 1
ATTN_WIDTH = D_MODEL // 2
N_HEADS = 4
HEAD_DIM = ATTN_WIDTH // (2 * N_HEADS)
V_DIM = 2 * HEAD_DIM
ROPE_THETA = 10000.0
CONV_WIDTH = D_MODEL // 4
CONV_K = 31
CONV_HIST = CONV_K - 1
MIX_WIDTH = POOL_WIDTH + ATTN_WIDTH + CONV_WIDTH
IN_WIDTH = POOL_WIDTH + 3 * ATTN_WIDTH + 2 * CONV_WIDTH
D_FF = ((-(-8 * D_MODEL // 3) + 255) // 256) * 256
EPS = 1e-6

LANES = 128
HALO = 32
ROW_TILE = 512
IN_PROJ_TILE = 1024
ATTN_TILE = 256
SAMPLE_KEY_TILE = 1024
KEY_TILES_PER_BLOCK = 4
Q_TILES_PER_STEP = 8
HEADS_PER_STEP = 4
VMEM_LIMIT = 56 * 1024 * 1024

F32 = jnp.float32
BF16 = jnp.bfloat16


def _rms(x, g):
    return x * lax.rsqrt(jnp.mean(x * x, axis=-1, keepdims=True) + EPS) * g


def _const_spec(shape):
    return pl.BlockSpec(shape, lambda *_: (0,) * len(shape), pipeline_mode=pl.Buffered(1))


def _layer_spec(shape, layer):
    return pl.BlockSpec((None,) + shape, lambda *_: (layer,) + (0,) * len(shape), pipeline_mode=pl.Buffered(1))


O_Q = POOL_WIDTH
O_K = O_Q + ATTN_WIDTH
O_V = O_K + ATTN_WIDTH
O_A = O_V + ATTN_WIDTH
O_B = O_A + CONV_WIDTH


def _qkv_epilogue(z, cos_ref, sin_ref, q_ref, k_ref, kb_ref, v_ref, vb_ref, cache_layout):
    tm = z.shape[0]
    o1, o2, o3 = O_Q, O_K, O_V
    cos = cos_ref[...]
    sin = sin_ref[...]
    lane = lax.broadcasted_iota(jnp.int32, cos.shape, 1)
    first_half = (lane & (HEAD_DIM // 2)) == 0

    def rope(t):
        up = pltpu.roll(t, LANES - HEAD_DIM // 2, axis=1)
        down = pltpu.roll(t, HEAD_DIM // 2, axis=1)
        return t * cos + jnp.where(first_half, up, down) * sin

    scale = HEAD_DIM ** -0.5 * math.log2(math.e)
    for j in range(ATTN_WIDTH // LANES):
        sl = slice(j * LANES, (j + 1) * LANES)
        qj = rope(z[:, o1 + j * LANES:o1 + (j + 1) * LANES])
        kj = rope(z[:, o2 + j * LANES:o2 + (j + 1) * LANES])
        vj = z[:, o3 + j * LANES:o3 + (j + 1) * LANES]
        q_ref[:, sl] = (qj * scale).astype(BF16)
        kb_ref[:, sl] = kj.astype(BF16)
        if cache_layout:
            k_ref[sl, :] = kj.T
            v_ref[pl.ds(j, tm, stride=N_HEADS), :] = vj
            vb_ref[sl, :] = vj.T.astype(BF16)
        else:
            k_ref[:, sl] = kj
            v_ref[:, sl] = vj
            vb_ref[:, sl] = vj.astype(BF16)


def _sample_in_proj_kernel(x_ref, g_ref, w_ref, cos_ref, sin_ref,
                           upool_ref, q_ref, k_ref, kb_ref, v_ref, vb_ref, uconv_ref):
    h = _rms(x_ref[...], g_ref[...]).astype(BF16)
    z = jnp.dot(h, w_ref[...], preferred_element_type=F32)
    upool_ref[...] = z[:, :O_Q]
    uconv_ref[...] = z[:, O_A:O_B] * jax.nn.sigmoid(z[:, O_B:])
    _qkv_epilogue(z, cos_ref, sin_ref, q_ref, k_ref, kb_ref, v_ref, vb_ref, False)


def _sample_in_proj(x, g, w_bf, cos_tab, sin_tab, layer):
    n = x.shape[0]
    full = lambda w: pl.BlockSpec((n, w), lambda i: (0, 0))
    flat = lambda dt: jax.ShapeDtypeStruct((n, ATTN_WIDTH), dt)
    return pl.pallas_call(
        _sample_in_proj_kernel,
        grid=(1,),
        in_specs=[full(D_MODEL), _layer_spec((1, D_MODEL), layer), _layer_spec((D_MODEL, IN_WIDTH), layer),
                  full(LANES), full(LANES)],
        out_specs=[full(POOL_WIDTH)] + [full(ATTN_WIDTH)] * 5 + [full(CONV_WIDTH)],
        out_shape=[jax.ShapeDtypeStruct((n, POOL_WIDTH), F32), flat(BF16), flat(F32), flat(BF16), flat(F32),
                   flat(BF16), jax.ShapeDtypeStruct((n, CONV_WIDTH), F32)],
        compiler_params=pltpu.CompilerParams(dimension_semantics=("arbitrary",),
                                             vmem_limit_bytes=VMEM_LIMIT),
        name="sample_in_proj",
    )(x, g, w_bf, cos_tab, sin_tab)


def _prompt_in_proj_kernel(*refs, n_unused):
    x_ref, g_ref, w_ref, cos_ref, sin_ref = refs[:5]
    upool_ref, q_ref, k_ref, kb_ref, v_ref, vb_ref, uconv_ref = refs[5 + n_unused:]
    h = _rms(x_ref[...], g_ref[...]).astype(BF16)
    z = jnp.dot(h, w_ref[...], preferred_element_type=F32)
    upool_ref[...] = z[:, :O_Q]
    uconv_ref[...] = z[:, O_A:O_B] * jax.nn.sigmoid(z[:, O_B:])
    _qkv_epilogue(z, cos_ref, sin_ref, q_ref, k_ref, kb_ref, v_ref, vb_ref, True)


def _prompt_in_proj(x, g, w_bf, cos_tab, sin_tab, tm, seq_len, kv_stack, layer):
    n = x.shape[0]
    tps = seq_len // tm
    batch = n // seq_len
    row = lambda w: pl.BlockSpec((tm, w), lambda i: (i, 0))
    tab = pl.BlockSpec((tm, LANES), lambda i: (i % tps, 0))
    flat = lambda dt, w=ATTN_WIDTH: jax.ShapeDtypeStruct((n, w), dt)
    extra_in, extra_specs, aliases = [], [], {}
    if isinstance(kv_stack, int):
        k_shape = jax.ShapeDtypeStruct((kv_stack, batch, ATTN_WIDTH, seq_len), F32)
        v_shape = jax.ShapeDtypeStruct((kv_stack, batch, N_HEADS * seq_len, V_DIM), F32)
    else:
        k_shape, v_shape = (jax.ShapeDtypeStruct(a.shape, a.dtype) for a in kv_stack)
        extra_in = list(kv_stack)
        extra_specs = [pl.BlockSpec(memory_space=pl.ANY)] * 2
        aliases = {5: 2, 6: 4}
    k_spec = pl.BlockSpec((None, None, ATTN_WIDTH, tm), lambda i: (layer, i // tps, 0, i % tps))
    v_spec = pl.BlockSpec((None, None, N_HEADS * tm, V_DIM), lambda i: (layer, i // tps, i % tps, 0))
    vt_spec = pl.BlockSpec((None, ATTN_WIDTH, tm), lambda i: (i // tps, 0, i % tps))
    return pl.pallas_call(
        functools.partial(_prompt_in_proj_kernel, n_unused=len(extra_in)),
        grid=(n // tm,),
        in_specs=[row(D_MODEL), _layer_spec((1, D_MODEL), layer), _layer_spec((D_MODEL, IN_WIDTH), layer), tab, tab]
        + extra_specs,
        out_specs=[row(POOL_WIDTH), row(ATTN_WIDTH), k_spec, row(ATTN_WIDTH), v_spec, vt_spec, row(CONV_WIDTH)],
        out_shape=[flat(F32, POOL_WIDTH), flat(BF16), k_shape, flat(BF16), v_shape,
                   jax.ShapeDtypeStruct((batch, ATTN_WIDTH, seq_len), BF16), flat(F32, CONV_WIDTH)],
        input_output_aliases=aliases,
        compiler_params=pltpu.CompilerParams(dimension_semantics=("arbitrary",),
                                             vmem_limit_bytes=VMEM_LIMIT),
        name="prompt_in_proj",
    )(x, g, w_bf, cos_tab, sin_tab, *extra_in)


def _lambda_scalar(lq, lam_init):
    a = jnp.sum(lq[0:1, :] * lq[1:2, :], axis=-1, keepdims=True)
    b = jnp.sum(lq[2:3, :] * lq[3:4, :], axis=-1, keepdims=True)
    return jnp.exp(a) - jnp.exp(b) + lam_init


def _stack_maps(q):
    lane = lax.broadcasted_iota(jnp.int32, q.shape, 1)
    zero = jnp.zeros_like(q)
    return jnp.concatenate([jnp.where(lane < HEAD_DIM, q, zero), jnp.where(lane >= HEAD_DIM, q, zero)], axis=0)


ONES_ROWS = 16


def _flash_scores(qq, kt):
    return lax.dot_general(kt, qq, (((1,), (1,)), ((), ())), preferred_element_type=F32)


def _flash_update(carry, s, vt_ones, p_stage):
    m, acc = carry
    m_new = jnp.maximum(m, jnp.max(s, axis=0, keepdims=True))
    alpha = jnp.exp2(m - m_new)
    p_stage[...] = jnp.exp2(s - m_new).astype(BF16)
    return m_new, alpha * acc + jnp.dot(vt_ones, p_stage[...], preferred_element_type=F32)


def _flash_init(cols):
    return jnp.full((1, cols), -jnp.inf, F32), jnp.zeros((V_DIM + ONES_ROWS, cols), F32)


def _attn_finish(carry, lam, g_col, lam_init, tq):
    _, acc = carry
    scaled = acc[:V_DIM] * (1.0 / acc[V_DIM:V_DIM + 1])
    o = scaled[:, :tq] - lam * scaled[:, tq:]
    o = o * lax.rsqrt(jnp.mean(o * o, axis=0, keepdims=True) + EPS) * g_col
    return (o * (1.0 - lam_init)).T


def _mask_diagonal_tile(s):
    t = s.shape[0]
    q_chunk = (lax.broadcasted_iota(jnp.int32, (1, s.shape[1]), 1) % t) // CHUNK
    blocks = [jnp.where(q_chunk >= a, s[a * CHUNK:(a + 1) * CHUNK, :], -1e30) for a in range(t // CHUNK)]
    return jnp.concatenate(blocks, axis=0)


def _prompt_attn_kernel(lq_ref, g_ref, q_ref, k_ref, vt_ref, o_ref, qq_ref, p_ref, *, lam_init):
    t = ATTN_TILE
    heads = [slice(h * LANES, (h + 1) * LANES) for h in range(HEADS_PER_STEP)]
    ones = jnp.ones((ONES_ROWS, t), BF16)
    lam = _lambda_scalar(lq_ref[...], lam_init)
    q_tiles = q_ref.shape[0] // t

    def one_query_tile(sub, _):
        qi = pl.program_id(2) * q_tiles + sub
        q_rows = pl.ds(pl.multiple_of(sub * t, t), t)
        for h, c in enumerate(heads):
            qq_ref[h] = _stack_maps(q_ref[q_rows, c])
        stage0 = jnp.minimum(qi, 0)

        def steps(tiles, carry):
            rows = [pl.ds(pl.multiple_of(j * t, t), t) for j, _ in tiles]
            s = [[_flash_scores(qq_ref[h], k_ref[r, c]) for h, c in enumerate(heads)] for r in rows]
            for ti, (_, diagonal) in enumerate(tiles):
                if diagonal:
                    s[ti] = [_mask_diagonal_tile(x) for x in s[ti]]
                carry = tuple(
                    _flash_update(carry[h], s[ti][h], jnp.concatenate([vt_ref[c, rows[ti]], ones], axis=0),
                                  p_ref.at[stage0 + ti * HEADS_PER_STEP + h])
                    for h, c in enumerate(heads))
            return carry

        nb = min(KEY_TILES_PER_BLOCK, k_ref.shape[0] // t)
        carry = (_flash_init(2 * t),) * HEADS_PER_STEP
        if nb < k_ref.shape[0] // t:
            carry = lax.fori_loop(0, qi // nb, lambda p, c: steps([(nb * p + i, False) for i in range(nb)], c),
                                  carry)
        left = qi % nb

        def last_block(n_full):
            return lambda c: steps([(qi - n_full + i, False) for i in range(n_full)] + [(qi, True)], c)

        carry = lax.switch(left, [last_block(n) for n in range(nb)], carry)
        for h, c in enumerate(heads):
            o_ref[q_rows, c] = _attn_finish(carry[h], lam, g_ref[...], lam_init, t).astype(BF16)
        return 0

    lax.fori_loop(0, q_tiles, one_query_tile, 0)


def _prompt_attention(lq, g, q, kb, vtb, lam_init, layer):
    b, s, _ = q.shape
    t = ATTN_TILE
    w = HEADS_PER_STEP * LANES
    tq = Q_TILES_PER_STEP * t
    return pl.pallas_call(
        functools.partial(_prompt_attn_kernel, lam_init=lam_init),
        grid=(b, N_HEADS // HEADS_PER_STEP, s // tq),
        in_specs=[
            _layer_spec((4, HEAD_DIM), layer),
            _layer_spec((V_DIM, 1), layer),
            pl.BlockSpec((None, tq, w), lambda b_, h, i: (b_, i, h)),
            pl.BlockSpec((None, s, w), lambda b_, h, i: (b_, 0, h)),
            pl.BlockSpec((None, w, s), lambda b_, h, i: (b_, h, 0)),
        ],
        out_specs=pl.BlockSpec((None, tq, w), lambda b_, h, i: (b_, i, h)),
        out_shape=jax.ShapeDtypeStruct((b, s, ATTN_WIDTH), BF16),
        scratch_shapes=[pltpu.VMEM((HEADS_PER_STEP, 2 * t, LANES), BF16),
                        pltpu.VMEM((KEY_TILES_PER_BLOCK * HEADS_PER_STEP, t, 2 * t), BF16)],
        compiler_params=pltpu.CompilerParams(dimension_semantics=("arbitrary",) * 3,
                                             vmem_limit_bytes=VMEM_LIMIT),
        name="prompt_attention",
    )(lq, g, q, kb, vtb)


def _row_scores(qq, kt, *, keys_on_rows):
    dims = (((1,), (1,)), ((), ())) if keys_on_rows else (((1,), (0,)), ((), ()))
    return lax.dot_general(qq, kt, dims, preferred_element_type=F32)


def _row_flash_update(carry, s, vt):
    m, l, acc = carry
    m_new = jnp.maximum(m, jnp.max(s, axis=-1, keepdims=True))
    alpha = jnp.exp2(m - m_new)
    p = jnp.exp2(s - m_new)
    l = alpha * l + jnp.sum(p, axis=-1, keepdims=True)
    acc = alpha * acc + jnp.dot(p.astype(BF16), vt, preferred_element_type=F32)
    return m_new, l, acc


def _sample_attn_kernel(lq_ref, g_ref, q_ref, ck_ref, cv_ref, k_ref, v_ref, o_ref, *, lam_init, past_len):
    t = SAMPLE_KEY_TILE
    tq = q_ref.shape[0]
    heads = [slice(h * LANES, (h + 1) * LANES) for h in range(N_HEADS)]
    qq = [_stack_maps(q_ref[:, c]) for c in heads]

    def body(j, carry):
        cols = pl.ds(pl.multiple_of(j * t, t), t)
        s = [_row_scores(qq[h], ck_ref[c, cols].astype(BF16), keys_on_rows=False) for h, c in enumerate(heads)]
        return tuple(
            _row_flash_update(carry[h], s[h], cv_ref[pl.ds(j * (t * N_HEADS) + h, t, stride=N_HEADS), :].astype(BF16))
            for h in range(N_HEADS))

    init = (jnp.full((2 * tq, 1), -jnp.inf, F32), jnp.zeros((2 * tq, 1), F32), jnp.zeros((2 * tq, V_DIM), F32))
    carry = lax.fori_loop(0, past_len // t, body, (init,) * N_HEADS)
    lam = _lambda_scalar(lq_ref[...], lam_init)
    for h, c in enumerate(heads):
        s = _row_scores(qq[h], k_ref[:, c], keys_on_rows=True)
        _, l, acc = _row_flash_update(carry[h], s, v_ref[:, c])
        o = acc[:tq] / l[:tq] - lam * (acc[tq:] / l[tq:])
        o_ref[:, c] = (_rms(o, g_ref[...]) * (1.0 - lam_init)).astype(BF16)


def _sample_attention(lq, g, q, cache_kt, cache_v4, kb, vb, lam_init, layer):
    b, ls, _ = q.shape
    past_len = cache_kt.shape[-1]
    assert past_len % SAMPLE_KEY_TILE == 0
    return pl.pallas_call(
        functools.partial(_sample_attn_kernel, lam_init=lam_init, past_len=past_len),
        grid=(b,),
        in_specs=[
            _layer_spec((4, HEAD_DIM), layer),
            _layer_spec((1, V_DIM), layer),
            pl.BlockSpec((None, ls, ATTN_WIDTH), lambda b_: (b_, 0, 0)),
            pl.BlockSpec((None, None, ATTN_WIDTH, past_len), lambda b_: (layer, b_, 0, 0)),
            pl.BlockSpec((None, None, N_HEADS * past_len, V_DIM), lambda b_: (layer, b_, 0, 0)),
            pl.BlockSpec((None, ls, ATTN_WIDTH), lambda b_: (b_, 0, 0)),
            pl.BlockSpec((None, ls, ATTN_WIDTH), lambda b_: (b_, 0, 0)),
        ],
        out_specs=pl.BlockSpec((None, ls, ATTN_WIDTH), lambda b_: (b_, 0, 0)),
        out_shape=jax.ShapeDtypeStruct((b, ls, ATTN_WIDTH), BF16),
        compiler_params=pltpu.CompilerParams(dimension_semantics=("arbitrary",),
                                             vmem_limit_bytes=VMEM_LIMIT),
        name="sample_attention",
    )(lq, g, q, cache_kt, cache_v4, kb, vb)


def _pool_mixer(ext_p, pool_w_ref, pool_scale_ref, tm, pos0):
    assert POOL_WINDOWS == (2, 4, 8, 16) and HALO == 32
    e = ext_p[...]
    n = HALO + tm
    s2 = e[8:] + e[7:n - 1]
    s4 = s2[8:] + s2[6:n - 10]
    s8 = s4[8:] + s4[4:n - 20]
    sums = {2: s2[24:], 4: s4[16:], 8: s8[8:], 16: s8[8:] + s8[:n - 32]}
    u = e[HALO:]
    lane = lax.broadcasted_iota(jnp.int32, (tm, POOL_WIDTH), 1)
    group = lane // POOL_GROUP_DIM
    win = sums[POOL_WINDOWS[-1]]
    width = jnp.full((tm, POOL_WIDTH), POOL_WINDOWS[-1], jnp.int32)
    for gi in range(len(POOL_WINDOWS) - 2, -1, -1):
        win = jnp.where(group == gi, sums[POOL_WINDOWS[gi]], win)
        width = jnp.where(group == gi, POOL_WINDOWS[gi], width)
    pos = pos0 + lax.broadcasted_iota(jnp.int32, (tm, POOL_WIDTH), 0)
    count = jnp.minimum(pos + 1, width).astype(F32)
    d = (win / count - u).astype(BF16)
    return jnp.dot(d, pool_w_ref[...], preferred_element_type=F32) * pool_scale_ref[...]


CONV_ROW_BLOCK = 64


def _conv_mixer_steps(ext_c, shift_c, conv_out, dw_ref, dw_b_ref, ln_g_ref, ln_b_ref, pw_ref, tm, store):
    first_row = HALO - CONV_HIST
    blk = min(tm, CONV_ROW_BLOCK)

    def shift(residues):
        for r in residues:
            shift_c[r, 0:HALO - 8 + tm, :] = ext_c[pl.ds(r, HALO - 8 + tm), :]

    def taps(rb):
        yb = jnp.zeros((blk, CONV_WIDTH), F32)
        for k in range(CONV_K):
            r, a = (first_row + k) % 8, (first_row + k) // 8
            rows = pl.ds(8 * a + rb * blk, blk)
            src = ext_c[rows, :] if r == 0 else shift_c[r, rows, :]
            yb = yb + src * dw_ref[k:k + 1, :]
        conv_out[rb * blk:(rb + 1) * blk, :] = yb

    def finish():
        y = conv_out[...] + dw_b_ref[...]
        mu = jnp.mean(y, axis=-1, keepdims=True)
        yc = y - mu
        yn = yc * lax.rsqrt(jnp.mean(yc * yc, axis=-1, keepdims=True) + EPS) * ln_g_ref[...] + ln_b_ref[...]
        act = (yn * jax.nn.sigmoid(yn)).astype(BF16)
        store(jnp.dot(act, pw_ref[...], preferred_element_type=F32))

    steps = [functools.partial(shift, rs) for rs in ((1, 2), (3, 4, 5), (6, 7))]
    steps += [functools.partial(taps, rb) for rb in range(tm // blk)]
    return steps + [finish]


def _conv_mixer(ext_c, shift_c, conv_out, dw_ref, dw_b_ref, ln_g_ref, ln_b_ref, pw_ref, tm):
    out = []
    for step in _conv_mixer_steps(ext_c, shift_c, conv_out, dw_ref, dw_b_ref, ln_g_ref, ln_b_ref, pw_ref, tm,
                                  out.append):
        step()
    return out[0]


def _mixer_scratch(tm):
    return [pltpu.VMEM((HALO + tm, POOL_WIDTH), F32), pltpu.VMEM((HALO + tm, CONV_WIDTH), F32),
            pltpu.VMEM((8, HALO + tm, CONV_WIDTH), F32), pltpu.VMEM((tm, CONV_WIDTH), F32)]


def _mixer_weight_specs(layer):
    shapes = [(POOL_WIDTH, POOL_WIDTH), (1, POOL_WIDTH), (CONV_K, CONV_WIDTH), (1, CONV_WIDTH), (1, CONV_WIDTH),
              (1, CONV_WIDTH), (CONV_WIDTH, CONV_WIDTH)]
    return [_layer_spec(shape, layer) for shape in shapes]


def _sample_mixers_kernel(up_ref, up_hist_ref, uc_ref, uc_hist_ref,
                          pool_w_ref, pool_scale_ref, dw_ref, dw_b_ref, ln_g_ref, ln_b_ref, pw_ref,
                          ypool_ref, yconv_ref, ext_p, ext_c, shift_c, conv_out, *, pos_base):
    tm = up_ref.shape[0]
    ext_p[0:HALO, :] = up_hist_ref[...]
    ext_p[HALO:, :] = up_ref[...]
    ext_c[0:HALO, :] = uc_hist_ref[...]
    ext_c[HALO:, :] = uc_ref[...]
    ypool_ref[...] = _pool_mixer(ext_p, pool_w_ref, pool_scale_ref, tm, pos_base).astype(BF16)
    yconv_ref[...] = _conv_mixer(ext_c, shift_c, conv_out, dw_ref, dw_b_ref, ln_g_ref, ln_b_ref, pw_ref,
                                 tm).astype(BF16)


def _sample_mixers(upool, uconv, hist_pool, hist_conv, mixer_w, seq_len, pos_base, layer):
    n = upool.shape[0]
    cur = pl.BlockSpec((seq_len, POOL_WIDTH), lambda i: (i, 0))
    hist = pl.BlockSpec((None, None, HALO, POOL_WIDTH), lambda i: (layer, i, 0, 0))
    return pl.pallas_call(
        functools.partial(_sample_mixers_kernel, pos_base=pos_base),
        grid=(n // seq_len,),
        in_specs=[cur, hist, cur, hist] + _mixer_weight_specs(layer),
        out_specs=[cur, cur],
        out_shape=[jax.ShapeDtypeStruct((n, POOL_WIDTH), BF16), jax.ShapeDtypeStruct((n, CONV_WIDTH), BF16)],
        scratch_shapes=_mixer_scratch(seq_len),
        compiler_params=pltpu.CompilerParams(dimension_semantics=("arbitrary",),
                                             vmem_limit_bytes=VMEM_LIMIT),
        name="sample_mixers",
    )(upool, hist_pool, uconv, hist_conv, *mixer_w)


def _dense_math(x, ypool, o, yconv, wout_ref, g_ref, wgu_ref, wd_ref, fg_ref, final):
    o1 = POOL_WIDTH
    o2 = o1 + ATTN_WIDTH
    mix = (jnp.dot(ypool, wout_ref[0:o1, :], preferred_element_type=F32)
           + jnp.dot(o, wout_ref[o1:o2, :], preferred_element_type=F32)
           + jnp.dot(yconv, wout_ref[o2:, :], preferred_element_type=F32))
    x1 = x + mix
    h = _rms(x1, g_ref[...]).astype(BF16)
    gu = jnp.dot(h, wgu_ref[...], preferred_element_type=F32)
    gate = gu[:, :D_FF]
    a = (gate * jax.nn.sigmoid(gate) * gu[:, D_FF:]).astype(BF16)
    x2 = x1 + jnp.dot(a, wd_ref[...], preferred_element_type=F32)
    return _rms(x2, fg_ref[...]) if final else x2


def _dense_weight_specs(layer):
    shapes = [(MIX_WIDTH, D_MODEL), (1, D_MODEL), (D_MODEL, 2 * D_FF), (D_FF, D_MODEL)]
    return [_layer_spec(shape, layer) for shape in shapes] + [_const_spec((1, D_MODEL))]


def _prompt_dense_kernel(x_ref, o_ref, up0_ref, uc0_ref, up_ref, uc_ref, hist_p_ref, hist_c_ref,
                         pool_w_ref, pool_scale_ref, dw_ref, dw_b_ref, ln_g_ref, ln_b_ref, pw_ref,
                         wout_ref, g_ref, wgu_ref, wd_ref, fg_ref, out_ref,
                         ext_p, ext_c, shift_c, conv_out, y_pool, y_conv, *, final, tiles_per_seq):
    tm = x_ref.shape[0]
    i = pl.program_id(0)

    def mixer_steps(up_ref_, uc_ref_, tile, slot, first):
        def fill():
            if first is True:
                ext_p[0:HALO, :] = hist_p_ref[...]
                ext_c[0:HALO, :] = hist_c_ref[...]
            else:
                ext_p[0:HALO, :] = jnp.where(first, hist_p_ref[...], ext_p[tm:, :])
                ext_c[0:HALO, :] = jnp.where(first, hist_c_ref[...], ext_c[tm:, :])
            ext_p[HALO:, :] = up_ref_[...]
            ext_c[HALO:, :] = uc_ref_[...]

        def pool():
            pos0 = (tile % tiles_per_seq) * tm
            y_pool[slot] = _pool_mixer(ext_p, pool_w_ref, pool_scale_ref, tm, pos0).astype(BF16)

        def store_conv(y):
            y_conv[slot] = y.astype(BF16)

        return [fill] + _conv_mixer_steps(ext_c, shift_c, conv_out, dw_ref, dw_b_ref, ln_g_ref, ln_b_ref, pw_ref,
                                          tm, store_conv) + [pool]

    @pl.when(i == 0)
    def _():
        for step in mixer_steps(up0_ref, uc0_ref, 0, 0, True):
            step()

    slot = i % 2
    nxt = jnp.minimum(i + 1, pl.num_programs(0) - 1)
    out_ref[...] = _dense_math(x_ref[...], y_pool[slot], o_ref[...], y_conv[slot],
                               wout_ref, g_ref, wgu_ref, wd_ref, fg_ref, final)
    for step in mixer_steps(up_ref, uc_ref, nxt, 1 - slot, nxt % tiles_per_seq == 0):
        step()


def _prompt_dense(x, o, upool, uconv, hist_pool, hist_conv, mixer_w, dense_w, tm, seq_len, final, layer):
    n = x.shape[0]
    tps = seq_len // tm
    last = n // tm - 1
    row = lambda w: pl.BlockSpec((tm, w), lambda i: (i, 0))
    first_tile = lambda w: pl.BlockSpec((tm, w), lambda i: (0, 0))
    next_tile = lambda w: pl.BlockSpec((tm, w), lambda i: (jnp.minimum(i + 1, last), 0))
    next_seq = lambda w: pl.BlockSpec((None, HALO, w), lambda i: (jnp.minimum(i + 1, last) // tps, 0, 0))
    return pl.pallas_call(
        functools.partial(_prompt_dense_kernel, final=final, tiles_per_seq=tps),
        grid=(n // tm,),
        in_specs=[row(D_MODEL), row(ATTN_WIDTH), first_tile(POOL_WIDTH), first_tile(CONV_WIDTH),
                  next_tile(POOL_WIDTH), next_tile(CONV_WIDTH), next_seq(POOL_WIDTH), next_seq(CONV_WIDTH)]
        + _mixer_weight_specs(layer) + _dense_weight_specs(layer),
        out_specs=row(D_MODEL),
        out_shape=jax.ShapeDtypeStruct((n, D_MODEL), F32),
        scratch_shapes=_mixer_scratch(tm) + [pltpu.VMEM((2, tm, POOL_WIDTH), BF16),
                                             pltpu.VMEM((2, tm, CONV_WIDTH), BF16)],
        compiler_params=pltpu.CompilerParams(dimension_semantics=("arbitrary",),
                                             vmem_limit_bytes=VMEM_LIMIT),
        name="prompt_dense",
    )(x, o, upool, uconv, upool, uconv, hist_pool, hist_conv, *mixer_w, *dense_w)


def _sample_dense_kernel(x_ref, yp_ref, o_ref, yc_ref, wout_ref, g_ref, wgu_ref, wd_ref, fg_ref, out_ref, *, final):
    out_ref[...] = _dense_math(x_ref[...], yp_ref[...], o_ref[...], yc_ref[...],
                               wout_ref, g_ref, wgu_ref, wd_ref, fg_ref, final)


def _sample_dense(x, ypool, o, yconv, wout_bf, g, wgu_bf, wd_bf, fg, tm, final, layer):
    n = x.shape[0]
    row = lambda w: pl.BlockSpec((tm, w), lambda i: (i, 0))
    return pl.pallas_call(
        functools.partial(_sample_dense_kernel, final=final),
        grid=(n // tm,),
        in_specs=[row(D_MODEL), row(POOL_WIDTH), row(ATTN_WIDTH), row(CONV_WIDTH)] + _dense_weight_specs(layer),
        out_specs=row(D_MODEL),
        out_shape=jax.ShapeDtypeStruct((n, D_MODEL), F32),
        compiler_params=pltpu.CompilerParams(dimension_semantics=("arbitrary",),
                                             vmem_limit_bytes=VMEM_LIMIT),
        name="sample_dense",
    )(x, ypool, o, yconv, wout_bf, g, wgu_bf, wd_bf, fg)


def _rope_tables(pos):
    half = HEAD_DIM // 2
    inv = ROPE_THETA ** (-jnp.arange(half, dtype=F32) / half)
    ang = pos.astype(F32)[:, None] * inv[None, :]
    reps = LANES // half
    cos = jnp.tile(jnp.cos(ang), (1, reps))
    sign = jnp.tile(jnp.concatenate([-jnp.ones((half,), F32), jnp.ones((half,), F32)]), LANES // HEAD_DIM)
    sin = jnp.tile(jnp.sin(ang), (1, reps)) * sign[None, :]
    return cos, sin


def _block_diag(pool_w):
    depth, g, d, _ = pool_w.shape
    eye = jnp.eye(g, dtype=pool_w.dtype)
    return (pool_w[:, :, :, None, :] * eye[None, :, None, :, None]).reshape(depth, g * d, g * d)


def _pad_hist(h):
    return jnp.pad(h, ((0, 0), (0, 0), (HALO - h.shape[2], 0), (0, 0)))


def kernel(x_prompt, x_sample, cache_k, cache_v, state_pool, state_conv, norm_mix_g, w_in, pool_w, pool_scale,
           lambda_qk, diff_norm_g, conv_dw, conv_dw_b, conv_ln_g, conv_ln_b, conv_pw, w_out, norm_ffn_g,
           w_gate_up, w_down, final_norm_g):
    B, S, _ = x_prompt.shape
    Bs, Ls, _ = x_sample.shape
    depth = w_in.shape[0]
    past_len = cache_k.shape[2]
    assert S % (Q_TILES_PER_STEP * ATTN_TILE) == 0 and S % ROW_TILE == 0 and S % IN_PROJ_TILE == 0
    assert S // ROW_TILE >= 2
    assert (Bs * Ls) % 16 == 0 and Ls % 16 == 0 and Ls >= CONV_HIST

    cos_p, sin_p = _rope_tables(jnp.arange(S, dtype=jnp.int32))
    cos_s, sin_s = _rope_tables(past_len + jnp.arange(Ls, dtype=jnp.int32))
    cos_s, sin_s = jnp.tile(cos_s, (Bs, 1)), jnp.tile(sin_s, (Bs, 1))
    zero_pool = jnp.zeros((B, HALO, POOL_WIDTH), F32)
    zero_conv = jnp.zeros((B, HALO, CONV_WIDTH), F32)

    xp = x_prompt.reshape(B * S, D_MODEL)
    xs = x_sample.reshape(Bs * Ls, D_MODEL)
    rows = lambda a: a.reshape(depth, 1, -1)
    norm_mix_rows, norm_ffn_rows = rows(norm_mix_g), rows(norm_ffn_g)
    diff_g_row, diff_g_col = rows(diff_norm_g), diff_norm_g.reshape(depth, -1, 1)
    w_in_bf, w_out_bf, w_gu_bf, w_d_bf = (w.astype(BF16) for w in (w_in, w_out, w_gate_up, w_down))
    mixer_w = (_block_diag(pool_w).astype(BF16), rows(pool_scale), conv_dw, rows(conv_dw_b), rows(conv_ln_g),
               rows(conv_ln_b), conv_pw.astype(BF16))
    dense_w = (w_out_bf, norm_ffn_rows, w_gu_bf, w_d_bf, final_norm_g.reshape(1, -1))
    hist_pool_s, hist_conv_s = _pad_hist(state_pool), _pad_hist(state_conv)
    cache_kt = jnp.transpose(cache_k, (0, 1, 3, 4, 5, 2)).reshape(depth, Bs, ATTN_WIDTH, past_len)
    cache_v4 = cache_v.reshape(depth, Bs, past_len * N_HEADS, V_DIM)
    kv_stack = depth
    pp, cp = [], []
    ks_, vs_, ps_, cs_ = [], [], [], []
    for l in range(depth):
        lam_init = 0.8 - 0.6 * math.exp(-0.3 * l)
        final = l == depth - 1

        upool, q, k, kb, v, vb, uconv = _prompt_in_proj(xp, norm_mix_rows, w_in_bf, cos_p, sin_p, IN_PROJ_TILE, S,
                                                        kv_stack, l)
        kv_stack = (k, v)
        o = _prompt_attention(lambda_qk, diff_g_col, q.reshape(B, S, -1), kb.reshape(B, S, -1),
                              vb, lam_init, l)
        xp = _prompt_dense(xp, o.reshape(B * S, -1), upool, uconv, zero_pool, zero_conv, mixer_w, dense_w,
                           ROW_TILE, S, final, l)
        pp.append(upool.reshape(B, S, -1)[:, S - POOL_HIST:])
        cp.append(uconv.reshape(B, S, -1)[:, S - CONV_HIST:])

        upool, q, k, kb, v, vb, uconv = _sample_in_proj(xs, norm_mix_rows, w_in_bf, cos_s, sin_s, l)
        o = _sample_attention(lambda_qk, diff_g_row, q.reshape(Bs, Ls, -1), cache_kt, cache_v4,
                              kb.reshape(Bs, Ls, -1), vb.reshape(Bs, Ls, -1), lam_init, l)
        ypool, yconv = _sample_mixers(upool, uconv, hist_pool_s, hist_conv_s, mixer_w, Ls, past_len, l)
        xs = _sample_dense(xs, ypool, o.reshape(Bs * Ls, -1), yconv, *dense_w, Bs * Ls, final, l)
        ks_.append(k.reshape(Bs, Ls, N_HEADS, 2, HEAD_DIM))
        vs_.append(v.reshape(Bs, Ls, N_HEADS, V_DIM))
        ps_.append(upool.reshape(Bs, Ls, -1)[:, Ls - POOL_HIST:])
        cs_.append(uconv.reshape(Bs, Ls, -1)[:, Ls - CONV_HIST:])

    kt_all, v4_all = kv_stack
    new_k = jnp.transpose(kt_all.reshape(depth, B, N_HEADS, 2, HEAD_DIM, S), (0, 1, 5, 2, 3, 4))
    new_v = v4_all.reshape(depth, B, S, N_HEADS, V_DIM)
    return (xp.reshape(B, S, D_MODEL), xs.reshape(Bs, Ls, D_MODEL),
            new_k, new_v, jnp.stack(pp), jnp.stack(cp),
            jnp.stack(ks_), jnp.stack(vs_), jnp.stack(ps_), jnp.stack(cs_))
```

```python
import functools
import math

import jax
import jax.numpy as jnp
from jax import lax
from jax.experimental import pallas as pl
from jax.experimental.pallas import tpu as pltpu

D_MODEL = 1024
CHUNK = 64
POOL_WIDTH = D_MODEL // 4
POOL_WINDOWS = (2, 4, 8, 16)
POOL_GROUP_DIM = POOL_WIDTH // len(POOL_WINDOWS)
POOL_HIST = max(POOL_WINDOWS) - 1
ATTN_WIDTH = D_MODEL // 2
N_HEADS = 4
HEAD_DIM = ATTN_WIDTH // (2 * N_HEADS)
V_DIM = 2 * HEAD_DIM
ROPE_THETA = 10000.0
CONV_WIDTH = D_MODEL // 4
CONV_K = 31
CONV_HIST = CONV_K - 1
MIX_WIDTH = POOL_WIDTH + ATTN_WIDTH + CONV_WIDTH
IN_WIDTH = POOL_WIDTH + 3 * ATTN_WIDTH + 2 * CONV_WIDTH
D_FF = ((-(-8 * D_MODEL // 3) + 255) // 256) * 256
EPS = 1e-6

LANES = 128
HALO = 32
ROW_TILE = 512
IN_PROJ_TILE = 1024
ATTN_TILE = 256
SAMPLE_KEY_TILE = 2048
KEY_TILES_PER_BLOCK = 4
Q_TILES_PER_STEP = 8
HEADS_PER_STEP = 4
VMEM_LIMIT = 56 * 1024 * 1024

F32 = jnp.float32
BF16 = jnp.bfloat16


def _rms(x, g):
    return x * lax.rsqrt(jnp.mean(x * x, axis=-1, keepdims=True) + EPS) * g


def _const_spec(shape):
    return pl.BlockSpec(shape, lambda *_: (0,) * len(shape), pipeline_mode=pl.Buffered(1))


def _layer_spec(shape, layer):
    return pl.BlockSpec((None,) + shape, lambda *_: (layer,) + (0,) * len(shape), pipeline_mode=pl.Buffered(1))


O_Q = POOL_WIDTH
O_K = O_Q + ATTN_WIDTH
O_V = O_K + ATTN_WIDTH
O_A = O_V + ATTN_WIDTH
O_B = O_A + CONV_WIDTH


def _qkv_epilogue(z, cos_ref, sin_ref, q_ref, k_ref, kb_ref, v_ref, vb_ref, cache_layout):
    tm = z.shape[0]
    o1, o2, o3 = O_Q, O_K, O_V
    cos = cos_ref[...]
    sin = sin_ref[...]
    lane = lax.broadcasted_iota(jnp.int32, cos.shape, 1)
    first_half = (lane & (HEAD_DIM // 2)) == 0

    def rope(t):
        up = pltpu.roll(t, LANES - HEAD_DIM // 2, axis=1)
        down = pltpu.roll(t, HEAD_DIM // 2, axis=1)
        return t * cos + jnp.where(first_half, up, down) * sin

    scale = HEAD_DIM ** -0.5 * math.log2(math.e)
    for j in range(ATTN_WIDTH // LANES):
        sl = slice(j * LANES, (j + 1) * LANES)
        qj = rope(z[:, o1 + j * LANES:o1 + (j + 1) * LANES])
        kj = rope(z[:, o2 + j * LANES:o2 + (j + 1) * LANES])
        vj = z[:, o3 + j * LANES:o3 + (j + 1) * LANES]
        q_ref[:, sl] = (qj * scale).astype(BF16)
        kb_ref[:, sl] = kj.astype(BF16)
        if cache_layout:
            k_ref[sl, :] = kj.T
            v_ref[pl.ds(j, tm, stride=N_HEADS), :] = vj
            vb_ref[sl, :] = vj.T.astype(BF16)
        else:
            k_ref[:, sl] = kj
            v_ref[:, sl] = vj
            vb_ref[:, sl] = vj.astype(BF16)


def _sample_in_proj_kernel(x_ref, g_ref, w_ref, cos_ref, sin_ref,
                           upool_ref, q_ref, k_ref, kb_ref, v_ref, vb_ref, uconv_ref):
    h = _rms(x_ref[...], g_ref[...]).astype(BF16)
    z = jnp.dot(h, w_ref[...], preferred_element_type=F32)
    upool_ref[...] = z[:, :O_Q]
    uconv_ref[...] = z[:, O_A:O_B] * jax.nn.sigmoid(z[:, O_B:])
    _qkv_epilogue(z, cos_ref, sin_ref, q_ref, k_ref, kb_ref, v_ref, vb_ref, False)


def _sample_in_proj(x, g, w_bf, cos_tab, sin_tab, layer):
    n = x.shape[0]
    full = lambda w: pl.BlockSpec((n, w), lambda i: (0, 0))
    flat = lambda dt: jax.ShapeDtypeStruct((n, ATTN_WIDTH), dt)
    return pl.pallas_call(
        _sample_in_proj_kernel,
        grid=(1,),
        in_specs=[full(D_MODEL), _layer_spec((1, D_MODEL), layer), _layer_spec((D_MODEL, IN_WIDTH), layer),
                  full(LANES), full(LANES)],
        out_specs=[full(POOL_WIDTH)] + [full(ATTN_WIDTH)] * 5 + [full(CONV_WIDTH)],
        out_shape=[jax.ShapeDtypeStruct((n, POOL_WIDTH), F32), flat(BF16), flat(F32), flat(BF16), flat(F32),
                   flat(BF16), jax.ShapeDtypeStruct((n, CONV_WIDTH), F32)],
        compiler_params=pltpu.CompilerParams(dimension_semantics=("arbitrary",),
                                             vmem_limit_bytes=VMEM_LIMIT),
        name="sample_in_proj",
    )(x, g, w_bf, cos_tab, sin_tab)


def _prompt_in_proj_kernel(*refs, n_unused):
    x_ref, g_ref, w_ref, cos_ref, sin_ref = refs[:5]
    upool_ref, q_ref, k_ref, kb_ref, v_ref, vb_ref, uconv_ref = refs[5 + n_unused:]
    h = _rms(x_ref[...], g_ref[...]).astype(BF16)
    z = jnp.dot(h, w_ref[...], preferred_element_type=F32)
    upool_ref[...] = z[:, :O_Q]
    uconv_ref[...] = z[:, O_A:O_B] * jax.nn.sigmoid(z[:, O_B:])
    _qkv_epilogue(z, cos_ref, sin_ref, q_ref, k_ref, kb_ref, v_ref, vb_ref, True)


def _prompt_in_proj(x, g, w_bf, cos_tab, sin_tab, tm, seq_len, kv_stack, layer):
    n = x.shape[0]
    tps = seq_len // tm
    batch = n // seq_len
    row = lambda w: pl.BlockSpec((tm, w), lambda i: (i, 0))
    tab = pl.BlockSpec((tm, LANES), lambda i: (i % tps, 0))
    flat = lambda dt, w=ATTN_WIDTH: jax.ShapeDtypeStruct((n, w), dt)
    extra_in, extra_specs, aliases = [], [], {}
    if isinstance(kv_stack, int):
        k_shape = jax.ShapeDtypeStruct((kv_stack, batch, ATTN_WIDTH, seq_len), F32)
        v_shape = jax.ShapeDtypeStruct((kv_stack, batch, N_HEADS * seq_len, V_DIM), F32)
    else:
        k_shape, v_shape = (jax.ShapeDtypeStruct(a.shape, a.dtype) for a in kv_stack)
        extra_in = list(kv_stack)
        extra_specs = [pl.BlockSpec(memory_space=pl.ANY)] * 2
        aliases = {5: 2, 6: 4}
    k_spec = pl.BlockSpec((None, None, ATTN_WIDTH, tm), lambda i: (layer, i // tps, 0, i % tps))
    v_spec = pl.BlockSpec((None, None, N_HEADS * tm, V_DIM), lambda i: (layer, i // tps, i % tps, 0))
    vt_spec = pl.BlockSpec((None, ATTN_WIDTH, tm), lambda i: (i // tps, 0, i % tps))
    return pl.pallas_call(
        functools.partial(_prompt_in_proj_kernel, n_unused=len(extra_in)),
        grid=(n // tm,),
        in_specs=[row(D_MODEL), _layer_spec((1, D_MODEL), layer), _layer_spec((D_MODEL, IN_WIDTH), layer), tab, tab]
        + extra_specs,
        out_specs=[row(POOL_WIDTH), row(ATTN_WIDTH), k_spec, row(ATTN_WIDTH), v_spec, vt_spec, row(CONV_WIDTH)],
        out_shape=[flat(F32, POOL_WIDTH), flat(BF16), k_shape, flat(BF16), v_shape,
                   jax.ShapeDtypeStruct((batch, ATTN_WIDTH, seq_len), BF16), flat(F32, CONV_WIDTH)],
        input_output_aliases=aliases,
        compiler_params=pltpu.CompilerParams(dimension_semantics=("arbitrary",),
                                             vmem_limit_bytes=VMEM_LIMIT),
        name="prompt_in_proj",
    )(x, g, w_bf, cos_tab, sin_tab, *extra_in)


def _lambda_scalar(lq, lam_init):
    a = jnp.sum(lq[0:1, :] * lq[1:2, :], axis=-1, keepdims=True)
    b = jnp.sum(lq[2:3, :] * lq[3:4, :], axis=-1, keepdims=True)
    return jnp.exp(a) - jnp.exp(b) + lam_init


def _stack_maps(q):
    lane = lax.broadcasted_iota(jnp.int32, q.shape, 1)
    zero = jnp.zeros_like(q)
    return jnp.concatenate([jnp.where(lane < HEAD_DIM, q, zero), jnp.where(lane >= HEAD_DIM, q, zero)], axis=0)


ONES_ROWS = 16


def _flash_scores(qq, kt):
    return lax.dot_general(kt, qq, (((1,), (1,)), ((), ())), preferred_element_type=F32)


def _flash_update(carry, s, vt_ones, p_stage):
    m, acc = carry
    m_new = jnp.maximum(m, jnp.max(s, axis=0, keepdims=True))
    alpha = jnp.exp2(m - m_new)
    p_stage[...] = jnp.exp2(s - m_new).astype(BF16)
    return m_new, alpha * acc + jnp.dot(vt_ones, p_stage[...], preferred_element_type=F32)


def _flash_init(cols):
    return jnp.full((1, cols), -jnp.inf, F32), jnp.zeros((V_DIM + ONES_ROWS, cols), F32)


def _attn_finish(carry, lam, g_col, lam_init, tq):
    _, acc = carry
    scaled = acc[:V_DIM] * (1.0 / acc[V_DIM:V_DIM + 1])
    o = scaled[:, :tq] - lam * scaled[:, tq:]
    o = o * lax.rsqrt(jnp.mean(o * o, axis=0, keepdims=True) + EPS) * g_col
    return (o * (1.0 - lam_init)).T


def _mask_diagonal_tile(s):
    t = s.shape[0]
    q_chunk = (lax.broadcasted_iota(jnp.int32, (1, s.shape[1]), 1) % t) // CHUNK
    blocks = [jnp.where(q_chunk >= a, s[a * CHUNK:(a + 1) * CHUNK, :], -1e30) for a in range(t // CHUNK)]
    return jnp.concatenate(blocks, axis=0)


def _prompt_attn_kernel(lq_ref, g_ref, q_ref, k_ref, vt_ref, o_ref, qq_ref, p_ref, *, lam_init):
    t = ATTN_TILE
    heads = [slice(h * LANES, (h + 1) * LANES) for h in range(HEADS_PER_STEP)]
    ones = jnp.ones((ONES_ROWS, t), BF16)
    lam = _lambda_scalar(lq_ref[...], lam_init)
    q_tiles = q_ref.shape[0] // t

    def one_query_tile(sub, _):
        qi = pl.program_id(2) * q_tiles + sub
        q_rows = pl.ds(pl.multiple_of(sub * t, t), t)
        for h, c in enumerate(heads):
            qq_ref[h] = _stack_maps(q_ref[q_rows, c])
        stage0 = jnp.minimum(qi, 0)

        def steps(tiles, carry):
            rows = [pl.ds(pl.multiple_of(j * t, t), t) for j, _ in tiles]
            s = [[_flash_scores(qq_ref[h], k_ref[r, c]) for h, c in enumerate(heads)] for r in rows]
            for ti, (_, diagonal) in enumerate(tiles):
                if diagonal:
                    s[ti] = [_mask_diagonal_tile(x) for x in s[ti]]
                carry = tuple(
                    _flash_update(carry[h], s[ti][h], jnp.concatenate([vt_ref[c, rows[ti]], ones], axis=0),
                                  p_ref.at[stage0 + ti * HEADS_PER_STEP + h])
                    for h, c in enumerate(heads))
            return carry

        nb = min(KEY_TILES_PER_BLOCK, k_ref.shape[0] // t)
        carry = (_flash_init(2 * t),) * HEADS_PER_STEP
        if nb < k_ref.shape[0] // t:
            carry = lax.fori_loop(0, qi // nb, lambda p, c: steps([(nb * p + i, False) for i in range(nb)], c),
                                  carry)
        left = qi % nb

        def last_block(n_full):
            return lambda c: steps([(qi - n_full + i, False) for i in range(n_full)] + [(qi, True)], c)

        carry = lax.switch(left, [last_block(n) for n in range(nb)], carry)
        for h, c in enumerate(heads):
            o_ref[q_rows, c] = _attn_finish(carry[h], lam, g_ref[...], lam_init, t).astype(BF16)
        return 0

    lax.fori_loop(0, q_tiles, one_query_tile, 0)


def _prompt_attention(lq, g, q, kb, vtb, lam_init, layer):
    b, s, _ = q.shape
    t = ATTN_TILE
    w = HEADS_PER_STEP * LANES
    tq = Q_TILES_PER_STEP * t
    return pl.pallas_call(
        functools.partial(_prompt_attn_kernel, lam_init=lam_init),
        grid=(b, N_HEADS // HEADS_PER_STEP, s // tq),
        in_specs=[
            _layer_spec((4, HEAD_DIM), layer),
            _layer_spec((V_DIM, 1), layer),
            pl.BlockSpec((None, tq, w), lambda b_, h, i: (b_, i, h)),
            pl.BlockSpec((None, s, w), lambda b_, h, i: (b_, 0, h)),
            pl.BlockSpec((None, w, s), lambda b_, h, i: (b_, h, 0)),
        ],
        out_specs=pl.BlockSpec((None, tq, w), lambda b_, h, i: (b_, i, h)),
        out_shape=jax.ShapeDtypeStruct((b, s, ATTN_WIDTH), BF16),
        scratch_shapes=[pltpu.VMEM((HEADS_PER_STEP, 2 * t, LANES), BF16),
                        pltpu.VMEM((KEY_TILES_PER_BLOCK * HEADS_PER_STEP, t, 2 * t), BF16)],
        compiler_params=pltpu.CompilerParams(dimension_semantics=("arbitrary",) * 3,
                                             vmem_limit_bytes=VMEM_LIMIT),
        name="prompt_attention",
    )(lq, g, q, kb, vtb)


def _row_scores(qq, kt, *, keys_on_rows):
    dims = (((1,), (1,)), ((), ())) if keys_on_rows else (((1,), (0,)), ((), ()))
    return lax.dot_general(qq, kt, dims, preferred_element_type=F32)


def _row_flash_update(carry, s, vt):
    m, l, acc = carry
    m_new = jnp.maximum(m, jnp.max(s, axis=-1, keepdims=True))
    alpha = jnp.exp2(m - m_new)
    p = jnp.exp2(s - m_new)
    l = alpha * l + jnp.sum(p, axis=-1, keepdims=True)
    acc = alpha * acc + jnp.dot(p.astype(BF16), vt, preferred_element_type=F32)
    return m_new, l, acc


def _sample_attn_kernel(lq_ref, g_ref, q_ref, ck_ref, cv_ref, k_ref, v_ref, o_ref, *, lam_init, past_len):
    t = SAMPLE_KEY_TILE
    tq = q_ref.shape[0]
    heads = [slice(h * LANES, (h + 1) * LANES) for h in range(N_HEADS)]
    qq = [_stack_maps(q_ref[:, c]) for c in heads]

    def body(j, carry):
        cols = pl.ds(pl.multiple_of(j * t, t), t)
        s = [_row_scores(qq[h], ck_ref[c, cols].astype(BF16), keys_on_rows=False) for h, c in enumerate(heads)]
        return tuple(
            _row_flash_update(carry[h], s[h], cv_ref[pl.ds(j * (t * N_HEADS) + h, t, stride=N_HEADS), :].astype(BF16))
            for h in range(N_HEADS))

    init = (jnp.full((2 * tq, 1), -jnp.inf, F32), jnp.zeros((2 * tq, 1), F32), jnp.zeros((2 * tq, V_DIM), F32))
    carry = lax.fori_loop(0, past_len // t, body, (init,) * N_HEADS)
    lam = _lambda_scalar(lq_ref[...], lam_init)
    for h, c in enumerate(heads):
        s = _row_scores(qq[h], k_ref[:, c], keys_on_rows=True)
        _, l, acc = _row_flash_update(carry[h], s, v_ref[:, c])
        o = acc[:tq] / l[:tq] - lam * (acc[tq:] / l[tq:])
        o_ref[:, c] = (_rms(o, g_ref[...]) * (1.0 - lam_init)).astype(BF16)


def _sample_attention(lq, g, q, cache_kt, cache_v4, kb, vb, lam_init, layer):
    b, ls, _ = q.shape
    past_len = cache_kt.shape[-1]
    assert past_len % SAMPLE_KEY_TILE == 0
    return pl.pallas_call(
        functools.partial(_sample_attn_kernel, lam_init=lam_init, past_len=past_len),
        grid=(b,),
        in_specs=[
            _layer_spec((4, HEAD_DIM), layer),
            _layer_spec((1, V_DIM), layer),
            pl.BlockSpec((None, ls, ATTN_WIDTH), lambda b_: (b_, 0, 0)),
            pl.BlockSpec((None, None, ATTN_WIDTH, past_len), lambda b_: (layer, b_, 0, 0)),
            pl.BlockSpec((None, None, N_HEADS * past_len, V_DIM), lambda b_: (layer, b_, 0, 0)),
            pl.BlockSpec((None, ls, ATTN_WIDTH), lambda b_: (b_, 0, 0)),
            pl.BlockSpec((None, ls, ATTN_WIDTH), lambda b_: (b_, 0, 0)),
        ],
        out_specs=pl.BlockSpec((None, ls, ATTN_WIDTH), lambda b_: (b_, 0, 0)),
        out_shape=jax.ShapeDtypeStruct((b, ls, ATTN_WIDTH), BF16),
        compiler_params=pltpu.CompilerParams(dimension_semantics=("arbitrary",),
                                             vmem_limit_bytes=VMEM_LIMIT),
        name="sample_attention",
    )(lq, g, q, cache_kt, cache_v4, kb, vb)


def _pool_mixer(ext_p, pool_w_ref, pool_scale_ref, tm, pos0):
    assert POOL_WINDOWS == (2, 4, 8, 16) and HALO == 32
    e = ext_p[...]
    n = HALO + tm
    s2 = e[8:] + e[7:n - 1]
    s4 = s2[8:] + s2[6:n - 10]
    s8 = s4[8:] + s4[4:n - 20]
    sums = {2: s2[24:], 4: s4[16:], 8: s8[8:], 16: s8[8:] + s8[:n - 32]}
    u = e[HALO:]
    lane = lax.broadcasted_iota(jnp.int32, (tm, POOL_WIDTH), 1)
    group = lane // POOL_GROUP_DIM
    win = sums[POOL_WINDOWS[-1]]
    width = jnp.full((tm, POOL_WIDTH), POOL_WINDOWS[-1], jnp.int32)
    for gi in range(len(POOL_WINDOWS) - 2, -1, -1):
        win = jnp.where(group == gi, sums[POOL_WINDOWS[gi]], win)
        width = jnp.where(group == gi, POOL_WINDOWS[gi], width)
    pos = pos0 + lax.broadcasted_iota(jnp.int32, (tm, POOL_WIDTH), 0)
    count = jnp.minimum(pos + 1, width).astype(F32)
    d = (win / count - u).astype(BF16)
    return jnp.dot(d, pool_w_ref[...], preferred_element_type=F32) * pool_scale_ref[...]


CONV_ROW_BLOCK = 64


def _conv_mixer_steps(ext_c, shift_c, conv_out, dw_ref, dw_b_ref, ln_g_ref, ln_b_ref, pw_ref, tm, store):
    first_row = HALO - CONV_HIST
    blk = min(tm, CONV_ROW_BLOCK)

    def shift(residues):
        for r in residues:
            shift_c[r, 0:HALO - 8 + tm, :] = ext_c[pl.ds(r, HALO - 8 + tm), :]

    def taps(rb):
        yb = jnp.zeros((blk, CONV_WIDTH), F32)
        for k in range(CONV_K):
            r, a = (first_row + k) % 8, (first_row + k) // 8
            rows = pl.ds(8 * a + rb * blk, blk)
            src = ext_c[rows, :] if r == 0 else shift_c[r, rows, :]
            yb = yb + src * dw_ref[k:k + 1, :]
        conv_out[rb * blk:(rb + 1) * blk, :] = yb

    def finish():
        y = conv_out[...] + dw_b_ref[...]
        mu = jnp.mean(y, axis=-1, keepdims=True)
        yc = y - mu
        yn = yc * lax.rsqrt(jnp.mean(yc * yc, axis=-1, keepdims=True) + EPS) * ln_g_ref[...] + ln_b_ref[...]
        act = (yn * jax.nn.sigmoid(yn)).astype(BF16)
        store(jnp.dot(act, pw_ref[...], preferred_element_type=F32))

    steps = [functools.partial(shift, rs) for rs in ((1, 2), (3, 4, 5), (6, 7))]
    steps += [functools.partial(taps, rb) for rb in range(tm // blk)]
    return steps + [finish]


def _conv_mixer(ext_c, shift_c, conv_out, dw_ref, dw_b_ref, ln_g_ref, ln_b_ref, pw_ref, tm):
    out = []
    for step in _conv_mixer_steps(ext_c, shift_c, conv_out, dw_ref, dw_b_ref, ln_g_ref, ln_b_ref, pw_ref, tm,
                                  out.append):
        step()
    return out[0]


def _mixer_scratch(tm):
    return [pltpu.VMEM((HALO + tm, POOL_WIDTH), F32), pltpu.VMEM((HALO + tm, CONV_WIDTH), F32),
            pltpu.VMEM((8, HALO + tm, CONV_WIDTH), F32), pltpu.VMEM((tm, CONV_WIDTH), F32)]


def _mixer_weight_specs(layer):
    shapes = [(POOL_WIDTH, POOL_WIDTH), (1, POOL_WIDTH), (CONV_K, CONV_WIDTH), (1, CONV_WIDTH), (1, CONV_WIDTH),
              (1, CONV_WIDTH), (CONV_WIDTH, CONV_WIDTH)]
    return [_layer_spec(shape, layer) for shape in shapes]


def _sample_mixers_kernel(up_ref, up_hist_ref, uc_ref, uc_hist_ref,
                          pool_w_ref, pool_scale_ref, dw_ref, dw_b_ref, ln_g_ref, ln_b_ref, pw_ref,
                          ypool_ref, yconv_ref, ext_p, ext_c, shift_c, conv_out, *, pos_base):
    tm = up_ref.shape[0]
    ext_p[0:HALO, :] = up_hist_ref[...]
    ext_p[HALO:, :] = up_ref[...]
    ext_c[0:HALO, :] = uc_hist_ref[...]
    ext_c[HALO:, :] = uc_ref[...]
    ypool_ref[...] = _pool_mixer(ext_p, pool_w_ref, pool_scale_ref, tm, pos_base).astype(BF16)
    yconv_ref[...] = _conv_mixer(ext_c, shift_c, conv_out, dw_ref, dw_b_ref, ln_g_ref, ln_b_ref, pw_ref,
                                 tm).astype(BF16)


def _sample_mixers(upool, uconv, hist_pool, hist_conv, mixer_w, seq_len, pos_base, layer):
    n = upool.shape[0]
    cur = pl.BlockSpec((seq_len, POOL_WIDTH), lambda i: (i, 0))
    hist = pl.BlockSpec((None, None, HALO, POOL_WIDTH), lambda i: (layer, i, 0, 0))
    return pl.pallas_call(
        functools.partial(_sample_mixers_kernel, pos_base=pos_base),
        grid=(n // seq_len,),
        in_specs=[cur, hist, cur, hist] + _mixer_weight_specs(layer),
        out_specs=[cur, cur],
        out_shape=[jax.ShapeDtypeStruct((n, POOL_WIDTH), BF16), jax.ShapeDtypeStruct((n, CONV_WIDTH), BF16)],
        scratch_shapes=_mixer_scratch(seq_len),
        compiler_params=pltpu.CompilerParams(dimension_semantics=("arbitrary",),
                                             vmem_limit_bytes=VMEM_LIMIT),
        name="sample_mixers",
    )(upool, hist_pool, uconv, hist_conv, *mixer_w)


def _dense_math(x, ypool, o, yconv, wout_ref, g_ref, wgu_ref, wd_ref, fg_ref, final):
    o1 = POOL_WIDTH
    o2 = o1 + ATTN_WIDTH
    mix = (jnp.dot(ypool, wout_ref[0:o1, :], preferred_element_type=F32)
           + jnp.dot(o, wout_ref[o1:o2, :], preferred_element_type=F32)
           + jnp.dot(yconv, wout_ref[o2:, :], preferred_element_type=F32))
    x1 = x + mix
    h = _rms(x1, g_ref[...]).astype(BF16)
    gu = jnp.dot(h, wgu_ref[...], preferred_element_type=F32)
    gate = gu[:, :D_FF]
    a = (gate * jax.nn.sigmoid(gate) * gu[:, D_FF:]).astype(BF16)
    x2 = x1 + jnp.dot(a, wd_ref[...], preferred_element_type=F32)
    return _rms(x2, fg_ref[...]) if final else x2


def _dense_weight_specs(layer):
    shapes = [(MIX_WIDTH, D_MODEL), (1, D_MODEL), (D_MODEL, 2 * D_FF), (D_FF, D_MODEL)]
    return [_layer_spec(shape, layer) for shape in shapes] + [_const_spec((1, D_MODEL))]


def _prompt_dense_kernel(x_ref, o_ref, up0_ref, uc0_ref, up_ref, uc_ref, hist_p_ref, hist_c_ref,
                         pool_w_ref, pool_scale_ref, dw_ref, dw_b_ref, ln_g_ref, ln_b_ref, pw_ref,
                         wout_ref, g_ref, wgu_ref, wd_ref, fg_ref, out_ref,
                         ext_p, ext_c, shift_c, conv_out, y_pool, y_conv, *, final, tiles_per_seq):
    tm = x_ref.shape[0]
    i = pl.program_id(0)

    def mixer_steps(up_ref_, uc_ref_, tile, slot, first):
        def fill():
            if first is True:
                ext_p[0:HALO, :] = hist_p_ref[...]
                ext_c[0:HALO, :] = hist_c_ref[...]
            else:
                ext_p[0:HALO, :] = jnp.where(first, hist_p_ref[...], ext_p[tm:, :])
                ext_c[0:HALO, :] = jnp.where(first, hist_c_ref[...], ext_c[tm:, :])
            ext_p[HALO:, :] = up_ref_[...]
            ext_c[HALO:, :] = uc_ref_[...]

        def pool():
            pos0 = (tile % tiles_per_seq) * tm
            y_pool[slot] = _pool_mixer(ext_p, pool_w_ref, pool_scale_ref, tm, pos0).astype(BF16)

        def store_conv(y):
            y_conv[slot] = y.astype(BF16)

        return [fill] + _conv_mixer_steps(ext_c, shift_c, conv_out, dw_ref, dw_b_ref, ln_g_ref, ln_b_ref, pw_ref,
                                          tm, store_conv) + [pool]

    @pl.when(i == 0)
    def _():
        for step in mixer_steps(up0_ref, uc0_ref, 0, 0, True):
            step()

    slot = i % 2
    nxt = jnp.minimum(i + 1, pl.num_programs(0) - 1)
    out_ref[...] = _dense_math(x_ref[...], y_pool[slot], o_ref[...], y_conv[slot],
                               wout_ref, g_ref, wgu_ref, wd_ref, fg_ref, final)
    for step in mixer_steps(up_ref, uc_ref, nxt, 1 - slot, nxt % tiles_per_seq == 0):
        step()


def _prompt_dense(x, o, upool, uconv, hist_pool, hist_conv, mixer_w, dense_w, tm, seq_len, final, layer):
    n = x.shape[0]
    tps = seq_len // tm
    last = n // tm - 1
    row = lambda w: pl.BlockSpec((tm, w), lambda i: (i, 0))
    first_tile = lambda w: pl.BlockSpec((tm, w), lambda i: (0, 0))
    next_tile = lambda w: pl.BlockSpec((tm, w), lambda i: (jnp.minimum(i + 1, last), 0))
    next_seq = lambda w: pl.BlockSpec((None, HALO, w), lambda i: (jnp.minimum(i + 1, last) // tps, 0, 0))
    return pl.pallas_call(
        functools.partial(_prompt_dense_kernel, final=final, tiles_per_seq=tps),
        grid=(n // tm,),
        in_specs=[row(D_MODEL), row(ATTN_WIDTH), first_tile(POOL_WIDTH), first_tile(CONV_WIDTH),
                  next_tile(POOL_WIDTH), next_tile(CONV_WIDTH), next_seq(POOL_WIDTH), next_seq(CONV_WIDTH)]
        + _mixer_weight_specs(layer) + _dense_weight_specs(layer),
        out_specs=row(D_MODEL),
        out_shape=jax.ShapeDtypeStruct((n, D_MODEL), F32),
        scratch_shapes=_mixer_scratch(tm) + [pltpu.VMEM((2, tm, POOL_WIDTH), BF16),
                                             pltpu.VMEM((2, tm, CONV_WIDTH), BF16)],
        compiler_params=pltpu.CompilerParams(dimension_semantics=("arbitrary",),
                                             vmem_limit_bytes=VMEM_LIMIT),
        name="prompt_dense",
    )(x, o, upool, uconv, upool, uconv, hist_pool, hist_conv, *mixer_w, *dense_w)


def _sample_dense_kernel(x_ref, yp_ref, o_ref, yc_ref, wout_ref, g_ref, wgu_ref, wd_ref, fg_ref, out_ref, *, final):
    out_ref[...] = _dense_math(x_ref[...], yp_ref[...], o_ref[...], yc_ref[...],
                               wout_ref, g_ref, wgu_ref, wd_ref, fg_ref, final)


def _sample_dense(x, ypool, o, yconv, wout_bf, g, wgu_bf, wd_bf, fg, tm, final, layer):
    n = x.shape[0]
    row = lambda w: pl.BlockSpec((tm, w), lambda i: (i, 0))
    return pl.pallas_call(
        functools.partial(_sample_dense_kernel, final=final),
        grid=(n // tm,),
        in_specs=[row(D_MODEL), row(POOL_WIDTH), row(ATTN_WIDTH), row(CONV_WIDTH)] + _dense_weight_specs(layer),
        out_specs=row(D_MODEL),
        out_shape=jax.ShapeDtypeStruct((n, D_MODEL), F32),
        compiler_params=pltpu.CompilerParams(dimension_semantics=("arbitrary",),
                                             vmem_limit_bytes=VMEM_LIMIT),
        name="sample_dense",
    )(x, ypool, o, yconv, wout_bf, g, wgu_bf, wd_bf, fg)


def _rope_tables(pos):
    half = HEAD_DIM // 2
    inv = ROPE_THETA ** (-jnp.arange(half, dtype=F32) / half)
    ang = pos.astype(F32)[:, None] * inv[None, :]
    reps = LANES // half
    cos = jnp.tile(jnp.cos(ang), (1, reps))
    sign = jnp.tile(jnp.concatenate([-jnp.ones((half,), F32), jnp.ones((half,), F32)]), LANES // HEAD_DIM)
    sin = jnp.tile(jnp.sin(ang), (1, reps)) * sign[None, :]
    return cos, sin


def _block_diag(pool_w):
    depth, g, d, _ = pool_w.shape
    eye = jnp.eye(g, dtype=pool_w.dtype)
    return (pool_w[:, :, :, None, :] * eye[None, :, None, :, None]).reshape(depth, g * d, g * d)


def _pad_hist(h):
    return jnp.pad(h, ((0, 0), (0, 0), (HALO - h.shape[2], 0), (0, 0)))


def kernel(x_prompt, x_sample, cache_k, cache_v, state_pool, state_conv, norm_mix_g, w_in, pool_w, pool_scale,
           lambda_qk, diff_norm_g, conv_dw, conv_dw_b, conv_ln_g, conv_ln_b, conv_pw, w_out, norm_ffn_g,
           w_gate_up, w_down, final_norm_g):
    B, S, _ = x_prompt.shape
    Bs, Ls, _ = x_sample.shape
    depth = w_in.shape[0]
    past_len = cache_k.shape[2]
    assert S % (Q_TILES_PER_STEP * ATTN_TILE) == 0 and S % ROW_TILE == 0 and S % IN_PROJ_TILE == 0
    assert S // ROW_TILE >= 2
    assert (Bs * Ls) % 16 == 0 and Ls % 16 == 0 and Ls >= CONV_HIST

    cos_p, sin_p = _rope_tables(jnp.arange(S, dtype=jnp.int32))
    cos_s, sin_s = _rope_tables(past_len + jnp.arange(Ls, dtype=jnp.int32))
    cos_s, sin_s = jnp.tile(cos_s, (Bs, 1)), jnp.tile(sin_s, (Bs, 1))
    zero_pool = jnp.zeros((B, HALO, POOL_WIDTH), F32)
    zero_conv = jnp.zeros((B, HALO, CONV_WIDTH), F32)

    xp = x_prompt.reshape(B * S, D_MODEL)
    xs = x_sample.reshape(Bs * Ls, D_MODEL)
    rows = lambda a: a.reshape(depth, 1, -1)
    norm_mix_rows, norm_ffn_rows = rows(norm_mix_g), rows(norm_ffn_g)
    diff_g_row, diff_g_col = rows(diff_norm_g), diff_norm_g.reshape(depth, -1, 1)
    w_in_bf, w_out_bf, w_gu_bf, w_d_bf = (w.astype(BF16) for w in (w_in, w_out, w_gate_up, w_down))
    mixer_w = (_block_diag(pool_w).astype(BF16), rows(pool_scale), conv_dw, rows(conv_dw_b), rows(conv_ln_g),
               rows(conv_ln_b), conv_pw.astype(BF16))
    dense_w = (w_out_bf, norm_ffn_rows, w_gu_bf, w_d_bf, final_norm_g.reshape(1, -1))
    hist_pool_s, hist_conv_s = _pad_hist(state_pool), _pad_hist(state_conv)
    cache_kt = jnp.transpose(cache_k, (0, 1, 3, 4, 5, 2)).reshape(depth, Bs, ATTN_WIDTH, past_len)
    cache_v4 = cache_v.reshape(depth, Bs, past_len * N_HEADS, V_DIM)
    kv_stack = depth
    pp, cp = [], []
    ks_, vs_, ps_, cs_ = [], [], [], []
    for l in range(depth):
        lam_init = 0.8 - 0.6 * math.exp(-0.3 * l)
        final = l == depth - 1

        upool, q, k, kb, v, vb, uconv = _prompt_in_proj(xp, norm_mix_rows, w_in_bf, cos_p, sin_p, IN_PROJ_TILE, S,
                                                        kv_stack, l)
        kv_stack = (k, v)
        o = _prompt_attention(lambda_qk, diff_g_col, q.reshape(B, S, -1), kb.reshape(B, S, -1),
                              vb, lam_init, l)
        xp = _prompt_dense(xp, o.reshape(B * S, -1), upool, uconv, zero_pool, zero_conv, mixer_w, dense_w,
                           ROW_TILE, S, final, l)
        pp.append(upool.reshape(B, S, -1)[:, S - POOL_HIST:])
        cp.append(uconv.reshape(B, S, -1)[:, S - CONV_HIST:])

        upool, q, k, kb, v, vb, uconv = _sample_in_proj(xs, norm_mix_rows, w_in_bf, cos_s, sin_s, l)
        o = _sample_attention(lambda_qk, diff_g_row, q.reshape(Bs, Ls, -1), cache_kt, cache_v4,
                              kb.reshape(Bs, Ls, -1), vb.reshape(Bs, Ls, -1), lam_init, l)
        ypool, yconv = _sample_mixers(upool, uconv, hist_pool_s, hist_conv_s, mixer_w, Ls, past_len, l)
        xs = _sample_dense(xs, ypool, o.reshape(Bs * Ls, -1), yconv, *dense_w, Bs * Ls, final, l)
        ks_.append(k.reshape(Bs, Ls, N_HEADS, 2, HEAD_DIM))
        vs_.append(v.reshape(Bs, Ls, N_HEADS, V_DIM))
        ps_.append(upool.reshape(Bs, Ls, -1)[:, Ls - POOL_HIST:])
        cs_.append(uconv.reshape(Bs, Ls, -1)[:, Ls - CONV_HIST:])

    kt_all, v4_all = kv_stack
    new_k = jnp.transpose(kt_all.reshape(depth, B, N_HEADS, 2, HEAD_DIM, S), (0, 1, 5, 2, 3, 4))
    new_v = v4_all.reshape(depth, B, S, N_HEADS, V_DIM)
    return (xp.reshape(B, S, D_MODEL), xs.reshape(Bs, Ls, D_MODEL),
            new_k, new_v, jnp.stack(pp), jnp.stack(cp),
            jnp.stack(ks_), jnp.stack(vs_), jnp.stack(ps_), jnp.stack(cs_))
```

```python
import functools
import math

import jax
import jax.numpy as jnp
from jax import lax
from jax.experimental import pallas as pl
from jax.experimental.pallas import tpu as pltpu

D_MODEL = 1024
CHUNK = 64
POOL_WIDTH = D_MODEL // 4
POOL_WINDOWS = (2, 4, 8, 16)
POOL_GROUP_DIM = POOL_WIDTH // len(POOL_WINDOWS)
POOL_HIST = max(POOL_WINDOWS) - 1
ATTN_WIDTH = D_MODEL // 2
N_HEADS = 4
HEAD_DIM = ATTN_WIDTH // (2 * N_HEADS)
V_DIM = 2 * HEAD_DIM
ROPE_THETA = 10000.0
CONV_WIDTH = D_MODEL // 4
CONV_K = 31
CONV_HIST = CONV_K - 1
MIX_WIDTH = POOL_WIDTH + ATTN_WIDTH + CONV_WIDTH
IN_WIDTH = POOL_WIDTH + 3 * ATTN_WIDTH + 2 * CONV_WIDTH
D_FF = ((-(-8 * D_MODEL // 3) + 255) // 256) * 256
EPS = 1e-6

LANES = 128
HALO = 32
ROW_TILE = 512
IN_PROJ_TILE = 1024
ATTN_TILE = 256
SAMPLE_KEY_TILE = 2048
KEY_TILES_PER_BLOCK = 4
Q_TILES_PER_STEP = 8
HEADS_PER_STEP = 4
VMEM_LIMIT = 56 * 1024 * 1024

F32 = jnp.float32
BF16 = jnp.bfloat16


def _rms(x, g):
    return x * lax.rsqrt(jnp.mean(x * x, axis=-1, keepdims=True) + EPS) * g


def _const_spec(shape):
    return pl.BlockSpec(shape, lambda *_: (0,) * len(shape), pipeline_mode=pl.Buffered(1))


def _layer_spec(shape, layer):
    return pl.BlockSpec((None,) + shape, lambda *_: (layer,) + (0,) * len(shape), pipeline_mode=pl.Buffered(1))


O_Q = POOL_WIDTH
O_K = O_Q + ATTN_WIDTH
O_V = O_K + ATTN_WIDTH
O_A = O_V + ATTN_WIDTH
O_B = O_A + CONV_WIDTH


def _qkv_epilogue(z, cos_ref, sin_ref, q_ref, k_ref, kb_ref, v_ref, vb_ref, cache_layout):
    tm = z.shape[0]
    o1, o2, o3 = O_Q, O_K, O_V
    cos = cos_ref[...]
    sin = sin_ref[...]
    lane = lax.broadcasted_iota(jnp.int32, cos.shape, 1)
    first_half = (lane & (HEAD_DIM // 2)) == 0

    def rope(t):
        up = pltpu.roll(t, LANES - HEAD_DIM // 2, axis=1)
        down = pltpu.roll(t, HEAD_DIM // 2, axis=1)
        return t * cos + jnp.where(first_half, up, down) * sin

    scale = HEAD_DIM ** -0.5 * math.log2(math.e)
    for j in range(ATTN_WIDTH // LANES):
        sl = slice(j * LANES, (j + 1) * LANES)
        qj = rope(z[:, o1 + j * LANES:o1 + (j + 1) * LANES])
        kj = rope(z[:, o2 + j * LANES:o2 + (j + 1) * LANES])
        vj = z[:, o3 + j * LANES:o3 + (j + 1) * LANES]
        q_ref[:, sl] = (qj * scale).astype(BF16)
        kb_ref[:, sl] = kj.astype(BF16)
        if cache_layout:
            k_ref[sl, :] = kj.T
            v_ref[pl.ds(j, tm, stride=N_HEADS), :] = vj
            vb_ref[sl, :] = vj.T.astype(BF16)
        else:
            k_ref[:, sl] = kj
            v_ref[:, sl] = vj
            vb_ref[:, sl] = vj.astype(BF16)


def _sample_in_proj_kernel(x_ref, g_ref, w_ref, cos_ref, sin_ref,
                           upool_ref, q_ref, k_ref, kb_ref, v_ref, vb_ref, uconv_ref):
    h = _rms(x_ref[...], g_ref[...]).astype(BF16)
    z = jnp.dot(h, w_ref[...], preferred_element_type=F32)
    upool_ref[...] = z[:, :O_Q]
    uconv_ref[...] = z[:, O_A:O_B] * jax.nn.sigmoid(z[:, O_B:])
    _qkv_epilogue(z, cos_ref, sin_ref, q_ref, k_ref, kb_ref, v_ref, vb_ref, False)


def _sample_in_proj(x, g, w_bf, cos_tab, sin_tab, layer):
    n = x.shape[0]
    full = lambda w: pl.BlockSpec((n, w), lambda i: (0, 0))
    flat = lambda dt: jax.ShapeDtypeStruct((n, ATTN_WIDTH), dt)
    return pl.pallas_call(
        _sample_in_proj_kernel,
        grid=(1,),
        in_specs=[full(D_MODEL), _layer_spec((1, D_MODEL), layer), _layer_spec((D_MODEL, IN_WIDTH), layer),
                  full(LANES), full(LANES)],
        out_specs=[full(POOL_WIDTH)] + [full(ATTN_WIDTH)] * 5 + [full(CONV_WIDTH)],
        out_shape=[jax.ShapeDtypeStruct((n, POOL_WIDTH), F32), flat(BF16), flat(F32), flat(BF16), flat(F32),
                   flat(BF16), jax.ShapeDtypeStruct((n, CONV_WIDTH), F32)],
        compiler_params=pltpu.CompilerParams(dimension_semantics=("arbitrary",),
                                             vmem_limit_bytes=VMEM_LIMIT),
        name="sample_in_proj",
    )(x, g, w_bf, cos_tab, sin_tab)


def _prompt_in_proj_kernel(*refs, n_unused):
    x_ref, g_ref, w_ref, cos_ref, sin_ref = refs[:5]
    upool_ref, q_ref, k_ref, kb_ref, v_ref, vb_ref, uconv_ref = refs[5 + n_unused:]
    h = _rms(x_ref[...], g_ref[...]).astype(BF16)
    z = jnp.dot(h, w_ref[...], preferred_element_type=F32)
    upool_ref[...] = z[:, :O_Q]
    uconv_ref[...] = z[:, O_A:O_B] * jax.nn.sigmoid(z[:, O_B:])
    _qkv_epilogue(z, cos_ref, sin_ref, q_ref, k_ref, kb_ref, v_ref, vb_ref, True)


def _prompt_in_proj(x, g, w_bf, cos_tab, sin_tab, tm, seq_len, kv_stack, layer):
    n = x.shape[0]
    tps = seq_len // tm
    batch = n // seq_len
    row = lambda w: pl.BlockSpec((tm, w), lambda i: (i, 0))
    tab = pl.BlockSpec((tm, LANES), lambda i: (i % tps, 0))
    flat = lambda dt, w=ATTN_WIDTH: jax.ShapeDtypeStruct((n, w), dt)
    extra_in, extra_specs, aliases = [], [], {}
    if isinstance(kv_stack, int):
        k_shape = jax.ShapeDtypeStruct((kv_stack, batch, ATTN_WIDTH, seq_len), F32)
        v_shape = jax.ShapeDtypeStruct((kv_stack, batch, N_HEADS * seq_len, V_DIM), F32)
    else:
        k_shape, v_shape = (jax.ShapeDtypeStruct(a.shape, a.dtype) for a in kv_stack)
        extra_in = list(kv_stack)
        extra_specs = [pl.BlockSpec(memory_space=pl.ANY)] * 2
        aliases = {5: 2, 6: 4}
    k_spec = pl.BlockSpec((None, None, ATTN_WIDTH, tm), lambda i: (layer, i // tps, 0, i % tps))
    v_spec = pl.BlockSpec((None, None, N_HEADS * tm, V_DIM), lambda i: (layer, i // tps, i % tps, 0))
    vt_spec = pl.BlockSpec((None, ATTN_WIDTH, tm), lambda i: (i // tps, 0, i % tps))
    return pl.pallas_call(
        functools.partial(_prompt_in_proj_kernel, n_unused=len(extra_in)),
        grid=(n // tm,),
        in_specs=[row(D_MODEL), _layer_spec((1, D_MODEL), layer), _layer_spec((D_MODEL, IN_WIDTH), layer), tab, tab]
        + extra_specs,
        out_specs=[row(POOL_WIDTH), row(ATTN_WIDTH), k_spec, row(ATTN_WIDTH), v_spec, vt_spec, row(CONV_WIDTH)],
        out_shape=[flat(F32, POOL_WIDTH), flat(BF16), k_shape, flat(BF16), v_shape,
                   jax.ShapeDtypeStruct((batch, ATTN_WIDTH, seq_len), BF16), flat(F32, CONV_WIDTH)],
        input_output_aliases=aliases,
        compiler_params=pltpu.CompilerParams(dimension_semantics=("arbitrary",),
                                             vmem_limit_bytes=VMEM_LIMIT),
        name="prompt_in_proj",
    )(x, g, w_bf, cos_tab, sin_tab, *extra_in)


def _lambda_scalar(lq, lam_init):
    a = jnp.sum(lq[0:1, :] * lq[1:2, :], axis=-1, keepdims=True)
    b = jnp.sum(lq[2:3, :] * lq[3:4, :], axis=-1, keepdims=True)
    return jnp.exp(a) - jnp.exp(b) + lam_init


def _stack_maps(q):
    lane = lax.broadcasted_iota(jnp.int32, q.shape, 1)
    zero = jnp.zeros_like(q)
    return jnp.concatenate([jnp.where(lane < HEAD_DIM, q, zero), jnp.where(lane >= HEAD_DIM, q, zero)], axis=0)


ONES_ROWS = 16


def _flash_scores(qq, kt):
    return lax.dot_general(kt, qq, (((1,), (1,)), ((), ())), preferred_element_type=F32)


def _flash_update(carry, s, vt_ones, p_stage):
    m, acc = carry
    m_new = jnp.maximum(m, jnp.max(s, axis=0, keepdims=True))
    alpha = jnp.exp2(m - m_new)
    p_stage[...] = jnp.exp2(s - m_new).astype(BF16)
    return m_new, alpha * acc + jnp.dot(vt_ones, p_stage[...], preferred_element_type=F32)


def _flash_init(cols):
    return jnp.full((1, cols), -jnp.inf, F32), jnp.zeros((V_DIM + ONES_ROWS, cols), F32)


def _attn_finish(carry, lam, g_col, lam_init, tq):
    _, acc = carry
    scaled = acc[:V_DIM] * (1.0 / acc[V_DIM:V_DIM + 1])
    o = scaled[:, :tq] - lam * scaled[:, tq:]
    o = o * lax.rsqrt(jnp.mean(o * o, axis=0, keepdims=True) + EPS) * g_col
    return (o * (1.0 - lam_init)).T


def _mask_diagonal_tile(s):
    t = s.shape[0]
    q_chunk = (lax.broadcasted_iota(jnp.int32, (1, s.shape[1]), 1) % t) // CHUNK
    blocks = [jnp.where(q_chunk >= a, s[a * CHUNK:(a + 1) * CHUNK, :], -1e30) for a in range(t // CHUNK)]
    return jnp.concatenate(blocks, axis=0)


def _prompt_attn_kernel(lq_ref, g_ref, q_ref, k_ref, vt_ref, o_ref, qq_ref, p_ref, m_ref, acc_ref, *, lam_init):
    t = ATTN_TILE
    heads = [slice(h * LANES, (h + 1) * LANES) for h in range(HEADS_PER_STEP)]
    ones = jnp.ones((ONES_ROWS, t), BF16)
    lam = _lambda_scalar(lq_ref[...], lam_init)
    q_tiles = q_ref.shape[0] // t

    def one_query_tile(sub, _):
        qi = pl.program_id(2) * q_tiles + sub
        q_rows = pl.ds(pl.multiple_of(sub * t, t), t)
        for h, c in enumerate(heads):
            qq_ref[h] = _stack_maps(q_ref[q_rows, c])
        stage0 = jnp.minimum(qi, 0)

        def steps(tiles):
            rows = [pl.ds(pl.multiple_of(j * t, t), t) for j, _ in tiles]
            s = [[_flash_scores(qq_ref[h], k_ref[r, c]) for h, c in enumerate(heads)] for r in rows]
            carry = [(m_ref[h], acc_ref[h]) for h in range(HEADS_PER_STEP)]
            for ti, (_, diagonal) in enumerate(tiles):
                if diagonal:
                    s[ti] = [_mask_diagonal_tile(x) for x in s[ti]]
                carry = [
                    _flash_update(carry[h], s[ti][h], jnp.concatenate([vt_ref[c, rows[ti]], ones], axis=0),
                                  p_ref.at[stage0 + ti * HEADS_PER_STEP + h])
                    for h, c in enumerate(heads)]
            for h, (m, acc) in enumerate(carry):
                m_ref[h] = m
                acc_ref[h] = acc
            return 0

        m0, acc0 = _flash_init(2 * t)
        for h in range(HEADS_PER_STEP):
            m_ref[h] = m0
            acc_ref[h] = acc0
        nb = min(KEY_TILES_PER_BLOCK, k_ref.shape[0] // t)
        if nb < k_ref.shape[0] // t:
            lax.fori_loop(0, qi // nb, lambda p, _: steps([(nb * p + i, False) for i in range(nb)]), 0)
        left = qi % nb

        def last_block(n_full):
            return lambda _: steps([(qi - n_full + i, False) for i in range(n_full)] + [(qi, True)])

        lax.switch(left, [last_block(n) for n in range(nb)], 0)
        for h, c in enumerate(heads):
            o_ref[q_rows, c] = _attn_finish((m_ref[h], acc_ref[h]), lam, g_ref[...], lam_init, t).astype(BF16)
        return 0

    lax.fori_loop(0, q_tiles, one_query_tile, 0)


def _prompt_attention(lq, g, q, kb, vtb, lam_init, layer):
    b, s, _ = q.shape
    t = ATTN_TILE
    w = HEADS_PER_STEP * LANES
    tq = Q_TILES_PER_STEP * t
    return pl.pallas_call(
        functools.partial(_prompt_attn_kernel, lam_init=lam_init),
        grid=(b, N_HEADS // HEADS_PER_STEP, s // tq),
        in_specs=[
            _layer_spec((4, HEAD_DIM), layer),
            _layer_spec((V_DIM, 1), layer),
            pl.BlockSpec((None, tq, w), lambda b_, h, i: (b_, i, h)),
            pl.BlockSpec((None, s, w), lambda b_, h, i: (b_, 0, h)),
            pl.BlockSpec((None, w, s), lambda b_, h, i: (b_, h, 0)),
        ],
        out_specs=pl.BlockSpec((None, tq, w), lambda b_, h, i: (b_, i, h)),
        out_shape=jax.ShapeDtypeStruct((b, s, ATTN_WIDTH), BF16),
        scratch_shapes=[pltpu.VMEM((HEADS_PER_STEP, 2 * t, LANES), BF16),
                        pltpu.VMEM((KEY_TILES_PER_BLOCK * HEADS_PER_STEP, t, 2 * t), BF16),
                        pltpu.VMEM((HEADS_PER_STEP, 1, 2 * t), F32),
                        pltpu.VMEM((HEADS_PER_STEP, V_DIM + ONES_ROWS, 2 * t), F32)],
        compiler_params=pltpu.CompilerParams(dimension_semantics=("arbitrary",) * 3,
                                             vmem_limit_bytes=VMEM_LIMIT),
        name="prompt_attention",
    )(lq, g, q, kb, vtb)


def _row_scores(qq, kt, *, keys_on_rows):
    dims = (((1,), (1,)), ((), ())) if keys_on_rows else (((1,), (0,)), ((), ()))
    return lax.dot_general(qq, kt, dims, preferred_element_type=F32)


def _row_flash_update(carry, s, vt):
    m, l, acc = carry
    m_new = jnp.maximum(m, jnp.max(s, axis=-1, keepdims=True))
    alpha = jnp.exp2(m - m_new)
    p = jnp.exp2(s - m_new)
    l = alpha * l + jnp.sum(p, axis=-1, keepdims=True)
    acc = alpha * acc + jnp.dot(p.astype(BF16), vt, preferred_element_type=F32)
    return m_new, l, acc


def _sample_attn_kernel(lq_ref, g_ref, q_ref, ck_ref, cv_ref, k_ref, v_ref, o_ref, *, lam_init, past_len):
    t = SAMPLE_KEY_TILE
    tq = q_ref.shape[0]
    heads = [slice(h * LANES, (h + 1) * LANES) for h in range(N_HEADS)]
    qq = [_stack_maps(q_ref[:, c]) for c in heads]

    def body(j, carry):
        cols = pl.ds(pl.multiple_of(j * t, t), t)
        s = [_row_scores(qq[h], ck_ref[c, cols].astype(BF16), keys_on_rows=False) for h, c in enumerate(heads)]
        return tuple(
            _row_flash_update(carry[h], s[h], cv_ref[pl.ds(j * (t * N_HEADS) + h, t, stride=N_HEADS), :].astype(BF16))
            for h in range(N_HEADS))

    init = (jnp.full((2 * tq, 1), -jnp.inf, F32), jnp.zeros((2 * tq, 1), F32), jnp.zeros((2 * tq, V_DIM), F32))
    carry = lax.fori_loop(0, past_len // t, body, (init,) * N_HEADS)
    lam = _lambda_scalar(lq_ref[...], lam_init)
    for h, c in enumerate(heads):
        s = _row_scores(qq[h], k_ref[:, c], keys_on_rows=True)
        _, l, acc = _row_flash_update(carry[h], s, v_ref[:, c])
        o = acc[:tq] / l[:tq] - lam * (acc[tq:] / l[tq:])
        o_ref[:, c] = (_rms(o, g_ref[...]) * (1.0 - lam_init)).astype(BF16)


def _sample_attention(lq, g, q, cache_kt, cache_v4, kb, vb, lam_init, layer):
    b, ls, _ = q.shape
    past_len = cache_kt.shape[-1]
    assert past_len % SAMPLE_KEY_TILE == 0
    return pl.pallas_call(
        functools.partial(_sample_attn_kernel, lam_init=lam_init, past_len=past_len),
        grid=(b,),
        in_specs=[
            _layer_spec((4, HEAD_DIM), layer),
            _layer_spec((1, V_DIM), layer),
            pl.BlockSpec((None, ls, ATTN_WIDTH), lambda b_: (b_, 0, 0)),
            pl.BlockSpec((None, None, ATTN_WIDTH, past_len), lambda b_: (layer, b_, 0, 0)),
            pl.BlockSpec((None, None, N_HEADS * past_len, V_DIM), lambda b_: (layer, b_, 0, 0)),
            pl.BlockSpec((None, ls, ATTN_WIDTH), lambda b_: (b_, 0, 0)),
            pl.BlockSpec((None, ls, ATTN_WIDTH), lambda b_: (b_, 0, 0)),
        ],
        out_specs=pl.BlockSpec((None, ls, ATTN_WIDTH), lambda b_: (b_, 0, 0)),
        out_shape=jax.ShapeDtypeStruct((b, ls, ATTN_WIDTH), BF16),
        compiler_params=pltpu.CompilerParams(dimension_semantics=("arbitrary",),
                                             vmem_limit_bytes=VMEM_LIMIT),
        name="sample_attention",
    )(lq, g, q, cache_kt, cache_v4, kb, vb)


def _pool_mixer(ext_p, pool_w_ref, pool_scale_ref, tm, pos0):
    assert POOL_WINDOWS == (2, 4, 8, 16) and HALO == 32
    e = ext_p[...]
    n = HALO + tm
    s2 = e[8:] + e[7:n - 1]
    s4 = s2[8:] + s2[6:n - 10]
    s8 = s4[8:] + s4[4:n - 20]
    sums = {2: s2[24:], 4: s4[16:], 8: s8[8:], 16: s8[8:] + s8[:n - 32]}
    u = e[HALO:]
    lane = lax.broadcasted_iota(jnp.int32, (tm, POOL_WIDTH), 1)
    group = lane // POOL_GROUP_DIM
    win = sums[POOL_WINDOWS[-1]]
    width = jnp.full((tm, POOL_WIDTH), POOL_WINDOWS[-1], jnp.int32)
    for gi in range(len(POOL_WINDOWS) - 2, -1, -1):
        win = jnp.where(group == gi, sums[POOL_WINDOWS[gi]], win)
        width = jnp.where(group == gi, POOL_WINDOWS[gi], width)
    pos = pos0 + lax.broadcasted_iota(jnp.int32, (tm, POOL_WIDTH), 0)
    count = jnp.minimum(pos + 1, width).astype(F32)
    d = (win / count - u).astype(BF16)
    return jnp.dot(d, pool_w_ref[...], preferred_element_type=F32) * pool_scale_ref[...]


CONV_ROW_BLOCK = 64


def _conv_mixer_steps(ext_c, shift_c, conv_out, dw_ref, dw_b_ref, ln_g_ref, ln_b_ref, pw_ref, tm, store):
    first_row = HALO - CONV_HIST
    blk = min(tm, CONV_ROW_BLOCK)

    def shift(residues):
        for r in residues:
            shift_c[r, 0:HALO - 8 + tm, :] = ext_c[pl.ds(r, HALO - 8 + tm), :]

    def taps(rb):
        yb = jnp.zeros((blk, CONV_WIDTH), F32)
        for k in range(CONV_K):
            r, a = (first_row + k) % 8, (first_row + k) // 8
            rows = pl.ds(8 * a + rb * blk, blk)
            src = ext_c[rows, :] if r == 0 else shift_c[r, rows, :]
            yb = yb + src * dw_ref[k:k + 1, :]
        conv_out[rb * blk:(rb + 1) * blk, :] = yb

    def finish():
        y = conv_out[...] + dw_b_ref[...]
        mu = jnp.mean(y, axis=-1, keepdims=True)
        yc = y - mu
        yn = yc * lax.rsqrt(jnp.mean(yc * yc, axis=-1, keepdims=True) + EPS) * ln_g_ref[...] + ln_b_ref[...]
        act = (yn * jax.nn.sigmoid(yn)).astype(BF16)
        store(jnp.dot(act, pw_ref[...], preferred_element_type=F32))

    steps = [functools.partial(shift, rs) for rs in ((1, 2), (3, 4, 5), (6, 7))]
    steps += [functools.partial(taps, rb) for rb in range(tm // blk)]
    return steps + [finish]


def _conv_mixer(ext_c, shift_c, conv_out, dw_ref, dw_b_ref, ln_g_ref, ln_b_ref, pw_ref, tm):
    out = []
    for step in _conv_mixer_steps(ext_c, shift_c, conv_out, dw_ref, dw_b_ref, ln_g_ref, ln_b_ref, pw_ref, tm,
                                  out.append):
        step()
    return out[0]


def _mixer_scratch(tm):
    return [pltpu.VMEM((HALO + tm, POOL_WIDTH), F32), pltpu.VMEM((HALO + tm, CONV_WIDTH), F32),
            pltpu.VMEM((8, HALO + tm, CONV_WIDTH), F32), pltpu.VMEM((tm, CONV_WIDTH), F32)]


def _mixer_weight_specs(layer):
    shapes = [(POOL_WIDTH, POOL_WIDTH), (1, POOL_WIDTH), (CONV_K, CONV_WIDTH), (1, CONV_WIDTH), (1, CONV_WIDTH),
              (1, CONV_WIDTH), (CONV_WIDTH, CONV_WIDTH)]
    return [_layer_spec(shape, layer) for shape in shapes]


def _sample_mixers_kernel(up_ref, up_hist_ref, uc_ref, uc_hist_ref,
                          pool_w_ref, pool_scale_ref, dw_ref, dw_b_ref, ln_g_ref, ln_b_ref, pw_ref,
                          ypool_ref, yconv_ref, ext_p, ext_c, shift_c, conv_out, *, pos_base):
    tm = up_ref.shape[0]
    ext_p[0:HALO, :] = up_hist_ref[...]
    ext_p[HALO:, :] = up_ref[...]
    ext_c[0:HALO, :] = uc_hist_ref[...]
    ext_c[HALO:, :] = uc_ref[...]
    ypool_ref[...] = _pool_mixer(ext_p, pool_w_ref, pool_scale_ref, tm, pos_base).astype(BF16)
    yconv_ref[...] = _conv_mixer(ext_c, shift_c, conv_out, dw_ref, dw_b_ref, ln_g_ref, ln_b_ref, pw_ref,
                                 tm).astype(BF16)


def _sample_mixers(upool, uconv, hist_pool, hist_conv, mixer_w, seq_len, pos_base, layer):
    n = upool.shape[0]
    cur = pl.BlockSpec((seq_len, POOL_WIDTH), lambda i: (i, 0))
    hist = pl.BlockSpec((None, None, HALO, POOL_WIDTH), lambda i: (layer, i, 0, 0))
    return pl.pallas_call(
        functools.partial(_sample_mixers_kernel, pos_base=pos_base),
        grid=(n // seq_len,),
        in_specs=[cur, hist, cur, hist] + _mixer_weight_specs(layer),
        out_specs=[cur, cur],
        out_shape=[jax.ShapeDtypeStruct((n, POOL_WIDTH), BF16), jax.ShapeDtypeStruct((n, CONV_WIDTH), BF16)],
        scratch_shapes=_mixer_scratch(seq_len),
        compiler_params=pltpu.CompilerParams(dimension_semantics=("arbitrary",),
                                             vmem_limit_bytes=VMEM_LIMIT),
        name="sample_mixers",
    )(upool, hist_pool, uconv, hist_conv, *mixer_w)


def _dense_math(x, ypool, o, yconv, wout_ref, g_ref, wgu_ref, wd_ref, fg_ref, final):
    o1 = POOL_WIDTH
    o2 = o1 + ATTN_WIDTH
    mix = (jnp.dot(ypool, wout_ref[0:o1, :], preferred_element_type=F32)
           + jnp.dot(o, wout_ref[o1:o2, :], preferred_element_type=F32)
           + jnp.dot(yconv, wout_ref[o2:, :], preferred_element_type=F32))
    x1 = x + mix
    h = _rms(x1, g_ref[...]).astype(BF16)
    gu = jnp.dot(h, wgu_ref[...], preferred_element_type=F32)
    gate = gu[:, :D_FF]
    a = (gate * jax.nn.sigmoid(gate) * gu[:, D_FF:]).astype(BF16)
    x2 = x1 + jnp.dot(a, wd_ref[...], preferred_element_type=F32)
    return _rms(x2, fg_ref[...]) if final else x2


def _dense_weight_specs(layer):
    shapes = [(MIX_WIDTH, D_MODEL), (1, D_MODEL), (D_MODEL, 2 * D_FF), (D_FF, D_MODEL)]
    return [_layer_spec(shape, layer) for shape in shapes] + [_const_spec((1, D_MODEL))]


def _prompt_dense_kernel(x_ref, o_ref, up0_ref, uc0_ref, up_ref, uc_ref, hist_p_ref, hist_c_ref,
                         pool_w_ref, pool_scale_ref, dw_ref, dw_b_ref, ln_g_ref, ln_b_ref, pw_ref,
                         wout_ref, g_ref, wgu_ref, wd_ref, fg_ref, out_ref,
                         ext_p, ext_c, shift_c, conv_out, y_pool, y_conv, *, final, tiles_per_seq):
    tm = x_ref.shape[0]
    i = pl.program_id(0)

    def mixer_steps(up_ref_, uc_ref_, tile, slot, first):
        def fill():
            if first is True:
                ext_p[0:HALO, :] = hist_p_ref[...]
                ext_c[0:HALO, :] = hist_c_ref[...]
            else:
                ext_p[0:HALO, :] = jnp.where(first, hist_p_ref[...], ext_p[tm:, :])
                ext_c[0:HALO, :] = jnp.where(first, hist_c_ref[...], ext_c[tm:, :])
            ext_p[HALO:, :] = up_ref_[...]
            ext_c[HALO:, :] = uc_ref_[...]

        def pool():
            pos0 = (tile % tiles_per_seq) * tm
            y_pool[slot] = _pool_mixer(ext_p, pool_w_ref, pool_scale_ref, tm, pos0).astype(BF16)

        def store_conv(y):
            y_conv[slot] = y.astype(BF16)

        return [fill] + _conv_mixer_steps(ext_c, shift_c, conv_out, dw_ref, dw_b_ref, ln_g_ref, ln_b_ref, pw_ref,
                                          tm, store_conv) + [pool]

    @pl.when(i == 0)
    def _():
        for step in mixer_steps(up0_ref, uc0_ref, 0, 0, True):
            step()

    slot = i % 2
    nxt = jnp.minimum(i + 1, pl.num_programs(0) - 1)
    out_ref[...] = _dense_math(x_ref[...], y_pool[slot], o_ref[...], y_conv[slot],
                               wout_ref, g_ref, wgu_ref, wd_ref, fg_ref, final)
    for step in mixer_steps(up_ref, uc_ref, nxt, 1 - slot, nxt % tiles_per_seq == 0):
        step()


def _prompt_dense(x, o, upool, uconv, hist_pool, hist_conv, mixer_w, dense_w, tm, seq_len, final, layer):
    n = x.shape[0]
    tps = seq_len // tm
    last = n // tm - 1
    row = lambda w: pl.BlockSpec((tm, w), lambda i: (i, 0))
    first_tile = lambda w: pl.BlockSpec((tm, w), lambda i: (0, 0))
    next_tile = lambda w: pl.BlockSpec((tm, w), lambda i: (jnp.minimum(i + 1, last), 0))
    next_seq = lambda w: pl.BlockSpec((None, HALO, w), lambda i: (jnp.minimum(i + 1, last) // tps, 0, 0))
    return pl.pallas_call(
        functools.partial(_prompt_dense_kernel, final=final, tiles_per_seq=tps),
        grid=(n // tm,),
        in_specs=[row(D_MODEL), row(ATTN_WIDTH), first_tile(POOL_WIDTH), first_tile(CONV_WIDTH),
                  next_tile(POOL_WIDTH), next_tile(CONV_WIDTH), next_seq(POOL_WIDTH), next_seq(CONV_WIDTH)]
        + _mixer_weight_specs(layer) + _dense_weight_specs(layer),
        out_specs=row(D_MODEL),
        out_shape=jax.ShapeDtypeStruct((n, D_MODEL), F32),
        scratch_shapes=_mixer_scratch(tm) + [pltpu.VMEM((2, tm, POOL_WIDTH), BF16),
                                             pltpu.VMEM((2, tm, CONV_WIDTH), BF16)],
        compiler_params=pltpu.CompilerParams(dimension_semantics=("arbitrary",),
                                             vmem_limit_bytes=VMEM_LIMIT),
        name="prompt_dense",
    )(x, o, upool, uconv, upool, uconv, hist_pool, hist_conv, *mixer_w, *dense_w)


def _sample_dense_kernel(x_ref, yp_ref, o_ref, yc_ref, wout_ref, g_ref, wgu_ref, wd_ref, fg_ref, out_ref, *, final):
    out_ref[...] = _dense_math(x_ref[...], yp_ref[...], o_ref[...], yc_ref[...],
                               wout_ref, g_ref, wgu_ref, wd_ref, fg_ref, final)


def _sample_dense(x, ypool, o, yconv, wout_bf, g, wgu_bf, wd_bf, fg, tm, final, layer):
    n = x.shape[0]
    row = lambda w: pl.BlockSpec((tm, w), lambda i: (i, 0))
    return pl.pallas_call(
        functools.partial(_sample_dense_kernel, final=final),
        grid=(n // tm,),
        in_specs=[row(D_MODEL), row(POOL_WIDTH), row(ATTN_WIDTH), row(CONV_WIDTH)] + _dense_weight_specs(layer),
        out_specs=row(D_MODEL),
        out_shape=jax.ShapeDtypeStruct((n, D_MODEL), F32),
        compiler_params=pltpu.CompilerParams(dimension_semantics=("arbitrary",),
                                             vmem_limit_bytes=VMEM_LIMIT),
        name="sample_dense",
    )(x, ypool, o, yconv, wout_bf, g, wgu_bf, wd_bf, fg)


def _rope_tables(pos):
    half = HEAD_DIM // 2
    inv = ROPE_THETA ** (-jnp.arange(half, dtype=F32) / half)
    ang = pos.astype(F32)[:, None] * inv[None, :]
    reps = LANES // half
    cos = jnp.tile(jnp.cos(ang), (1, reps))
    sign = jnp.tile(jnp.concatenate([-jnp.ones((half,), F32), jnp.ones((half,), F32)]), LANES // HEAD_DIM)
    sin = jnp.tile(jnp.sin(ang), (1, reps)) * sign[None, :]
    return cos, sin


def _block_diag(pool_w):
    depth, g, d, _ = pool_w.shape
    eye = jnp.eye(g, dtype=pool_w.dtype)
    return (pool_w[:, :, :, None, :] * eye[None, :, None, :, None]).reshape(depth, g * d, g * d)


def _pad_hist(h):
    return jnp.pad(h, ((0, 0), (0, 0), (HALO - h.shape[2], 0), (0, 0)))


def kernel(x_prompt, x_sample, cache_k, cache_v, state_pool, state_conv, norm_mix_g, w_in, pool_w, pool_scale,
           lambda_qk, diff_norm_g, conv_dw, conv_dw_b, conv_ln_g, conv_ln_b, conv_pw, w_out, norm_ffn_g,
           w_gate_up, w_down, final_norm_g):
    B, S, _ = x_prompt.shape
    Bs, Ls, _ = x_sample.shape
    depth = w_in.shape[0]
    past_len = cache_k.shape[2]
    assert S % (Q_TILES_PER_STEP * ATTN_TILE) == 0 and S % ROW_TILE == 0 and S % IN_PROJ_TILE == 0
    assert S // ROW_TILE >= 2
    assert (Bs * Ls) % 16 == 0 and Ls % 16 == 0 and Ls >= CONV_HIST

    cos_p, sin_p = _rope_tables(jnp.arange(S, dtype=jnp.int32))
    cos_s, sin_s = _rope_tables(past_len + jnp.arange(Ls, dtype=jnp.int32))
    cos_s, sin_s = jnp.tile(cos_s, (Bs, 1)), jnp.tile(sin_s, (Bs, 1))
    zero_pool = jnp.zeros((B, HALO, POOL_WIDTH), F32)
    zero_conv = jnp.zeros((B, HALO, CONV_WIDTH), F32)

    xp = x_prompt.reshape(B * S, D_MODEL)
    xs = x_sample.reshape(Bs * Ls, D_MODEL)
    rows = lambda a: a.reshape(depth, 1, -1)
    norm_mix_rows, norm_ffn_rows = rows(norm_mix_g), rows(norm_ffn_g)
    diff_g_row, diff_g_col = rows(diff_norm_g), diff_norm_g.reshape(depth, -1, 1)
    w_in_bf, w_out_bf, w_gu_bf, w_d_bf = (w.astype(BF16) for w in (w_in, w_out, w_gate_up, w_down))
    mixer_w = (_block_diag(pool_w).astype(BF16), rows(pool_scale), conv_dw, rows(conv_dw_b), rows(conv_ln_g),
               rows(conv_ln_b), conv_pw.astype(BF16))
    dense_w = (w_out_bf, norm_ffn_rows, w_gu_bf, w_d_bf, final_norm_g.reshape(1, -1))
    hist_pool_s, hist_conv_s = _pad_hist(state_pool), _pad_hist(state_conv)
    cache_kt = jnp.transpose(cache_k, (0, 1, 3, 4, 5, 2)).reshape(depth, Bs, ATTN_WIDTH, past_len)
    cache_v4 = cache_v.reshape(depth, Bs, past_len * N_HEADS, V_DIM)
    kv_stack = depth
    pp, cp = [], []
    ks_, vs_, ps_, cs_ = [], [], [], []
    for l in range(depth):
        lam_init = 0.8 - 0.6 * math.exp(-0.3 * l)
        final = l == depth - 1

        upool, q, k, kb, v, vb, uconv = _prompt_in_proj(xp, norm_mix_rows, w_in_bf, cos_p, sin_p, IN_PROJ_TILE, S,
                                                        kv_stack, l)
        kv_stack = (k, v)
        o = _prompt_attention(lambda_qk, diff_g_col, q.reshape(B, S, -1), kb.reshape(B, S, -1),
                              vb, lam_init, l)
        xp = _prompt_dense(xp, o.reshape(B * S, -1), upool, uconv, zero_pool, zero_conv, mixer_w, dense_w,
                           ROW_TILE, S, final, l)
        pp.append(upool.reshape(B, S, -1)[:, S - POOL_HIST:])
        cp.append(uconv.reshape(B, S, -1)[:, S - CONV_HIST:])

        upool, q, k, kb, v, vb, uconv = _sample_in_proj(xs, norm_mix_rows, w_in_bf, cos_s, sin_s, l)
        o = _sample_attention(lambda_qk, diff_g_row, q.reshape(Bs, Ls, -1), cache_kt, cache_v4,
                              kb.reshape(Bs, Ls, -1), vb.reshape(Bs, Ls, -1), lam_init, l)
        ypool, yconv = _sample_mixers(upool, uconv, hist_pool_s, hist_conv_s, mixer_w, Ls, past_len, l)
        xs = _sample_dense(xs, ypool, o.reshape(Bs * Ls, -1), yconv, *dense_w, Bs * Ls, final, l)
        ks_.append(k.reshape(Bs, Ls, N_HEADS, 2, HEAD_DIM))
        vs_.append(v.reshape(Bs, Ls, N_HEADS, V_DIM))
        ps_.append(upool.reshape(Bs, Ls, -1)[:, Ls - POOL_HIST:])
        cs_.append(uconv.reshape(Bs, Ls, -1)[:, Ls - CONV_HIST:])

    kt_all, v4_all = kv_stack
    new_k = jnp.transpose(kt_all.reshape(depth, B, N_HEADS, 2, HEAD_DIM, S), (0, 1, 5, 2, 3, 4))
    new_v = v4_all.reshape(depth, B, S, N_HEADS, V_DIM)
    return (xp.reshape(B, S, D_MODEL), xs.reshape(Bs, Ls, D_MODEL),
            new_k, new_v, jnp.stack(pp), jnp.stack(cp),
            jnp.stack(ks_), jnp.stack(vs_), jnp.stack(ps_), jnp.stack(cs_))
```

```python
import functools
import math

import jax
import jax.numpy as jnp
from jax import lax
from jax.experimental import pallas as pl
from jax.experimental.pallas import tpu as pltpu

D_MODEL = 1024
CHUNK = 64
POOL_WIDTH = D_MODEL // 4
POOL_WINDOWS = (2, 4, 8, 16)
POOL_GROUP_DIM = POOL_WIDTH // len(POOL_WINDOWS)
POOL_HIST = max(POOL_WINDOWS) - 1
ATTN_WIDTH = D_MODEL // 2
N_HEADS = 4
HEAD_DIM = ATTN_WIDTH // (2 * N_HEADS)
V_DIM = 2 * HEAD_DIM
ROPE_THETA = 10000.0
CONV_WIDTH = D_MODEL // 4
CONV_K = 31
CONV_HIST = CONV_K - 1
MIX_WIDTH = POOL_WIDTH + ATTN_WIDTH + CONV_WIDTH
IN_WIDTH = POOL_WIDTH + 3 * ATTN_WIDTH + 2 * CONV_WIDTH
D_FF = ((-(-8 * D_MODEL // 3) + 255) // 256) * 256
EPS = 1e-6

LANES = 128
HALO = 32
ROW_TILE = 512
IN_PROJ_TILE = 1024
ATTN_TILE = 256
SAMPLE_KEY_TILE = 2048
KEY_TILES_PER_BLOCK = 4
Q_TILES_PER_STEP = 8
HEADS_PER_STEP = 4
VMEM_LIMIT = 56 * 1024 * 1024

F32 = jnp.float32
BF16 = jnp.bfloat16


def _rms(x, g):
    return x * lax.rsqrt(jnp.mean(x * x, axis=-1, keepdims=True) + EPS) * g


def _const_spec(shape):
    return pl.BlockSpec(shape, lambda *_: (0,) * len(shape), pipeline_mode=pl.Buffered(1))


def _layer_spec(shape, layer):
    return pl.BlockSpec((None,) + shape, lambda *_: (layer,) + (0,) * len(shape), pipeline_mode=pl.Buffered(1))


O_Q = POOL_WIDTH
O_K = O_Q + ATTN_WIDTH
O_V = O_K + ATTN_WIDTH
O_A = O_V + ATTN_WIDTH
O_B = O_A + CONV_WIDTH


def _qkv_epilogue(z, cos_ref, sin_ref, q_ref, k_ref, kb_ref, v_ref, vb_ref, cache_layout):
    tm = z.shape[0]
    o1, o2, o3 = O_Q, O_K, O_V
    cos = cos_ref[...]
    sin = sin_ref[...]
    lane = lax.broadcasted_iota(jnp.int32, cos.shape, 1)
    first_half = (lane & (HEAD_DIM // 2)) == 0

    def rope(t):
        up = pltpu.roll(t, LANES - HEAD_DIM // 2, axis=1)
        down = pltpu.roll(t, HEAD_DIM // 2, axis=1)
        return t * cos + jnp.where(first_half, up, down) * sin

    scale = HEAD_DIM ** -0.5 * math.log2(math.e)
    for j in range(ATTN_WIDTH // LANES):
        sl = slice(j * LANES, (j + 1) * LANES)
        qj = rope(z[:, o1 + j * LANES:o1 + (j + 1) * LANES])
        kj = rope(z[:, o2 + j * LANES:o2 + (j + 1) * LANES])
        vj = z[:, o3 + j * LANES:o3 + (j + 1) * LANES]
        q_ref[:, sl] = (qj * scale).astype(BF16)
        kb_ref[:, sl] = kj.astype(BF16)
        if cache_layout:
            k_ref[sl, :] = kj.T
            v_ref[pl.ds(j, tm, stride=N_HEADS), :] = vj
            vb_ref[sl, :] = vj.T.astype(BF16)
        else:
            k_ref[:, sl] = kj
            v_ref[:, sl] = vj
            vb_ref[:, sl] = vj.astype(BF16)


def _sample_in_proj_kernel(x_ref, g_ref, w_ref, cos_ref, sin_ref,
                           upool_ref, q_ref, k_ref, kb_ref, v_ref, vb_ref, uconv_ref):
    h = _rms(x_ref[...], g_ref[...]).astype(BF16)
    z = jnp.dot(h, w_ref[...], preferred_element_type=F32)
    upool_ref[...] = z[:, :O_Q]
    uconv_ref[...] = z[:, O_A:O_B] * jax.nn.sigmoid(z[:, O_B:])
    _qkv_epilogue(z, cos_ref, sin_ref, q_ref, k_ref, kb_ref, v_ref, vb_ref, False)


def _sample_in_proj(x, g, w_bf, cos_tab, sin_tab, layer):
    n = x.shape[0]
    full = lambda w: pl.BlockSpec((n, w), lambda i: (0, 0))
    flat = lambda dt: jax.ShapeDtypeStruct((n, ATTN_WIDTH), dt)
    return pl.pallas_call(
        _sample_in_proj_kernel,
        grid=(1,),
        in_specs=[full(D_MODEL), _layer_spec((1, D_MODEL), layer), _layer_spec((D_MODEL, IN_WIDTH), layer),
                  full(LANES), full(LANES)],
        out_specs=[full(POOL_WIDTH)] + [full(ATTN_WIDTH)] * 5 + [full(CONV_WIDTH)],
        out_shape=[jax.ShapeDtypeStruct((n, POOL_WIDTH), F32), flat(BF16), flat(F32), flat(BF16), flat(F32),
                   flat(BF16), jax.ShapeDtypeStruct((n, CONV_WIDTH), F32)],
        compiler_params=pltpu.CompilerParams(dimension_semantics=("arbitrary",),
                                             vmem_limit_bytes=VMEM_LIMIT),
        name="sample_in_proj",
    )(x, g, w_bf, cos_tab, sin_tab)


def _prompt_in_proj_kernel(*refs, n_unused):
    x_ref, g_ref, w_ref, cos_ref, sin_ref = refs[:5]
    upool_ref, q_ref, k_ref, kb_ref, v_ref, vb_ref, uconv_ref = refs[5 + n_unused:]
    h = _rms(x_ref[...], g_ref[...]).astype(BF16)
    z = jnp.dot(h, w_ref[...], preferred_element_type=F32)
    upool_ref[...] = z[:, :O_Q]
    uconv_ref[...] = z[:, O_A:O_B] * jax.nn.sigmoid(z[:, O_B:])
    _qkv_epilogue(z, cos_ref, sin_ref, q_ref, k_ref, kb_ref, v_ref, vb_ref, True)


def _prompt_in_proj(x, g, w_bf, cos_tab, sin_tab, tm, seq_len, kv_stack, layer):
    n = x.shape[0]
    tps = seq_len // tm
    batch = n // seq_len
    row = lambda w: pl.BlockSpec((tm, w), lambda i: (i, 0))
    tab = pl.BlockSpec((tm, LANES), lambda i: (i % tps, 0))
    flat = lambda dt, w=ATTN_WIDTH: jax.ShapeDtypeStruct((n, w), dt)
    extra_in, extra_specs, aliases = [], [], {}
    if isinstance(kv_stack, int):
        k_shape = jax.ShapeDtypeStruct((kv_stack, batch, ATTN_WIDTH, seq_len), F32)
        v_shape = jax.ShapeDtypeStruct((kv_stack, batch, N_HEADS * seq_len, V_DIM), F32)
    else:
        k_shape, v_shape = (jax.ShapeDtypeStruct(a.shape, a.dtype) for a in kv_stack)
        extra_in = list(kv_stack)
        extra_specs = [pl.BlockSpec(memory_space=pl.ANY)] * 2
        aliases = {5: 2, 6: 4}
    k_spec = pl.BlockSpec((None, None, ATTN_WIDTH, tm), lambda i: (layer, i // tps, 0, i % tps))
    v_spec = pl.BlockSpec((None, None, N_HEADS * tm, V_DIM), lambda i: (layer, i // tps, i % tps, 0))
    vt_spec = pl.BlockSpec((None, ATTN_WIDTH, tm), lambda i: (i // tps, 0, i % tps))
    return pl.pallas_call(
        functools.partial(_prompt_in_proj_kernel, n_unused=len(extra_in)),
        grid=(n // tm,),
        in_specs=[row(D_MODEL), _layer_spec((1, D_MODEL), layer), _layer_spec((D_MODEL, IN_WIDTH), layer), tab, tab]
        + extra_specs,
        out_specs=[row(POOL_WIDTH), row(ATTN_WIDTH), k_spec, row(ATTN_WIDTH), v_spec, vt_spec, row(CONV_WIDTH)],
        out_shape=[flat(F32, POOL_WIDTH), flat(BF16), k_shape, flat(BF16), v_shape,
                   jax.ShapeDtypeStruct((batch, ATTN_WIDTH, seq_len), BF16), flat(F32, CONV_WIDTH)],
        input_output_aliases=aliases,
        compiler_params=pltpu.CompilerParams(dimension_semantics=("arbitrary",),
                                             vmem_limit_bytes=VMEM_LIMIT),
        name="prompt_in_proj",
    )(x, g, w_bf, cos_tab, sin_tab, *extra_in)


def _lambda_scalar(lq, lam_init):
    a = jnp.sum(lq[0:1, :] * lq[1:2, :], axis=-1, keepdims=True)
    b = jnp.sum(lq[2:3, :] * lq[3:4, :], axis=-1, keepdims=True)
    return jnp.exp(a) - jnp.exp(b) + lam_init


def _stack_maps(q):
    lane = lax.broadcasted_iota(jnp.int32, q.shape, 1)
    zero = jnp.zeros_like(q)
    return jnp.concatenate([jnp.where(lane < HEAD_DIM, q, zero), jnp.where(lane >= HEAD_DIM, q, zero)], axis=0)


ONES_ROWS = 16


def _flash_scores(qq, kt):
    return lax.dot_general(kt, qq, (((1,), (1,)), ((), ())), preferred_element_type=F32)


def _flash_update(carry, s, vt_ones, p_stage):
    m, acc = carry
    m_new = jnp.maximum(m, jnp.max(s, axis=0, keepdims=True))
    alpha = jnp.exp2(m - m_new)
    p_stage[...] = jnp.exp2(s - m_new).astype(BF16)
    return m_new, alpha * acc + jnp.dot(vt_ones, p_stage[...], preferred_element_type=F32)


def _flash_init(cols):
    return jnp.full((1, cols), -jnp.inf, F32), jnp.zeros((V_DIM + ONES_ROWS, cols), F32)


def _attn_finish(carry, lam, g_col, lam_init, tq):
    _, acc = carry
    scaled = acc[:V_DIM] * (1.0 / acc[V_DIM:V_DIM + 1])
    o = scaled[:, :tq] - lam * scaled[:, tq:]
    o = o * lax.rsqrt(jnp.mean(o * o, axis=0, keepdims=True) + EPS) * g_col
    return (o * (1.0 - lam_init)).T


def _mask_diagonal_tile(s):
    t = s.shape[0]
    q_chunk = (lax.broadcasted_iota(jnp.int32, (1, s.shape[1]), 1) % t) // CHUNK
    blocks = [jnp.where(q_chunk >= a, s[a * CHUNK:(a + 1) * CHUNK, :], -1e30) for a in range(t // CHUNK)]
    return jnp.concatenate(blocks, axis=0)


def _prompt_attn_kernel(lq_ref, g_ref, q_ref, k_ref, vt_ref, o_ref, qq_ref, p_ref, m_ref, acc_ref, *, lam_init):
    t = ATTN_TILE
    heads = [slice(h * LANES, (h + 1) * LANES) for h in range(HEADS_PER_STEP)]
    ones = jnp.ones((ONES_ROWS, t), BF16)
    lam = _lambda_scalar(lq_ref[...], lam_init)
    q_tiles = q_ref.shape[0] // t

    def one_query_tile(sub, _):
        qi = pl.program_id(2) * q_tiles + sub
        q_rows = pl.ds(pl.multiple_of(sub * t, t), t)
        for h, c in enumerate(heads):
            qq_ref[h] = _stack_maps(q_ref[q_rows, c])
        stage0 = jnp.minimum(qi, 0)

        def steps(tiles):
            rows = [pl.ds(pl.multiple_of(j * t, t), t) for j, _ in tiles]
            s = [[_flash_scores(qq_ref[h], k_ref[r, c]) for h, c in enumerate(heads)] for r in rows]
            carry = [(m_ref[h], acc_ref[h]) for h in range(HEADS_PER_STEP)]
            for ti, (_, diagonal) in enumerate(tiles):
                if diagonal:
                    s[ti] = [_mask_diagonal_tile(x) for x in s[ti]]
                carry = [
                    _flash_update(carry[h], s[ti][h], jnp.concatenate([vt_ref[c, rows[ti]], ones], axis=0),
                                  p_ref.at[stage0 + ti * HEADS_PER_STEP + h])
                    for h, c in enumerate(heads)]
            for h, (m, acc) in enumerate(carry):
                m_ref[h] = m
                acc_ref[h] = acc
            return 0

        m0, acc0 = _flash_init(2 * t)
        for h in range(HEADS_PER_STEP):
            m_ref[h] = m0
            acc_ref[h] = acc0
        nb = min(KEY_TILES_PER_BLOCK, k_ref.shape[0] // t)
        if nb < k_ref.shape[0] // t:
            lax.fori_loop(0, qi // nb, lambda p, _: steps([(nb * p + i, False) for i in range(nb)]), 0)
        left = qi % nb

        def last_block(n_full):
            return lambda _: steps([(qi - n_full + i, False) for i in range(n_full)] + [(qi, True)])

        lax.switch(left, [last_block(n) for n in range(nb)], 0)
        for h, c in enumerate(heads):
            o_ref[q_rows, c] = _attn_finish((m_ref[h], acc_ref[h]), lam, g_ref[...], lam_init, t).astype(BF16)
        return 0

    lax.fori_loop(0, q_tiles, one_query_tile, 0)


def _prompt_attention(lq, g, q, kb, vtb, lam_init, layer):
    b, s, _ = q.shape
    t = ATTN_TILE
    w = HEADS_PER_STEP * LANES
    tq = Q_TILES_PER_STEP * t
    return pl.pallas_call(
        functools.partial(_prompt_attn_kernel, lam_init=lam_init),
        grid=(b, N_HEADS // HEADS_PER_STEP, s // tq),
        in_specs=[
            _layer_spec((4, HEAD_DIM), layer),
            _layer_spec((V_DIM, 1), layer),
            pl.BlockSpec((None, tq, w), lambda b_, h, i: (b_, i, h)),
            pl.BlockSpec((None, s, w), lambda b_, h, i: (b_, 0, h)),
            pl.BlockSpec((None, w, s), lambda b_, h, i: (b_, h, 0)),
        ],
        out_specs=pl.BlockSpec((None, tq, w), lambda b_, h, i: (b_, i, h)),
        out_shape=jax.ShapeDtypeStruct((b, s, ATTN_WIDTH), BF16),
        scratch_shapes=[pltpu.VMEM((HEADS_PER_STEP, 2 * t, LANES), BF16),
                        pltpu.VMEM((KEY_TILES_PER_BLOCK * HEADS_PER_STEP, t, 2 * t), BF16),
                        pltpu.VMEM((HEADS_PER_STEP, 1, 2 * t), F32),
                        pltpu.VMEM((HEADS_PER_STEP, V_DIM + ONES_ROWS, 2 * t), F32)],
        compiler_params=pltpu.CompilerParams(dimension_semantics=("arbitrary",) * 3,
                                             vmem_limit_bytes=VMEM_LIMIT),
        name="prompt_attention",
    )(lq, g, q, kb, vtb)


def _row_scores(qq, kt, *, keys_on_rows):
    dims = (((1,), (1,)), ((), ())) if keys_on_rows else (((1,), (0,)), ((), ()))
    return lax.dot_general(qq, kt, dims, preferred_element_type=F32)


def _row_flash_update(carry, s, vt):
    m, l, acc = carry
    m_new = jnp.maximum(m, jnp.max(s, axis=-1, keepdims=True))
    alpha = jnp.exp2(m - m_new)
    p = jnp.exp2(s - m_new)
    l = alpha * l + jnp.sum(p, axis=-1, keepdims=True)
    acc = alpha * acc + jnp.dot(p.astype(BF16), vt, preferred_element_type=F32)
    return m_new, l, acc


def _sample_attn_kernel(lq_ref, g_ref, q_ref, ck_ref, cv_ref, k_ref, v_ref, o_ref, *, lam_init, past_len):
    t = SAMPLE_KEY_TILE
    tq = q_ref.shape[0]
    heads = [slice(h * LANES, (h + 1) * LANES) for h in range(N_HEADS)]
    qq = [_stack_maps(q_ref[:, c]) for c in heads]

    def body(j, carry):
        cols = pl.ds(pl.multiple_of(j * t, t), t)
        s = [_row_scores(qq[h], ck_ref[c, cols].astype(BF16), keys_on_rows=False) for h, c in enumerate(heads)]
        return tuple(
            _row_flash_update(carry[h], s[h], cv_ref[pl.ds(j * (t * N_HEADS) + h, t, stride=N_HEADS), :].astype(BF16))
            for h in range(N_HEADS))

    init = (jnp.full((2 * tq, 1), -jnp.inf, F32), jnp.zeros((2 * tq, 1), F32), jnp.zeros((2 * tq, V_DIM), F32))
    carry = lax.fori_loop(0, past_len // t, body, (init,) * N_HEADS)
    lam = _lambda_scalar(lq_ref[...], lam_init)
    for h, c in enumerate(heads):
        s = _row_scores(qq[h], k_ref[:, c], keys_on_rows=True)
        _, l, acc = _row_flash_update(carry[h], s, v_ref[:, c])
        o = acc[:tq] / l[:tq] - lam * (acc[tq:] / l[tq:])
        o_ref[:, c] = (_rms(o, g_ref[...]) * (1.0 - lam_init)).astype(BF16)


def _sample_attention(lq, g, q, cache_kt, cache_v4, kb, vb, lam_init, layer):
    b, ls, _ = q.shape
    past_len = cache_kt.shape[-1]
    assert past_len % SAMPLE_KEY_TILE == 0
    return pl.pallas_call(
        functools.partial(_sample_attn_kernel, lam_init=lam_init, past_len=past_len),
        grid=(b,),
        in_specs=[
            _layer_spec((4, HEAD_DIM), layer),
            _layer_spec((1, V_DIM), layer),
            pl.BlockSpec((None, ls, ATTN_WIDTH), lambda b_: (b_, 0, 0)),
            pl.BlockSpec((None, None, ATTN_WIDTH, past_len), lambda b_: (layer, b_, 0, 0)),
            pl.BlockSpec((None, None, N_HEADS * past_len, V_DIM), lambda b_: (layer, b_, 0, 0)),
            pl.BlockSpec((None, ls, ATTN_WIDTH), lambda b_: (b_, 0, 0)),
            pl.BlockSpec((None, ls, ATTN_WIDTH), lambda b_: (b_, 0, 0)),
        ],
        out_specs=pl.BlockSpec((None, ls, ATTN_WIDTH), lambda b_: (b_, 0, 0)),
        out_shape=jax.ShapeDtypeStruct((b, ls, ATTN_WIDTH), BF16),
        compiler_params=pltpu.CompilerParams(dimension_semantics=("arbitrary",),
                                             vmem_limit_bytes=VMEM_LIMIT),
        name="sample_attention",
    )(lq, g, q, cache_kt, cache_v4, kb, vb)


def _pool_mixer(ext_p, pool_w_ref, pool_scale_ref, tm, pos0):
    assert POOL_WINDOWS == (2, 4, 8, 16) and HALO == 32
    e = ext_p[...]
    n = HALO + tm
    s2 = e[8:] + e[7:n - 1]
    s4 = s2[8:] + s2[6:n - 10]
    s8 = s4[8:] + s4[4:n - 20]
    sums = {2: s2[24:], 4: s4[16:], 8: s8[8:], 16: s8[8:] + s8[:n - 32]}
    u = e[HALO:]
    group = lax.broadcasted_iota(jnp.int32, (1, POOL_WIDTH), 1) // POOL_GROUP_DIM
    win = sums[POOL_WINDOWS[-1]]
    width = jnp.full((1, POOL_WIDTH), POOL_WINDOWS[-1], jnp.int32)
    for gi in range(len(POOL_WINDOWS) - 2, -1, -1):
        win = jnp.where(group == gi, sums[POOL_WINDOWS[gi]], win)
        width = jnp.where(group == gi, POOL_WINDOWS[gi], width)
    head = max(POOL_WINDOWS)
    pos = pos0 + lax.broadcasted_iota(jnp.int32, (head, POOL_WIDTH), 0)
    d_head = win[:head] / jnp.minimum(pos + 1, width).astype(F32) - u[:head]
    d_rest = win[head:] * (1.0 / width.astype(F32)) - u[head:]
    d = jnp.concatenate([d_head, d_rest], axis=0).astype(BF16)
    return jnp.dot(d, pool_w_ref[...], preferred_element_type=F32) * pool_scale_ref[...]


CONV_ROW_BLOCK = 64


def _conv_mixer_steps(ext_c, shift_c, conv_out, dw_ref, dw_b_ref, ln_g_ref, ln_b_ref, pw_ref, tm, store):
    first_row = HALO - CONV_HIST
    blk = min(tm, CONV_ROW_BLOCK)

    def shift(residues):
        for r in residues:
            shift_c[r, 0:HALO - 8 + tm, :] = ext_c[pl.ds(r, HALO - 8 + tm), :]

    weights = []

    def broadcast_weights():
        weights.extend(jnp.broadcast_to(dw_ref[k:k + 1, :], (8, CONV_WIDTH)) for k in range(CONV_K))

    def taps(rb):
        yb = jnp.zeros((blk, CONV_WIDTH), F32)
        for k in range(CONV_K):
            r, a = (first_row + k) % 8, (first_row + k) // 8
            rows = pl.ds(8 * a + rb * blk, blk)
            src = ext_c[rows, :] if r == 0 else shift_c[r, rows, :]
            yb = yb + src * pltpu.repeat(weights[k], blk // 8, axis=0)
        conv_out[rb * blk:(rb + 1) * blk, :] = yb

    def finish():
        y = conv_out[...] + dw_b_ref[...]
        mu = jnp.mean(y, axis=-1, keepdims=True)
        yc = y - mu
        yn = yc * lax.rsqrt(jnp.mean(yc * yc, axis=-1, keepdims=True) + EPS) * ln_g_ref[...] + ln_b_ref[...]
        act = (yn * jax.nn.sigmoid(yn)).astype(BF16)
        store(jnp.dot(act, pw_ref[...], preferred_element_type=F32))

    steps = [broadcast_weights] + [functools.partial(shift, rs) for rs in ((1, 2), (3, 4, 5), (6, 7))]
    steps += [functools.partial(taps, rb) for rb in range(tm // blk)]
    return steps + [finish]


def _conv_mixer(ext_c, shift_c, conv_out, dw_ref, dw_b_ref, ln_g_ref, ln_b_ref, pw_ref, tm):
    out = []
    for step in _conv_mixer_steps(ext_c, shift_c, conv_out, dw_ref, dw_b_ref, ln_g_ref, ln_b_ref, pw_ref, tm,
                                  out.append):
        step()
    return out[0]


def _mixer_scratch(tm):
    return [pltpu.VMEM((HALO + tm, POOL_WIDTH), F32), pltpu.VMEM((HALO + tm, CONV_WIDTH), F32),
            pltpu.VMEM((8, HALO + tm, CONV_WIDTH), F32), pltpu.VMEM((tm, CONV_WIDTH), F32)]


def _mixer_weight_specs(layer):
    shapes = [(POOL_WIDTH, POOL_WIDTH), (1, POOL_WIDTH), (CONV_K, CONV_WIDTH), (1, CONV_WIDTH), (1, CONV_WIDTH),
              (1, CONV_WIDTH), (CONV_WIDTH, CONV_WIDTH)]
    return [_layer_spec(shape, layer) for shape in shapes]


def _sample_mixers_kernel(up_ref, up_hist_ref, uc_ref, uc_hist_ref,
                          pool_w_ref, pool_scale_ref, dw_ref, dw_b_ref, ln_g_ref, ln_b_ref, pw_ref,
                          ypool_ref, yconv_ref, ext_p, ext_c, shift_c, conv_out, *, pos_base):
    tm = up_ref.shape[0]
    ext_p[0:HALO, :] = up_hist_ref[...]
    ext_p[HALO:, :] = up_ref[...]
    ext_c[0:HALO, :] = uc_hist_ref[...]
    ext_c[HALO:, :] = uc_ref[...]
    ypool_ref[...] = _pool_mixer(ext_p, pool_w_ref, pool_scale_ref, tm, pos_base).astype(BF16)
    yconv_ref[...] = _conv_mixer(ext_c, shift_c, conv_out, dw_ref, dw_b_ref, ln_g_ref, ln_b_ref, pw_ref,
                                 tm).astype(BF16)


def _sample_mixers(upool, uconv, hist_pool, hist_conv, mixer_w, seq_len, pos_base, layer):
    n = upool.shape[0]
    cur = pl.BlockSpec((seq_len, POOL_WIDTH), lambda i: (i, 0))
    hist = pl.BlockSpec((None, None, HALO, POOL_WIDTH), lambda i: (layer, i, 0, 0))
    return pl.pallas_call(
        functools.partial(_sample_mixers_kernel, pos_base=pos_base),
        grid=(n // seq_len,),
        in_specs=[cur, hist, cur, hist] + _mixer_weight_specs(layer),
        out_specs=[cur, cur],
        out_shape=[jax.ShapeDtypeStruct((n, POOL_WIDTH), BF16), jax.ShapeDtypeStruct((n, CONV_WIDTH), BF16)],
        scratch_shapes=_mixer_scratch(seq_len),
        compiler_params=pltpu.CompilerParams(dimension_semantics=("arbitrary",),
                                             vmem_limit_bytes=VMEM_LIMIT),
        name="sample_mixers",
    )(upool, hist_pool, uconv, hist_conv, *mixer_w)


def _dense_math(x, ypool, o, yconv, wout_ref, g_ref, wgu_ref, wd_ref, fg_ref, final):
    o1 = POOL_WIDTH
    o2 = o1 + ATTN_WIDTH
    mix = (jnp.dot(ypool, wout_ref[0:o1, :], preferred_element_type=F32)
           + jnp.dot(o, wout_ref[o1:o2, :], preferred_element_type=F32)
           + jnp.dot(yconv, wout_ref[o2:, :], preferred_element_type=F32))
    x1 = x + mix
    h = _rms(x1, g_ref[...]).astype(BF16)
    gu = jnp.dot(h, wgu_ref[...], preferred_element_type=F32)
    gate = gu[:, :D_FF]
    a = (gate * jax.nn.sigmoid(gate) * gu[:, D_FF:]).astype(BF16)
    x2 = x1 + jnp.dot(a, wd_ref[...], preferred_element_type=F32)
    return _rms(x2, fg_ref[...]) if final else x2


def _dense_weight_specs(layer):
    shapes = [(MIX_WIDTH, D_MODEL), (1, D_MODEL), (D_MODEL, 2 * D_FF), (D_FF, D_MODEL)]
    return [_layer_spec(shape, layer) for shape in shapes] + [_const_spec((1, D_MODEL))]


def _prompt_dense_kernel(x_ref, o_ref, up0_ref, uc0_ref, up_ref, uc_ref, hist_p_ref, hist_c_ref,
                         pool_w_ref, pool_scale_ref, dw_ref, dw_b_ref, ln_g_ref, ln_b_ref, pw_ref,
                         wout_ref, g_ref, wgu_ref, wd_ref, fg_ref, out_ref,
                         ext_p, ext_c, shift_c, conv_out, y_pool, y_conv, *, final, tiles_per_seq):
    tm = x_ref.shape[0]
    i = pl.program_id(0)

    def mixer_steps(up_ref_, uc_ref_, tile, slot, first):
        def fill():
            if first is True:
                ext_p[0:HALO, :] = hist_p_ref[...]
                ext_c[0:HALO, :] = hist_c_ref[...]
            else:
                ext_p[0:HALO, :] = jnp.where(first, hist_p_ref[...], ext_p[tm:, :])
                ext_c[0:HALO, :] = jnp.where(first, hist_c_ref[...], ext_c[tm:, :])
            ext_p[HALO:, :] = up_ref_[...]
            ext_c[HALO:, :] = uc_ref_[...]

        def pool():
            pos0 = (tile % tiles_per_seq) * tm
            y_pool[slot] = _pool_mixer(ext_p, pool_w_ref, pool_scale_ref, tm, pos0).astype(BF16)

        def store_conv(y):
            y_conv[slot] = y.astype(BF16)

        return [fill] + _conv_mixer_steps(ext_c, shift_c, conv_out, dw_ref, dw_b_ref, ln_g_ref, ln_b_ref, pw_ref,
                                          tm, store_conv) + [pool]

    @pl.when(i == 0)
    def _():
        for step in mixer_steps(up0_ref, uc0_ref, 0, 0, True):
            step()

    slot = i % 2
    nxt = jnp.minimum(i + 1, pl.num_programs(0) - 1)
    out_ref[...] = _dense_math(x_ref[...], y_pool[slot], o_ref[...], y_conv[slot],
                               wout_ref, g_ref, wgu_ref, wd_ref, fg_ref, final)
    for step in mixer_steps(up_ref, uc_ref, nxt, 1 - slot, nxt % tiles_per_seq == 0):
        step()


def _prompt_dense(x, o, upool, uconv, hist_pool, hist_conv, mixer_w, dense_w, tm, seq_len, final, layer):
    n = x.shape[0]
    tps = seq_len // tm
    last = n // tm - 1
    row = lambda w: pl.BlockSpec((tm, w), lambda i: (i, 0))
    first_tile = lambda w: pl.BlockSpec((tm, w), lambda i: (0, 0))
    next_tile = lambda w: pl.BlockSpec((tm, w), lambda i: (jnp.minimum(i + 1, last), 0))
    next_seq = lambda w: pl.BlockSpec((None, HALO, w), lambda i: (jnp.minimum(i + 1, last) // tps, 0, 0))
    return pl.pallas_call(
        functools.partial(_prompt_dense_kernel, final=final, tiles_per_seq=tps),
        grid=(n // tm,),
        in_specs=[row(D_MODEL), row(ATTN_WIDTH), first_tile(POOL_WIDTH), first_tile(CONV_WIDTH),
                  next_tile(POOL_WIDTH), next_tile(CONV_WIDTH), next_seq(POOL_WIDTH), next_seq(CONV_WIDTH)]
        + _mixer_weight_specs(layer) + _dense_weight_specs(layer),
        out_specs=row(D_MODEL),
        out_shape=jax.ShapeDtypeStruct((n, D_MODEL), F32),
        scratch_shapes=_mixer_scratch(tm) + [pltpu.VMEM((2, tm, POOL_WIDTH), BF16),
                                             pltpu.VMEM((2, tm, CONV_WIDTH), BF16)],
        compiler_params=pltpu.CompilerParams(dimension_semantics=("arbitrary",),
                                             vmem_limit_bytes=VMEM_LIMIT),
        name="prompt_dense",
    )(x, o, upool, uconv, upool, uconv, hist_pool, hist_conv, *mixer_w, *dense_w)


def _sample_dense_kernel(x_ref, yp_ref, o_ref, yc_ref, wout_ref, g_ref, wgu_ref, wd_ref, fg_ref, out_ref, *, final):
    out_ref[...] = _dense_math(x_ref[...], yp_ref[...], o_ref[...], yc_ref[...],
                               wout_ref, g_ref, wgu_ref, wd_ref, fg_ref, final)


def _sample_dense(x, ypool, o, yconv, wout_bf, g, wgu_bf, wd_bf, fg, tm, final, layer):
    n = x.shape[0]
    row = lambda w: pl.BlockSpec((tm, w), lambda i: (i, 0))
    return pl.pallas_call(
        functools.partial(_sample_dense_kernel, final=final),
        grid=(n // tm,),
        in_specs=[row(D_MODEL), row(POOL_WIDTH), row(ATTN_WIDTH), row(CONV_WIDTH)] + _dense_weight_specs(layer),
        out_specs=row(D_MODEL),
        out_shape=jax.ShapeDtypeStruct((n, D_MODEL), F32),
        compiler_params=pltpu.CompilerParams(dimension_semantics=("arbitrary",),
                                             vmem_limit_bytes=VMEM_LIMIT),
        name="sample_dense",
    )(x, ypool, o, yconv, wout_bf, g, wgu_bf, wd_bf, fg)


def _rope_tables(pos):
    half = HEAD_DIM // 2
    inv = ROPE_THETA ** (-jnp.arange(half, dtype=F32) / half)
    ang = pos.astype(F32)[:, None] * inv[None, :]
    reps = LANES // half
    cos = jnp.tile(jnp.cos(ang), (1, reps))
    sign = jnp.tile(jnp.concatenate([-jnp.ones((half,), F32), jnp.ones((half,), F32)]), LANES // HEAD_DIM)
    sin = jnp.tile(jnp.sin(ang), (1, reps)) * sign[None, :]
    return cos, sin


def _block_diag(pool_w):
    depth, g, d, _ = pool_w.shape
    eye = jnp.eye(g, dtype=pool_w.dtype)
    return (pool_w[:, :, :, None, :] * eye[None, :, None, :, None]).reshape(depth, g * d, g * d)


def _pad_hist(h):
    return jnp.pad(h, ((0, 0), (0, 0), (HALO - h.shape[2], 0), (0, 0)))


def kernel(x_prompt, x_sample, cache_k, cache_v, state_pool, state_conv, norm_mix_g, w_in, pool_w, pool_scale,
           lambda_qk, diff_norm_g, conv_dw, conv_dw_b, conv_ln_g, conv_ln_b, conv_pw, w_out, norm_ffn_g,
           w_gate_up, w_down, final_norm_g):
    B, S, _ = x_prompt.shape
    Bs, Ls, _ = x_sample.shape
    depth = w_in.shape[0]
    past_len = cache_k.shape[2]
    assert S % (Q_TILES_PER_STEP * ATTN_TILE) == 0 and S % ROW_TILE == 0 and S % IN_PROJ_TILE == 0
    assert S // ROW_TILE >= 2
    assert (Bs * Ls) % 16 == 0 and Ls % 16 == 0 and Ls >= CONV_HIST

    cos_p, sin_p = _rope_tables(jnp.arange(S, dtype=jnp.int32))
    cos_s, sin_s = _rope_tables(past_len + jnp.arange(Ls, dtype=jnp.int32))
    cos_s, sin_s = jnp.tile(cos_s, (Bs, 1)), jnp.tile(sin_s, (Bs, 1))
    zero_pool = jnp.zeros((B, HALO, POOL_WIDTH), F32)
    zero_conv = jnp.zeros((B, HALO, CONV_WIDTH), F32)

    xp = x_prompt.reshape(B * S, D_MODEL)
    xs = x_sample.reshape(Bs * Ls, D_MODEL)
    rows = lambda a: a.reshape(depth, 1, -1)
    norm_mix_rows, norm_ffn_rows = rows(norm_mix_g), rows(norm_ffn_g)
    diff_g_row, diff_g_col = rows(diff_norm_g), diff_norm_g.reshape(depth, -1, 1)
    w_in_bf, w_out_bf, w_gu_bf, w_d_bf = (w.astype(BF16) for w in (w_in, w_out, w_gate_up, w_down))
    mixer_w = (_block_diag(pool_w).astype(BF16), rows(pool_scale), conv_dw, rows(conv_dw_b), rows(conv_ln_g),
               rows(conv_ln_b), conv_pw.astype(BF16))
    dense_w = (w_out_bf, norm_ffn_rows, w_gu_bf, w_d_bf, final_norm_g.reshape(1, -1))
    hist_pool_s, hist_conv_s = _pad_hist(state_pool), _pad_hist(state_conv)
    cache_kt = jnp.transpose(cache_k, (0, 1, 3, 4, 5, 2)).reshape(depth, Bs, ATTN_WIDTH, past_len)
    cache_v4 = cache_v.reshape(depth, Bs, past_len * N_HEADS, V_DIM)
    kv_stack = depth
    pp, cp = [], []
    ks_, vs_, ps_, cs_ = [], [], [], []
    for l in range(depth):
        lam_init = 0.8 - 0.6 * math.exp(-0.3 * l)
        final = l == depth - 1

        upool, q, k, kb, v, vb, uconv = _prompt_in_proj(xp, norm_mix_rows, w_in_bf, cos_p, sin_p, IN_PROJ_TILE, S,
                                                        kv_stack, l)
        kv_stack = (k, v)
        o = _prompt_attention(lambda_qk, diff_g_col, q.reshape(B, S, -1), kb.reshape(B, S, -1),
                              vb, lam_init, l)
        xp = _prompt_dense(xp, o.reshape(B * S, -1), upool, uconv, zero_pool, zero_conv, mixer_w, dense_w,
                           ROW_TILE, S, final, l)
        pp.append(upool.reshape(B, S, -1)[:, S - POOL_HIST:])
        cp.append(uconv.reshape(B, S, -1)[:, S - CONV_HIST:])

        upool, q, k, kb, v, vb, uconv = _sample_in_proj(xs, norm_mix_rows, w_in_bf, cos_s, sin_s, l)
        o = _sample_attention(lambda_qk, diff_g_row, q.reshape(Bs, Ls, -1), cache_kt, cache_v4,
                              kb.reshape(Bs, Ls, -1), vb.reshape(Bs, Ls, -1), lam_init, l)
        ypool, yconv = _sample_mixers(upool, uconv, hist_pool_s, hist_conv_s, mixer_w, Ls, past_len, l)
        xs = _sample_dense(xs, ypool, o.reshape(Bs * Ls, -1), yconv, *dense_w, Bs * Ls, final, l)
        ks_.append(k.reshape(Bs, Ls, N_HEADS, 2, HEAD_DIM))
        vs_.append(v.reshape(Bs, Ls, N_HEADS, V_DIM))
        ps_.append(upool.reshape(Bs, Ls, -1)[:, Ls - POOL_HIST:])
        cs_.append(uconv.reshape(Bs, Ls, -1)[:, Ls - CONV_HIST:])

    kt_all, v4_all = kv_stack
    new_k = jnp.transpose(kt_all.reshape(depth, B, N_HEADS, 2, HEAD_DIM, S), (0, 1, 5, 2, 3, 4))
    new_v = v4_all.reshape(depth, B, S, N_HEADS, V_DIM)
    return (xp.reshape(B, S, D_MODEL), xs.reshape(Bs, Ls, D_MODEL),
            new_k, new_v, jnp.stack(pp), jnp.stack(cp),
            jnp.stack(ks_), jnp.stack(vs_), jnp.stack(ps_), jnp.stack(cs_))
```

```python
import functools
import math

import jax
import jax.numpy as jnp
from jax import lax
from jax.experimental import pallas as pl
from jax.experimental.pallas import tpu as pltpu

D_MODEL = 1024
CHUNK = 64
POOL_WIDTH = D_MODEL // 4
POOL_WINDOWS = (2, 4, 8, 16)
POOL_GROUP_DIM = POOL_WIDTH // len(POOL_WINDOWS)
POOL_HIST = max(POOL_WINDOWS) - 1
ATTN_WIDTH = D_MODEL // 2
N_HEADS = 4
HEAD_DIM = ATTN_WIDTH // (2 * N_HEADS)
V_DIM = 2 * HEAD_DIM
ROPE_THETA = 10000.0
CONV_WIDTH = D_MODEL // 4
CONV_K = 31
CONV_HIST = CONV_K - 1
MIX_WIDTH = POOL_WIDTH + ATTN_WIDTH + CONV_WIDTH
IN_WIDTH = POOL_WIDTH + 3 * ATTN_WIDTH + 2 * CONV_WIDTH
D_FF = ((-(-8 * D_MODEL // 3) + 255) // 256) * 256
EPS = 1e-6

LANES = 128
HALO = 32
ROW_TILE = 512
IN_PROJ_TILE = 1024
ATTN_TILE = 256
SAMPLE_KEY_TILE = 2048
KEY_TILES_PER_BLOCK = 4
Q_TILES_PER_STEP = 8
HEADS_PER_STEP = 4
VMEM_LIMIT = 56 * 1024 * 1024

F32 = jnp.float32
BF16 = jnp.bfloat16


def _rms(x, g):
    return x * lax.rsqrt(jnp.mean(x * x, axis=-1, keepdims=True) + EPS) * g


def _const_spec(shape):
    return pl.BlockSpec(shape, lambda *_: (0,) * len(shape), pipeline_mode=pl.Buffered(1))


def _layer_spec(shape, layer):
    return pl.BlockSpec((None,) + shape, lambda *_: (layer,) + (0,) * len(shape), pipeline_mode=pl.Buffered(1))


O_Q = POOL_WIDTH
O_K = O_Q + ATTN_WIDTH
O_V = O_K + ATTN_WIDTH
O_A = O_V + ATTN_WIDTH
O_B = O_A + CONV_WIDTH


def _qkv_epilogue(z, cos_ref, sin_ref, q_ref, k_ref, kb_ref, v_ref, vb_ref, cache_layout):
    tm = z.shape[0]
    o1, o2, o3 = O_Q, O_K, O_V
    cos = cos_ref[...]
    sin = sin_ref[...]
    lane = lax.broadcasted_iota(jnp.int32, cos.shape, 1)
    first_half = (lane & (HEAD_DIM // 2)) == 0

    def rope(t):
        up = pltpu.roll(t, LANES - HEAD_DIM // 2, axis=1)
        down = pltpu.roll(t, HEAD_DIM // 2, axis=1)
        return t * cos + jnp.where(first_half, up, down) * sin

    scale = HEAD_DIM ** -0.5 * math.log2(math.e)
    for j in range(ATTN_WIDTH // LANES):
        sl = slice(j * LANES, (j + 1) * LANES)
        qj = rope(z[:, o1 + j * LANES:o1 + (j + 1) * LANES])
        kj = rope(z[:, o2 + j * LANES:o2 + (j + 1) * LANES])
        vj = z[:, o3 + j * LANES:o3 + (j + 1) * LANES]
        q_ref[:, sl] = (qj * scale).astype(BF16)
        kb_ref[:, sl] = kj.astype(BF16)
        if cache_layout:
            k_ref[sl, :] = kj.T
            v_ref[pl.ds(j, tm, stride=N_HEADS), :] = vj
            vb_ref[sl, :] = vj.T.astype(BF16)
        else:
            k_ref[:, sl] = kj
            v_ref[:, sl] = vj
            vb_ref[:, sl] = vj.astype(BF16)


def _sample_in_proj_kernel(x_ref, g_ref, w_ref, cos_ref, sin_ref,
                           upool_ref, q_ref, k_ref, kb_ref, v_ref, vb_ref, uconv_ref):
    h = _rms(x_ref[...], g_ref[...]).astype(BF16)
    z = jnp.dot(h, w_ref[...], preferred_element_type=F32)
    upool_ref[...] = z[:, :O_Q]
    uconv_ref[...] = z[:, O_A:O_B] * jax.nn.sigmoid(z[:, O_B:])
    _qkv_epilogue(z, cos_ref, sin_ref, q_ref, k_ref, kb_ref, v_ref, vb_ref, False)


def _sample_in_proj(x, g, w_bf, cos_tab, sin_tab, layer):
    n = x.shape[0]
    full = lambda w: pl.BlockSpec((n, w), lambda i: (0, 0))
    flat = lambda dt: jax.ShapeDtypeStruct((n, ATTN_WIDTH), dt)
    return pl.pallas_call(
        _sample_in_proj_kernel,
        grid=(1,),
        in_specs=[full(D_MODEL), _layer_spec((1, D_MODEL), layer), _layer_spec((D_MODEL, IN_WIDTH), layer),
                  full(LANES), full(LANES)],
        out_specs=[full(POOL_WIDTH)] + [full(ATTN_WIDTH)] * 5 + [full(CONV_WIDTH)],
        out_shape=[jax.ShapeDtypeStruct((n, POOL_WIDTH), F32), flat(BF16), flat(F32), flat(BF16), flat(F32),
                   flat(BF16), jax.ShapeDtypeStruct((n, CONV_WIDTH), F32)],
        compiler_params=pltpu.CompilerParams(dimension_semantics=("arbitrary",),
                                             vmem_limit_bytes=VMEM_LIMIT),
        name="sample_in_proj",
    )(x, g, w_bf, cos_tab, sin_tab)


def _prompt_in_proj_kernel(*refs, n_unused):
    x_ref, g_ref, w_ref, cos_ref, sin_ref = refs[:5]
    upool_ref, q_ref, k_ref, kb_ref, v_ref, vb_ref, uconv_ref = refs[5 + n_unused:]
    h = _rms(x_ref[...], g_ref[...]).astype(BF16)
    z = jnp.dot(h, w_ref[...], preferred_element_type=F32)
    upool_ref[...] = z[:, :O_Q]
    uconv_ref[...] = z[:, O_A:O_B] * jax.nn.sigmoid(z[:, O_B:])
    _qkv_epilogue(z, cos_ref, sin_ref, q_ref, k_ref, kb_ref, v_ref, vb_ref, True)


def _prompt_in_proj(x, g, w_bf, cos_tab, sin_tab, tm, seq_len, kv_stack, layer):
    n = x.shape[0]
    tps = seq_len // tm
    batch = n // seq_len
    row = lambda w: pl.BlockSpec((tm, w), lambda i: (i, 0))
    tab = pl.BlockSpec((tm, LANES), lambda i: (i % tps, 0))
    flat = lambda dt, w=ATTN_WIDTH: jax.ShapeDtypeStruct((n, w), dt)
    extra_in, extra_specs, aliases = [], [], {}
    if isinstance(kv_stack, int):
        k_shape = jax.ShapeDtypeStruct((kv_stack, batch, ATTN_WIDTH, seq_len), F32)
        v_shape = jax.ShapeDtypeStruct((kv_stack, batch, N_HEADS * seq_len, V_DIM), F32)
    else:
        k_shape, v_shape = (jax.ShapeDtypeStruct(a.shape, a.dtype) for a in kv_stack)
        extra_in = list(kv_stack)
        extra_specs = [pl.BlockSpec(memory_space=pl.ANY)] * 2
        aliases = {5: 2, 6: 4}
    k_spec = pl.BlockSpec((None, None, ATTN_WIDTH, tm), lambda i: (layer, i // tps, 0, i % tps))
    v_spec = pl.BlockSpec((None, None, N_HEADS * tm, V_DIM), lambda i: (layer, i // tps, i % tps, 0))
    vt_spec = pl.BlockSpec((None, ATTN_WIDTH, tm), lambda i: (i // tps, 0, i % tps))
    return pl.pallas_call(
        functools.partial(_prompt_in_proj_kernel, n_unused=len(extra_in)),
        grid=(n // tm,),
        in_specs=[row(D_MODEL), _layer_spec((1, D_MODEL), layer), _layer_spec((D_MODEL, IN_WIDTH), layer), tab, tab]
        + extra_specs,
        out_specs=[row(POOL_WIDTH), row(ATTN_WIDTH), k_spec, row(ATTN_WIDTH), v_spec, vt_spec, row(CONV_WIDTH)],
        out_shape=[flat(F32, POOL_WIDTH), flat(BF16), k_shape, flat(BF16), v_shape,
                   jax.ShapeDtypeStruct((batch, ATTN_WIDTH, seq_len), BF16), flat(F32, CONV_WIDTH)],
        input_output_aliases=aliases,
        compiler_params=pltpu.CompilerParams(dimension_semantics=("arbitrary",),
                                             vmem_limit_bytes=VMEM_LIMIT),
        name="prompt_in_proj",
    )(x, g, w_bf, cos_tab, sin_tab, *extra_in)


def _lambda_scalar(lq, lam_init):
    a = jnp.sum(lq[0:1, :] * lq[1:2, :], axis=-1, keepdims=True)
    b = jnp.sum(lq[2:3, :] * lq[3:4, :], axis=-1, keepdims=True)
    return jnp.exp(a) - jnp.exp(b) + lam_init


def _stack_maps(q):
    lane = lax.broadcasted_iota(jnp.int32, q.shape, 1)
    zero = jnp.zeros_like(q)
    return jnp.concatenate([jnp.where(lane < HEAD_DIM, q, zero), jnp.where(lane >= HEAD_DIM, q, zero)], axis=0)


ONES_ROWS = 16


def _flash_scores(qq, kt):
    return lax.dot_general(kt, qq, (((1,), (1,)), ((), ())), preferred_element_type=F32)


def _flash_update(carry, s, vt_ones, p_stage):
    m, acc = carry
    m_new = jnp.maximum(m, jnp.max(s, axis=0, keepdims=True))
    alpha = jnp.exp2(m - m_new)
    p_stage[...] = jnp.exp2(s - m_new).astype(BF16)
    return m_new, alpha * acc + jnp.dot(vt_ones, p_stage[...], preferred_element_type=F32)


def _flash_init(cols):
    return jnp.full((1, cols), -jnp.inf, F32), jnp.zeros((V_DIM + ONES_ROWS, cols), F32)


def _attn_finish(carry, lam, g_col, lam_init, tq):
    _, acc = carry
    scaled = acc[:V_DIM] * (1.0 / acc[V_DIM:V_DIM + 1])
    o = scaled[:, :tq] - lam * scaled[:, tq:]
    o = o * lax.rsqrt(jnp.mean(o * o, axis=0, keepdims=True) + EPS) * g_col
    return (o * (1.0 - lam_init)).T


def _mask_diagonal_tile(s):
    t = s.shape[0]
    q_chunk = (lax.broadcasted_iota(jnp.int32, (1, s.shape[1]), 1) % t) // CHUNK
    blocks = [jnp.where(q_chunk >= a, s[a * CHUNK:(a + 1) * CHUNK, :], -1e30) for a in range(t // CHUNK)]
    return jnp.concatenate(blocks, axis=0)


def _prompt_attn_kernel(lq_ref, g_ref, q_ref, k_ref, vt_ref, o_ref, qq_ref, p_ref, m_ref, acc_ref, *, lam_init):
    t = ATTN_TILE
    heads = [slice(h * LANES, (h + 1) * LANES) for h in range(HEADS_PER_STEP)]
    ones = jnp.ones((ONES_ROWS, t), BF16)
    lam = _lambda_scalar(lq_ref[...], lam_init)
    q_tiles = q_ref.shape[0] // t

    def one_query_tile(sub, _):
        qi = pl.program_id(2) * q_tiles + sub
        q_rows = pl.ds(pl.multiple_of(sub * t, t), t)
        for h, c in enumerate(heads):
            qq_ref[h] = _stack_maps(q_ref[q_rows, c])
        stage0 = jnp.minimum(qi, 0)

        def steps(tiles):
            rows = [pl.ds(pl.multiple_of(j * t, t), t) for j, _ in tiles]
            s = [[_flash_scores(qq_ref[h], k_ref[r, c]) for h, c in enumerate(heads)] for r in rows]
            carry = [(m_ref[h], acc_ref[h]) for h in range(HEADS_PER_STEP)]
            for ti, (_, diagonal) in enumerate(tiles):
                if diagonal:
                    s[ti] = [_mask_diagonal_tile(x) for x in s[ti]]
                carry = [
                    _flash_update(carry[h], s[ti][h], jnp.concatenate([vt_ref[c, rows[ti]], ones], axis=0),
                                  p_ref.at[stage0 + ti * HEADS_PER_STEP + h])
                    for h, c in enumerate(heads)]
            for h, (m, acc) in enumerate(carry):
                m_ref[h] = m
                acc_ref[h] = acc
            return 0

        m0, acc0 = _flash_init(2 * t)
        for h in range(HEADS_PER_STEP):
            m_ref[h] = m0
            acc_ref[h] = acc0
        nb = min(KEY_TILES_PER_BLOCK, k_ref.shape[0] // t)
        if nb < k_ref.shape[0] // t:
            lax.fori_loop(0, qi // nb, lambda p, _: steps([(nb * p + i, False) for i in range(nb)]), 0)
        left = qi % nb

        def last_block(n_full):
            return lambda _: steps([(qi - n_full + i, False) for i in range(n_full)] + [(qi, True)])

        lax.switch(left, [last_block(n) for n in range(nb)], 0)
        for h, c in enumerate(heads):
            o_ref[q_rows, c] = _attn_finish((m_ref[h], acc_ref[h]), lam, g_ref[...], lam_init, t).astype(BF16)
        return 0

    lax.fori_loop(0, q_tiles, one_query_tile, 0)


def _prompt_attention(lq, g, q, kb, vtb, lam_init, layer):
    b, s, _ = q.shape
    t = ATTN_TILE
    w = HEADS_PER_STEP * LANES
    tq = Q_TILES_PER_STEP * t
    return pl.pallas_call(
        functools.partial(_prompt_attn_kernel, lam_init=lam_init),
        grid=(b, N_HEADS // HEADS_PER_STEP, s // tq),
        in_specs=[
            _layer_spec((4, HEAD_DIM), layer),
            _layer_spec((V_DIM, 1), layer),
            pl.BlockSpec((None, tq, w), lambda b_, h, i: (b_, i, h)),
            pl.BlockSpec((None, s, w), lambda b_, h, i: (b_, 0, h)),
            pl.BlockSpec((None, w, s), lambda b_, h, i: (b_, h, 0)),
        ],
        out_specs=pl.BlockSpec((None, tq, w), lambda b_, h, i: (b_, i, h)),
        out_shape=jax.ShapeDtypeStruct((b, s, ATTN_WIDTH), BF16),
        scratch_shapes=[pltpu.VMEM((HEADS_PER_STEP, 2 * t, LANES), BF16),
                        pltpu.VMEM((KEY_TILES_PER_BLOCK * HEADS_PER_STEP, t, 2 * t), BF16),
                        pltpu.VMEM((HEADS_PER_STEP, 1, 2 * t), F32),
                        pltpu.VMEM((HEADS_PER_STEP, V_DIM + ONES_ROWS, 2 * t), F32)],
        compiler_params=pltpu.CompilerParams(dimension_semantics=("arbitrary",) * 3,
                                             vmem_limit_bytes=VMEM_LIMIT),
        name="prompt_attention",
    )(lq, g, q, kb, vtb)


def _row_scores(qq, kt, *, keys_on_rows):
    dims = (((1,), (1,)), ((), ())) if keys_on_rows else (((1,), (0,)), ((), ()))
    return lax.dot_general(qq, kt, dims, preferred_element_type=F32)


def _row_flash_update(carry, s, vt):
    m, l, acc = carry
    m_new = jnp.maximum(m, jnp.max(s, axis=-1, keepdims=True))
    alpha = jnp.exp2(m - m_new)
    p = jnp.exp2(s - m_new)
    l = alpha * l + jnp.sum(p, axis=-1, keepdims=True)
    acc = alpha * acc + jnp.dot(p.astype(BF16), vt, preferred_element_type=F32)
    return m_new, l, acc


def _sample_attn_kernel(lq_ref, g_ref, q_ref, ck_ref, cv_ref, k_ref, v_ref, o_ref, *, lam_init, past_len):
    t = SAMPLE_KEY_TILE
    tq = q_ref.shape[0]
    heads = [slice(h * LANES, (h + 1) * LANES) for h in range(N_HEADS)]
    qq = [_stack_maps(q_ref[:, c]) for c in heads]

    def body(j, carry):
        cols = pl.ds(pl.multiple_of(j * t, t), t)
        s = [_row_scores(qq[h], ck_ref[c, cols].astype(BF16), keys_on_rows=False) for h, c in enumerate(heads)]
        return tuple(
            _row_flash_update(carry[h], s[h], cv_ref[pl.ds(j * (t * N_HEADS) + h, t, stride=N_HEADS), :].astype(BF16))
            for h in range(N_HEADS))

    init = (jnp.full((2 * tq, 1), -jnp.inf, F32), jnp.zeros((2 * tq, 1), F32), jnp.zeros((2 * tq, V_DIM), F32))
    carry = lax.fori_loop(0, past_len // t, body, (init,) * N_HEADS)
    lam = _lambda_scalar(lq_ref[...], lam_init)
    for h, c in enumerate(heads):
        s = _row_scores(qq[h], k_ref[:, c], keys_on_rows=True)
        _, l, acc = _row_flash_update(carry[h], s, v_ref[:, c])
        o = acc[:tq] / l[:tq] - lam * (acc[tq:] / l[tq:])
        o_ref[:, c] = (_rms(o, g_ref[...]) * (1.0 - lam_init)).astype(BF16)


def _sample_attention(lq, g, q, cache_kt, cache_v4, kb, vb, lam_init, layer):
    b, ls, _ = q.shape
    past_len = cache_kt.shape[-1]
    assert past_len % SAMPLE_KEY_TILE == 0
    return pl.pallas_call(
        functools.partial(_sample_attn_kernel, lam_init=lam_init, past_len=past_len),
        grid=(b,),
        in_specs=[
            _layer_spec((4, HEAD_DIM), layer),
            _layer_spec((1, V_DIM), layer),
            pl.BlockSpec((None, ls, ATTN_WIDTH), lambda b_: (b_, 0, 0)),
            pl.BlockSpec((None, None, ATTN_WIDTH, past_len), lambda b_: (layer, b_, 0, 0)),
            pl.BlockSpec((None, None, N_HEADS * past_len, V_DIM), lambda b_: (layer, b_, 0, 0)),
            pl.BlockSpec((None, ls, ATTN_WIDTH), lambda b_: (b_, 0, 0)),
            pl.BlockSpec((None, ls, ATTN_WIDTH), lambda b_: (b_, 0, 0)),
        ],
        out_specs=pl.BlockSpec((None, ls, ATTN_WIDTH), lambda b_: (b_, 0, 0)),
        out_shape=jax.ShapeDtypeStruct((b, ls, ATTN_WIDTH), BF16),
        compiler_params=pltpu.CompilerParams(dimension_semantics=("arbitrary",),
                                             vmem_limit_bytes=VMEM_LIMIT),
        name="sample_attention",
    )(lq, g, q, cache_kt, cache_v4, kb, vb)


def _pool_mixer(ext_p, pool_w_ref, pool_scale_ref, tm, pos0):
    assert POOL_WINDOWS == (2, 4, 8, 16) and HALO == 32
    e = ext_p[...]
    n = HALO + tm
    s2 = e[8:] + e[7:n - 1]
    s4 = s2[8:] + s2[6:n - 10]
    s8 = s4[8:] + s4[4:n - 20]
    sums = {2: s2[24:], 4: s4[16:], 8: s8[8:], 16: s8[8:] + s8[:n - 32]}
    u = e[HALO:]
    group = lax.broadcasted_iota(jnp.int32, (1, POOL_WIDTH), 1) // POOL_GROUP_DIM
    win = sums[POOL_WINDOWS[-1]]
    width = jnp.full((1, POOL_WIDTH), POOL_WINDOWS[-1], jnp.int32)
    for gi in range(len(POOL_WINDOWS) - 2, -1, -1):
        win = jnp.where(group == gi, sums[POOL_WINDOWS[gi]], win)
        width = jnp.where(group == gi, POOL_WINDOWS[gi], width)
    head = max(POOL_WINDOWS)
    pos = pos0 + lax.broadcasted_iota(jnp.int32, (head, POOL_WIDTH), 0)
    d_head = win[:head] / jnp.minimum(pos + 1, width).astype(F32) - u[:head]
    d_rest = win[head:] * (1.0 / width.astype(F32)) - u[head:]
    d = jnp.concatenate([d_head, d_rest], axis=0).astype(BF16)
    return jnp.dot(d, pool_w_ref[...], preferred_element_type=F32) * pool_scale_ref[...]


CONV_ROW_BLOCK = 64


def _conv_mixer_steps(ext_c, shift_c, conv_out, dw_ref, dw_b_ref, ln_g_ref, ln_b_ref, pw_ref, tm, store):
    first_row = HALO - CONV_HIST
    blk = min(tm, CONV_ROW_BLOCK)

    def shift(residues):
        for r in residues:
            shift_c[r, 0:HALO - 8 + tm, :] = ext_c[pl.ds(r, HALO - 8 + tm), :]

    weights = []

    def broadcast_weights():
        weights.extend(jnp.broadcast_to(dw_ref[k:k + 1, :], (8, CONV_WIDTH)) for k in range(CONV_K))

    def taps(rb):
        yb = jnp.zeros((blk, CONV_WIDTH), F32)
        for k in range(CONV_K):
            r, a = (first_row + k) % 8, (first_row + k) // 8
            rows = pl.ds(8 * a + rb * blk, blk)
            src = ext_c[rows, :] if r == 0 else shift_c[r, rows, :]
            yb = yb + src * jnp.concatenate([weights[k]] * (blk // 8), axis=0)
        conv_out[rb * blk:(rb + 1) * blk, :] = yb

    def finish():
        y = conv_out[...] + dw_b_ref[...]
        mu = jnp.mean(y, axis=-1, keepdims=True)
        yc = y - mu
        yn = yc * lax.rsqrt(jnp.mean(yc * yc, axis=-1, keepdims=True) + EPS) * ln_g_ref[...] + ln_b_ref[...]
        act = (yn * jax.nn.sigmoid(yn)).astype(BF16)
        store(jnp.dot(act, pw_ref[...], preferred_element_type=F32))

    steps = [broadcast_weights] + [functools.partial(shift, rs) for rs in ((1, 2), (3, 4, 5), (6, 7))]
    steps += [functools.partial(taps, rb) for rb in range(tm // blk)]
    return steps + [finish]


def _conv_mixer(ext_c, shift_c, conv_out, dw_ref, dw_b_ref, ln_g_ref, ln_b_ref, pw_ref, tm):
    out = []
    for step in _conv_mixer_steps(ext_c, shift_c, conv_out, dw_ref, dw_b_ref, ln_g_ref, ln_b_ref, pw_ref, tm,
                                  out.append):
        step()
    return out[0]


def _mixer_scratch(tm):
    return [pltpu.VMEM((HALO + tm, POOL_WIDTH), F32), pltpu.VMEM((HALO + tm, CONV_WIDTH), F32),
            pltpu.VMEM((8, HALO + tm, CONV_WIDTH), F32), pltpu.VMEM((tm, CONV_WIDTH), F32)]


def _mixer_weight_specs(layer):
    shapes = [(POOL_WIDTH, POOL_WIDTH), (1, POOL_WIDTH), (CONV_K, CONV_WIDTH), (1, CONV_WIDTH), (1, CONV_WIDTH),
              (1, CONV_WIDTH), (CONV_WIDTH, CONV_WIDTH)]
    return [_layer_spec(shape, layer) for shape in shapes]


def _sample_mixers_kernel(up_ref, up_hist_ref, uc_ref, uc_hist_ref,
                          pool_w_ref, pool_scale_ref, dw_ref, dw_b_ref, ln_g_ref, ln_b_ref, pw_ref,
                          ypool_ref, yconv_ref, ext_p, ext_c, shift_c, conv_out, *, pos_base):
    tm = up_ref.shape[0]
    ext_p[0:HALO, :] = up_hist_ref[...]
    ext_p[HALO:, :] = up_ref[...]
    ext_c[0:HALO, :] = uc_hist_ref[...]
    ext_c[HALO:, :] = uc_ref[...]
    ypool_ref[...] = _pool_mixer(ext_p, pool_w_ref, pool_scale_ref, tm, pos_base).astype(BF16)
    yconv_ref[...] = _conv_mixer(ext_c, shift_c, conv_out, dw_ref, dw_b_ref, ln_g_ref, ln_b_ref, pw_ref,
                                 tm).astype(BF16)


def _sample_mixers(upool, uconv, hist_pool, hist_conv, mixer_w, seq_len, pos_base, layer):
    n = upool.shape[0]
    cur = pl.BlockSpec((seq_len, POOL_WIDTH), lambda i: (i, 0))
    hist = pl.BlockSpec((None, None, HALO, POOL_WIDTH), lambda i: (layer, i, 0, 0))
    return pl.pallas_call(
        functools.partial(_sample_mixers_kernel, pos_base=pos_base),
        grid=(n // seq_len,),
        in_specs=[cur, hist, cur, hist] + _mixer_weight_specs(layer),
        out_specs=[cur, cur],
        out_shape=[jax.ShapeDtypeStruct((n, POOL_WIDTH), BF16), jax.ShapeDtypeStruct((n, CONV_WIDTH), BF16)],
        scratch_shapes=_mixer_scratch(seq_len),
        compiler_params=pltpu.CompilerParams(dimension_semantics=("arbitrary",),
                                             vmem_limit_bytes=VMEM_LIMIT),
        name="sample_mixers",
    )(upool, hist_pool, uconv, hist_conv, *mixer_w)


def _dense_math(x, ypool, o, yconv, wout_ref, g_ref, wgu_ref, wd_ref, fg_ref, final):
    o1 = POOL_WIDTH
    o2 = o1 + ATTN_WIDTH
    mix = (jnp.dot(ypool, wout_ref[0:o1, :], preferred_element_type=F32)
           + jnp.dot(o, wout_ref[o1:o2, :], preferred_element_type=F32)
           + jnp.dot(yconv, wout_ref[o2:, :], preferred_element_type=F32))
    x1 = x + mix
    h = _rms(x1, g_ref[...]).astype(BF16)
    gu = jnp.dot(h, wgu_ref[...], preferred_element_type=F32)
    gate = gu[:, :D_FF]
    a = (gate * jax.nn.sigmoid(gate) * gu[:, D_FF:]).astype(BF16)
    x2 = x1 + jnp.dot(a, wd_ref[...], preferred_element_type=F32)
    return _rms(x2, fg_ref[...]) if final else x2


def _dense_weight_specs(layer):
    shapes = [(MIX_WIDTH, D_MODEL), (1, D_MODEL), (D_MODEL, 2 * D_FF), (D_FF, D_MODEL)]
    return [_layer_spec(shape, layer) for shape in shapes] + [_const_spec((1, D_MODEL))]


def _prompt_dense_kernel(x_ref, o_ref, up0_ref, uc0_ref, up_ref, uc_ref, hist_p_ref, hist_c_ref,
                         pool_w_ref, pool_scale_ref, dw_ref, dw_b_ref, ln_g_ref, ln_b_ref, pw_ref,
                         wout_ref, g_ref, wgu_ref, wd_ref, fg_ref, out_ref,
                         ext_p, ext_c, shift_c, conv_out, y_pool, y_conv, *, final, tiles_per_seq):
    tm = x_ref.shape[0]
    i = pl.program_id(0)

    def mixer_steps(up_ref_, uc_ref_, tile, slot, first):
        def fill():
            if first is True:
                ext_p[0:HALO, :] = hist_p_ref[...]
                ext_c[0:HALO, :] = hist_c_ref[...]
            else:
                ext_p[0:HALO, :] = jnp.where(first, hist_p_ref[...], ext_p[tm:, :])
                ext_c[0:HALO, :] = jnp.where(first, hist_c_ref[...], ext_c[tm:, :])
            ext_p[HALO:, :] = up_ref_[...]
            ext_c[HALO:, :] = uc_ref_[...]

        def pool():
            pos0 = (tile % tiles_per_seq) * tm
            y_pool[slot] = _pool_mixer(ext_p, pool_w_ref, pool_scale_ref, tm, pos0).astype(BF16)

        def store_conv(y):
            y_conv[slot] = y.astype(BF16)

        return [fill] + _conv_mixer_steps(ext_c, shift_c, conv_out, dw_ref, dw_b_ref, ln_g_ref, ln_b_ref, pw_ref,
                                          tm, store_conv) + [pool]

    @pl.when(i == 0)
    def _():
        for step in mixer_steps(up0_ref, uc0_ref, 0, 0, True):
            step()

    slot = i % 2
    nxt = jnp.minimum(i + 1, pl.num_programs(0) - 1)
    out_ref[...] = _dense_math(x_ref[...], y_pool[slot], o_ref[...], y_conv[slot],
                               wout_ref, g_ref, wgu_ref, wd_ref, fg_ref, final)
    for step in mixer_steps(up_ref, uc_ref, nxt, 1 - slot, nxt % tiles_per_seq == 0):
        step()


def _prompt_dense(x, o, upool, uconv, hist_pool, hist_conv, mixer_w, dense_w, tm, seq_len, final, layer):
    n = x.shape[0]
    tps = seq_len // tm
    last = n // tm - 1
    row = lambda w: pl.BlockSpec((tm, w), lambda i: (i, 0))
    first_tile = lambda w: pl.BlockSpec((tm, w), lambda i: (0, 0))
    next_tile = lambda w: pl.BlockSpec((tm, w), lambda i: (jnp.minimum(i + 1, last), 0))
    next_seq = lambda w: pl.BlockSpec((None, HALO, w), lambda i: (jnp.minimum(i + 1, last) // tps, 0, 0))
    return pl.pallas_call(
        functools.partial(_prompt_dense_kernel, final=final, tiles_per_seq=tps),
        grid=(n // tm,),
        in_specs=[row(D_MODEL), row(ATTN_WIDTH), first_tile(POOL_WIDTH), first_tile(CONV_WIDTH),
                  next_tile(POOL_WIDTH), next_tile(CONV_WIDTH), next_seq(POOL_WIDTH), next_seq(CONV_WIDTH)]
        + _mixer_weight_specs(layer) + _dense_weight_specs(layer),
        out_specs=row(D_MODEL),
        out_shape=jax.ShapeDtypeStruct((n, D_MODEL), F32),
        scratch_shapes=_mixer_scratch(tm) + [pltpu.VMEM((2, tm, POOL_WIDTH), BF16),
                                             pltpu.VMEM((2, tm, CONV_WIDTH), BF16)],
        compiler_params=pltpu.CompilerParams(dimension_semantics=("arbitrary",),
                                             vmem_limit_bytes=VMEM_LIMIT),
        name="prompt_dense",
    )(x, o, upool, uconv, upool, uconv, hist_pool, hist_conv, *mixer_w, *dense_w)


def _sample_dense_kernel(x_ref, yp_ref, o_ref, yc_ref, wout_ref, g_ref, wgu_ref, wd_ref, fg_ref, out_ref, *, final):
    out_ref[...] = _dense_math(x_ref[...], yp_ref[...], o_ref[...], yc_ref[...],
                               wout_ref, g_ref, wgu_ref, wd_ref, fg_ref, final)


def _sample_dense(x, ypool, o, yconv, wout_bf, g, wgu_bf, wd_bf, fg, tm, final, layer):
    n = x.shape[0]
    row = lambda w: pl.BlockSpec((tm, w), lambda i: (i, 0))
    return pl.pallas_call(
        functools.partial(_sample_dense_kernel, final=final),
        grid=(n // tm,),
        in_specs=[row(D_MODEL), row(POOL_WIDTH), row(ATTN_WIDTH), row(CONV_WIDTH)] + _dense_weight_specs(layer),
        out_specs=row(D_MODEL),
        out_shape=jax.ShapeDtypeStruct((n, D_MODEL), F32),
        compiler_params=pltpu.CompilerParams(dimension_semantics=("arbitrary",),
                                             vmem_limit_bytes=VMEM_LIMIT),
        name="sample_dense",
    )(x, ypool, o, yconv, wout_bf, g, wgu_bf, wd_bf, fg)


def _rope_tables(pos):
    half = HEAD_DIM // 2
    inv = ROPE_THETA ** (-jnp.arange(half, dtype=F32) / half)
    ang = pos.astype(F32)[:, None] * inv[None, :]
    reps = LANES // half
    cos = jnp.tile(jnp.cos(ang), (1, reps))
    sign = jnp.tile(jnp.concatenate([-jnp.ones((half,), F32), jnp.ones((half,), F32)]), LANES // HEAD_DIM)
    sin = jnp.tile(jnp.sin(ang), (1, reps)) * sign[None, :]
    return cos, sin


def _block_diag(pool_w):
    depth, g, d, _ = pool_w.shape
    eye = jnp.eye(g, dtype=pool_w.dtype)
    return (pool_w[:, :, :, None, :] * eye[None, :, None, :, None]).reshape(depth, g * d, g * d)


def _pad_hist(h):
    return jnp.pad(h, ((0, 0), (0, 0), (HALO - h.shape[2], 0), (0, 0)))


def kernel(x_prompt, x_sample, cache_k, cache_v, state_pool, state_conv, norm_mix_g, w_in, pool_w, pool_scale,
           lambda_qk, diff_norm_g, conv_dw, conv_dw_b, conv_ln_g, conv_ln_b, conv_pw, w_out, norm_ffn_g,
           w_gate_up, w_down, final_norm_g):
    B, S, _ = x_prompt.shape
    Bs, Ls, _ = x_sample.shape
    depth = w_in.shape[0]
    past_len = cache_k.shape[2]
    assert S % (Q_TILES_PER_STEP * ATTN_TILE) == 0 and S % ROW_TILE == 0 and S % IN_PROJ_TILE == 0
    assert S // ROW_TILE >= 2
    assert (Bs * Ls) % 16 == 0 and Ls % 16 == 0 and Ls >= CONV_HIST

    cos_p, sin_p = _rope_tables(jnp.arange(S, dtype=jnp.int32))
    cos_s, sin_s = _rope_tables(past_len + jnp.arange(Ls, dtype=jnp.int32))
    cos_s, sin_s = jnp.tile(cos_s, (Bs, 1)), jnp.tile(sin_s, (Bs, 1))
    zero_pool = jnp.zeros((B, HALO, POOL_WIDTH), F32)
    zero_conv = jnp.zeros((B, HALO, CONV_WIDTH), F32)

    xp = x_prompt.reshape(B * S, D_MODEL)
    xs = x_sample.reshape(Bs * Ls, D_MODEL)
    rows = lambda a: a.reshape(depth, 1, -1)
    norm_mix_rows, norm_ffn_rows = rows(norm_mix_g), rows(norm_ffn_g)
    diff_g_row, diff_g_col = rows(diff_norm_g), diff_norm_g.reshape(depth, -1, 1)
    w_in_bf, w_out_bf, w_gu_bf, w_d_bf = (w.astype(BF16) for w in (w_in, w_out, w_gate_up, w_down))
    mixer_w = (_block_diag(pool_w).astype(BF16), rows(pool_scale), conv_dw, rows(conv_dw_b), rows(conv_ln_g),
               rows(conv_ln_b), conv_pw.astype(BF16))
    dense_w = (w_out_bf, norm_ffn_rows, w_gu_bf, w_d_bf, final_norm_g.reshape(1, -1))
    hist_pool_s, hist_conv_s = _pad_hist(state_pool), _pad_hist(state_conv)
    cache_kt = jnp.transpose(cache_k, (0, 1, 3, 4, 5, 2)).reshape(depth, Bs, ATTN_WIDTH, past_len)
    cache_v4 = cache_v.reshape(depth, Bs, past_len * N_HEADS, V_DIM)
    kv_stack = depth
    pp, cp = [], []
    ks_, vs_, ps_, cs_ = [], [], [], []
    for l in range(depth):
        lam_init = 0.8 - 0.6 * math.exp(-0.3 * l)
        final = l == depth - 1

        upool, q, k, kb, v, vb, uconv = _prompt_in_proj(xp, norm_mix_rows, w_in_bf, cos_p, sin_p, IN_PROJ_TILE, S,
                                                        kv_stack, l)
        kv_stack = (k, v)
        o = _prompt_attention(lambda_qk, diff_g_col, q.reshape(B, S, -1), kb.reshape(B, S, -1),
                              vb, lam_init, l)
        xp = _prompt_dense(xp, o.reshape(B * S, -1), upool, uconv, zero_pool, zero_conv, mixer_w, dense_w,
                           ROW_TILE, S, final, l)
        pp.append(upool.reshape(B, S, -1)[:, S - POOL_HIST:])
        cp.append(uconv.reshape(B, S, -1)[:, S - CONV_HIST:])

        upool, q, k, kb, v, vb, uconv = _sample_in_proj(xs, norm_mix_rows, w_in_bf, cos_s, sin_s, l)
        o = _sample_attention(lambda_qk, diff_g_row, q.reshape(Bs, Ls, -1), cache_kt, cache_v4,
                              kb.reshape(Bs, Ls, -1), vb.reshape(Bs, Ls, -1), lam_init, l)
        ypool, yconv = _sample_mixers(upool, uconv, hist_pool_s, hist_conv_s, mixer_w, Ls, past_len, l)
        xs = _sample_dense(xs, ypool, o.reshape(Bs * Ls, -1), yconv, *dense_w, Bs * Ls, final, l)
        ks_.append(k.reshape(Bs, Ls, N_HEADS, 2, HEAD_DIM))
        vs_.append(v.reshape(Bs, Ls, N_HEADS, V_DIM))
        ps_.append(upool.reshape(Bs, Ls, -1)[:, Ls - POOL_HIST:])
        cs_.append(uconv.reshape(Bs, Ls, -1)[:, Ls - CONV_HIST:])

    kt_all, v4_all = kv_stack
    new_k = jnp.transpose(kt_all.reshape(depth, B, N_HEADS, 2, HEAD_DIM, S), (0, 1, 5, 2, 3, 4))
    new_v = v4_all.reshape(depth, B, S, N_HEADS, V_DIM)
    return (xp.reshape(B, S, D_MODEL), xs.reshape(Bs, Ls, D_MODEL),
            new_k, new_v, jnp.stack(pp), jnp.stack(cp),
            jnp.stack(ks_), jnp.stack(vs_), jnp.stack(ps_), jnp.stack(cs_))
```

```python
import functools
import math

import jax
import jax.numpy as jnp
from jax import lax
from jax.experimental import pallas as pl
from jax.experimental.pallas import tpu as pltpu

D_MODEL = 1024
CHUNK = 64
POOL_WIDTH = D_MODEL // 4
POOL_WINDOWS = (2, 4, 8, 16)
POOL_GROUP_DIM = POOL_WIDTH // len(POOL_WINDOWS)
POOL_HIST = max(POOL_WINDOWS) - 1
ATTN_WIDTH = D_MODEL // 2
N_HEADS = 4
HEAD_DIM = ATTN_WIDTH // (2 * N_HEADS)
V_DIM = 2 * HEAD_DIM
ROPE_THETA = 10000.0
CONV_WIDTH = D_MODEL // 4
CONV_K = 31
CONV_HIST = CONV_K - 1
MIX_WIDTH = POOL_WIDTH + ATTN_WIDTH + CONV_WIDTH
IN_WIDTH = POOL_WIDTH + 3 * ATTN_WIDTH + 2 * CONV_WIDTH
D_FF = ((-(-8 * D_MODEL // 3) + 255) // 256) * 256
EPS = 1e-6

LANES = 128
HALO = 32
ROW_TILE = 512
IN_PROJ_TILE = 1024
ATTN_TILE = 256
SAMPLE_KEY_TILE = 2048
KEY_TILES_PER_BLOCK = 8
Q_TILES_PER_STEP = 8
HEADS_PER_STEP = 4
VMEM_LIMIT = 56 * 1024 * 1024

F32 = jnp.float32
BF16 = jnp.bfloat16


def _rms(x, g):
    return x * lax.rsqrt(jnp.mean(x * x, axis=-1, keepdims=True) + EPS) * g


def _const_spec(shape):
    return pl.BlockSpec(shape, lambda *_: (0,) * len(shape), pipeline_mode=pl.Buffered(1))


def _layer_spec(shape, layer):
    return pl.BlockSpec((None,) + shape, lambda *_: (layer,) + (0,) * len(shape), pipeline_mode=pl.Buffered(1))


O_Q = POOL_WIDTH
O_K = O_Q + ATTN_WIDTH
O_V = O_K + ATTN_WIDTH
O_A = O_V + ATTN_WIDTH
O_B = O_A + CONV_WIDTH


def _qkv_epilogue(z, cos_ref, sin_ref, q_ref, k_ref, kb_ref, v_ref, vb_ref, cache_layout):
    tm = z.shape[0]
    o1, o2, o3 = O_Q, O_K, O_V
    cos = cos_ref[...]
    sin = sin_ref[...]
    lane = lax.broadcasted_iota(jnp.int32, cos.shape, 1)
    first_half = (lane & (HEAD_DIM // 2)) == 0

    def rope(t):
        up = pltpu.roll(t, LANES - HEAD_DIM // 2, axis=1)
        down = pltpu.roll(t, HEAD_DIM // 2, axis=1)
        return t * cos + jnp.where(first_half, up, down) * sin

    scale = HEAD_DIM ** -0.5 * math.log2(math.e)
    for j in range(ATTN_WIDTH // LANES):
        sl = slice(j * LANES, (j + 1) * LANES)
        qj = rope(z[:, o1 + j * LANES:o1 + (j + 1) * LANES])
        kj = rope(z[:, o2 + j * LANES:o2 + (j + 1) * LANES])
        vj = z[:, o3 + j * LANES:o3 + (j + 1) * LANES]
        q_ref[:, sl] = (qj * scale).astype(BF16)
        kb_ref[:, sl] = kj.astype(BF16)
        if cache_layout:
            k_ref[sl, :] = kj.T
            v_ref[pl.ds(j, tm, stride=N_HEADS), :] = vj
            vb_ref[sl, :] = vj.T.astype(BF16)
        else:
            k_ref[:, sl] = kj
            v_ref[:, sl] = vj
            vb_ref[:, sl] = vj.astype(BF16)


def _sample_in_proj_kernel(x_ref, g_ref, w_ref, cos_ref, sin_ref,
                           upool_ref, q_ref, k_ref, kb_ref, v_ref, vb_ref, uconv_ref):
    h = _rms(x_ref[...], g_ref[...]).astype(BF16)
    z = jnp.dot(h, w_ref[...], preferred_element_type=F32)
    upool_ref[...] = z[:, :O_Q]
    uconv_ref[...] = z[:, O_A:O_B] * jax.nn.sigmoid(z[:, O_B:])
    _qkv_epilogue(z, cos_ref, sin_ref, q_ref, k_ref, kb_ref, v_ref, vb_ref, False)


def _sample_in_proj(x, g, w_bf, cos_tab, sin_tab, layer):
    n = x.shape[0]
    full = lambda w: pl.BlockSpec((n, w), lambda i: (0, 0))
    flat = lambda dt: jax.ShapeDtypeStruct((n, ATTN_WIDTH), dt)
    return pl.pallas_call(
        _sample_in_proj_kernel,
        grid=(1,),
        in_specs=[full(D_MODEL), _layer_spec((1, D_MODEL), layer), _layer_spec((D_MODEL, IN_WIDTH), layer),
                  full(LANES), full(LANES)],
        out_specs=[full(POOL_WIDTH)] + [full(ATTN_WIDTH)] * 5 + [full(CONV_WIDTH)],
        out_shape=[jax.ShapeDtypeStruct((n, POOL_WIDTH), F32), flat(BF16), flat(F32), flat(BF16), flat(F32),
                   flat(BF16), jax.ShapeDtypeStruct((n, CONV_WIDTH), F32)],
        compiler_params=pltpu.CompilerParams(dimension_semantics=("arbitrary",),
                                             vmem_limit_bytes=VMEM_LIMIT),
        name="sample_in_proj",
    )(x, g, w_bf, cos_tab, sin_tab)


def _prompt_in_proj_kernel(*refs, n_unused):
    x_ref, g_ref, w_ref, cos_ref, sin_ref = refs[:5]
    upool_ref, q_ref, k_ref, kb_ref, v_ref, vb_ref, uconv_ref = refs[5 + n_unused:]
    h = _rms(x_ref[...], g_ref[...]).astype(BF16)
    z = jnp.dot(h, w_ref[...], preferred_element_type=F32)
    upool_ref[...] = z[:, :O_Q]
    uconv_ref[...] = z[:, O_A:O_B] * jax.nn.sigmoid(z[:, O_B:])
    _qkv_epilogue(z, cos_ref, sin_ref, q_ref, k_ref, kb_ref, v_ref, vb_ref, True)


def _prompt_in_proj(x, g, w_bf, cos_tab, sin_tab, tm, seq_len, kv_stack, layer):
    n = x.shape[0]
    tps = seq_len // tm
    batch = n // seq_len
    row = lambda w: pl.BlockSpec((tm, w), lambda i: (i, 0))
    tab = pl.BlockSpec((tm, LANES), lambda i: (i % tps, 0))
    flat = lambda dt, w=ATTN_WIDTH: jax.ShapeDtypeStruct((n, w), dt)
    extra_in, extra_specs, aliases = [], [], {}
    if isinstance(kv_stack, int):
        k_shape = jax.ShapeDtypeStruct((kv_stack, batch, ATTN_WIDTH, seq_len), F32)
        v_shape = jax.ShapeDtypeStruct((kv_stack, batch, N_HEADS * seq_len, V_DIM), F32)
    else:
        k_shape, v_shape = (jax.ShapeDtypeStruct(a.shape, a.dtype) for a in kv_stack)
        extra_in = list(kv_stack)
        extra_specs = [pl.BlockSpec(memory_space=pl.ANY)] * 2
        aliases = {5: 2, 6: 4}
    k_spec = pl.BlockSpec((None, None, ATTN_WIDTH, tm), lambda i: (layer, i // tps, 0, i % tps))
    v_spec = pl.BlockSpec((None, None, N_HEADS * tm, V_DIM), lambda i: (layer, i // tps, i % tps, 0))
    vt_spec = pl.BlockSpec((None, ATTN_WIDTH, tm), lambda i: (i // tps, 0, i % tps))
    return pl.pallas_call(
        functools.partial(_prompt_in_proj_kernel, n_unused=len(extra_in)),
        grid=(n // tm,),
        in_specs=[row(D_MODEL), _layer_spec((1, D_MODEL), layer), _layer_spec((D_MODEL, IN_WIDTH), layer), tab, tab]
        + extra_specs,
        out_specs=[row(POOL_WIDTH), row(ATTN_WIDTH), k_spec, row(ATTN_WIDTH), v_spec, vt_spec, row(CONV_WIDTH)],
        out_shape=[flat(F32, POOL_WIDTH), flat(BF16), k_shape, flat(BF16), v_shape,
                   jax.ShapeDtypeStruct((batch, ATTN_WIDTH, seq_len), BF16), flat(F32, CONV_WIDTH)],
        input_output_aliases=aliases,
        compiler_params=pltpu.CompilerParams(dimension_semantics=("arbitrary",),
                                             vmem_limit_bytes=VMEM_LIMIT),
        name="prompt_in_proj",
    )(x, g, w_bf, cos_tab, sin_tab, *extra_in)


def _lambda_scalar(lq, lam_init):
    a = jnp.sum(lq[0:1, :] * lq[1:2, :], axis=-1, keepdims=True)
    b = jnp.sum(lq[2:3, :] * lq[3:4, :], axis=-1, keepdims=True)
    return jnp.exp(a) - jnp.exp(b) + lam_init


def _stack_maps(q):
    lane = lax.broadcasted_iota(jnp.int32, q.shape, 1)
    zero = jnp.zeros_like(q)
    return jnp.concatenate([jnp.where(lane < HEAD_DIM, q, zero), jnp.where(lane >= HEAD_DIM, q, zero)], axis=0)


ONES_ROWS = 16


def _flash_scores(qq, kt):
    return lax.dot_general(kt, qq, (((1,), (1,)), ((), ())), preferred_element_type=F32)


def _flash_update(carry, s, vt_ones, p_stage):
    m, acc = carry
    m_new = jnp.maximum(m, jnp.max(s, axis=0, keepdims=True))
    alpha = jnp.exp2(m - m_new)
    p_stage[...] = jnp.exp2(s - m_new).astype(BF16)
    return m_new, alpha * acc + jnp.dot(vt_ones, p_stage[...], preferred_element_type=F32)


def _flash_init(cols):
    return jnp.full((1, cols), -jnp.inf, F32), jnp.zeros((V_DIM + ONES_ROWS, cols), F32)


def _attn_finish(carry, lam, g_col, lam_init, tq):
    _, acc = carry
    scaled = acc[:V_DIM] * (1.0 / acc[V_DIM:V_DIM + 1])
    o = scaled[:, :tq] - lam * scaled[:, tq:]
    o = o * lax.rsqrt(jnp.mean(o * o, axis=0, keepdims=True) + EPS) * g_col
    return (o * (1.0 - lam_init)).T


def _mask_diagonal_tile(s):
    t = s.shape[0]
    q_chunk = (lax.broadcasted_iota(jnp.int32, (1, s.shape[1]), 1) % t) // CHUNK
    blocks = [jnp.where(q_chunk >= a, s[a * CHUNK:(a + 1) * CHUNK, :], -1e30) for a in range(t // CHUNK)]
    return jnp.concatenate(blocks, axis=0)


def _prompt_attn_kernel(lq_ref, g_ref, q_ref, k_ref, vt_ref, o_ref, qq_ref, p_ref, m_ref, acc_ref, *, lam_init):
    t = ATTN_TILE
    heads = [slice(h * LANES, (h + 1) * LANES) for h in range(HEADS_PER_STEP)]
    ones = jnp.ones((ONES_ROWS, t), BF16)
    lam = _lambda_scalar(lq_ref[...], lam_init)
    q_tiles = q_ref.shape[0] // t

    def one_query_tile(sub, _):
        qi = pl.program_id(2) * q_tiles + sub
        q_rows = pl.ds(pl.multiple_of(sub * t, t), t)
        for h, c in enumerate(heads):
            qq_ref[h] = _stack_maps(q_ref[q_rows, c])
        stage0 = jnp.minimum(qi, 0)

        def steps(tiles):
            rows = [pl.ds(pl.multiple_of(j * t, t), t) for j, _ in tiles]
            s = [[_flash_scores(qq_ref[h], k_ref[r, c]) for h, c in enumerate(heads)] for r in rows]
            carry = [(m_ref[h], acc_ref[h]) for h in range(HEADS_PER_STEP)]
            for ti, (_, diagonal) in enumerate(tiles):
                if diagonal:
                    s[ti] = [_mask_diagonal_tile(x) for x in s[ti]]
                carry = [
                    _flash_update(carry[h], s[ti][h], jnp.concatenate([vt_ref[c, rows[ti]], ones], axis=0),
                                  p_ref.at[stage0 + ti * HEADS_PER_STEP + h])
                    for h, c in enumerate(heads)]
            for h, (m, acc) in enumerate(carry):
                m_ref[h] = m
                acc_ref[h] = acc
            return 0

        m0, acc0 = _flash_init(2 * t)
        for h in range(HEADS_PER_STEP):
            m_ref[h] = m0
            acc_ref[h] = acc0
        nb = min(KEY_TILES_PER_BLOCK, k_ref.shape[0] // t)
        if nb < k_ref.shape[0] // t:
            lax.fori_loop(0, qi // nb, lambda p, _: steps([(nb * p + i, False) for i in range(nb)]), 0)
        left = qi % nb

        def last_block(n_full):
            return lambda _: steps([(qi - n_full + i, False) for i in range(n_full)] + [(qi, True)])

        lax.switch(left, [last_block(n) for n in range(nb)], 0)
        for h, c in enumerate(heads):
            o_ref[q_rows, c] = _attn_finish((m_ref[h], acc_ref[h]), lam, g_ref[...], lam_init, t).astype(BF16)
        return 0

    lax.fori_loop(0, q_tiles, one_query_tile, 0)


def _prompt_attention(lq, g, q, kb, vtb, lam_init, layer):
    b, s, _ = q.shape
    t = ATTN_TILE
    w = HEADS_PER_STEP * LANES
    tq = Q_TILES_PER_STEP * t
    return pl.pallas_call(
        functools.partial(_prompt_attn_kernel, lam_init=lam_init),
        grid=(b, N_HEADS // HEADS_PER_STEP, s // tq),
        in_specs=[
            _layer_spec((4, HEAD_DIM), layer),
            _layer_spec((V_DIM, 1), layer),
            pl.BlockSpec((None, tq, w), lambda b_, h, i: (b_, i, h)),
            pl.BlockSpec((None, s, w), lambda b_, h, i: (b_, 0, h)),
            pl.BlockSpec((None, w, s), lambda b_, h, i: (b_, h, 0)),
        ],
        out_specs=pl.BlockSpec((None, tq, w), lambda b_, h, i: (b_, i, h)),
        out_shape=jax.ShapeDtypeStruct((b, s, ATTN_WIDTH), BF16),
        scratch_shapes=[pltpu.VMEM((HEADS_PER_STEP, 2 * t, LANES), BF16),
                        pltpu.VMEM((KEY_TILES_PER_BLOCK * HEADS_PER_STEP, t, 2 * t), BF16),
                        pltpu.VMEM((HEADS_PER_STEP, 1, 2 * t), F32),
                        pltpu.VMEM((HEADS_PER_STEP, V_DIM + ONES_ROWS, 2 * t), F32)],
        compiler_params=pltpu.CompilerParams(dimension_semantics=("arbitrary",) * 3,
                                             vmem_limit_bytes=VMEM_LIMIT),
        name="prompt_attention",
    )(lq, g, q, kb, vtb)


def _row_scores(qq, kt, *, keys_on_rows):
    dims = (((1,), (1,)), ((), ())) if keys_on_rows else (((1,), (0,)), ((), ()))
    return lax.dot_general(qq, kt, dims, preferred_element_type=F32)


def _row_flash_update(carry, s, vt):
    m, l, acc = carry
    m_new = jnp.maximum(m, jnp.max(s, axis=-1, keepdims=True))
    alpha = jnp.exp2(m - m_new)
    p = jnp.exp2(s - m_new)
    l = alpha * l + jnp.sum(p, axis=-1, keepdims=True)
    acc = alpha * acc + jnp.dot(p.astype(BF16), vt, preferred_element_type=F32)
    return m_new, l, acc


def _sample_attn_kernel(lq_ref, g_ref, q_ref, ck_ref, cv_ref, k_ref, v_ref, o_ref, *, lam_init, past_len):
    t = SAMPLE_KEY_TILE
    tq = q_ref.shape[0]
    heads = [slice(h * LANES, (h + 1) * LANES) for h in range(N_HEADS)]
    qq = [_stack_maps(q_ref[:, c]) for c in heads]

    def body(j, carry):
        cols = pl.ds(pl.multiple_of(j * t, t), t)
        s = [_row_scores(qq[h], ck_ref[c, cols].astype(BF16), keys_on_rows=False) for h, c in enumerate(heads)]
        return tuple(
            _row_flash_update(carry[h], s[h], cv_ref[pl.ds(j * (t * N_HEADS) + h, t, stride=N_HEADS), :].astype(BF16))
            for h in range(N_HEADS))

    init = (jnp.full((2 * tq, 1), -jnp.inf, F32), jnp.zeros((2 * tq, 1), F32), jnp.zeros((2 * tq, V_DIM), F32))
    carry = lax.fori_loop(0, past_len // t, body, (init,) * N_HEADS)
    lam = _lambda_scalar(lq_ref[...], lam_init)
    for h, c in enumerate(heads):
        s = _row_scores(qq[h], k_ref[:, c], keys_on_rows=True)
        _, l, acc = _row_flash_update(carry[h], s, v_ref[:, c])
        o = acc[:tq] / l[:tq] - lam * (acc[tq:] / l[tq:])
        o_ref[:, c] = (_rms(o, g_ref[...]) * (1.0 - lam_init)).astype(BF16)


def _sample_attention(lq, g, q, cache_kt, cache_v4, kb, vb, lam_init, layer):
    b, ls, _ = q.shape
    past_len = cache_kt.shape[-1]
    assert past_len % SAMPLE_KEY_TILE == 0
    return pl.pallas_call(
        functools.partial(_sample_attn_kernel, lam_init=lam_init, past_len=past_len),
        grid=(b,),
        in_specs=[
            _layer_spec((4, HEAD_DIM), layer),
            _layer_spec((1, V_DIM), layer),
            pl.BlockSpec((None, ls, ATTN_WIDTH), lambda b_: (b_, 0, 0)),
            pl.BlockSpec((None, None, ATTN_WIDTH, past_len), lambda b_: (layer, b_, 0, 0)),
            pl.BlockSpec((None, None, N_HEADS * past_len, V_DIM), lambda b_: (layer, b_, 0, 0)),
            pl.BlockSpec((None, ls, ATTN_WIDTH), lambda b_: (b_, 0, 0)),
            pl.BlockSpec((None, ls, ATTN_WIDTH), lambda b_: (b_, 0, 0)),
        ],
        out_specs=pl.BlockSpec((None, ls, ATTN_WIDTH), lambda b_: (b_, 0, 0)),
        out_shape=jax.ShapeDtypeStruct((b, ls, ATTN_WIDTH), BF16),
        compiler_params=pltpu.CompilerParams(dimension_semantics=("arbitrary",),
                                             vmem_limit_bytes=VMEM_LIMIT),
        name="sample_attention",
    )(lq, g, q, cache_kt, cache_v4, kb, vb)


def _pool_mixer(ext_p, pool_w_ref, pool_scale_ref, tm, pos0):
    assert POOL_WINDOWS == (2, 4, 8, 16) and HALO == 32
    e = ext_p[...]
    n = HALO + tm
    s2 = e[8:] + e[7:n - 1]
    s4 = s2[8:] + s2[6:n - 10]
    s8 = s4[8:] + s4[4:n - 20]
    sums = {2: s2[24:], 4: s4[16:], 8: s8[8:], 16: s8[8:] + s8[:n - 32]}
    u = e[HALO:]
    group = lax.broadcasted_iota(jnp.int32, (1, POOL_WIDTH), 1) // POOL_GROUP_DIM
    win = sums[POOL_WINDOWS[-1]]
    width = jnp.full((1, POOL_WIDTH), POOL_WINDOWS[-1], jnp.int32)
    for gi in range(len(POOL_WINDOWS) - 2, -1, -1):
        win = jnp.where(group == gi, sums[POOL_WINDOWS[gi]], win)
        width = jnp.where(group == gi, POOL_WINDOWS[gi], width)
    head = max(POOL_WINDOWS)
    pos = pos0 + lax.broadcasted_iota(jnp.int32, (head, POOL_WIDTH), 0)
    d_head = win[:head] / jnp.minimum(pos + 1, width).astype(F32) - u[:head]
    d_rest = win[head:] * (1.0 / width.astype(F32)) - u[head:]
    d = jnp.concatenate([d_head, d_rest], axis=0).astype(BF16)
    return jnp.dot(d, pool_w_ref[...], preferred_element_type=F32) * pool_scale_ref[...]


CONV_ROW_BLOCK = 64


def _conv_mixer_steps(ext_c, shift_c, conv_out, dw_ref, dw_b_ref, ln_g_ref, ln_b_ref, pw_ref, tm, store):
    first_row = HALO - CONV_HIST
    blk = min(tm, CONV_ROW_BLOCK)

    def shift(residues):
        for r in residues:
            shift_c[r, 0:HALO - 8 + tm, :] = ext_c[pl.ds(r, HALO - 8 + tm), :]

    weights = []

    def broadcast_weights():
        weights.extend(jnp.broadcast_to(dw_ref[k:k + 1, :], (8, CONV_WIDTH)) for k in range(CONV_K))

    def taps(rb):
        yb = jnp.zeros((blk, CONV_WIDTH), F32)
        for k in range(CONV_K):
            r, a = (first_row + k) % 8, (first_row + k) // 8
            rows = pl.ds(8 * a + rb * blk, blk)
            src = ext_c[rows, :] if r == 0 else shift_c[r, rows, :]
            yb = yb + src * jnp.concatenate([weights[k]] * (blk // 8), axis=0)
        conv_out[rb * blk:(rb + 1) * blk, :] = yb

    def finish():
        y = conv_out[...] + dw_b_ref[...]
        mu = jnp.mean(y, axis=-1, keepdims=True)
        yc = y - mu
        yn = yc * lax.rsqrt(jnp.mean(yc * yc, axis=-1, keepdims=True) + EPS) * ln_g_ref[...] + ln_b_ref[...]
        act = (yn * jax.nn.sigmoid(yn)).astype(BF16)
        store(jnp.dot(act, pw_ref[...], preferred_element_type=F32))

    steps = [broadcast_weights] + [functools.partial(shift, rs) for rs in ((1, 2), (3, 4, 5), (6, 7))]
    steps += [functools.partial(taps, rb) for rb in range(tm // blk)]
    return steps + [finish]


def _conv_mixer(ext_c, shift_c, conv_out, dw_ref, dw_b_ref, ln_g_ref, ln_b_ref, pw_ref, tm):
    out = []
    for step in _conv_mixer_steps(ext_c, shift_c, conv_out, dw_ref, dw_b_ref, ln_g_ref, ln_b_ref, pw_ref, tm,
                                  out.append):
        step()
    return out[0]


def _mixer_scratch(tm):
    return [pltpu.VMEM((HALO + tm, POOL_WIDTH), F32), pltpu.VMEM((HALO + tm, CONV_WIDTH), F32),
            pltpu.VMEM((8, HALO + tm, CONV_WIDTH), F32), pltpu.VMEM((tm, CONV_WIDTH), F32)]


def _mixer_weight_specs(layer):
    shapes = [(POOL_WIDTH, POOL_WIDTH), (1, POOL_WIDTH), (CONV_K, CONV_WIDTH), (1, CONV_WIDTH), (1, CONV_WIDTH),
              (1, CONV_WIDTH), (CONV_WIDTH, CONV_WIDTH)]
    return [_layer_spec(shape, layer) for shape in shapes]


def _sample_mixers_kernel(up_ref, up_hist_ref, uc_ref, uc_hist_ref,
                          pool_w_ref, pool_scale_ref, dw_ref, dw_b_ref, ln_g_ref, ln_b_ref, pw_ref,
                          ypool_ref, yconv_ref, ext_p, ext_c, shift_c, conv_out, *, pos_base):
    tm = up_ref.shape[0]
    ext_p[0:HALO, :] = up_hist_ref[...]
    ext_p[HALO:, :] = up_ref[...]
    ext_c[0:HALO, :] = uc_hist_ref[...]
    ext_c[HALO:, :] = uc_ref[...]
    ypool_ref[...] = _pool_mixer(ext_p, pool_w_ref, pool_scale_ref, tm, pos_base).astype(BF16)
    yconv_ref[...] = _conv_mixer(ext_c, shift_c, conv_out, dw_ref, dw_b_ref, ln_g_ref, ln_b_ref, pw_ref,
                                 tm).astype(BF16)


def _sample_mixers(upool, uconv, hist_pool, hist_conv, mixer_w, seq_len, pos_base, layer):
    n = upool.shape[0]
    cur = pl.BlockSpec((seq_len, POOL_WIDTH), lambda i: (i, 0))
    hist = pl.BlockSpec((None, None, HALO, POOL_WIDTH), lambda i: (layer, i, 0, 0))
    return pl.pallas_call(
        functools.partial(_sample_mixers_kernel, pos_base=pos_base),
        grid=(n // seq_len,),
        in_specs=[cur, hist, cur, hist] + _mixer_weight_specs(layer),
        out_specs=[cur, cur],
        out_shape=[jax.ShapeDtypeStruct((n, POOL_WIDTH), BF16), jax.ShapeDtypeStruct((n, CONV_WIDTH), BF16)],
        scratch_shapes=_mixer_scratch(seq_len),
        compiler_params=pltpu.CompilerParams(dimension_semantics=("arbitrary",),
                                             vmem_limit_bytes=VMEM_LIMIT),
        name="sample_mixers",
    )(upool, hist_pool, uconv, hist_conv, *mixer_w)


def _dense_math(x, ypool, o, yconv, wout_ref, g_ref, wgu_ref, wd_ref, fg_ref, final):
    o1 = POOL_WIDTH
    o2 = o1 + ATTN_WIDTH
    mix = (jnp.dot(ypool, wout_ref[0:o1, :], preferred_element_type=F32)
           + jnp.dot(o, wout_ref[o1:o2, :], preferred_element_type=F32)
           + jnp.dot(yconv, wout_ref[o2:, :], preferred_element_type=F32))
    x1 = x + mix
    h = _rms(x1, g_ref[...]).astype(BF16)
    gu = jnp.dot(h, wgu_ref[...], preferred_element_type=F32)
    gate = gu[:, :D_FF]
    a = (gate * jax.nn.sigmoid(gate) * gu[:, D_FF:]).astype(BF16)
    x2 = x1 + jnp.dot(a, wd_ref[...], preferred_element_type=F32)
    return _rms(x2, fg_ref[...]) if final else x2


def _dense_weight_specs(layer):
    shapes = [(MIX_WIDTH, D_MODEL), (1, D_MODEL), (D_MODEL, 2 * D_FF), (D_FF, D_MODEL)]
    return [_layer_spec(shape, layer) for shape in shapes] + [_const_spec((1, D_MODEL))]


def _prompt_dense_kernel(x_ref, o_ref, up0_ref, uc0_ref, up_ref, uc_ref, hist_p_ref, hist_c_ref,
                         pool_w_ref, pool_scale_ref, dw_ref, dw_b_ref, ln_g_ref, ln_b_ref, pw_ref,
                         wout_ref, g_ref, wgu_ref, wd_ref, fg_ref, out_ref,
                         ext_p, ext_c, shift_c, conv_out, y_pool, y_conv, *, final, tiles_per_seq):
    tm = x_ref.shape[0]
    i = pl.program_id(0)

    def mixer_steps(up_ref_, uc_ref_, tile, slot, first):
        def fill():
            if first is True:
                ext_p[0:HALO, :] = hist_p_ref[...]
                ext_c[0:HALO, :] = hist_c_ref[...]
            else:
                ext_p[0:HALO, :] = jnp.where(first, hist_p_ref[...], ext_p[tm:, :])
                ext_c[0:HALO, :] = jnp.where(first, hist_c_ref[...], ext_c[tm:, :])
            ext_p[HALO:, :] = up_ref_[...]
            ext_c[HALO:, :] = uc_ref_[...]

        def pool():
            pos0 = (tile % tiles_per_seq) * tm
            y_pool[slot] = _pool_mixer(ext_p, pool_w_ref, pool_scale_ref, tm, pos0).astype(BF16)

        def store_conv(y):
            y_conv[slot] = y.astype(BF16)

        return [fill] + _conv_mixer_steps(ext_c, shift_c, conv_out, dw_ref, dw_b_ref, ln_g_ref, ln_b_ref, pw_ref,
                                          tm, store_conv) + [pool]

    @pl.when(i == 0)
    def _():
        for step in mixer_steps(up0_ref, uc0_ref, 0, 0, True):
            step()

    slot = i % 2
    nxt = jnp.minimum(i + 1, pl.num_programs(0) - 1)
    out_ref[...] = _dense_math(x_ref[...], y_pool[slot], o_ref[...], y_conv[slot],
                               wout_ref, g_ref, wgu_ref, wd_ref, fg_ref, final)
    for step in mixer_steps(up_ref, uc_ref, nxt, 1 - slot, nxt % tiles_per_seq == 0):
        step()


def _prompt_dense(x, o, upool, uconv, hist_pool, hist_conv, mixer_w, dense_w, tm, seq_len, final, layer):
    n = x.shape[0]
    tps = seq_len // tm
    last = n // tm - 1
    row = lambda w: pl.BlockSpec((tm, w), lambda i: (i, 0))
    first_tile = lambda w: pl.BlockSpec((tm, w), lambda i: (0, 0))
    next_tile = lambda w: pl.BlockSpec((tm, w), lambda i: (jnp.minimum(i + 1, last), 0))
    next_seq = lambda w: pl.BlockSpec((None, HALO, w), lambda i: (jnp.minimum(i + 1, last) // tps, 0, 0))
    return pl.pallas_call(
        functools.partial(_prompt_dense_kernel, final=final, tiles_per_seq=tps),
        grid=(n // tm,),
        in_specs=[row(D_MODEL), row(ATTN_WIDTH), first_tile(POOL_WIDTH), first_tile(CONV_WIDTH),
                  next_tile(POOL_WIDTH), next_tile(CONV_WIDTH), next_seq(POOL_WIDTH), next_seq(CONV_WIDTH)]
        + _mixer_weight_specs(layer) + _dense_weight_specs(layer),
        out_specs=row(D_MODEL),
        out_shape=jax.ShapeDtypeStruct((n, D_MODEL), F32),
        scratch_shapes=_mixer_scratch(tm) + [pltpu.VMEM((2, tm, POOL_WIDTH), BF16),
                                             pltpu.VMEM((2, tm, CONV_WIDTH), BF16)],
        compiler_params=pltpu.CompilerParams(dimension_semantics=("arbitrary",),
                                             vmem_limit_bytes=VMEM_LIMIT),
        name="prompt_dense",
    )(x, o, upool, uconv, upool, uconv, hist_pool, hist_conv, *mixer_w, *dense_w)


def _sample_dense_kernel(x_ref, yp_ref, o_ref, yc_ref, wout_ref, g_ref, wgu_ref, wd_ref, fg_ref, out_ref, *, final):
    out_ref[...] = _dense_math(x_ref[...], yp_ref[...], o_ref[...], yc_ref[...],
                               wout_ref, g_ref, wgu_ref, wd_ref, fg_ref, final)


def _sample_dense(x, ypool, o, yconv, wout_bf, g, wgu_bf, wd_bf, fg, tm, final, layer):
    n = x.shape[0]
    row = lambda w: pl.BlockSpec((tm, w), lambda i: (i, 0))
    return pl.pallas_call(
        functools.partial(_sample_dense_kernel, final=final),
        grid=(n // tm,),
        in_specs=[row(D_MODEL), row(POOL_WIDTH), row(ATTN_WIDTH), row(CONV_WIDTH)] + _dense_weight_specs(layer),
        out_specs=row(D_MODEL),
        out_shape=jax.ShapeDtypeStruct((n, D_MODEL), F32),
        compiler_params=pltpu.CompilerParams(dimension_semantics=("arbitrary",),
                                             vmem_limit_bytes=VMEM_LIMIT),
        name="sample_dense",
    )(x, ypool, o, yconv, wout_bf, g, wgu_bf, wd_bf, fg)


def _rope_tables(pos):
    half = HEAD_DIM // 2
    inv = ROPE_THETA ** (-jnp.arange(half, dtype=F32) / half)
    ang = pos.astype(F32)[:, None] * inv[None, :]
    reps = LANES // half
    cos = jnp.tile(jnp.cos(ang), (1, reps))
    sign = jnp.tile(jnp.concatenate([-jnp.ones((half,), F32), jnp.ones((half,), F32)]), LANES // HEAD_DIM)
    sin = jnp.tile(jnp.sin(ang), (1, reps)) * sign[None, :]
    return cos, sin


def _block_diag(pool_w):
    depth, g, d, _ = pool_w.shape
    eye = jnp.eye(g, dtype=pool_w.dtype)
    return (pool_w[:, :, :, None, :] * eye[None, :, None, :, None]).reshape(depth, g * d, g * d)


def _pad_hist(h):
    return jnp.pad(h, ((0, 0), (0, 0), (HALO - h.shape[2], 0), (0, 0)))


def kernel(x_prompt, x_sample, cache_k, cache_v, state_pool, state_conv, norm_mix_g, w_in, pool_w, pool_scale,
           lambda_qk, diff_norm_g, conv_dw, conv_dw_b, conv_ln_g, conv_ln_b, conv_pw, w_out, norm_ffn_g,
           w_gate_up, w_down, final_norm_g):
    B, S, _ = x_prompt.shape
    Bs, Ls, _ = x_sample.shape
    depth = w_in.shape[0]
    past_len = cache_k.shape[2]
    assert S % (Q_TILES_PER_STEP * ATTN_TILE) == 0 and S % ROW_TILE == 0 and S % IN_PROJ_TILE == 0
    assert S // ROW_TILE >= 2
    assert (Bs * Ls) % 16 == 0 and Ls % 16 == 0 and Ls >= CONV_HIST

    cos_p, sin_p = _rope_tables(jnp.arange(S, dtype=jnp.int32))
    cos_s, sin_s = _rope_tables(past_len + jnp.arange(Ls, dtype=jnp.int32))
    cos_s, sin_s = jnp.tile(cos_s, (Bs, 1)), jnp.tile(sin_s, (Bs, 1))
    zero_pool = jnp.zeros((B, HALO, POOL_WIDTH), F32)
    zero_conv = jnp.zeros((B, HALO, CONV_WIDTH), F32)

    xp = x_prompt.reshape(B * S, D_MODEL)
    xs = x_sample.reshape(Bs * Ls, D_MODEL)
    rows = lambda a: a.reshape(depth, 1, -1)
    norm_mix_rows, norm_ffn_rows = rows(norm_mix_g), rows(norm_ffn_g)
    diff_g_row, diff_g_col = rows(diff_norm_g), diff_norm_g.reshape(depth, -1, 1)
    w_in_bf, w_out_bf, w_gu_bf, w_d_bf = (w.astype(BF16) for w in (w_in, w_out, w_gate_up, w_down))
    mixer_w = (_block_diag(pool_w).astype(BF16), rows(pool_scale), conv_dw, rows(conv_dw_b), rows(conv_ln_g),
               rows(conv_ln_b), conv_pw.astype(BF16))
    dense_w = (w_out_bf, norm_ffn_rows, w_gu_bf, w_d_bf, final_norm_g.reshape(1, -1))
    hist_pool_s, hist_conv_s = _pad_hist(state_pool), _pad_hist(state_conv)
    cache_kt = jnp.transpose(cache_k, (0, 1, 3, 4, 5, 2)).reshape(depth, Bs, ATTN_WIDTH, past_len)
    cache_v4 = cache_v.reshape(depth, Bs, past_len * N_HEADS, V_DIM)
    kv_stack = depth
    pp, cp = [], []
    ks_, vs_, ps_, cs_ = [], [], [], []
    for l in range(depth):
        lam_init = 0.8 - 0.6 * math.exp(-0.3 * l)
        final = l == depth - 1

        upool, q, k, kb, v, vb, uconv = _prompt_in_proj(xp, norm_mix_rows, w_in_bf, cos_p, sin_p, IN_PROJ_TILE, S,
                                                        kv_stack, l)
        kv_stack = (k, v)
        o = _prompt_attention(lambda_qk, diff_g_col, q.reshape(B, S, -1), kb.reshape(B, S, -1),
                              vb, lam_init, l)
        xp = _prompt_dense(xp, o.reshape(B * S, -1), upool, uconv, zero_pool, zero_conv, mixer_w, dense_w,
                           ROW_TILE, S, final, l)
        pp.append(upool.reshape(B, S, -1)[:, S - POOL_HIST:])
        cp.append(uconv.reshape(B, S, -1)[:, S - CONV_HIST:])

        upool, q, k, kb, v, vb, uconv = _sample_in_proj(xs, norm_mix_rows, w_in_bf, cos_s, sin_s, l)
        o = _sample_attention(lambda_qk, diff_g_row, q.reshape(Bs, Ls, -1), cache_kt, cache_v4,
                              kb.reshape(Bs, Ls, -1), vb.reshape(Bs, Ls, -1), lam_init, l)
        ypool, yconv = _sample_mixers(upool, uconv, hist_pool_s, hist_conv_s, mixer_w, Ls, past_len, l)
        xs = _sample_dense(xs, ypool, o.reshape(Bs * Ls, -1), yconv, *dense_w, Bs * Ls, final, l)
        ks_.append(k.reshape(Bs, Ls, N_HEADS, 2, HEAD_DIM))
        vs_.append(v.reshape(Bs, Ls, N_HEADS, V_DIM))
        ps_.append(upool.reshape(Bs, Ls, -1)[:, Ls - POOL_HIST:])
        cs_.append(uconv.reshape(Bs, Ls, -1)[:, Ls - CONV_HIST:])

    kt_all, v4_all = kv_stack
    new_k = jnp.transpose(kt_all.reshape(depth, B, N_HEADS, 2, HEAD_DIM, S), (0, 1, 5, 2, 3, 4))
    new_v = v4_all.reshape(depth, B, S, N_HEADS, V_DIM)
    return (xp.reshape(B, S, D_MODEL), xs.reshape(Bs, Ls, D_MODEL),
            new_k, new_v, jnp.stack(pp), jnp.stack(cp),
            jnp.stack(ks_), jnp.stack(vs_), jnp.stack(ps_), jnp.stack(cs_))
```

```python
import functools
import math

import jax
import jax.numpy as jnp
from jax import lax
from jax.experimental import pallas as pl
from jax.experimental.pallas import tpu as pltpu

D_MODEL = 1024
CHUNK = 64
POOL_WIDTH = D_MODEL // 4
POOL_WINDOWS = (2, 4, 8, 16)
POOL_GROUP_DIM = POOL_WIDTH // len(POOL_WINDOWS)
POOL_HIST = max(POOL_WINDOWS) - 1
ATTN_WIDTH = D_MODEL // 2
N_HEADS = 4
HEAD_DIM = ATTN_WIDTH // (2 * N_HEADS)
V_DIM = 2 * HEAD_DIM
ROPE_THETA = 10000.0
CONV_WIDTH = D_MODEL // 4
CONV_K = 31
CONV_HIST = CONV_K - 1
MIX_WIDTH = POOL_WIDTH + ATTN_WIDTH + CONV_WIDTH
IN_WIDTH = POOL_WIDTH + 3 * ATTN_WIDTH + 2 * CONV_WIDTH
D_FF = ((-(-8 * D_MODEL // 3) + 255) // 256) * 256
EPS = 1e-6

LANES = 128
HALO = 32
ROW_TILE = 512
IN_PROJ_TILE = 1024
ATTN_TILE = 256
SAMPLE_KEY_TILE = 2048
KEY_TILES_PER_BLOCK = 8
Q_TILES_PER_STEP = 8
HEADS_PER_STEP = 4
VMEM_LIMIT = 56 * 1024 * 1024

F32 = jnp.float32
BF16 = jnp.bfloat16


def _rms(x, g):
    return x * lax.rsqrt(jnp.mean(x * x, axis=-1, keepdims=True) + EPS) * g


def _const_spec(shape):
    return pl.BlockSpec(shape, lambda *_: (0,) * len(shape), pipeline_mode=pl.Buffered(1))


def _layer_spec(shape, layer):
    return pl.BlockSpec((None,) + shape, lambda *_: (layer,) + (0,) * len(shape), pipeline_mode=pl.Buffered(1))


O_Q = POOL_WIDTH
O_K = O_Q + ATTN_WIDTH
O_V = O_K + ATTN_WIDTH
O_A = O_V + ATTN_WIDTH
O_B = O_A + CONV_WIDTH


def _qkv_epilogue(z, cos_ref, sin_ref, q_ref, k_ref, kb_ref, v_ref, vb_ref, cache_layout):
    tm = z.shape[0]
    o1, o2, o3 = O_Q, O_K, O_V
    cos = cos_ref[...]
    sin = sin_ref[...]
    lane = lax.broadcasted_iota(jnp.int32, cos.shape, 1)
    first_half = (lane & (HEAD_DIM // 2)) == 0

    def rope(t):
        up = pltpu.roll(t, LANES - HEAD_DIM // 2, axis=1)
        down = pltpu.roll(t, HEAD_DIM // 2, axis=1)
        return t * cos + jnp.where(first_half, up, down) * sin

    scale = HEAD_DIM ** -0.5 * math.log2(math.e)
    for j in range(ATTN_WIDTH // LANES):
        sl = slice(j * LANES, (j + 1) * LANES)
        qj = rope(z[:, o1 + j * LANES:o1 + (j + 1) * LANES])
        kj = rope(z[:, o2 + j * LANES:o2 + (j + 1) * LANES])
        vj = z[:, o3 + j * LANES:o3 + (j + 1) * LANES]
        q_ref[:, sl] = (qj * scale).astype(BF16)
        kb_ref[:, sl] = kj.astype(BF16)
        if cache_layout:
            k_ref[sl, :] = kj.T
            v_ref[pl.ds(j, tm, stride=N_HEADS), :] = vj
            vb_ref[sl, :] = vj.T.astype(BF16)
        else:
            k_ref[:, sl] = kj
            v_ref[:, sl] = vj
            vb_ref[:, sl] = vj.astype(BF16)


def _sample_in_proj_kernel(x_ref, g_ref, w_ref, cos_ref, sin_ref,
                           upool_ref, q_ref, k_ref, kb_ref, v_ref, vb_ref, uconv_ref):
    h = _rms(x_ref[...], g_ref[...]).astype(BF16)
    z = jnp.dot(h, w_ref[...], preferred_element_type=F32)
    upool_ref[...] = z[:, :O_Q]
    uconv_ref[...] = z[:, O_A:O_B] * jax.nn.sigmoid(z[:, O_B:])
    _qkv_epilogue(z, cos_ref, sin_ref, q_ref, k_ref, kb_ref, v_ref, vb_ref, False)


def _sample_in_proj(x, g, w_bf, cos_tab, sin_tab, layer):
    n = x.shape[0]
    full = lambda w: pl.BlockSpec((n, w), lambda i: (0, 0))
    flat = lambda dt: jax.ShapeDtypeStruct((n, ATTN_WIDTH), dt)
    return pl.pallas_call(
        _sample_in_proj_kernel,
        grid=(1,),
        in_specs=[full(D_MODEL), _layer_spec((1, D_MODEL), layer), _layer_spec((D_MODEL, IN_WIDTH), layer),
                  full(LANES), full(LANES)],
        out_specs=[full(POOL_WIDTH)] + [full(ATTN_WIDTH)] * 5 + [full(CONV_WIDTH)],
        out_shape=[jax.ShapeDtypeStruct((n, POOL_WIDTH), F32), flat(BF16), flat(F32), flat(BF16), flat(F32),
                   flat(BF16), jax.ShapeDtypeStruct((n, CONV_WIDTH), F32)],
        compiler_params=pltpu.CompilerParams(dimension_semantics=("arbitrary",),
                                             vmem_limit_bytes=VMEM_LIMIT),
        name="sample_in_proj",
    )(x, g, w_bf, cos_tab, sin_tab)


def _prompt_in_proj_kernel(*refs, n_unused):
    x_ref, g_ref, w_ref, cos_ref, sin_ref = refs[:5]
    upool_ref, q_ref, k_ref, kb_ref, v_ref, vb_ref, uconv_ref = refs[5 + n_unused:]
    h = _rms(x_ref[...], g_ref[...]).astype(BF16)
    z = jnp.dot(h, w_ref[...], preferred_element_type=F32)
    upool_ref[...] = z[:, :O_Q]
    uconv_ref[...] = z[:, O_A:O_B] * jax.nn.sigmoid(z[:, O_B:])
    _qkv_epilogue(z, cos_ref, sin_ref, q_ref, k_ref, kb_ref, v_ref, vb_ref, True)


def _prompt_in_proj(x, g, w_bf, cos_tab, sin_tab, tm, seq_len, kv_stack, layer):
    n = x.shape[0]
    tps = seq_len // tm
    batch = n // seq_len
    row = lambda w: pl.BlockSpec((tm, w), lambda i: (i, 0))
    tab = pl.BlockSpec((tm, LANES), lambda i: (i % tps, 0))
    flat = lambda dt, w=ATTN_WIDTH: jax.ShapeDtypeStruct((n, w), dt)
    extra_in, extra_specs, aliases = [], [], {}
    if isinstance(kv_stack, int):
        k_shape = jax.ShapeDtypeStruct((kv_stack, batch, ATTN_WIDTH, seq_len), F32)
        v_shape = jax.ShapeDtypeStruct((kv_stack, batch, N_HEADS * seq_len, V_DIM), F32)
    else:
        k_shape, v_shape = (jax.ShapeDtypeStruct(a.shape, a.dtype) for a in kv_stack)
        extra_in = list(kv_stack)
        extra_specs = [pl.BlockSpec(memory_space=pl.ANY)] * 2
        aliases = {5: 2, 6: 4}
    k_spec = pl.BlockSpec((None, None, ATTN_WIDTH, tm), lambda i: (layer, i // tps, 0, i % tps))
    v_spec = pl.BlockSpec((None, None, N_HEADS * tm, V_DIM), lambda i: (layer, i // tps, i % tps, 0))
    vt_spec = pl.BlockSpec((None, ATTN_WIDTH, tm), lambda i: (i // tps, 0, i % tps))
    return pl.pallas_call(
        functools.partial(_prompt_in_proj_kernel, n_unused=len(extra_in)),
        grid=(n // tm,),
        in_specs=[row(D_MODEL), _layer_spec((1, D_MODEL), layer), _layer_spec((D_MODEL, IN_WIDTH), layer), tab, tab]
        + extra_specs,
        out_specs=[row(POOL_WIDTH), row(ATTN_WIDTH), k_spec, row(ATTN_WIDTH), v_spec, vt_spec, row(CONV_WIDTH)],
        out_shape=[flat(F32, POOL_WIDTH), flat(BF16), k_shape, flat(BF16), v_shape,
                   jax.ShapeDtypeStruct((batch, ATTN_WIDTH, seq_len), BF16), flat(F32, CONV_WIDTH)],
        input_output_aliases=aliases,
        compiler_params=pltpu.CompilerParams(dimension_semantics=("arbitrary",),
                                             vmem_limit_bytes=VMEM_LIMIT),
        name="prompt_in_proj",
    )(x, g, w_bf, cos_tab, sin_tab, *extra_in)


def _lambda_scalar(lq, lam_init):
    a = jnp.sum(lq[0:1, :] * lq[1:2, :], axis=-1, keepdims=True)
    b = jnp.sum(lq[2:3, :] * lq[3:4, :], axis=-1, keepdims=True)
    return jnp.exp(a) - jnp.exp(b) + lam_init


def _stack_maps(q):
    lane = lax.broadcasted_iota(jnp.int32, q.shape, 1)
    zero = jnp.zeros_like(q)
    return jnp.concatenate([jnp.where(lane < HEAD_DIM, q, zero), jnp.where(lane >= HEAD_DIM, q, zero)], axis=0)


ONES_ROWS = 16


def _flash_scores(qq, kt):
    return lax.dot_general(kt, qq, (((1,), (1,)), ((), ())), preferred_element_type=F32)


def _flash_update(carry, s, vt_ones, p_stage):
    m, acc = carry
    m_new = jnp.maximum(m, jnp.max(s, axis=0, keepdims=True))
    alpha = jnp.exp2(m - m_new)
    p_stage[...] = jnp.exp2(s - m_new).astype(BF16)
    return m_new, alpha * acc + jnp.dot(vt_ones, p_stage[...], preferred_element_type=F32)


def _flash_init(cols):
    return jnp.full((1, cols), -jnp.inf, F32), jnp.zeros((V_DIM + ONES_ROWS, cols), F32)


def _attn_finish(carry, lam, g_col, lam_init, tq):
    _, acc = carry
    scaled = acc[:V_DIM] * (1.0 / acc[V_DIM:V_DIM + 1])
    o = scaled[:, :tq] - lam * scaled[:, tq:]
    o = o * lax.rsqrt(jnp.mean(o * o, axis=0, keepdims=True) + EPS) * g_col
    return (o * (1.0 - lam_init)).T


def _mask_diagonal_tile(s):
    t = s.shape[0]
    q_chunk = (lax.broadcasted_iota(jnp.int32, (1, s.shape[1]), 1) % t) // CHUNK
    blocks = [jnp.where(q_chunk >= a, s[a * CHUNK:(a + 1) * CHUNK, :], -1e30) for a in range(t // CHUNK)]
    return jnp.concatenate(blocks, axis=0)


def _prompt_attn_kernel(lq_ref, g_ref, q_ref, k_ref, vt_ref, o_ref, qq_ref, p_ref, m_ref, acc_ref, *, lam_init):
    t = ATTN_TILE
    heads = [slice(h * LANES, (h + 1) * LANES) for h in range(HEADS_PER_STEP)]
    ones = jnp.ones((ONES_ROWS, t), BF16)
    lam = _lambda_scalar(lq_ref[...], lam_init)
    q_tiles = q_ref.shape[0] // t

    def one_query_tile(sub, _):
        qi = pl.program_id(2) * q_tiles + sub
        q_rows = pl.ds(pl.multiple_of(sub * t, t), t)
        for h, c in enumerate(heads):
            qq_ref[h] = _stack_maps(q_ref[q_rows, c])
        stage0 = jnp.minimum(qi, 0)

        def steps(tiles):
            rows = [pl.ds(pl.multiple_of(j * t, t), t) for j, _ in tiles]
            chains = [(h, c, mp) for h, c in enumerate(heads) for mp in range(2)]
            half = lambda mp: slice(mp * t, (mp + 1) * t)
            s = [[_flash_scores(qq_ref[h, half(mp), :], k_ref[r, c]) for h, c, mp in chains] for r in rows]
            carry = [(m_ref[h, :, half(mp)], acc_ref[h, :, half(mp)]) for h, _, mp in chains]
            for ti, (_, diagonal) in enumerate(tiles):
                if diagonal:
                    s[ti] = [_mask_diagonal_tile(x) for x in s[ti]]
                vt_ones = [jnp.concatenate([vt_ref[c, rows[ti]], ones], axis=0) for c in heads]
                carry = [
                    _flash_update(carry[i], s[ti][i], vt_ones[h],
                                  p_ref.at[stage0 + ti * HEADS_PER_STEP + h, :, half(mp)])
                    for i, (h, _, mp) in enumerate(chains)]
            for (h, _, mp), (m, acc) in zip(chains, carry):
                m_ref[h, :, half(mp)] = m
                acc_ref[h, :, half(mp)] = acc
            return 0

        m0, acc0 = _flash_init(2 * t)
        for h in range(HEADS_PER_STEP):
            m_ref[h] = m0
            acc_ref[h] = acc0
        nb = min(KEY_TILES_PER_BLOCK, k_ref.shape[0] // t)
        if nb < k_ref.shape[0] // t:
            lax.fori_loop(0, qi // nb, lambda p, _: steps([(nb * p + i, False) for i in range(nb)]), 0)
        left = qi % nb

        def last_block(n_full):
            return lambda _: steps([(qi - n_full + i, False) for i in range(n_full)] + [(qi, True)])

        lax.switch(left, [last_block(n) for n in range(nb)], 0)
        for h, c in enumerate(heads):
            o_ref[q_rows, c] = _attn_finish((m_ref[h], acc_ref[h]), lam, g_ref[...], lam_init, t).astype(BF16)
        return 0

    lax.fori_loop(0, q_tiles, one_query_tile, 0)


def _prompt_attention(lq, g, q, kb, vtb, lam_init, layer):
    b, s, _ = q.shape
    t = ATTN_TILE
    w = HEADS_PER_STEP * LANES
    tq = Q_TILES_PER_STEP * t
    return pl.pallas_call(
        functools.partial(_prompt_attn_kernel, lam_init=lam_init),
        grid=(b, N_HEADS // HEADS_PER_STEP, s // tq),
        in_specs=[
            _layer_spec((4, HEAD_DIM), layer),
            _layer_spec((V_DIM, 1), layer),
            pl.BlockSpec((None, tq, w), lambda b_, h, i: (b_, i, h)),
            pl.BlockSpec((None, s, w), lambda b_, h, i: (b_, 0, h)),
            pl.BlockSpec((None, w, s), lambda b_, h, i: (b_, h, 0)),
        ],
        out_specs=pl.BlockSpec((None, tq, w), lambda b_, h, i: (b_, i, h)),
        out_shape=jax.ShapeDtypeStruct((b, s, ATTN_WIDTH), BF16),
        scratch_shapes=[pltpu.VMEM((HEADS_PER_STEP, 2 * t, LANES), BF16),
                        pltpu.VMEM((KEY_TILES_PER_BLOCK * HEADS_PER_STEP, t, 2 * t), BF16),
                        pltpu.VMEM((HEADS_PER_STEP, 1, 2 * t), F32),
                        pltpu.VMEM((HEADS_PER_STEP, V_DIM + ONES_ROWS, 2 * t), F32)],
        compiler_params=pltpu.CompilerParams(dimension_semantics=("arbitrary",) * 3,
                                             vmem_limit_bytes=VMEM_LIMIT),
        name="prompt_attention",
    )(lq, g, q, kb, vtb)


def _row_scores(qq, kt, *, keys_on_rows):
    dims = (((1,), (1,)), ((), ())) if keys_on_rows else (((1,), (0,)), ((), ()))
    return lax.dot_general(qq, kt, dims, preferred_element_type=F32)


def _row_flash_update(carry, s, vt):
    m, l, acc = carry
    m_new = jnp.maximum(m, jnp.max(s, axis=-1, keepdims=True))
    alpha = jnp.exp2(m - m_new)
    p = jnp.exp2(s - m_new)
    l = alpha * l + jnp.sum(p, axis=-1, keepdims=True)
    acc = alpha * acc + jnp.dot(p.astype(BF16), vt, preferred_element_type=F32)
    return m_new, l, acc


def _sample_attn_kernel(lq_ref, g_ref, q_ref, ck_ref, cv_ref, k_ref, v_ref, o_ref, *, lam_init, past_len):
    t = SAMPLE_KEY_TILE
    tq = q_ref.shape[0]
    heads = [slice(h * LANES, (h + 1) * LANES) for h in range(N_HEADS)]
    qq = [_stack_maps(q_ref[:, c]) for c in heads]

    def body(j, carry):
        cols = pl.ds(pl.multiple_of(j * t, t), t)
        s = [_row_scores(qq[h], ck_ref[c, cols].astype(BF16), keys_on_rows=False) for h, c in enumerate(heads)]
        return tuple(
            _row_flash_update(carry[h], s[h], cv_ref[pl.ds(j * (t * N_HEADS) + h, t, stride=N_HEADS), :].astype(BF16))
            for h in range(N_HEADS))

    init = (jnp.full((2 * tq, 1), -jnp.inf, F32), jnp.zeros((2 * tq, 1), F32), jnp.zeros((2 * tq, V_DIM), F32))
    carry = lax.fori_loop(0, past_len // t, body, (init,) * N_HEADS)
    lam = _lambda_scalar(lq_ref[...], lam_init)
    for h, c in enumerate(heads):
        s = _row_scores(qq[h], k_ref[:, c], keys_on_rows=True)
        _, l, acc = _row_flash_update(carry[h], s, v_ref[:, c])
        o = acc[:tq] / l[:tq] - lam * (acc[tq:] / l[tq:])
        o_ref[:, c] = (_rms(o, g_ref[...]) * (1.0 - lam_init)).astype(BF16)


def _sample_attention(lq, g, q, cache_kt, cache_v4, kb, vb, lam_init, layer):
    b, ls, _ = q.shape
    past_len = cache_kt.shape[-1]
    assert past_len % SAMPLE_KEY_TILE == 0
    return pl.pallas_call(
        functools.partial(_sample_attn_kernel, lam_init=lam_init, past_len=past_len),
        grid=(b,),
        in_specs=[
            _layer_spec((4, HEAD_DIM), layer),
            _layer_spec((1, V_DIM), layer),
            pl.BlockSpec((None, ls, ATTN_WIDTH), lambda b_: (b_, 0, 0)),
            pl.BlockSpec((None, None, ATTN_WIDTH, past_len), lambda b_: (layer, b_, 0, 0)),
            pl.BlockSpec((None, None, N_HEADS * past_len, V_DIM), lambda b_: (layer, b_, 0, 0)),
            pl.BlockSpec((None, ls, ATTN_WIDTH), lambda b_: (b_, 0, 0)),
            pl.BlockSpec((None, ls, ATTN_WIDTH), lambda b_: (b_, 0, 0)),
        ],
        out_specs=pl.BlockSpec((None, ls, ATTN_WIDTH), lambda b_: (b_, 0, 0)),
        out_shape=jax.ShapeDtypeStruct((b, ls, ATTN_WIDTH), BF16),
        compiler_params=pltpu.CompilerParams(dimension_semantics=("arbitrary",),
                                             vmem_limit_bytes=VMEM_LIMIT),
        name="sample_attention",
    )(lq, g, q, cache_kt, cache_v4, kb, vb)


def _pool_mixer(ext_p, pool_w_ref, pool_scale_ref, tm, pos0):
    assert POOL_WINDOWS == (2, 4, 8, 16) and HALO == 32
    e = ext_p[...]
    n = HALO + tm
    s2 = e[8:] + e[7:n - 1]
    s4 = s2[8:] + s2[6:n - 10]
    s8 = s4[8:] + s4[4:n - 20]
    sums = {2: s2[24:], 4: s4[16:], 8: s8[8:], 16: s8[8:] + s8[:n - 32]}
    u = e[HALO:]
    group = lax.broadcasted_iota(jnp.int32, (1, POOL_WIDTH), 1) // POOL_GROUP_DIM
    win = sums[POOL_WINDOWS[-1]]
    width = jnp.full((1, POOL_WIDTH), POOL_WINDOWS[-1], jnp.int32)
    for gi in range(len(POOL_WINDOWS) - 2, -1, -1):
        win = jnp.where(group == gi, sums[POOL_WINDOWS[gi]], win)
        width = jnp.where(group == gi, POOL_WINDOWS[gi], width)
    head = max(POOL_WINDOWS)
    pos = pos0 + lax.broadcasted_iota(jnp.int32, (head, POOL_WIDTH), 0)
    d_head = win[:head] / jnp.minimum(pos + 1, width).astype(F32) - u[:head]
    d_rest = win[head:] * (1.0 / width.astype(F32)) - u[head:]
    d = jnp.concatenate([d_head, d_rest], axis=0).astype(BF16)
    return jnp.dot(d, pool_w_ref[...], preferred_element_type=F32) * pool_scale_ref[...]


CONV_ROW_BLOCK = 64


def _conv_mixer_steps(ext_c, shift_c, conv_out, dw_ref, dw_b_ref, ln_g_ref, ln_b_ref, pw_ref, tm, store):
    first_row = HALO - CONV_HIST
    blk = min(tm, CONV_ROW_BLOCK)

    def shift(residues):
        for r in residues:
            shift_c[r, 0:HALO - 8 + tm, :] = ext_c[pl.ds(r, HALO - 8 + tm), :]

    weights = []

    def broadcast_weights():
        weights.extend(jnp.broadcast_to(dw_ref[k:k + 1, :], (8, CONV_WIDTH)) for k in range(CONV_K))

    def taps(rb):
        yb = jnp.zeros((blk, CONV_WIDTH), F32)
        for k in range(CONV_K):
            r, a = (first_row + k) % 8, (first_row + k) // 8
            rows = pl.ds(8 * a + rb * blk, blk)
            src = ext_c[rows, :] if r == 0 else shift_c[r, rows, :]
            yb = yb + src * jnp.concatenate([weights[k]] * (blk // 8), axis=0)
        conv_out[rb * blk:(rb + 1) * blk, :] = yb

    def finish():
        y = conv_out[...] + dw_b_ref[...]
        mu = jnp.mean(y, axis=-1, keepdims=True)
        yc = y - mu
        yn = yc * lax.rsqrt(jnp.mean(yc * yc, axis=-1, keepdims=True) + EPS) * ln_g_ref[...] + ln_b_ref[...]
        act = (yn * jax.nn.sigmoid(yn)).astype(BF16)
        store(jnp.dot(act, pw_ref[...], preferred_element_type=F32))

    steps = [broadcast_weights] + [functools.partial(shift, rs) for rs in ((1, 2), (3, 4, 5), (6, 7))]
    steps += [functools.partial(taps, rb) for rb in range(tm // blk)]
    return steps + [finish]


def _conv_mixer(ext_c, shift_c, conv_out, dw_ref, dw_b_ref, ln_g_ref, ln_b_ref, pw_ref, tm):
    out = []
    for step in _conv_mixer_steps(ext_c, shift_c, conv_out, dw_ref, dw_b_ref, ln_g_ref, ln_b_ref, pw_ref, tm,
                                  out.append):
        step()
    return out[0]


def _mixer_scratch(tm):
    return [pltpu.VMEM((HALO + tm, POOL_WIDTH), F32), pltpu.VMEM((HALO + tm, CONV_WIDTH), F32),
            pltpu.VMEM((8, HALO + tm, CONV_WIDTH), F32), pltpu.VMEM((tm, CONV_WIDTH), F32)]


def _mixer_weight_specs(layer):
    shapes = [(POOL_WIDTH, POOL_WIDTH), (1, POOL_WIDTH), (CONV_K, CONV_WIDTH), (1, CONV_WIDTH), (1, CONV_WIDTH),
              (1, CONV_WIDTH), (CONV_WIDTH, CONV_WIDTH)]
    return [_layer_spec(shape, layer) for shape in shapes]


def _sample_mixers_kernel(up_ref, up_hist_ref, uc_ref, uc_hist_ref,
                          pool_w_ref, pool_scale_ref, dw_ref, dw_b_ref, ln_g_ref, ln_b_ref, pw_ref,
                          ypool_ref, yconv_ref, ext_p, ext_c, shift_c, conv_out, *, pos_base):
    tm = up_ref.shape[0]
    ext_p[0:HALO, :] = up_hist_ref[...]
    ext_p[HALO:, :] = up_ref[...]
    ext_c[0:HALO, :] = uc_hist_ref[...]
    ext_c[HALO:, :] = uc_ref[...]
    ypool_ref[...] = _pool_mixer(ext_p, pool_w_ref, pool_scale_ref, tm, pos_base).astype(BF16)
    yconv_ref[...] = _conv_mixer(ext_c, shift_c, conv_out, dw_ref, dw_b_ref, ln_g_ref, ln_b_ref, pw_ref,
                                 tm).astype(BF16)


def _sample_mixers(upool, uconv, hist_pool, hist_conv, mixer_w, seq_len, pos_base, layer):
    n = upool.shape[0]
    cur = pl.BlockSpec((seq_len, POOL_WIDTH), lambda i: (i, 0))
    hist = pl.BlockSpec((None, None, HALO, POOL_WIDTH), lambda i: (layer, i, 0, 0))
    return pl.pallas_call(
        functools.partial(_sample_mixers_kernel, pos_base=pos_base),
        grid=(n // seq_len,),
        in_specs=[cur, hist, cur, hist] + _mixer_weight_specs(layer),
        out_specs=[cur, cur],
        out_shape=[jax.ShapeDtypeStruct((n, POOL_WIDTH), BF16), jax.ShapeDtypeStruct((n, CONV_WIDTH), BF16)],
        scratch_shapes=_mixer_scratch(seq_len),
        compiler_params=pltpu.CompilerParams(dimension_semantics=("arbitrary",),
                                             vmem_limit_bytes=VMEM_LIMIT),
        name="sample_mixers",
    )(upool, hist_pool, uconv, hist_conv, *mixer_w)


def _dense_math(x, ypool, o, yconv, wout_ref, g_ref, wgu_ref, wd_ref, fg_ref, final):
    o1 = POOL_WIDTH
    o2 = o1 + ATTN_WIDTH
    mix = (jnp.dot(ypool, wout_ref[0:o1, :], preferred_element_type=F32)
           + jnp.dot(o, wout_ref[o1:o2, :], preferred_element_type=F32)
           + jnp.dot(yconv, wout_ref[o2:, :], preferred_element_type=F32))
    x1 = x + mix
    h = _rms(x1, g_ref[...]).astype(BF16)
    gu = jnp.dot(h, wgu_ref[...], preferred_element_type=F32)
    gate = gu[:, :D_FF]
    a = (gate * jax.nn.sigmoid(gate) * gu[:, D_FF:]).astype(BF16)
    x2 = x1 + jnp.dot(a, wd_ref[...], preferred_element_type=F32)
    return _rms(x2, fg_ref[...]) if final else x2


def _dense_weight_specs(layer):
    shapes = [(MIX_WIDTH, D_MODEL), (1, D_MODEL), (D_MODEL, 2 * D_FF), (D_FF, D_MODEL)]
    return [_layer_spec(shape, layer) for shape in shapes] + [_const_spec((1, D_MODEL))]


def _prompt_dense_kernel(x_ref, o_ref, up0_ref, uc0_ref, up_ref, uc_ref, hist_p_ref, hist_c_ref,
                         pool_w_ref, pool_scale_ref, dw_ref, dw_b_ref, ln_g_ref, ln_b_ref, pw_ref,
                         wout_ref, g_ref, wgu_ref, wd_ref, fg_ref, out_ref,
                         ext_p, ext_c, shift_c, conv_out, y_pool, y_conv, *, final, tiles_per_seq):
    tm = x_ref.shape[0]
    i = pl.program_id(0)

    def mixer_steps(up_ref_, uc_ref_, tile, slot, first):
        def fill():
            if first is True:
                ext_p[0:HALO, :] = hist_p_ref[...]
                ext_c[0:HALO, :] = hist_c_ref[...]
            else:
                ext_p[0:HALO, :] = jnp.where(first, hist_p_ref[...], ext_p[tm:, :])
                ext_c[0:HALO, :] = jnp.where(first, hist_c_ref[...], ext_c[tm:, :])
            ext_p[HALO:, :] = up_ref_[...]
            ext_c[HALO:, :] = uc_ref_[...]

        def pool():
            pos0 = (tile % tiles_per_seq) * tm
            y_pool[slot] = _pool_mixer(ext_p, pool_w_ref, pool_scale_ref, tm, pos0).astype(BF16)

        def store_conv(y):
            y_conv[slot] = y.astype(BF16)

        return [fill] + _conv_mixer_steps(ext_c, shift_c, conv_out, dw_ref, dw_b_ref, ln_g_ref, ln_b_ref, pw_ref,
                                          tm, store_conv) + [pool]

    @pl.when(i == 0)
    def _():
        for step in mixer_steps(up0_ref, uc0_ref, 0, 0, True):
            step()

    slot = i % 2
    nxt = jnp.minimum(i + 1, pl.num_programs(0) - 1)
    out_ref[...] = _dense_math(x_ref[...], y_pool[slot], o_ref[...], y_conv[slot],
                               wout_ref, g_ref, wgu_ref, wd_ref, fg_ref, final)
    for step in mixer_steps(up_ref, uc_ref, nxt, 1 - slot, nxt % tiles_per_seq == 0):
        step()


def _prompt_dense(x, o, upool, uconv, hist_pool, hist_conv, mixer_w, dense_w, tm, seq_len, final, layer):
    n = x.shape[0]
    tps = seq_len // tm
    last = n // tm - 1
    row = lambda w: pl.BlockSpec((tm, w), lambda i: (i, 0))
    first_tile = lambda w: pl.BlockSpec((tm, w), lambda i: (0, 0))
    next_tile = lambda w: pl.BlockSpec((tm, w), lambda i: (jnp.minimum(i + 1, last), 0))
    next_seq = lambda w: pl.BlockSpec((None, HALO, w), lambda i: (jnp.minimum(i + 1, last) // tps, 0, 0))
    return pl.pallas_call(
        functools.partial(_prompt_dense_kernel, final=final, tiles_per_seq=tps),
        grid=(n // tm,),
        in_specs=[row(D_MODEL), row(ATTN_WIDTH), first_tile(POOL_WIDTH), first_tile(CONV_WIDTH),
                  next_tile(POOL_WIDTH), next_tile(CONV_WIDTH), next_seq(POOL_WIDTH), next_seq(CONV_WIDTH)]
        + _mixer_weight_specs(layer) + _dense_weight_specs(layer),
        out_specs=row(D_MODEL),
        out_shape=jax.ShapeDtypeStruct((n, D_MODEL), F32),
        scratch_shapes=_mixer_scratch(tm) + [pltpu.VMEM((2, tm, POOL_WIDTH), BF16),
                                             pltpu.VMEM((2, tm, CONV_WIDTH), BF16)],
        compiler_params=pltpu.CompilerParams(dimension_semantics=("arbitrary",),
                                             vmem_limit_bytes=VMEM_LIMIT),
        name="prompt_dense",
    )(x, o, upool, uconv, upool, uconv, hist_pool, hist_conv, *mixer_w, *dense_w)


def _sample_dense_kernel(x_ref, yp_ref, o_ref, yc_ref, wout_ref, g_ref, wgu_ref, wd_ref, fg_ref, out_ref, *, final):
    out_ref[...] = _dense_math(x_ref[...], yp_ref[...], o_ref[...], yc_ref[...],
                               wout_ref, g_ref, wgu_ref, wd_ref, fg_ref, final)


def _sample_dense(x, ypool, o, yconv, wout_bf, g, wgu_bf, wd_bf, fg, tm, final, layer):
    n = x.shape[0]
    row = lambda w: pl.BlockSpec((tm, w), lambda i: (i, 0))
    return pl.pallas_call(
        functools.partial(_sample_dense_kernel, final=final),
        grid=(n // tm,),
        in_specs=[row(D_MODEL), row(POOL_WIDTH), row(ATTN_WIDTH), row(CONV_WIDTH)] + _dense_weight_specs(layer),
        out_specs=row(D_MODEL),
        out_shape=jax.ShapeDtypeStruct((n, D_MODEL), F32),
        compiler_params=pltpu.CompilerParams(dimension_semantics=("arbitrary",),
                                             vmem_limit_bytes=VMEM_LIMIT),
        name="sample_dense",
    )(x, ypool, o, yconv, wout_bf, g, wgu_bf, wd_bf, fg)


def _rope_tables(pos):
    half = HEAD_DIM // 2
    inv = ROPE_THETA ** (-jnp.arange(half, dtype=F32) / half)
    ang = pos.astype(F32)[:, None] * inv[None, :]
    reps = LANES // half
    cos = jnp.tile(jnp.cos(ang), (1, reps))
    sign = jnp.tile(jnp.concatenate([-jnp.ones((half,), F32), jnp.ones((half,), F32)]), LANES // HEAD_DIM)
    sin = jnp.tile(jnp.sin(ang), (1, reps)) * sign[None, :]
    return cos, sin


def _block_diag(pool_w):
    depth, g, d, _ = pool_w.shape
    eye = jnp.eye(g, dtype=pool_w.dtype)
    return (pool_w[:, :, :, None, :] * eye[None, :, None, :, None]).reshape(depth, g * d, g * d)


def _pad_hist(h):
    return jnp.pad(h, ((0, 0), (0, 0), (HALO - h.shape[2], 0), (0, 0)))


def kernel(x_prompt, x_sample, cache_k, cache_v, state_pool, state_conv, norm_mix_g, w_in, pool_w, pool_scale,
           lambda_qk, diff_norm_g, conv_dw, conv_dw_b, conv_ln_g, conv_ln_b, conv_pw, w_out, norm_ffn_g,
           w_gate_up, w_down, final_norm_g):
    B, S, _ = x_prompt.shape
    Bs, Ls, _ = x_sample.shape
    depth = w_in.shape[0]
    past_len = cache_k.shape[2]
    assert S % (Q_TILES_PER_STEP * ATTN_TILE) == 0 and S % ROW_TILE == 0 and S % IN_PROJ_TILE == 0
    assert S // ROW_TILE >= 2
    assert (Bs * Ls) % 16 == 0 and Ls % 16 == 0 and Ls >= CONV_HIST

    cos_p, sin_p = _rope_tables(jnp.arange(S, dtype=jnp.int32))
    cos_s, sin_s = _rope_tables(past_len + jnp.arange(Ls, dtype=jnp.int32))
    cos_s, sin_s = jnp.tile(cos_s, (Bs, 1)), jnp.tile(sin_s, (Bs, 1))
    zero_pool = jnp.zeros((B, HALO, POOL_WIDTH), F32)
    zero_conv = jnp.zeros((B, HALO, CONV_WIDTH), F32)

    xp = x_prompt.reshape(B * S, D_MODEL)
    xs = x_sample.reshape(Bs * Ls, D_MODEL)
    rows = lambda a: a.reshape(depth, 1, -1)
    norm_mix_rows, norm_ffn_rows = rows(norm_mix_g), rows(norm_ffn_g)
    diff_g_row, diff_g_col = rows(diff_norm_g), diff_norm_g.reshape(depth, -1, 1)
    w_in_bf, w_out_bf, w_gu_bf, w_d_bf = (w.astype(BF16) for w in (w_in, w_out, w_gate_up, w_down))
    mixer_w = (_block_diag(pool_w).astype(BF16), rows(pool_scale), conv_dw, rows(conv_dw_b), rows(conv_ln_g),
               rows(conv_ln_b), conv_pw.astype(BF16))
    dense_w = (w_out_bf, norm_ffn_rows, w_gu_bf, w_d_bf, final_norm_g.reshape(1, -1))
    hist_pool_s, hist_conv_s = _pad_hist(state_pool), _pad_hist(state_conv)
    cache_kt = jnp.transpose(cache_k, (0, 1, 3, 4, 5, 2)).reshape(depth, Bs, ATTN_WIDTH, past_len)
    cache_v4 = cache_v.reshape(depth, Bs, past_len * N_HEADS, V_DIM)
    kv_stack = depth
    pp, cp = [], []
    ks_, vs_, ps_, cs_ = [], [], [], []
    for l in range(depth):
        lam_init = 0.8 - 0.6 * math.exp(-0.3 * l)
        final = l == depth - 1

        upool, q, k, kb, v, vb, uconv = _prompt_in_proj(xp, norm_mix_rows, w_in_bf, cos_p, sin_p, IN_PROJ_TILE, S,
                                                        kv_stack, l)
        kv_stack = (k, v)
        o = _prompt_attention(lambda_qk, diff_g_col, q.reshape(B, S, -1), kb.reshape(B, S, -1),
                              vb, lam_init, l)
        xp = _prompt_dense(xp, o.reshape(B * S, -1), upool, uconv, zero_pool, zero_conv, mixer_w, dense_w,
                           ROW_TILE, S, final, l)
        pp.append(upool.reshape(B, S, -1)[:, S - POOL_HIST:])
        cp.append(uconv.reshape(B, S, -1)[:, S - CONV_HIST:])

        upool, q, k, kb, v, vb, uconv = _sample_in_proj(xs, norm_mix_rows, w_in_bf, cos_s, sin_s, l)
        o = _sample_attention(lambda_qk, diff_g_row, q.reshape(Bs, Ls, -1), cache_kt, cache_v4,
                              kb.reshape(Bs, Ls, -1), vb.reshape(Bs, Ls, -1), lam_init, l)
        ypool, yconv = _sample_mixers(upool, uconv, hist_pool_s, hist_conv_s, mixer_w, Ls, past_len, l)
        xs = _sample_dense(xs, ypool, o.reshape(Bs * Ls, -1), yconv, *dense_w, Bs * Ls, final, l)
        ks_.append(k.reshape(Bs, Ls, N_HEADS, 2, HEAD_DIM))
        vs_.append(v.reshape(Bs, Ls, N_HEADS, V_DIM))
        ps_.append(upool.reshape(Bs, Ls, -1)[:, Ls - POOL_HIST:])
        cs_.append(uconv.reshape(Bs, Ls, -1)[:, Ls - CONV_HIST:])

    kt_all, v4_all = kv_stack
    new_k = jnp.transpose(kt_all.reshape(depth, B, N_HEADS, 2, HEAD_DIM, S), (0, 1, 5, 2, 3, 4))
    new_v = v4_all.reshape(depth, B, S, N_HEADS, V_DIM)
    return (xp.reshape(B, S, D_MODEL), xs.reshape(Bs, Ls, D_MODEL),
            new_k, new_v, jnp.stack(pp), jnp.stack(cp),
            jnp.stack(ks_), jnp.stack(vs_), jnp.stack(ps_), jnp.stack(cs_))
```
